```python
import jax, jax.numpy as jnp
from jax import lax
import numpy as np

D_MODEL = 1024
BATCH = 8
SEQ = 2048
DEPTH = 2
DEC_BATCH = 128
DEC_SEQ = 4
PAST_LEN = 16384
PAGE_SIZE = 128

N_MIXERS = 2
N_A = (DEPTH + 1) // 2
N_B = DEPTH // 2
H_A = 8
DV_A = D_MODEL // H_A
DK_A = DV_A // 2
CHUNK_A = 64
IN_A = 2 * H_A * DK_A + 2 * D_MODEL + 2 * H_A
CONV_W = 3
D_FF = 4 * D_MODEL
N_MOD = 6
EPS = 1e-6

kernel_name = "mlstm_shortconv_hybrid_step"


def rmsnorm(x, g):
    xf = x.astype(jnp.float32)
    y = xf * lax.rsqrt(jnp.mean(xf * xf, axis=-1, keepdims=True) + EPS)
    return y.astype(x.dtype) * g


def modulate(x, g, shift, scale):
    return rmsnorm(x, g) * (1 + scale[:, None, :]) + shift[:, None, :]


def mlstm_chunk(carry, xs):
    C0, n0, m0 = carry
    q, k, v, ig, lf = xs
    L = q.shape[2]
    b = jnp.cumsum(lf, axis=-1)
    causal = jnp.tril(jnp.ones((L, L), dtype=bool))
    dmat = jnp.where(causal, b[..., :, None] - b[..., None, :] + ig[..., None, :], -jnp.inf)
    inter = b + m0[..., None]
    m = jnp.maximum(inter, jnp.max(dmat, axis=-1))
    s = jnp.einsum('bhtd,bhsd->bhts', q, k) * jnp.exp(dmat - m[..., None])
    e_inter = jnp.exp(inter - m)
    num = jnp.einsum('bhts,bhse->bhte', s, v) + e_inter[..., None] * jnp.einsum('bhtd,bhde->bhte', q, C0)
    den = jnp.sum(s, axis=-1) + e_inter * jnp.einsum('bhtd,bhd->bht', q, n0)
    h = num / jnp.maximum(jnp.abs(den), jnp.exp(-m))[..., None]
    m_l = m[..., -1]
    a = jnp.exp(b[..., -1:] - b + ig - m_l[..., None])
    decay = jnp.exp(b[..., -1] + m0 - m_l)
    C1 = decay[..., None, None] * C0 + jnp.einsum('bhs,bhsd,bhse->bhde', a, k, v)
    n1 = decay[..., None] * n0 + jnp.einsum('bhs,bhsd->bhd', a, k)
    return (C1, n1, m_l), h


def mlstm_mixer(h, C0, n0, m0, w_in, b_gates, g_head, w_out):
    bn, t, _ = h.shape
    hk = H_A * DK_A
    p = (h @ w_in).astype(jnp.float32)
    q = p[..., :hk]
    k = p[..., hk:2 * hk] * (DK_A ** -0.5)
    v = p[..., 2 * hk:2 * hk + D_MODEL]
    o = p[..., 2 * hk + D_MODEL:2 * hk + 2 * D_MODEL]
    gates = p[..., 2 * hk + 2 * D_MODEL:] + b_gates.astype(jnp.float32)
    ig = gates[..., :H_A]
    lf = jax.nn.log_sigmoid(gates[..., H_A:])
    L = CHUNK_A if t % CHUNK_A == 0 else t
    nc = t // L

    def heads(z, d):
        return z.reshape(bn, t, H_A, d).transpose(0, 2, 1, 3)

    def chunks(z):
        return jnp.moveaxis(z.reshape(bn, H_A, nc, L, *z.shape[3:]), 2, 0)

    xs = (chunks(heads(q, DK_A)), chunks(heads(k, DK_A)), chunks(heads(v, DV_A)),
          chunks(ig.transpose(0, 2, 1)), chunks(lf.transpose(0, 2, 1)))
    init = (C0.astype(jnp.float32), n0.astype(jnp.float32), m0.astype(jnp.float32))
    (C1, n1, m1), hs = lax.scan(mlstm_chunk, init, xs)
    hs = jnp.moveaxis(hs, 0, 2).reshape(bn, H_A, t, DV_A)
    hs = hs * lax.rsqrt(jnp.mean(hs * hs, axis=-1, keepdims=True) + EPS)
    hs = hs.transpose(0, 2, 1, 3).reshape(bn, t, D_MODEL)
    y = (hs * g_head.astype(jnp.float32) * jax.nn.sigmoid(o)).astype(h.dtype)
    return y @ w_out, C1.astype(h.dtype), n1.astype(h.dtype), m1.astype(h.dtype)


def conv_mixer(h, buf, w_in, w_conv, w_out):
    t = h.shape[1]
    p = h @ w_in
    bg = p[..., :D_MODEL]
    cg = p[..., D_MODEL:2 * D_MODEL]
    xt = p[..., 2 * D_MODEL:]
    u = cg * xt
    full = jnp.concatenate([buf.astype(u.dtype), u], axis=1)
    y = w_conv[0] * full[:, 0:t]
    for j in range(1, CONV_W):
        y = y + w_conv[j] * full[:, j:j + t]
    return (bg * y) @ w_out, full[:, t:]


def trunk(x, c, C0s, n0s, m0s, bufs, ada_w, ada_b, norm_g, final_norm_g,
          mlstm_w_in, mlstm_b_gates, mlstm_head_g, mlstm_w_out,
          conv_w_in, conv_w, conv_w_out, mlp_w_up, mlp_w_down):
    new_C, new_n, new_m, new_buf = [], [], [], []
    for i in range(DEPTH):
        mod = (jax.nn.silu(c) @ ada_w[i] + ada_b[i]).reshape(c.shape[0], N_MOD, D_MODEL)
        hn = modulate(x, norm_g[i, 0], mod[:, 0], mod[:, 1])
        j = i // N_MIXERS
        if i % N_MIXERS == 0:
            out, C1, n1, m1 = mlstm_mixer(hn, C0s[j], n0s[j], m0s[j], mlstm_w_in[j],
                                          mlstm_b_gates[j], mlstm_head_g[j], mlstm_w_out[j])
            new_C.append(C1)
            new_n.append(n1)
            new_m.append(m1)
        else:
            out, b1 = conv_mixer(hn, bufs[j], conv_w_in[j], conv_w[j], conv_w_out[j])
            new_buf.append(b1)
        x = x + mod[:, 2][:, None, :] * out
        hn = modulate(x, norm_g[i, 1], mod[:, 3], mod[:, 4])
        f = jnp.square(jax.nn.relu(hn @ mlp_w_up[i])) @ mlp_w_down[i]
        x = x + mod[:, 5][:, None, :] * f
    return (rmsnorm(x, final_norm_g), jnp.stack(new_C), jnp.stack(new_n),
            jnp.stack(new_m), jnp.stack(new_buf))


def setup_inputs(seed: int = 0) -> dict:
    key = jax.random.key(seed)
    ks = jax.random.split(key, 24)

    def nrm(k, shape, s):
        return jax.random.normal(k, shape, jnp.float32) * s

    d = D_MODEL
    return {
        "x_prompt": nrm(ks[0], (BATCH, SEQ, d), 1.0),
        "x_sample": nrm(ks[1], (DEC_BATCH, DEC_SEQ, d), 1.0),
        "c_prompt": nrm(ks[2], (BATCH, d), 1.0),
        "c_sample": nrm(ks[3], (DEC_BATCH, d), 1.0),
        "state_mlstm_C": nrm(ks[4], (N_A, DEC_BATCH, H_A, DK_A, DV_A), DK_A ** -0.5),
        "state_mlstm_n": nrm(ks[5], (N_A, DEC_BATCH, H_A, DK_A), DK_A ** -0.5),
        "state_mlstm_m": nrm(ks[6], (N_A, DEC_BATCH, H_A), 1.0),
        "state_conv": nrm(ks[7], (N_B, DEC_BATCH, CONV_W - 1, d), 1.0),
        "ada_w": nrm(ks[8], (DEPTH, d, N_MOD * d), d ** -0.5),
        "ada_b": nrm(ks[9], (DEPTH, N_MOD * d), 0.02),
        "norm_g": 1.0 + nrm(ks[10], (DEPTH, 2, d), 0.02),
        "final_norm_g": 1.0 + nrm(ks[11], (d,), 0.02),
        "mlstm_w_in": nrm(ks[12], (N_A, d, IN_A), d ** -0.5),
        "mlstm_b_gates": jnp.concatenate([nrm(ks[13], (N_A, H_A), 0.1),
                                          3.0 + nrm(ks[14], (N_A, H_A), 0.1)], axis=-1),
        "mlstm_head_g": 1.0 + nrm(ks[15], (N_A, d), 0.02),
        "mlstm_w_out": nrm(ks[16], (N_A, d, d), d ** -0.5),
        "conv_w_in": nrm(ks[17], (N_B, d, 3 * d), d ** -0.5),
        "conv_w": nrm(ks[18], (N_B, CONV_W, d), CONV_W ** -0.5),
        "conv_w_out": nrm(ks[19], (N_B, d, d), d ** -0.5),
        "mlp_w_up": nrm(ks[20], (DEPTH, d, D_FF), d ** -0.5),
        "mlp_w_down": nrm(ks[21], (DEPTH, D_FF, d), D_FF ** -0.5),
    }


def reference(x_prompt, x_sample, c_prompt, c_sample, state_mlstm_C, state_mlstm_n,
              state_mlstm_m, state_conv, ada_w, ada_b, norm_g, final_norm_g,
              mlstm_w_in, mlstm_b_gates, mlstm_head_g, mlstm_w_out,
              conv_w_in, conv_w, conv_w_out, mlp_w_up, mlp_w_down):
    dt = x_prompt.dtype
    bp = x_prompt.shape[0]
    zC = jnp.zeros((N_A, bp, H_A, DK_A, DV_A), dt)
    zn = jnp.zeros((N_A, bp, H_A, DK_A), dt)
    zm = jnp.zeros((N_A, bp, H_A), dt)
    zb = jnp.zeros((N_B, bp, CONV_W - 1, D_MODEL), dt)
    y_prompt, prompt_C, prompt_n, prompt_m, prompt_conv = trunk(
        x_prompt, c_prompt, zC, zn, zm, zb, ada_w, ada_b, norm_g, final_norm_g,
        mlstm_w_in, mlstm_b_gates, mlstm_head_g, mlstm_w_out,
        conv_w_in, conv_w, conv_w_out, mlp_w_up, mlp_w_down)
    y_sample, sample_C, sample_n, sample_m, sample_conv = trunk(
        x_sample, c_sample, state_mlstm_C, state_mlstm_n, state_mlstm_m, state_conv,
        ada_w, ada_b, norm_g, final_norm_g,
        mlstm_w_in, mlstm_b_gates, mlstm_head_g, mlstm_w_out,
        conv_w_in, conv_w, conv_w_out, mlp_w_up, mlp_w_down)
    return (y_prompt, y_sample, prompt_C, prompt_n, prompt_m, prompt_conv,
            sample_C, sample_n, sample_m, sample_conv)
```

```python
import functools

import jax
import jax.numpy as jnp
from jax import lax
from jax.experimental import pallas as pl
from jax.experimental.pallas import tpu as pltpu

D = 1024
H = 8
DK = 64
DV = 128
HK = H * DK
NP = 2 * HK + 2 * D
GW = 256
FG = 128
DFF = 4 * D
N_MOD = 6
CONV_W = 3
EPS = 1e-6
CHUNK = 128
FF_CHUNK = 1024
BF = jnp.bfloat16
F32 = jnp.float32
VMEM_LIMIT_BYTES = 56 * 1024 * 1024


def _dot(a, b):
    return jnp.dot(a, b, preferred_element_type=F32)


def _dot_nt(a, b):
    return lax.dot_general(a, b, (((1,), (1,)), ((), ())), preferred_element_type=F32)


def _dot_tn(a, b):
    return lax.dot_general(a, b, (((0,), (0,)), ((), ())), preferred_element_type=F32)


def _rms(x):
    return x * lax.rsqrt(jnp.mean(x * x, axis=-1, keepdims=True) + EPS)


def _modulate(x, g, shift, scale):
    return _rms(x) * g * (1.0 + scale) + shift


def _log_sigmoid(x):
    return jnp.minimum(x, 0.0) - jnp.log1p(jnp.exp(-jnp.abs(x)))


def _scan_rows(x, op, ident):
    n = x.shape[0]
    row = lax.broadcasted_iota(jnp.int32, x.shape, 0)
    d = 1
    while d < n:
        x = op(x, jnp.where(row >= d, pltpu.roll(x, d, 0), ident))
        d *= 2
    return x


def _const_spec(shape):
    nd = len(shape)
    return pl.BlockSpec(shape, lambda *_: (0,) * nd, pipeline_mode=pl.Buffered(1))


def _params(*sem):
    return pltpu.CompilerParams(dimension_semantics=sem, vmem_limit_bytes=VMEM_LIMIT_BYTES)


def _ada_kernel(c_ref, w_ref, b_ref, o_ref):
    c = c_ref[...]
    sc = (c * jax.nn.sigmoid(c)).astype(BF)
    o_ref[0] = _dot(sc, w_ref[0].astype(BF)) + b_ref[0]


def _ada(c_all, ada_w, ada_b):
    depth, _, n = ada_w.shape
    nb = c_all.shape[0]
    tn = 1024
    return pl.pallas_call(
        _ada_kernel,
        grid=(depth, n // tn),
        in_specs=[
            pl.BlockSpec((nb, D), lambda i, j: (0, 0)),
            pl.BlockSpec((1, D, tn), lambda i, j: (i, 0, j)),
            pl.BlockSpec((1, 1, tn), lambda i, j: (i, 0, j)),
        ],
        out_specs=pl.BlockSpec((1, nb, tn), lambda i, j: (i, 0, j)),
        out_shape=jax.ShapeDtypeStruct((depth, nb, n), F32),
        compiler_params=_params("arbitrary", "arbitrary"),
        name="ada",
    )(c_all, ada_w, ada_b.reshape(depth, 1, n))


def _mlp_kernel(*refs, pre, final):
    refs = list(refs)
    x_ref = refs.pop(0)
    if pre:
        y_ref = refs.pop(0)
        wo_ref = refs.pop(0)
    mod_ref, g_ref, wu_ref, wd_ref = refs[:4]
    refs = refs[4:]
    if final:
        fg_ref = refs.pop(0)
    o_ref = refs.pop(0)

    x = x_ref[0]
    if pre:
        x = x + mod_ref[0, 2] * _dot(y_ref[0].astype(BF), wo_ref[...])
    hb = _modulate(x, g_ref[...], mod_ref[0, 3], mod_ref[0, 4]).astype(BF)
    acc = jnp.zeros(x.shape, F32)
    for c in range(DFF // FF_CHUNK):
        cs = slice(c * FF_CHUNK, (c + 1) * FF_CHUNK)
        hid = jnp.maximum(_dot(hb, wu_ref[:, cs]), 0.0)
        acc = acc + _dot((hid * hid).astype(BF), wd_ref[cs, :])
    x = x + mod_ref[0, 5] * acc
    if final:
        x = _rms(x) * fg_ref[...]
    o_ref[0] = x


def _mlp(x, mod, g, w_up, w_down, *, tm, pre=None, final_g=None):
    nb, t, _ = x.shape
    r = mod.shape[2]
    tok = pl.BlockSpec((1, tm, D), lambda b, i: (b, i, 0))
    if r == 1:
        mod_spec = pl.BlockSpec((1, N_MOD, 1, D), lambda b, i: (b, 0, 0, 0))
    else:
        mod_spec = pl.BlockSpec((1, N_MOD, tm, D), lambda b, i: (b, 0, i, 0))
    args, specs = [x], [tok]
    if pre is not None:
        y, w_out = pre
        args += [y, w_out]
        specs += [tok, _const_spec((D, D))]
    args += [mod, g.reshape(1, D), w_up, w_down]
    specs += [mod_spec, _const_spec((1, D)), _const_spec((D, DFF)), _const_spec((DFF, D))]
    if final_g is not None:
        args.append(final_g.reshape(1, D))
        specs.append(_const_spec((1, D)))
    return pl.pallas_call(
        functools.partial(_mlp_kernel, pre=pre is not None, final=final_g is not None),
        grid=(nb, t // tm),
        in_specs=specs,
        out_specs=tok,
        out_shape=jax.ShapeDtypeStruct(x.shape, F32),
        compiler_params=_params("arbitrary", "arbitrary"),
        name="mlp",
    )(*args)


def _mlstm_prompt_kernel(x_ref, mod_ref, g_ref, win_ref, wg_ref, bg_ref, gh_ref, wout_ref,
                         xo_ref, s_ref, m_ref, y_scr, *, tm):
    @pl.when(pl.program_id(1) == 0)
    def _():
        s_ref[...] = jnp.zeros(s_ref.shape, F32)
        m_ref[...] = jnp.zeros(m_ref.shape, F32)

    x = x_ref[0]
    hb = _modulate(x, g_ref[...], mod_ref[0, 0], mod_ref[0, 1]).astype(BF)
    p = _dot(hb, win_ref[...])
    gts = _dot(hb, wg_ref[...]) + bg_ref[...]

    n = CHUNK
    causal = (lax.broadcasted_iota(jnp.int32, (n, n), 0)
              >= lax.broadcasted_iota(jnp.int32, (n, n), 1))
    one_col = (lax.broadcasted_iota(jnp.int32, (n, DV), 1) == 0).astype(BF)

    for c in range(tm // n):
        rs = slice(c * n, (c + 1) * n)
        ig = gts[rs, 0:FG]
        lf = _log_sigmoid(gts[rs, FG:GW])
        b = _scan_rows(lf, jnp.add, 0.0)
        r = ig - b
        m0 = m_ref[0]
        mx = jnp.maximum(m0, _scan_rows(r, jnp.maximum, -jnp.inf))
        e_int = jnp.exp(m0 - mx)
        m = b + mx
        e_neg = jnp.exp(-m)
        mx_last = mx[n - 1:n]
        a = jnp.exp(r - mx_last)
        decay = jnp.exp(m0 - mx_last)
        m_ref[0] = m[n - 1:n]
        r_t = r.T

        for h in range(H):
            q = p[rs, h * DK:(h + 1) * DK]
            k = p[rs, HK + h * DK:HK + (h + 1) * DK] * (DK ** -0.5)
            v = p[rs, 2 * HK + h * DV:2 * HK + (h + 1) * DV].astype(BF)
            dec = jnp.where(causal,
                            jnp.exp(jnp.minimum(r_t[h:h + 1, :] - mx[:, h:h + 1], 0.0)), 0.0)
            s = _dot_nt(q.astype(BF), k.astype(BF)) * dec
            st = s_ref[0, h]
            inter = _dot((q * e_int[:, h:h + 1]).astype(BF), st.astype(BF))
            num = _dot(s.astype(BF), v) + inter[:, :DV]
            den = jnp.sum(s, axis=-1, keepdims=True) + inter[:, DV:DV + 1]
            hh = num / jnp.maximum(jnp.abs(den), e_neg[:, h:h + 1])
            hh = hh * lax.rsqrt(jnp.mean(hh * hh, axis=-1, keepdims=True) + EPS)
            y_scr[rs, h * DV:(h + 1) * DV] = hh
            ak = (k * a[:, h:h + 1]).astype(BF)
            v_aug = jnp.concatenate([v, one_col], axis=1)
            s_ref[0, h] = decay[:, h:h + 1] * st + _dot_tn(ak, v_aug)

    o = p[:, 2 * HK + D:NP]
    y = (y_scr[...] * gh_ref[...] * jax.nn.sigmoid(o)).astype(BF)
    xo_ref[0] = x + mod_ref[0, 2] * _dot(y, wout_ref[...])


def _mlstm_prompt(x, mod, g, w_in, w_g, b_g, g_head, w_out, *, tm):
    nb, t, _ = x.shape
    tok = pl.BlockSpec((1, tm, D), lambda b, i: (b, i, 0))
    return pl.pallas_call(
        functools.partial(_mlstm_prompt_kernel, tm=tm),
        grid=(nb, t // tm),
        in_specs=[
            tok,
            pl.BlockSpec((1, N_MOD, 1, D), lambda b, i: (b, 0, 0, 0)),
            _const_spec((1, D)),
            _const_spec((D, NP)),
            _const_spec((D, GW)),
            _const_spec((1, GW)),
            _const_spec((1, D)),
            _const_spec((D, D)),
        ],
        out_specs=[
            tok,
            pl.BlockSpec((1, H, DK, 2 * DV), lambda b, i: (b, 0, 0, 0)),
            pl.BlockSpec((1, 1, FG), lambda b, i: (b, 0, 0)),
        ],
        out_shape=[
            jax.ShapeDtypeStruct(x.shape, F32),
            jax.ShapeDtypeStruct((nb, H, DK, 2 * DV), F32),
            jax.ShapeDtypeStruct((nb, 1, FG), F32),
        ],
        scratch_shapes=[pltpu.VMEM((tm, D), F32)],
        compiler_params=_params("arbitrary", "arbitrary"),
        name="mlstm_prompt",
    )(x, mod, g.reshape(1, D), w_in, w_g, b_g, g_head.reshape(1, D), w_out)


def _proj_kernel(x_ref, mod_ref, g_ref, win_ref, wg_ref, bg_ref, p_ref):
    hb = _modulate(x_ref[0], g_ref[...], mod_ref[0, 0], mod_ref[0, 1]).astype(BF)
    p_ref[:, :NP] = _dot(hb, win_ref[...])
    p_ref[:, NP:] = _dot(hb, wg_ref[...]) + bg_ref[...]


def _proj(x, mod, g, w_in, w_g, b_g):
    _, t, _ = x.shape
    return pl.pallas_call(
        _proj_kernel,
        grid=(1,),
        in_specs=[
            pl.BlockSpec((1, t, D), lambda i: (0, 0, 0)),
            pl.BlockSpec((1, N_MOD, t, D), lambda i: (0, 0, 0, 0)),
            _const_spec((1, D)),
            _const_spec((D, NP)),
            _const_spec((D, GW)),
            _const_spec((1, GW)),
        ],
        out_specs=pl.BlockSpec((t, NP + GW), lambda i: (0, 0)),
        out_shape=jax.ShapeDtypeStruct((t, NP + GW), F32),
        compiler_params=_params("arbitrary"),
        name="mlstm_proj",
    )(x, mod, g.reshape(1, D), w_in, w_g, b_g)


def _mlstm_sample_kernel(p_ref, c0_ref, n0_ref, m0_ref, gh_ref,
                         y_ref, c1_ref, n1_ref, m1_ref, *, bb, t):
    trow = lax.broadcasted_iota(jnp.int32, (t, 1), 0)

    def one_sequence(j, carry):
        pj = p_ref[j]
        ig = pj[:, NP:NP + H]
        lf = _log_sigmoid(pj[:, NP + FG:NP + FG + H])
        b_rows = [lf[0:1]]
        for i in range(1, t):
            b_rows.append(b_rows[-1] + lf[i:i + 1])
        b = jnp.concatenate(b_rows, axis=0)
        r = ig - b
        cm_rows = [r[0:1]]
        for i in range(1, t):
            cm_rows.append(jnp.maximum(cm_rows[-1], r[i:i + 1]))
        m0 = m0_ref[pl.ds(j, 1), :]
        mx = jnp.maximum(m0, jnp.concatenate(cm_rows, axis=0))
        e_int = jnp.exp(m0 - mx)
        m = b + mx
        e_neg = jnp.exp(-m)
        mx_last = mx[t - 1:t]
        a = jnp.exp(r - mx_last)
        decay = jnp.exp(m0 - mx_last)
        m1_ref[pl.ds(j, 1), :] = m[t - 1:t]
        n0_all = n0_ref[j]

        n1_rows = []
        for h in range(H):
            q = pj[:, h * DK:(h + 1) * DK]
            k = pj[:, HK + h * DK:HK + (h + 1) * DK] * (DK ** -0.5)
            v = pj[:, 2 * HK + h * DV:2 * HK + (h + 1) * DV]
            o = pj[:, 2 * HK + D + h * DV:2 * HK + D + (h + 1) * DV]
            c0 = c0_ref[j, h]
            n0 = n0_all[h:h + 1]
            ei = e_int[:, h:h + 1]
            mxh = mx[:, h:h + 1]
            num = ei * _dot(q.astype(BF), c0.astype(BF))
            den = ei * jnp.sum(q * n0, axis=-1, keepdims=True)
            for s in range(t):
                qk = jnp.sum(q * k[s:s + 1], axis=-1, keepdims=True)
                dec = jnp.where(trow >= s,
                                jnp.exp(jnp.minimum(r[s:s + 1, h:h + 1] - mxh, 0.0)), 0.0)
                w = qk * dec
                num = num + w * v[s:s + 1]
                den = den + w
            hh = num / jnp.maximum(jnp.abs(den), e_neg[:, h:h + 1])
            hh = hh * lax.rsqrt(jnp.mean(hh * hh, axis=-1, keepdims=True) + EPS)
            y_ref[j, :, h * DV:(h + 1) * DV] = (
                hh * gh_ref[:, h * DV:(h + 1) * DV] * jax.nn.sigmoid(o))
            ah = a[:, h:h + 1]
            ak = k * ah
            c1_ref[j, h] = decay[:, h:h + 1] * c0 + _dot_tn(ak.astype(BF), v.astype(BF))
            n1_rows.append(decay[:, h:h + 1] * n0 + jnp.sum(ak, axis=0, keepdims=True))
        n1_ref[j] = jnp.concatenate(n1_rows, axis=0)
        return carry

    lax.fori_loop(0, bb, one_sequence, 0)


def _mlstm_sample(p, c0, n0, m0, g_head, *, bb):
    nb, t, w = p.shape
    return pl.pallas_call(
        functools.partial(_mlstm_sample_kernel, bb=bb, t=t),
        grid=(nb // bb,),
        in_specs=[
            pl.BlockSpec((bb, t, w), lambda i: (i, 0, 0)),
            pl.BlockSpec((bb, H, DK, DV), lambda i: (i, 0, 0, 0)),
            pl.BlockSpec((bb, H, DK), lambda i: (i, 0, 0)),
            pl.BlockSpec((bb, H), lambda i: (i, 0)),
            _const_spec((1, D)),
        ],
        out_specs=[
            pl.BlockSpec((bb, t, D), lambda i: (i, 0, 0)),
            pl.BlockSpec((bb, H, DK, DV), lambda i: (i, 0, 0, 0)),
            pl.BlockSpec((bb, H, DK), lambda i: (i, 0, 0)),
            pl.BlockSpec((bb, H), lambda i: (i, 0)),
        ],
        out_shape=[
            jax.ShapeDtypeStruct((nb, t, D), F32),
            jax.ShapeDtypeStruct(c0.shape, F32),
            jax.ShapeDtypeStruct(n0.shape, F32),
            jax.ShapeDtypeStruct(m0.shape, F32),
        ],
        compiler_params=_params("arbitrary"),
        name="mlstm_sample",
    )(p, c0, n0, m0, g_head.reshape(1, D))


def _conv_taps(u, prev2, prev1, wc_ref):
    return wc_ref[0:1] * prev2 + wc_ref[1:2] * prev1 + wc_ref[2:3] * u


def _conv_prompt_kernel(x_ref, mod_ref, g_ref, win_ref, wc_ref, wout_ref,
                        xo_ref, st_ref, carry_scr):
    @pl.when(pl.program_id(1) == 0)
    def _():
        carry_scr[...] = jnp.zeros(carry_scr.shape, F32)

    x = x_ref[0]
    tm = x.shape[0]
    hb = _modulate(x, g_ref[...], mod_ref[0, 0], mod_ref[0, 1]).astype(BF)
    u = _dot(hb, win_ref[:, D:2 * D]) * _dot(hb, win_ref[:, 2 * D:])
    row = lax.broadcasted_iota(jnp.int32, (tm, D), 0)
    c0 = carry_scr[0:1]
    c1 = carry_scr[1:2]
    prev1 = jnp.where(row == 0, c1, pltpu.roll(u, 1, 0))
    prev2 = jnp.where(row == 0, c0, jnp.where(row == 1, c1, pltpu.roll(u, 2, 0)))
    y = _conv_taps(u, prev2, prev1, wc_ref)
    bg = _dot(hb, win_ref[:, :D])
    xo_ref[0] = x + mod_ref[0, 2] * _dot((bg * y).astype(BF), wout_ref[...])
    carry_scr[...] = u[tm - 2:tm]
    st_ref[0] = u[tm - 2:tm]


def _conv_prompt(x, mod, g, w_in, w_conv, w_out, *, tm):
    nb, t, _ = x.shape
    tok = pl.BlockSpec((1, tm, D), lambda b, i: (b, i, 0))
    return pl.pallas_call(
        _conv_prompt_kernel,
        grid=(nb, t // tm),
        in_specs=[
            tok,
            pl.BlockSpec((1, N_MOD, 1, D), lambda b, i: (b, 0, 0, 0)),
            _const_spec((1, D)),
            _const_spec((D, 3 * D)),
            _const_spec((CONV_W, D)),
            _const_spec((D, D)),
        ],
        out_specs=[tok, pl.BlockSpec((1, CONV_W - 1, D), lambda b, i: (b, 0, 0))],
        out_shape=[jax.ShapeDtypeStruct(x.shape, F32),
                   jax.ShapeDtypeStruct((nb, CONV_W - 1, D), F32)],
        scratch_shapes=[pltpu.VMEM((CONV_W - 1, D), F32)],
        compiler_params=_params("arbitrary", "arbitrary"),
        name="conv_prompt",
    )(x, mod, g.reshape(1, D), w_in, w_conv, w_out)


def _conv_sample_kernel(x_ref, mod_ref, g_ref, win_ref, wc_ref, wout_ref, h2_ref, h1_ref,
                        xo_ref, u_ref, *, t):
    x = x_ref[0]
    rows = x.shape[0]
    hb = _modulate(x, g_ref[...], mod_ref[0, 0], mod_ref[0, 1]).astype(BF)
    u = _dot(hb, win_ref[:, D:2 * D]) * _dot(hb, win_ref[:, 2 * D:])
    pos = lax.broadcasted_iota(jnp.int32, (rows, D), 0) % t
    prev1 = jnp.where(pos < 1, h1_ref[...], pltpu.roll(u, 1, 0))
    prev2 = jnp.where(pos < 2, h2_ref[...], pltpu.roll(u, 2, 0))
    y = _conv_taps(u, prev2, prev1, wc_ref)
    bg = _dot(hb, win_ref[:, :D])
    xo_ref[0] = x + mod_ref[0, 2] * _dot((bg * y).astype(BF), wout_ref[...])
    u_ref[...] = u


def _conv_sample(x, mod, g, w_in, w_conv, w_out, halo2, halo1, *, t):
    _, rows, _ = x.shape
    full = pl.BlockSpec((1, rows, D), lambda i: (0, 0, 0))
    flat = pl.BlockSpec((rows, D), lambda i: (0, 0))
    return pl.pallas_call(
        functools.partial(_conv_sample_kernel, t=t),
        grid=(1,),
        in_specs=[
            full,
            pl.BlockSpec((1, N_MOD, rows, D), lambda i: (0, 0, 0, 0)),
            _const_spec((1, D)),
            _const_spec((D, 3 * D)),
            _const_spec((CONV_W, D)),
            _const_spec((D, D)),
            flat,
            flat,
        ],
        out_specs=[full, flat],
        out_shape=[jax.ShapeDtypeStruct(x.shape, F32), jax.ShapeDtypeStruct((rows, D), F32)],
        compiler_params=_params("arbitrary"),
        name="conv_sample",
    )(x, mod, g.reshape(1, D), w_in, w_conv, w_out, halo2, halo1)


def kernel(x_prompt, x_sample, c_prompt, c_sample, state_mlstm_C, state_mlstm_n, state_mlstm_m,
           state_conv, ada_w, ada_b, norm_g, final_norm_g, mlstm_w_in, mlstm_b_gates,
           mlstm_head_g, mlstm_w_out, conv_w_in, conv_w, conv_w_out, mlp_w_up, mlp_w_down):
    assert ada_w.shape[0] == 2 and mlstm_w_in.shape[0] == 1 and conv_w_in.shape[0] == 1
    bp, tp, _ = x_prompt.shape
    bs, ts, _ = x_sample.shape
    assert ts == 4 and tp % 512 == 0

    mod = _ada(jnp.concatenate([c_prompt, c_sample], axis=0), ada_w, ada_b)
    mod = mod.reshape(2, bp + bs, N_MOD, D)
    mod_p = mod[:, :bp].reshape(2, bp, N_MOD, 1, D)
    mod_s = jnp.repeat(jnp.transpose(mod[:, bp:], (0, 2, 1, 3)), ts, axis=2)[:, None]

    w_in0 = mlstm_w_in[0]
    win = w_in0[:, :NP].astype(BF)
    w_g = jnp.zeros((D, GW), F32)
    w_g = w_g.at[:, 0:H].set(w_in0[:, NP:NP + H]).at[:, FG:FG + H].set(w_in0[:, NP + H:])
    w_g = w_g.astype(BF)
    b_g = jnp.zeros((1, GW), F32)
    b_g = b_g.at[0, 0:H].set(mlstm_b_gates[0, :H]).at[0, FG:FG + H].set(mlstm_b_gates[0, H:])
    wout0 = mlstm_w_out[0].astype(BF)
    cwin = conv_w_in[0].astype(BF)
    cwout = conv_w_out[0].astype(BF)
    up = mlp_w_up.astype(BF)
    down = mlp_w_down.astype(BF)

    x, s_p, m_p = _mlstm_prompt(x_prompt, mod_p[0], norm_g[0, 0], win, w_g, b_g,
                                mlstm_head_g[0], wout0, tm=256)
    x = _mlp(x, mod_p[0], norm_g[0, 1], up[0], down[0], tm=512)
    x, conv_p = _conv_prompt(x, mod_p[1], norm_g[1, 0], cwin, conv_w[0], cwout, tm=512)
    y_prompt = _mlp(x, mod_p[1], norm_g[1, 1], up[1], down[1], tm=512, final_g=final_norm_g)
    prompt_c = s_p[None, :, :, :, :DV]
    prompt_n = s_p[None, :, :, :, DV]
    prompt_m = m_p[None, :, 0, :H]
    prompt_conv = conv_p[None]

    xs = x_sample.reshape(1, bs * ts, D)
    p = _proj(xs, mod_s[0], norm_g[0, 0], win, w_g, b_g).reshape(bs, ts, NP + GW)
    y, c1, n1, m1 = _mlstm_sample(p, state_mlstm_C[0], state_mlstm_n[0], state_mlstm_m[0],
                                  mlstm_head_g[0], bb=8)
    xs = _mlp(xs, mod_s[0], norm_g[0, 1], up[0], down[0], tm=bs * ts,
              pre=(y.reshape(1, bs * ts, D), wout0))
    buf = state_conv[0]
    zeros = jnp.zeros((bs, ts - 2, D), F32)
    halo2 = jnp.concatenate([buf, zeros], axis=1).reshape(bs * ts, D)
    halo1 = jnp.concatenate([buf[:, 1:], zeros, zeros[:, :1]], axis=1).reshape(bs * ts, D)
    xs, u = _conv_sample(xs, mod_s[1], norm_g[1, 0], cwin, conv_w[0], cwout, halo2, halo1, t=ts)
    y_sample = _mlp(xs, mod_s[1], norm_g[1, 1], up[1], down[1], tm=bs * ts,
                    final_g=final_norm_g).reshape(bs, ts, D)
    sample_conv = u.reshape(bs, ts, D)[None, :, ts - 2:]

    return (y_prompt, y_sample, prompt_c, prompt_n, prompt_m, prompt_conv,
            c1[None], n1[None], m1[None], sample_conv)
```

```python
import functools

import jax
import jax.numpy as jnp
from jax import lax
from jax.experimental import pallas as pl
from jax.experimental.pallas import tpu as pltpu

D = 1024
H = 8
DK = 64
DV = 128
HK = H * DK
NP = 2 * HK + 2 * D
GW = 256
FG = 128
DFF = 4 * D
N_MOD = 6
CONV_W = 3
EPS = 1e-6
CHUNK = 128
FF_CHUNK = 1024
BF = jnp.bfloat16
F32 = jnp.float32
VMEM_LIMIT_BYTES = 56 * 1024 * 1024


def _dot(a, b):
    return jnp.dot(a, b, preferred_element_type=F32)


def _dot_nt(a, b):
    return lax.dot_general(a, b, (((1,), (1,)), ((), ())), preferred_element_type=F32)


def _dot_tn(a, b):
    return lax.dot_general(a, b, (((0,), (0,)), ((), ())), preferred_element_type=F32)


def _rms(x):
    return x * lax.rsqrt(jnp.mean(x * x, axis=-1, keepdims=True) + EPS)


def _modulate(x, g, shift, scale):
    return _rms(x) * g * (1.0 + scale) + shift


def _log_sigmoid(x):
    return jnp.minimum(x, 0.0) - jnp.log1p(jnp.exp(-jnp.abs(x)))


def _scan_lanes(x, op, ident, n):
    lane = lax.broadcasted_iota(jnp.int32, x.shape, 1) % n
    d = 1
    while d < n:
        x = op(x, jnp.where(lane >= d, pltpu.roll(x, d, 1), ident))
        d *= 2
    return x


def _const_spec(shape):
    nd = len(shape)
    return pl.BlockSpec(shape, lambda *_: (0,) * nd, pipeline_mode=pl.Buffered(1))


def _params(*sem):
    return pltpu.CompilerParams(dimension_semantics=sem, vmem_limit_bytes=VMEM_LIMIT_BYTES)


def _ada_kernel(c_ref, w_ref, b_ref, o_ref):
    c = c_ref[...]
    sc = (c * jax.nn.sigmoid(c)).astype(BF)
    o_ref[0] = _dot(sc, w_ref[0].astype(BF)) + b_ref[0]


def _ada(c_all, ada_w, ada_b):
    depth, _, n = ada_w.shape
    nb = c_all.shape[0]
    tn = 1024
    return pl.pallas_call(
        _ada_kernel,
        grid=(depth, n // tn),
        in_specs=[
            pl.BlockSpec((nb, D), lambda i, j: (0, 0)),
            pl.BlockSpec((1, D, tn), lambda i, j: (i, 0, j)),
            pl.BlockSpec((1, 1, tn), lambda i, j: (i, 0, j)),
        ],
        out_specs=pl.BlockSpec((1, nb, tn), lambda i, j: (i, 0, j)),
        out_shape=jax.ShapeDtypeStruct((depth, nb, n), F32),
        compiler_params=_params("arbitrary", "arbitrary"),
        name="ada",
    )(c_all, ada_w, ada_b.reshape(depth, 1, n))


def _mlp_kernel(*refs, pre, final):
    refs = list(refs)
    x_ref = refs.pop(0)
    if pre:
        y_ref = refs.pop(0)
        wo_ref = refs.pop(0)
    mod_ref, g_ref, wu_ref, wd_ref = refs[:4]
    refs = refs[4:]
    if final:
        fg_ref = refs.pop(0)
    o_ref = refs.pop(0)

    x = x_ref[0]
    if pre:
        x = x + mod_ref[0, 2] * _dot(y_ref[0].astype(BF), wo_ref[...])
    hb = _modulate(x, g_ref[...], mod_ref[0, 3], mod_ref[0, 4]).astype(BF)
    acc = jnp.zeros(x.shape, F32)
    for c in range(DFF // FF_CHUNK):
        cs = slice(c * FF_CHUNK, (c + 1) * FF_CHUNK)
        hid = jnp.maximum(_dot(hb, wu_ref[:, cs]), 0.0)
        acc = acc + _dot((hid * hid).astype(BF), wd_ref[cs, :])
    x = x + mod_ref[0, 5] * acc
    if final:
        x = _rms(x) * fg_ref[...]
    o_ref[0] = x


def _mlp(x, mod, g, w_up, w_down, *, tm, pre=None, final_g=None):
    nb, t, _ = x.shape
    r = mod.shape[2]
    tok = pl.BlockSpec((1, tm, D), lambda b, i: (b, i, 0))
    if r == 1:
        mod_spec = pl.BlockSpec((1, N_MOD, 1, D), lambda b, i: (b, 0, 0, 0))
    else:
        mod_spec = pl.BlockSpec((1, N_MOD, tm, D), lambda b, i: (b, 0, i, 0))
    args, specs = [x], [tok]
    if pre is not None:
        y, w_out = pre
        args += [y, w_out]
        specs += [tok, _const_spec((D, D))]
    args += [mod, g.reshape(1, D), w_up, w_down]
    specs += [mod_spec, _const_spec((1, D)), _const_spec((D, DFF)), _const_spec((DFF, D))]
    if final_g is not None:
        args.append(final_g.reshape(1, D))
        specs.append(_const_spec((1, D)))
    return pl.pallas_call(
        functools.partial(_mlp_kernel, pre=pre is not None, final=final_g is not None),
        grid=(nb, t // tm),
        in_specs=specs,
        out_specs=tok,
        out_shape=jax.ShapeDtypeStruct(x.shape, F32),
        compiler_params=_params("arbitrary", "arbitrary"),
        name="mlp",
    )(*args)


def _mlstm_prompt_kernel(x_ref, mod_ref, g_ref, wrow_ref, wcol_ref, bg_ref, gh_ref, wout_ref,
                         xo_ref, s_ref, m_ref, y_scr, *, tm):
    @pl.when(pl.program_id(1) == 0)
    def _():
        s_ref[...] = jnp.zeros(s_ref.shape, F32)
        m_ref[...] = jnp.zeros(m_ref.shape, F32)

    x = x_ref[0]
    hb = _modulate(x, g_ref[...], mod_ref[0, 0], mod_ref[0, 1]).astype(BF)
    n = CHUNK
    gates = _dot_nt(wcol_ref[HK:HK + 2 * H, :], hb) + bg_ref[...]
    b_all = _scan_lanes(_log_sigmoid(gates[H:2 * H]), jnp.add, 0.0, n)
    r_all = gates[0:H] - b_all
    cm_all = _scan_lanes(r_all, jnp.maximum, -jnp.inf, n)
    prow = _dot(hb, wrow_ref[...])
    pcol = _dot_nt(wcol_ref[0:HK, :], hb)
    causal = (lax.broadcasted_iota(jnp.int32, (n, n), 0)
              >= lax.broadcasted_iota(jnp.int32, (n, n), 1))
    zeros_k = jnp.zeros((DK, n), BF)
    zeros_s = jnp.zeros((DK, 2 * DV), BF)
    zeros_v = jnp.zeros((n, DV), BF)
    ones_v = jnp.ones((n, DV), BF)
    pad = jnp.zeros((n - 2 * H, n), F32)

    for c in range(tm // n):
        ts = slice(c * n, (c + 1) * n)
        r = r_all[:, ts]
        m0 = m_ref[0]
        mx = jnp.maximum(m0, cm_all[:, ts])
        m = b_all[:, ts] + mx
        e_neg = jnp.exp(-m)
        mx_last = jnp.broadcast_to(mx[:, n - 1:n], (H, n))
        a = jnp.exp(r - mx_last)
        decay = jnp.exp(m0 - mx_last)
        m_ref[0] = jnp.broadcast_to(m[:, n - 1:n], (H, n))
        cols = jnp.concatenate([mx, e_neg, pad], axis=0).T

        for h in range(H):
            lo = h % 2 == 0
            mx_b = jnp.broadcast_to(cols[:, h:h + 1], (n, n))
            floor_b = jnp.broadcast_to(cols[:, H + h:H + h + 1], (n, DV))
            dec = jnp.where(causal, jnp.exp(jnp.minimum(r[h:h + 1, :] - mx_b, 0.0)), 0.0)
            e_int = jnp.exp(m0[h:h + 1, :] - mx_b)
            qp = prow[ts, (h // 2) * 2 * DK:(h // 2 + 1) * 2 * DK]
            kt = pcol[h * DK:(h + 1) * DK, ts]
            ktb = kt.astype(BF)
            s = _dot(qp.astype(BF),
                     jnp.concatenate([ktb, zeros_k] if lo else [zeros_k, ktb], axis=0)) * dec
            st = s_ref[0, h]
            stb = st.astype(BF)
            v = prow[ts, HK + h * DV:HK + (h + 1) * DV].astype(BF)
            rhs = jnp.concatenate(
                [jnp.concatenate([v, zeros_v], axis=1)]
                + ([stb, zeros_s] if lo else [zeros_s, stb]), axis=0)
            lhs = jnp.concatenate([s.astype(BF), (qp * e_int).astype(BF)], axis=1)
            res = _dot(lhs, rhs)
            den = jnp.sum(s, axis=-1, keepdims=True) + res[:, DV:]
            hh = res[:, :DV] / jnp.maximum(jnp.abs(den), floor_b)
            hh = hh * lax.rsqrt(jnp.mean(hh * hh, axis=-1, keepdims=True) + EPS)
            y_scr[ts, h * DV:(h + 1) * DV] = hh
            ka = (kt * a[h:h + 1, :]).astype(BF)
            dc = decay[h:h + 1, :]
            s_ref[0, h] = (jnp.concatenate([dc, dc], axis=1) * st
                           + _dot(ka, jnp.concatenate([v, ones_v], axis=1)))

    o = prow[:, HK + D:]
    y = (y_scr[...] * gh_ref[...] * jax.nn.sigmoid(o)).astype(BF)
    xo_ref[0] = x + mod_ref[0, 2] * _dot(y, wout_ref[...])


def _mlstm_prompt(x, mod, g, w_row, w_col, b_g, g_head, w_out, *, tm):
    nb, t, _ = x.shape
    tok = pl.BlockSpec((1, tm, D), lambda b, i: (b, i, 0))
    return pl.pallas_call(
        functools.partial(_mlstm_prompt_kernel, tm=tm),
        grid=(nb, t // tm),
        in_specs=[
            tok,
            pl.BlockSpec((1, N_MOD, 1, D), lambda b, i: (b, 0, 0, 0)),
            _const_spec((1, D)),
            _const_spec((D, HK + 2 * D)),
            _const_spec((HK + 2 * H, D)),
            _const_spec((2 * H, tm)),
            _const_spec((1, D)),
            _const_spec((D, D)),
        ],
        out_specs=[
            tok,
            pl.BlockSpec((1, H, DK, 2 * DV), lambda b, i: (b, 0, 0, 0)),
            pl.BlockSpec((1, H, CHUNK), lambda b, i: (b, 0, 0)),
        ],
        out_shape=[
            jax.ShapeDtypeStruct(x.shape, F32),
            jax.ShapeDtypeStruct((nb, H, DK, 2 * DV), F32),
            jax.ShapeDtypeStruct((nb, H, CHUNK), F32),
        ],
        scratch_shapes=[pltpu.VMEM((tm, D), F32)],
        compiler_params=_params("arbitrary", "arbitrary"),
        name="mlstm_prompt",
    )(x, mod, g.reshape(1, D), w_row, w_col, jnp.broadcast_to(b_g.reshape(2 * H, 1), (2 * H, tm)),
      g_head.reshape(1, D), w_out)


def _proj_kernel(x_ref, mod_ref, g_ref, win_ref, wg_ref, bg_ref, p_ref):
    hb = _modulate(x_ref[0], g_ref[...], mod_ref[0, 0], mod_ref[0, 1]).astype(BF)
    p_ref[:, :NP] = _dot(hb, win_ref[...])
    p_ref[:, NP:] = _dot(hb, wg_ref[...]) + bg_ref[...]


def _proj(x, mod, g, w_in, w_g, b_g):
    _, t, _ = x.shape
    return pl.pallas_call(
        _proj_kernel,
        grid=(1,),
        in_specs=[
            pl.BlockSpec((1, t, D), lambda i: (0, 0, 0)),
            pl.BlockSpec((1, N_MOD, t, D), lambda i: (0, 0, 0, 0)),
            _const_spec((1, D)),
            _const_spec((D, NP)),
            _const_spec((D, GW)),
            _const_spec((1, GW)),
        ],
        out_specs=pl.BlockSpec((t, NP + GW), lambda i: (0, 0)),
        out_shape=jax.ShapeDtypeStruct((t, NP + GW), F32),
        compiler_params=_params("arbitrary"),
        name="mlstm_proj",
    )(x, mod, g.reshape(1, D), w_in, w_g, b_g)


def _mlstm_sample_kernel(p_ref, c0_ref, n0_ref, m0_ref, gh_ref,
                         y_ref, c1_ref, n1_ref, m1_ref, *, bb, t):
    trow = lax.broadcasted_iota(jnp.int32, (t, 1), 0)

    def one_sequence(j, carry):
        pj = p_ref[j]
        ig = pj[:, NP:NP + H]
        lf = _log_sigmoid(pj[:, NP + FG:NP + FG + H])
        b_rows = [lf[0:1]]
        for i in range(1, t):
            b_rows.append(b_rows[-1] + lf[i:i + 1])
        b = jnp.concatenate(b_rows, axis=0)
        r = ig - b
        cm_rows = [r[0:1]]
        for i in range(1, t):
            cm_rows.append(jnp.maximum(cm_rows[-1], r[i:i + 1]))
        m0 = m0_ref[pl.ds(j, 1), :]
        mx = jnp.maximum(m0, jnp.concatenate(cm_rows, axis=0))
        e_int = jnp.exp(m0 - mx)
        m = b + mx
        e_neg = jnp.exp(-m)
        mx_last = mx[t - 1:t]
        a = jnp.exp(r - mx_last)
        decay = jnp.exp(m0 - mx_last)
        m1_ref[pl.ds(j, 1), :] = m[t - 1:t]
        n0_all = n0_ref[j]

        n1_rows, y_parts, c1_parts = [], [], []
        for h in range(H):
            q = pj[:, h * DK:(h + 1) * DK]
            k = pj[:, HK + h * DK:HK + (h + 1) * DK] * (DK ** -0.5)
            v = pj[:, 2 * HK + h * DV:2 * HK + (h + 1) * DV]
            o = pj[:, 2 * HK + D + h * DV:2 * HK + D + (h + 1) * DV]
            c0 = c0_ref[j, h]
            n0 = n0_all[h:h + 1]
            ei = e_int[:, h:h + 1]
            mxh = mx[:, h:h + 1]
            num = ei * _dot(q.astype(BF), c0.astype(BF))
            den = ei * jnp.sum(q * n0, axis=-1, keepdims=True)
            for s in range(t):
                qk = jnp.sum(q * k[s:s + 1], axis=-1, keepdims=True)
                dec = jnp.where(trow >= s,
                                jnp.exp(jnp.minimum(r[s:s + 1, h:h + 1] - mxh, 0.0)), 0.0)
                w = qk * dec
                num = num + w * v[s:s + 1]
                den = den + w
            hh = num / jnp.maximum(jnp.abs(den), e_neg[:, h:h + 1])
            hh = hh * lax.rsqrt(jnp.mean(hh * hh, axis=-1, keepdims=True) + EPS)
            y_parts.append(hh * gh_ref[:, h * DV:(h + 1) * DV] * jax.nn.sigmoid(o))
            ah = a[:, h:h + 1]
            ak = k * ah
            c1_parts.append(decay[:, h:h + 1] * c0 + _dot_tn(ak.astype(BF), v.astype(BF)))
            n1_rows.append(decay[:, h:h + 1] * n0 + jnp.sum(ak, axis=0, keepdims=True))
        y_ref[j] = jnp.concatenate(y_parts, axis=1)
        for h in range(H):
            c1_ref[j, h] = c1_parts[h]
        n1_ref[j] = jnp.concatenate(n1_rows, axis=0)
        return carry

    lax.fori_loop(0, bb, one_sequence, 0)


def _mlstm_sample(p, c0, n0, m0, g_head, *, bb):
    nb, t, w = p.shape
    return pl.pallas_call(
        functools.partial(_mlstm_sample_kernel, bb=bb, t=t),
        grid=(nb // bb,),
        in_specs=[
            pl.BlockSpec((bb, t, w), lambda i: (i, 0, 0)),
            pl.BlockSpec((bb, H, DK, DV), lambda i: (i, 0, 0, 0)),
            pl.BlockSpec((bb, H, DK), lambda i: (i, 0, 0)),
            pl.BlockSpec((bb, H), lambda i: (i, 0)),
            _const_spec((1, D)),
        ],
        out_specs=[
            pl.BlockSpec((bb, t, D), lambda i: (i, 0, 0)),
            pl.BlockSpec((bb, H, DK, DV), lambda i: (i, 0, 0, 0)),
            pl.BlockSpec((bb, H, DK), lambda i: (i, 0, 0)),
            pl.BlockSpec((bb, H), lambda i: (i, 0)),
        ],
        out_shape=[
            jax.ShapeDtypeStruct((nb, t, D), F32),
            jax.ShapeDtypeStruct(c0.shape, F32),
            jax.ShapeDtypeStruct(n0.shape, F32),
            jax.ShapeDtypeStruct(m0.shape, F32),
        ],
        compiler_params=_params("arbitrary"),
        name="mlstm_sample",
    )(p, c0, n0, m0, g_head.reshape(1, D))


def _conv_taps(u, prev2, prev1, wc_ref):
    return wc_ref[0:1] * prev2 + wc_ref[1:2] * prev1 + wc_ref[2:3] * u


def _conv_prompt_kernel(x_ref, mod_ref, g_ref, win_ref, wc_ref, wout_ref,
                        xo_ref, st_ref, carry_scr):
    @pl.when(pl.program_id(1) == 0)
    def _():
        carry_scr[...] = jnp.zeros(carry_scr.shape, F32)

    x = x_ref[0]
    tm = x.shape[0]
    hb = _modulate(x, g_ref[...], mod_ref[0, 0], mod_ref[0, 1]).astype(BF)
    u = _dot(hb, win_ref[:, D:2 * D]) * _dot(hb, win_ref[:, 2 * D:])
    row = lax.broadcasted_iota(jnp.int32, (tm, D), 0)
    c0 = carry_scr[0:1]
    c1 = carry_scr[1:2]
    prev1 = jnp.where(row == 0, c1, pltpu.roll(u, 1, 0))
    prev2 = jnp.where(row == 0, c0, jnp.where(row == 1, c1, pltpu.roll(u, 2, 0)))
    y = _conv_taps(u, prev2, prev1, wc_ref)
    bg = _dot(hb, win_ref[:, :D])
    xo_ref[0] = x + mod_ref[0, 2] * _dot((bg * y).astype(BF), wout_ref[...])
    carry_scr[...] = u[tm - 2:tm]
    st_ref[0] = u[tm - 2:tm]


def _conv_prompt(x, mod, g, w_in, w_conv, w_out, *, tm):
    nb, t, _ = x.shape
    tok = pl.BlockSpec((1, tm, D), lambda b, i: (b, i, 0))
    return pl.pallas_call(
        _conv_prompt_kernel,
        grid=(nb, t // tm),
        in_specs=[
            tok,
            pl.BlockSpec((1, N_MOD, 1, D), lambda b, i: (b, 0, 0, 0)),
            _const_spec((1, D)),
            _const_spec((D, 3 * D)),
            _const_spec((CONV_W, D)),
            _const_spec((D, D)),
        ],
        out_specs=[tok, pl.BlockSpec((1, CONV_W - 1, D), lambda b, i: (b, 0, 0))],
        out_shape=[jax.ShapeDtypeStruct(x.shape, F32),
                   jax.ShapeDtypeStruct((nb, CONV_W - 1, D), F32)],
        scratch_shapes=[pltpu.VMEM((CONV_W - 1, D), F32)],
        compiler_params=_params("arbitrary", "arbitrary"),
        name="conv_prompt",
    )(x, mod, g.reshape(1, D), w_in, w_conv, w_out)


def _conv_sample_kernel(x_ref, mod_ref, g_ref, win_ref, wc_ref, wout_ref, h2_ref, h1_ref,
                        xo_ref, u_ref, *, t):
    x = x_ref[0]
    rows = x.shape[0]
    hb = _modulate(x, g_ref[...], mod_ref[0, 0], mod_ref[0, 1]).astype(BF)
    u = _dot(hb, win_ref[:, D:2 * D]) * _dot(hb, win_ref[:, 2 * D:])
    pos = lax.broadcasted_iota(jnp.int32, (rows, D), 0) % t
    prev1 = jnp.where(pos < 1, h1_ref[...], pltpu.roll(u, 1, 0))
    prev2 = jnp.where(pos < 2, h2_ref[...], pltpu.roll(u, 2, 0))
    y = _conv_taps(u, prev2, prev1, wc_ref)
    bg = _dot(hb, win_ref[:, :D])
    xo_ref[0] = x + mod_ref[0, 2] * _dot((bg * y).astype(BF), wout_ref[...])
    u_ref[...] = u


def _conv_sample(x, mod, g, w_in, w_conv, w_out, halo2, halo1, *, t):
    _, rows, _ = x.shape
    full = pl.BlockSpec((1, rows, D), lambda i: (0, 0, 0))
    flat = pl.BlockSpec((rows, D), lambda i: (0, 0))
    return pl.pallas_call(
        functools.partial(_conv_sample_kernel, t=t),
        grid=(1,),
        in_specs=[
            full,
            pl.BlockSpec((1, N_MOD, rows, D), lambda i: (0, 0, 0, 0)),
            _const_spec((1, D)),
            _const_spec((D, 3 * D)),
            _const_spec((CONV_W, D)),
            _const_spec((D, D)),
            flat,
            flat,
        ],
        out_specs=[full, flat],
        out_shape=[jax.ShapeDtypeStruct(x.shape, F32), jax.ShapeDtypeStruct((rows, D), F32)],
        compiler_params=_params("arbitrary"),
        name="conv_sample",
    )(x, mod, g.reshape(1, D), w_in, w_conv, w_out, halo2, halo1)


def kernel(x_prompt, x_sample, c_prompt, c_sample, state_mlstm_C, state_mlstm_n, state_mlstm_m,
           state_conv, ada_w, ada_b, norm_g, final_norm_g, mlstm_w_in, mlstm_b_gates,
           mlstm_head_g, mlstm_w_out, conv_w_in, conv_w, conv_w_out, mlp_w_up, mlp_w_down):
    assert ada_w.shape[0] == 2 and mlstm_w_in.shape[0] == 1 and conv_w_in.shape[0] == 1
    bp, tp, _ = x_prompt.shape
    bs, ts, _ = x_sample.shape
    assert ts == 4 and tp % 512 == 0

    mod = _ada(jnp.concatenate([c_prompt, c_sample], axis=0), ada_w, ada_b)
    mod = mod.reshape(2, bp + bs, N_MOD, D)
    mod_p = mod[:, :bp].reshape(2, bp, N_MOD, 1, D)
    mod_s = jnp.repeat(jnp.transpose(mod[:, bp:], (0, 2, 1, 3)), ts, axis=2)[:, None]

    w_in0 = mlstm_w_in[0]
    win = w_in0[:, :NP].astype(BF)
    w_g = jnp.zeros((D, GW), F32)
    w_g = w_g.at[:, 0:H].set(w_in0[:, NP:NP + H]).at[:, FG:FG + H].set(w_in0[:, NP + H:])
    w_g = w_g.astype(BF)
    b_g = jnp.zeros((1, GW), F32)
    b_g = b_g.at[0, 0:H].set(mlstm_b_gates[0, :H]).at[0, FG:FG + H].set(mlstm_b_gates[0, H:])
    w_row = jnp.concatenate([w_in0[:, :HK], w_in0[:, 2 * HK:NP]], axis=1).astype(BF)
    w_col = jnp.concatenate([w_in0[:, HK:2 * HK] * (DK ** -0.5), w_in0[:, NP:]], axis=1).T.astype(BF)
    wout0 = mlstm_w_out[0].astype(BF)
    cwin = conv_w_in[0].astype(BF)
    cwout = conv_w_out[0].astype(BF)
    up = mlp_w_up.astype(BF)
    down = mlp_w_down.astype(BF)

    x, s_p, m_p = _mlstm_prompt(x_prompt, mod_p[0], norm_g[0, 0], w_row, w_col, mlstm_b_gates[0],
                                mlstm_head_g[0], wout0, tm=256)
    x = _mlp(x, mod_p[0], norm_g[0, 1], up[0], down[0], tm=512)
    x, conv_p = _conv_prompt(x, mod_p[1], norm_g[1, 0], cwin, conv_w[0], cwout, tm=512)
    y_prompt = _mlp(x, mod_p[1], norm_g[1, 1], up[1], down[1], tm=512, final_g=final_norm_g)
    prompt_c = s_p[None, :, :, :, :DV]
    prompt_n = s_p[None, :, :, :, DV]
    prompt_m = m_p[None, :, :, 0]
    prompt_conv = conv_p[None]

    xs = x_sample.reshape(1, bs * ts, D)
    p = _proj(xs, mod_s[0], norm_g[0, 0], win, w_g, b_g).reshape(bs, ts, NP + GW)
    y, c1, n1, m1 = _mlstm_sample(p, state_mlstm_C[0], state_mlstm_n[0], state_mlstm_m[0],
                                  mlstm_head_g[0], bb=8)
    xs = _mlp(xs, mod_s[0], norm_g[0, 1], up[0], down[0], tm=bs * ts,
              pre=(y.reshape(1, bs * ts, D), wout0))
    buf = state_conv[0]
    zeros = jnp.zeros((bs, ts - 2, D), F32)
    halo2 = jnp.concatenate([buf, zeros], axis=1).reshape(bs * ts, D)
    halo1 = jnp.concatenate([buf[:, 1:], zeros, zeros[:, :1]], axis=1).reshape(bs * ts, D)
    xs, u = _conv_sample(xs, mod_s[1], norm_g[1, 0], cwin, conv_w[0], cwout, halo2, halo1, t=ts)
    y_sample = _mlp(xs, mod_s[1], norm_g[1, 1], up[1], down[1], tm=bs * ts,
                    final_g=final_norm_g).reshape(bs, ts, D)
    sample_conv = u.reshape(bs, ts, D)[None, :, ts - 2:]

    return (y_prompt, y_sample, prompt_c, prompt_n, prompt_m, prompt_conv,
            c1[None], n1[None], m1[None], sample_conv)
```

```python
import functools

import jax
import jax.numpy as jnp
from jax import lax
from jax.experimental import pallas as pl
from jax.experimental.pallas import tpu as pltpu

D = 1024
H = 8
DK = 64
DV = 128
HK = H * DK
NP = 2 * HK + 2 * D
GW = 256
FG = 128
LANES = 128
SUBLANES = 8
DFF = 4 * D
N_MOD = 6
CONV_W = 3
EPS = 1e-6
CHUNK = 128
FF_CHUNK = 1024
BF = jnp.bfloat16
F32 = jnp.float32
VMEM_LIMIT_BYTES = 56 * 1024 * 1024


def _dot(a, b):
    return jnp.dot(a, b, preferred_element_type=F32)


def _dot_nt(a, b):
    return lax.dot_general(a, b, (((1,), (1,)), ((), ())), preferred_element_type=F32)


def _dot_tn(a, b):
    return lax.dot_general(a, b, (((0,), (0,)), ((), ())), preferred_element_type=F32)


def _rms(x):
    return x * lax.rsqrt(jnp.mean(x * x, axis=-1, keepdims=True) + EPS)


def _modulate(x, g, shift, scale):
    return _rms(x) * g * (1.0 + scale) + shift


def _log_sigmoid(x):
    return jnp.minimum(x, 0.0) - jnp.log1p(jnp.exp(-jnp.abs(x)))


def _scan_lanes(x, op, ident, n):
    lane = lax.broadcasted_iota(jnp.int32, x.shape, 1) % n
    d = 1
    while d < n:
        x = op(x, jnp.where(lane >= d, pltpu.roll(x, d, 1), ident))
        d *= 2
    return x


def _by_group(fn, xs, mods):
    rows, r = xs[0].shape[0], mods[0].shape[0]
    if r == 1 or r == rows:
        return fn(*xs, *mods)
    return jnp.concatenate(
        [fn(*[x[i:i + r] for x in xs], *mods) for i in range(0, rows, r)], axis=0)


def _split_dot(x, m_ref):
    hi = x.astype(BF)
    lo = (x - hi.astype(F32)).astype(BF)
    m = m_ref[...]
    return _dot(hi, m) + _dot(lo, m)


def _const_spec(shape):
    nd = len(shape)
    return pl.BlockSpec(shape, lambda *_: (0,) * nd, pipeline_mode=pl.Buffered(1))


def _params(*sem):
    return pltpu.CompilerParams(dimension_semantics=sem, vmem_limit_bytes=VMEM_LIMIT_BYTES)


def _ada_kernel(c_ref, w_ref, b_ref, o_ref):
    c = c_ref[...]
    sc = (c * jax.nn.sigmoid(c)).astype(BF)
    o_ref[0] = _dot(sc, w_ref[0].astype(BF)) + b_ref[0]


def _ada(c_all, ada_w, ada_b):
    depth, _, n = ada_w.shape
    nb = c_all.shape[0]
    tn = 1024
    return pl.pallas_call(
        _ada_kernel,
        grid=(depth, n // tn),
        in_specs=[
            pl.BlockSpec((nb, D), lambda i, j: (0, 0)),
            pl.BlockSpec((1, D, tn), lambda i, j: (i, 0, j)),
            pl.BlockSpec((1, 1, tn), lambda i, j: (i, 0, j)),
        ],
        out_specs=pl.BlockSpec((1, nb, tn), lambda i, j: (i, 0, j)),
        out_shape=jax.ShapeDtypeStruct((depth, nb, n), F32),
        compiler_params=_params("arbitrary", "arbitrary"),
        name="ada",
    )(c_all, ada_w, ada_b.reshape(depth, 1, n))


def _mlp_kernel(*refs, pre, final):
    refs = list(refs)
    x_ref = refs.pop(0)
    if pre:
        y_ref = refs.pop(0)
        wo_ref = refs.pop(0)
    mod_ref, g_ref, wu_ref, wd_ref = refs[:4]
    refs = refs[4:]
    if final:
        fg_ref = refs.pop(0)
    o_ref = refs.pop(0)

    def residual(xr, fr, gate):
        return xr + gate * fr

    def norm(xr, shift, scale):
        return _modulate(xr, g_ref[...], shift, scale)

    x = x_ref[0]
    if pre:
        x = _by_group(residual, (x, _dot(y_ref[0].astype(BF), wo_ref[...])), (mod_ref[0, 2],))
    hb = _by_group(norm, (x,), (mod_ref[0, 3], mod_ref[0, 4])).astype(BF)
    acc = jnp.zeros(x.shape, F32)
    for c in range(DFF // FF_CHUNK):
        cs = slice(c * FF_CHUNK, (c + 1) * FF_CHUNK)
        hid = jnp.maximum(_dot(hb, wu_ref[:, cs]), 0.0)
        acc = acc + _dot((hid * hid).astype(BF), wd_ref[cs, :])
    x = _by_group(residual, (x, acc), (mod_ref[0, 5],))
    if final:
        x = _rms(x) * fg_ref[...]
    o_ref[0] = x


def _mlp(x, mod, g, w_up, w_down, *, tm, pre=None, final_g=None):
    nb, t, _ = x.shape
    r = mod.shape[2]
    tok = pl.BlockSpec((1, tm, D), lambda b, i: (b, i, 0))
    assert r == 1 or (tm == t and tm % r == 0)
    mod_spec = pl.BlockSpec((1, N_MOD, r, D), lambda b, i: (b, 0, 0, 0))
    args, specs = [x], [tok]
    if pre is not None:
        y, w_out = pre
        args += [y, w_out]
        specs += [tok, _const_spec((D, D))]
    args += [mod, g.reshape(1, D), w_up, w_down]
    specs += [mod_spec, _const_spec((1, D)), _const_spec((D, DFF)), _const_spec((DFF, D))]
    if final_g is not None:
        args.append(final_g.reshape(1, D))
        specs.append(_const_spec((1, D)))
    return pl.pallas_call(
        functools.partial(_mlp_kernel, pre=pre is not None, final=final_g is not None),
        grid=(nb, t // tm),
        in_specs=specs,
        out_specs=tok,
        out_shape=jax.ShapeDtypeStruct(x.shape, F32),
        compiler_params=_params("arbitrary", "arbitrary"),
        name="mlp",
    )(*args)


def _mlstm_prompt_kernel(x_ref, mod_ref, g_ref, wrow_ref, wcol_ref, bg_ref, gh_ref, wout_ref,
                         xo_ref, s_ref, m_ref, y_scr, *, tm):
    @pl.when(pl.program_id(1) == 0)
    def _():
        s_ref[...] = jnp.zeros(s_ref.shape, F32)
        m_ref[...] = jnp.zeros(m_ref.shape, F32)

    x = x_ref[0]
    hb = _modulate(x, g_ref[...], mod_ref[0, 0], mod_ref[0, 1]).astype(BF)
    n = CHUNK
    gates = _dot_nt(wcol_ref[HK:HK + 2 * H, :], hb) + bg_ref[...]
    b_all = _scan_lanes(_log_sigmoid(gates[H:2 * H]), jnp.add, 0.0, n)
    r_all = gates[0:H] - b_all
    cm_all = _scan_lanes(r_all, jnp.maximum, -jnp.inf, n)
    prow = _dot(hb, wrow_ref[...])
    pcol = _dot_nt(wcol_ref[0:HK, :], hb)
    causal = (lax.broadcasted_iota(jnp.int32, (n, n), 0)
              >= lax.broadcasted_iota(jnp.int32, (n, n), 1))
    zeros_k = jnp.zeros((DK, n), BF)
    zeros_s = jnp.zeros((DK, 2 * DV), BF)
    zeros_v = jnp.zeros((n, DV), BF)
    ones_v = jnp.ones((n, DV), BF)
    pad = jnp.zeros((n - 2 * H, n), F32)

    for c in range(tm // n):
        ts = slice(c * n, (c + 1) * n)
        r = r_all[:, ts]
        m0 = m_ref[0]
        mx = jnp.maximum(m0, cm_all[:, ts])
        m = b_all[:, ts] + mx
        e_neg = jnp.exp(-m)
        mx_last = jnp.broadcast_to(mx[:, n - 1:n], (H, n))
        a = jnp.exp(r - mx_last)
        decay = jnp.exp(m0 - mx_last)
        m_ref[0] = jnp.broadcast_to(m[:, n - 1:n], (H, n))
        cols = jnp.concatenate([mx, e_neg, pad], axis=0).T

        for h in range(H):
            lo = h % 2 == 0
            mx_b = jnp.broadcast_to(cols[:, h:h + 1], (n, n))
            floor_b = jnp.broadcast_to(cols[:, H + h:H + h + 1], (n, DV))
            dec = jnp.where(causal, jnp.exp(jnp.minimum(r[h:h + 1, :] - mx_b, 0.0)), 0.0)
            e_int = jnp.exp(m0[h:h + 1, :] - mx_b)
            qp = prow[ts, (h // 2) * 2 * DK:(h // 2 + 1) * 2 * DK]
            kt = pcol[h * DK:(h + 1) * DK, ts]
            ktb = kt.astype(BF)
            s = _dot(qp.astype(BF),
                     jnp.concatenate([ktb, zeros_k] if lo else [zeros_k, ktb], axis=0)) * dec
            st = s_ref[0, h]
            stb = st.astype(BF)
            v = prow[ts, HK + h * DV:HK + (h + 1) * DV].astype(BF)
            rhs = jnp.concatenate(
                [jnp.concatenate([v, zeros_v], axis=1)]
                + ([stb, zeros_s] if lo else [zeros_s, stb]), axis=0)
            lhs = jnp.concatenate([s.astype(BF), (qp * e_int).astype(BF)], axis=1)
            res = _dot(lhs, rhs)
            den = jnp.sum(s, axis=-1, keepdims=True) + res[:, DV:]
            hh = res[:, :DV] / jnp.maximum(jnp.abs(den), floor_b)
            hh = hh * lax.rsqrt(jnp.mean(hh * hh, axis=-1, keepdims=True) + EPS)
            y_scr[ts, h * DV:(h + 1) * DV] = hh
            ka = (kt * a[h:h + 1, :]).astype(BF)
            dc = decay[h:h + 1, :]
            s_ref[0, h] = (jnp.concatenate([dc, dc], axis=1) * st
                           + _dot(ka, jnp.concatenate([v, ones_v], axis=1)))

    o = prow[:, HK + D:]
    y = (y_scr[...] * gh_ref[...] * jax.nn.sigmoid(o)).astype(BF)
    xo_ref[0] = x + mod_ref[0, 2] * _dot(y, wout_ref[...])


def _mlstm_prompt(x, mod, g, w_row, w_col, b_g, g_head, w_out, *, tm):
    nb, t, _ = x.shape
    tok = pl.BlockSpec((1, tm, D), lambda b, i: (b, i, 0))
    return pl.pallas_call(
        functools.partial(_mlstm_prompt_kernel, tm=tm),
        grid=(nb, t // tm),
        in_specs=[
            tok,
            pl.BlockSpec((1, N_MOD, 1, D), lambda b, i: (b, 0, 0, 0)),
            _const_spec((1, D)),
            _const_spec((D, HK + 2 * D)),
            _const_spec((HK + 2 * H, D)),
            _const_spec((2 * H, tm)),
            _const_spec((1, D)),
            _const_spec((D, D)),
        ],
        out_specs=[
            tok,
            pl.BlockSpec((1, H, DK, 2 * DV), lambda b, i: (b, 0, 0, 0)),
            pl.BlockSpec((1, H, CHUNK), lambda b, i: (b, 0, 0)),
        ],
        out_shape=[
            jax.ShapeDtypeStruct(x.shape, F32),
            jax.ShapeDtypeStruct((nb, H, DK, 2 * DV), F32),
            jax.ShapeDtypeStruct((nb, H, CHUNK), F32),
        ],
        scratch_shapes=[pltpu.VMEM((tm, D), F32)],
        compiler_params=_params("arbitrary", "arbitrary"),
        name="mlstm_prompt",
    )(x, mod, g.reshape(1, D), w_row, w_col, jnp.broadcast_to(b_g.reshape(2 * H, 1), (2 * H, tm)),
      g_head.reshape(1, D), w_out)


def _mlstm_sample_pre_kernel(x_ref, mod_ref, g_ref, w_ref, bg_ref, m0_ref, n0_ref, gh_ref,
                             seg_ref, e64_ref, e128_ref,
                             q_ref, ka_ref, v_ref, dec_ref, numi_ref, rd_ref, eint_ref, go_ref,
                             n1_ref, m1_ref, *, t, nb):
    def norm(xr, shift, scale):
        return _modulate(xr, g_ref[...], shift, scale)

    hb = _by_group(norm, (x_ref[0],), (mod_ref[0, 0], mod_ref[0, 1])).astype(BF)
    p = _dot(hb, w_ref[...])

    def rows(i):
        return slice(i * nb, (i + 1) * nb)

    q = [p[rows(i), 0:HK] for i in range(t)]
    k = [p[rows(i), HK:2 * HK] for i in range(t)]
    v = [p[rows(i), 2 * HK:2 * HK + D] for i in range(t)]
    m0 = m0_ref[...]
    n0 = n0_ref[...]

    b, r, mx = [], [], []
    for i in range(t):
        lf = _log_sigmoid(p[rows(i), NP + FG:NP + GW] + bg_ref[:, FG:GW])
        b.append(lf if i == 0 else b[-1] + lf)
        r.append(p[rows(i), NP:NP + FG] + bg_ref[:, 0:FG] - b[i])
        cm = r[i] if i == 0 else jnp.maximum(cm, r[i])
        mx.append(jnp.maximum(m0, cm))
    mx_last = mx[t - 1]
    decay = jnp.exp(m0 - mx_last)
    m1_ref[...] = b[t - 1] + mx_last
    dec_x = _split_dot(decay, e128_ref)
    n1 = _split_dot(decay, e64_ref) * n0

    for i in range(t):
        ka = k[i] * _split_dot(jnp.exp(r[i] - mx_last), e64_ref)
        n1 = n1 + ka
        seq_rows = pl.ds(i, nb, stride=t)
        for c in range(HK // LANES):
            cs = slice(c * LANES, (c + 1) * LANES)
            ka_ref[c, seq_rows, :] = ka[:, cs]
            q_ref[c, seq_rows, :] = q[i][:, cs]
        for c in range(D // LANES):
            cs = slice(c * LANES, (c + 1) * LANES)
            v_ref[c, seq_rows, :] = v[i][:, cs]
            dec_ref[c, seq_rows, :] = dec_x[:, cs]
        e_int = jnp.exp(m0 - mx[i])
        den = e_int * _split_dot(q[i] * n0, seg_ref)
        numi = jnp.zeros((nb, D), F32)
        for s in range(i + 1):
            w = _split_dot(q[i] * k[s], seg_ref) * jnp.exp(jnp.minimum(r[s] - mx[i], 0.0))
            den = den + w
            numi = numi + _split_dot(w, e128_ref) * v[s]
        numi_ref[rows(i), :] = numi
        rd_ref[rows(i), :] = 1.0 / jnp.maximum(jnp.abs(den), jnp.exp(-(b[i] + mx[i])))
        eint_ref[rows(i), :] = e_int
        go_ref[rows(i), :] = jax.nn.sigmoid(p[rows(i), 2 * HK + D:NP]) * gh_ref[...]
    n1_ref[...] = n1


def _mlstm_sample_pre(x, mod, g, w, b_g, m0, n0, g_head, seg64, e64, e128, *, t):
    rows = x.shape[1]
    nb = rows // t

    def whole(shape):
        return pl.BlockSpec(shape, lambda i: (0,) * len(shape))

    outs = [(HK // LANES, rows, LANES), (HK // LANES, rows, LANES), (D // LANES, rows, LANES),
            (D // LANES, rows, LANES), (rows, D), (rows, FG), (rows, FG), (rows, D), (nb, HK),
            (nb, FG)]
    return pl.pallas_call(
        functools.partial(_mlstm_sample_pre_kernel, t=t, nb=nb),
        grid=(1,),
        in_specs=[
            whole((1, rows, D)),
            whole((1, N_MOD, nb, D)),
            _const_spec((1, D)),
            _const_spec((D, NP + GW)),
            _const_spec((1, GW)),
            whole((nb, FG)),
            whole((nb, HK)),
            _const_spec((1, D)),
            _const_spec((HK, FG)),
            _const_spec((FG, HK)),
            _const_spec((FG, D)),
        ],
        out_specs=[whole(s) for s in outs],
        out_shape=[jax.ShapeDtypeStruct(s, F32) for s in outs],
        compiler_params=_params("arbitrary"),
        name="mlstm_sample_pre",
    )(x, mod, g.reshape(1, D), w, b_g, m0, n0, g_head.reshape(1, D), seg64, e64, e128)


def _mlstm_sample_state_kernel(q_ref, ka_ref, v_ref, dec_ref, c0_ref, inter_ref, c1_ref, *, bb, t):
    per = SUBLANES // t
    seq_of_row = lax.broadcasted_iota(jnp.int32, (SUBLANES, 1), 0) // t
    zeros_c = jnp.zeros((DK, DV), BF)

    def tile(i, carry):
        rs = pl.ds(pl.multiple_of(i * SUBLANES, SUBLANES), SUBLANES)
        for h in range(H):
            lo = h % 2 == 0
            half = slice((h % 2) * DK, (h % 2 + 1) * DK)
            qp = q_ref[h // 2, rs, :].astype(BF)
            kap = ka_ref[h // 2, rs, :]
            vh = v_ref[h, rs, :].astype(BF)
            dech = dec_ref[h, rs, :]
            res = None
            for w in range(per):
                seq = i * per + w
                c0 = c0_ref[seq, h]
                c0b = c0.astype(BF)
                rw = _dot(qp, jnp.concatenate([c0b, zeros_c] if lo else [zeros_c, c0b], axis=0))
                res = rw if res is None else jnp.where(seq_of_row == w, rw, res)
                kaw = jnp.where(seq_of_row == w, kap, 0.0).astype(BF)
                c1_ref[seq, h] = dech[w * t:w * t + 1, :] * c0 + _dot_tn(kaw, vh)[half, :]
            inter_ref[h, rs, :] = res
        return carry

    lax.fori_loop(0, bb // per, tile, 0)


def _mlstm_sample_state(q, ka, v, dec, c0, *, t, bb):
    rows = q.shape[1]
    nb = rows // t
    assert SUBLANES % t == 0 and bb % (SUBLANES // t) == 0 and nb % bb == 0

    def tok(blocks):
        return pl.BlockSpec((blocks, bb * t, LANES), lambda i: (0, i, 0))

    state = pl.BlockSpec((bb, H, DK, DV), lambda i: (i, 0, 0, 0))
    return pl.pallas_call(
        functools.partial(_mlstm_sample_state_kernel, bb=bb, t=t),
        grid=(nb // bb,),
        in_specs=[tok(HK // LANES), tok(HK // LANES), tok(H), tok(H), state],
        out_specs=[tok(H), state],
        out_shape=[jax.ShapeDtypeStruct((H, rows, LANES), F32),
                   jax.ShapeDtypeStruct(c0.shape, F32)],
        compiler_params=_params("arbitrary"),
        name="mlstm_sample_state",
    )(q, ka, v, dec, c0)


def _mlstm_sample_post_kernel(inter_ref, numi_ref, rd_ref, eint_ref, go_ref, e128_ref, seg_ref,
                              y_ref, *, t, nb):
    for i in range(t):
        rs = slice(i * nb, (i + 1) * nb)
        inter = jnp.concatenate(
            [inter_ref[h, pl.ds(i, nb, stride=t), :] for h in range(H)], axis=1)
        num = numi_ref[rs, :] + _split_dot(eint_ref[rs, :], e128_ref) * inter
        hh = num * _split_dot(rd_ref[rs, :], e128_ref)
        ms = _split_dot(hh * hh, seg_ref) * (1.0 / DV)
        y_ref[rs, :] = hh * _split_dot(lax.rsqrt(ms + EPS), e128_ref) * go_ref[rs, :]


def _mlstm_sample_post(inter, numi, rd, eint, go, e128, seg128, *, t):
    rows = inter.shape[1]

    def whole(shape):
        return pl.BlockSpec(shape, lambda i: (0,) * len(shape))

    return pl.pallas_call(
        functools.partial(_mlstm_sample_post_kernel, t=t, nb=rows // t),
        grid=(1,),
        in_specs=[whole((H, rows, LANES)), whole((rows, D)), whole((rows, FG)), whole((rows, FG)),
                  whole((rows, D)), _const_spec((FG, D)), _const_spec((D, FG))],
        out_specs=whole((rows, D)),
        out_shape=jax.ShapeDtypeStruct((rows, D), F32),
        compiler_params=_params("arbitrary"),
        name="mlstm_sample_post",
    )(inter, numi, rd, eint, go, e128, seg128)


def _conv_taps(u, prev2, prev1, wc_ref):
    return wc_ref[0:1] * prev2 + wc_ref[1:2] * prev1 + wc_ref[2:3] * u


def _conv_prompt_kernel(x_ref, mod_ref, g_ref, win_ref, wc_ref, wout_ref,
                        xo_ref, st_ref, carry_scr):
    @pl.when(pl.program_id(1) == 0)
    def _():
        carry_scr[...] = jnp.zeros(carry_scr.shape, F32)

    x = x_ref[0]
    tm = x.shape[0]
    hb = _modulate(x, g_ref[...], mod_ref[0, 0], mod_ref[0, 1]).astype(BF)
    u = _dot(hb, win_ref[:, D:2 * D]) * _dot(hb, win_ref[:, 2 * D:])
    row = lax.broadcasted_iota(jnp.int32, (tm, D), 0)
    c0 = carry_scr[0:1]
    c1 = carry_scr[1:2]
    prev1 = jnp.where(row == 0, c1, pltpu.roll(u, 1, 0))
    prev2 = jnp.where(row == 0, c0, jnp.where(row == 1, c1, pltpu.roll(u, 2, 0)))
    y = _conv_taps(u, prev2, prev1, wc_ref)
    bg = _dot(hb, win_ref[:, :D])
    xo_ref[0] = x + mod_ref[0, 2] * _dot((bg * y).astype(BF), wout_ref[...])
    carry_scr[...] = u[tm - 2:tm]
    st_ref[0] = u[tm - 2:tm]


def _conv_prompt(x, mod, g, w_in, w_conv, w_out, *, tm):
    nb, t, _ = x.shape
    tok = pl.BlockSpec((1, tm, D), lambda b, i: (b, i, 0))
    return pl.pallas_call(
        _conv_prompt_kernel,
        grid=(nb, t // tm),
        in_specs=[
            tok,
            pl.BlockSpec((1, N_MOD, 1, D), lambda b, i: (b, 0, 0, 0)),
            _const_spec((1, D)),
            _const_spec((D, 3 * D)),
            _const_spec((CONV_W, D)),
            _const_spec((D, D)),
        ],
        out_specs=[tok, pl.BlockSpec((1, CONV_W - 1, D), lambda b, i: (b, 0, 0))],
        out_shape=[jax.ShapeDtypeStruct(x.shape, F32),
                   jax.ShapeDtypeStruct((nb, CONV_W - 1, D), F32)],
        scratch_shapes=[pltpu.VMEM((CONV_W - 1, D), F32)],
        compiler_params=_params("arbitrary", "arbitrary"),
        name="conv_prompt",
    )(x, mod, g.reshape(1, D), w_in, w_conv, w_out)


def _conv_sample_kernel(x_ref, mod_ref, g_ref, win_ref, wc_ref, wout_ref, buf_ref,
                        xo_ref, st_ref, *, t, nb):
    def norm(xr, shift, scale):
        return _modulate(xr, g_ref[...], shift, scale)

    def residual(xr, fr, gate):
        return xr + gate * fr

    x = x_ref[0]
    hb = _by_group(norm, (x,), (mod_ref[0, 0], mod_ref[0, 1])).astype(BF)
    u = _dot(hb, win_ref[:, D:2 * D]) * _dot(hb, win_ref[:, 2 * D:])
    full = [buf_ref[j] for j in range(CONV_W - 1)] + [u[i * nb:(i + 1) * nb] for i in range(t)]
    y = jnp.concatenate([_conv_taps(full[i + 2], full[i], full[i + 1], wc_ref) for i in range(t)],
                        axis=0)
    bg = _dot(hb, win_ref[:, :D])
    out = _dot((bg * y).astype(BF), wout_ref[...])
    xo_ref[0] = _by_group(residual, (x, out), (mod_ref[0, 2],))
    for j in range(CONV_W - 1):
        st_ref[j] = full[t + j]


def _conv_sample(x, mod, g, w_in, w_conv, w_out, buf, *, t):
    _, rows, _ = x.shape
    nb = rows // t
    full = pl.BlockSpec((1, rows, D), lambda i: (0, 0, 0))
    state = pl.BlockSpec((CONV_W - 1, nb, D), lambda i: (0, 0, 0))
    return pl.pallas_call(
        functools.partial(_conv_sample_kernel, t=t, nb=nb),
        grid=(1,),
        in_specs=[
            full,
            pl.BlockSpec((1, N_MOD, nb, D), lambda i: (0, 0, 0, 0)),
            _const_spec((1, D)),
            _const_spec((D, 3 * D)),
            _const_spec((CONV_W, D)),
            _const_spec((D, D)),
            state,
        ],
        out_specs=[full, state],
        out_shape=[jax.ShapeDtypeStruct(x.shape, F32),
                   jax.ShapeDtypeStruct((CONV_W - 1, nb, D), F32)],
        compiler_params=_params("arbitrary"),
        name="conv_sample",
    )(x, mod, g.reshape(1, D), w_in, w_conv, w_out, buf)


def kernel(x_prompt, x_sample, c_prompt, c_sample, state_mlstm_C, state_mlstm_n, state_mlstm_m,
           state_conv, ada_w, ada_b, norm_g, final_norm_g, mlstm_w_in, mlstm_b_gates,
           mlstm_head_g, mlstm_w_out, conv_w_in, conv_w, conv_w_out, mlp_w_up, mlp_w_down):
    assert ada_w.shape[0] == 2 and mlstm_w_in.shape[0] == 1 and conv_w_in.shape[0] == 1
    bp, tp, _ = x_prompt.shape
    bs, ts, _ = x_sample.shape
    assert ts == 4 and tp % 512 == 0

    mod = _ada(jnp.concatenate([c_prompt, c_sample], axis=0), ada_w, ada_b)
    mod = mod.reshape(2, bp + bs, N_MOD, D)
    mod_p = mod[:, :bp].reshape(2, bp, N_MOD, 1, D)
    mod_s = jnp.transpose(mod[:, bp:], (0, 2, 1, 3))[:, None]

    w_in0 = mlstm_w_in[0]
    w_k = w_in0[:, HK:2 * HK] * (DK ** -0.5)
    w_g = jnp.zeros((D, GW), F32)
    w_g = w_g.at[:, 0:H].set(w_in0[:, NP:NP + H]).at[:, FG:FG + H].set(w_in0[:, NP + H:])
    b_g = jnp.zeros((1, GW), F32)
    b_g = b_g.at[0, 0:H].set(mlstm_b_gates[0, :H]).at[0, FG:FG + H].set(mlstm_b_gates[0, H:])
    w_all = jnp.concatenate([w_in0[:, :HK], w_k, w_in0[:, 2 * HK:NP], w_g], axis=1).astype(BF)
    w_row = jnp.concatenate([w_in0[:, :HK], w_in0[:, 2 * HK:NP]], axis=1).astype(BF)
    w_col = jnp.concatenate([w_k, w_in0[:, NP:]], axis=1).T.astype(BF)
    wout0 = mlstm_w_out[0].astype(BF)
    cwin = conv_w_in[0].astype(BF)
    cwout = conv_w_out[0].astype(BF)
    up = mlp_w_up.astype(BF)
    down = mlp_w_down.astype(BF)

    x, s_p, m_p = _mlstm_prompt(x_prompt, mod_p[0], norm_g[0, 0], w_row, w_col, mlstm_b_gates[0],
                                mlstm_head_g[0], wout0, tm=256)
    x = _mlp(x, mod_p[0], norm_g[0, 1], up[0], down[0], tm=512)
    x, conv_p = _conv_prompt(x, mod_p[1], norm_g[1, 0], cwin, conv_w[0], cwout, tm=512)
    y_prompt = _mlp(x, mod_p[1], norm_g[1, 1], up[1], down[1], tm=512, final_g=final_norm_g)
    prompt_c = s_p[None, :, :, :, :DV]
    prompt_n = s_p[None, :, :, :, DV]
    prompt_m = m_p[None, :, :, 0]
    prompt_conv = conv_p[None]

    xs = jnp.transpose(x_sample, (1, 0, 2)).reshape(1, ts * bs, D)
    head_of_v = jnp.arange(D, dtype=jnp.int32) // DV
    head_of_k = jnp.arange(HK, dtype=jnp.int32) // DK
    lanes = jnp.arange(FG, dtype=jnp.int32)
    e128 = (lanes[:, None] == head_of_v[None, :]).astype(BF)
    e64 = (lanes[:, None] == head_of_k[None, :]).astype(BF)
    m0 = jnp.pad(state_mlstm_m[0], ((0, 0), (0, FG - H)))
    n0 = state_mlstm_n[0].reshape(bs, HK)
    q, ka, v, dec, numi, rd, eint, go, n1, m1 = _mlstm_sample_pre(
        xs, mod_s[0], norm_g[0, 0], w_all, b_g, m0, n0, mlstm_head_g[0], e64.T, e64, e128, t=ts)
    inter, c1 = _mlstm_sample_state(q, ka, v, dec, state_mlstm_C[0], t=ts, bb=16)
    y = _mlstm_sample_post(inter, numi, rd, eint, go, e128, e128.T, t=ts)
    xs = _mlp(xs, mod_s[0], norm_g[0, 1], up[0], down[0], tm=ts * bs,
              pre=(y.reshape(1, ts * bs, D), wout0))
    xs, conv_s = _conv_sample(xs, mod_s[1], norm_g[1, 0], cwin, conv_w[0], cwout,
                              jnp.transpose(state_conv[0], (1, 0, 2)), t=ts)
    ys = _mlp(xs, mod_s[1], norm_g[1, 1], up[1], down[1], tm=ts * bs, final_g=final_norm_g)
    y_sample = jnp.transpose(ys.reshape(ts, bs, D), (1, 0, 2))
    sample_n = n1.reshape(bs, H, DK)[None]
    sample_m = m1[None, :, :H]
    sample_conv = jnp.transpose(conv_s, (1, 0, 2))[None]

    return (y_prompt, y_sample, prompt_c, prompt_n, prompt_m, prompt_conv,
            c1[None], sample_n, sample_m, sample_conv)
```

```python
import functools

import jax
import jax.numpy as jnp
from jax import lax
from jax.experimental import pallas as pl
from jax.experimental.pallas import tpu as pltpu

D = 1024
H = 8
DK = 64
DV = 128
HK = H * DK
NP = 2 * HK + 2 * D
GW = 256
FG = 128
LANES = 128
SUBLANES = 8
DFF = 4 * D
N_MOD = 6
CONV_W = 3
EPS = 1e-6
CHUNK = 128
FF_CHUNK = 1024
BF = jnp.bfloat16
F32 = jnp.float32
VMEM_LIMIT_BYTES = 56 * 1024 * 1024


def _dot(a, b):
    return jnp.dot(a, b, preferred_element_type=F32)


def _dot_nt(a, b):
    return lax.dot_general(a, b, (((1,), (1,)), ((), ())), preferred_element_type=F32)


def _dot_tn(a, b):
    return lax.dot_general(a, b, (((0,), (0,)), ((), ())), preferred_element_type=F32)


def _rms(x):
    return x * lax.rsqrt(jnp.mean(x * x, axis=-1, keepdims=True) + EPS)


def _modulate(x, g, shift, scale):
    return _rms(x) * g * (1.0 + scale) + shift


def _log_sigmoid(x):
    return jnp.minimum(x, 0.0) - jnp.log1p(jnp.exp(-jnp.abs(x)))


def _scan_lanes(x, op, ident, n):
    lane = lax.broadcasted_iota(jnp.int32, x.shape, 1) % n
    d = 1
    while d < n:
        x = op(x, jnp.where(lane >= d, pltpu.roll(x, d, 1), ident))
        d *= 2
    return x


def _by_group(fn, xs, mods):
    rows, r = xs[0].shape[0], mods[0].shape[0]
    if r == 1 or r == rows:
        return fn(*xs, *mods)
    return jnp.concatenate(
        [fn(*[x[i:i + r] for x in xs], *mods) for i in range(0, rows, r)], axis=0)


def _split_dot(x, m_ref):
    hi = x.astype(BF)
    lo = (x - hi.astype(F32)).astype(BF)
    m = m_ref[...]
    return _dot(hi, m) + _dot(lo, m)


def _const_spec(shape):
    nd = len(shape)
    return pl.BlockSpec(shape, lambda *_: (0,) * nd, pipeline_mode=pl.Buffered(1))


def _params(*sem):
    return pltpu.CompilerParams(dimension_semantics=sem, vmem_limit_bytes=VMEM_LIMIT_BYTES)


def _ada_kernel(c_ref, w_ref, b_ref, o_ref):
    c = c_ref[...]
    sc = (c * jax.nn.sigmoid(c)).astype(BF)
    o_ref[0, 0] = _dot(sc, w_ref[0].astype(BF)) + b_ref[0]


def _ada(c_all, ada_w, ada_b):
    depth, _, n = ada_w.shape
    nb = c_all.shape[0]
    return pl.pallas_call(
        _ada_kernel,
        grid=(depth, n // D),
        in_specs=[
            pl.BlockSpec((nb, D), lambda i, j: (0, 0)),
            pl.BlockSpec((1, D, D), lambda i, j: (i, 0, j)),
            pl.BlockSpec((1, 1, D), lambda i, j: (i, 0, j)),
        ],
        out_specs=pl.BlockSpec((1, 1, nb, D), lambda i, j: (i, j, 0, 0)),
        out_shape=jax.ShapeDtypeStruct((depth, n // D, nb, D), F32),
        compiler_params=_params("arbitrary", "arbitrary"),
        name="ada",
    )(c_all, ada_w, ada_b.reshape(depth, 1, n))


def _mlp_kernel(*refs, pre, final):
    refs = list(refs)
    x_ref = refs.pop(0)
    if pre:
        y_ref = refs.pop(0)
        wo_ref = refs.pop(0)
    mod_ref, g_ref, wu_ref, wd_ref = refs[:4]
    refs = refs[4:]
    if final:
        fg_ref = refs.pop(0)
    o_ref = refs.pop(0)

    def residual(xr, fr, gate):
        return xr + gate * fr

    def norm(xr, shift, scale):
        return _modulate(xr, g_ref[...], shift, scale)

    x = x_ref[0]
    if pre:
        x = _by_group(residual, (x, _dot(y_ref[0].astype(BF), wo_ref[...])), (mod_ref[0, 2],))
    hb = _by_group(norm, (x,), (mod_ref[0, 3], mod_ref[0, 4])).astype(BF)
    acc = jnp.zeros(x.shape, F32)
    for c in range(DFF // FF_CHUNK):
        cs = slice(c * FF_CHUNK, (c + 1) * FF_CHUNK)
        hid = jnp.maximum(_dot(hb, wu_ref[:, cs]), 0.0)
        acc = acc + _dot((hid * hid).astype(BF), wd_ref[cs, :])
    x = _by_group(residual, (x, acc), (mod_ref[0, 5],))
    if final:
        x = _rms(x) * fg_ref[...]
    o_ref[0] = x


def _group_mod_spec(layer, r):
    return pl.BlockSpec((1, N_MOD, r, D), lambda *_: (layer, 0, 0, 0))


def _mlp(x, mod, g, w_up, w_down, *, tm, pre=None, final_g=None, group=None):
    nb, t, _ = x.shape
    tok = pl.BlockSpec((1, tm, D), lambda b, i: (b, i, 0))
    if group is None:
        mod_spec = pl.BlockSpec((1, N_MOD, 1, D), lambda b, i: (b, 0, 0, 0))
    else:
        assert tm == t and tm % group[1] == 0
        mod_spec = _group_mod_spec(*group)
    args, specs = [x], [tok]
    if pre is not None:
        y, w_out = pre
        args += [y, w_out]
        specs += [tok, _const_spec((D, D))]
    args += [mod, g.reshape(1, D), w_up, w_down]
    specs += [mod_spec, _const_spec((1, D)), _const_spec((D, DFF)), _const_spec((DFF, D))]
    if final_g is not None:
        args.append(final_g.reshape(1, D))
        specs.append(_const_spec((1, D)))
    return pl.pallas_call(
        functools.partial(_mlp_kernel, pre=pre is not None, final=final_g is not None),
        grid=(nb, t // tm),
        in_specs=specs,
        out_specs=tok,
        out_shape=jax.ShapeDtypeStruct(x.shape, F32),
        compiler_params=_params("arbitrary", "arbitrary"),
        name="mlp",
    )(*args)


def _mlstm_prompt_kernel(x_ref, mod_ref, g_ref, wrow_ref, wcol_ref, bg_ref, gh_ref, wout_ref,
                         xo_ref, s_ref, m_ref, y_scr, *, tm):
    @pl.when(pl.program_id(1) == 0)
    def _():
        s_ref[...] = jnp.zeros(s_ref.shape, F32)
        m_ref[...] = jnp.zeros(m_ref.shape, F32)

    x = x_ref[0]
    hb = _modulate(x, g_ref[...], mod_ref[0, 0], mod_ref[0, 1]).astype(BF)
    n = CHUNK
    ig_all = _dot_nt(wcol_ref[HK:HK + H, :], hb) + bg_ref[0:H]
    fg_all = _dot_nt(wcol_ref[HK + LANES:HK + LANES + H, :], hb) + bg_ref[H:2 * H]
    b_all = _scan_lanes(_log_sigmoid(fg_all), jnp.add, 0.0, n)
    r_all = ig_all - b_all
    cm_all = _scan_lanes(r_all, jnp.maximum, -jnp.inf, n)
    pcol = _dot_nt(wcol_ref[0:HK, :], hb)
    prow = _dot(hb, wrow_ref[...])
    causal = (lax.broadcasted_iota(jnp.int32, (n, n), 0)
              >= lax.broadcasted_iota(jnp.int32, (n, n), 1))
    zeros_k = jnp.zeros((DK, n), BF)
    zeros_s = jnp.zeros((DK, 2 * DV), BF)
    zeros_v = jnp.zeros((n, DV), BF)
    ones_v = jnp.ones((n, DV), BF)
    pad = jnp.zeros((n - 2 * H, n), F32)

    for c in range(tm // n):
        ts = slice(c * n, (c + 1) * n)
        r = r_all[:, ts]
        m0 = m_ref[0]
        mx = jnp.maximum(m0, cm_all[:, ts])
        m = b_all[:, ts] + mx
        e_neg = jnp.exp(-m)
        mx_last = jnp.broadcast_to(mx[:, n - 1:n], (H, n))
        a = jnp.exp(r - mx_last)
        decay = jnp.exp(m0 - mx_last)
        m_ref[0] = jnp.broadcast_to(m[:, n - 1:n], (H, n))
        cols = jnp.concatenate([mx, e_neg, pad], axis=0).T

        qk = []
        for j in range(H // 2):
            k0 = pcol[2 * j * DK:(2 * j + 1) * DK, ts].astype(BF)
            k1 = pcol[(2 * j + 1) * DK:(2 * j + 2) * DK, ts].astype(BF)
            qk.append(_dot(prow[ts, j * 2 * DK:(j + 1) * 2 * DK].astype(BF),
                           jnp.concatenate([jnp.concatenate([k0, zeros_k], axis=1),
                                            jnp.concatenate([zeros_k, k1], axis=1)], axis=0)))

        for h in range(H):
            lo = h % 2 == 0
            mx_b = jnp.broadcast_to(cols[:, h:h + 1], (n, n))
            floor_b = jnp.broadcast_to(cols[:, H + h:H + h + 1], (n, DV))
            dec = jnp.exp(jnp.where(causal, r[h:h + 1, :] - mx_b, -jnp.inf))
            e_int = jnp.exp(m0[h:h + 1, :] - mx_b)
            qp = prow[ts, (h // 2) * 2 * DK:(h // 2 + 1) * 2 * DK]
            kt = pcol[h * DK:(h + 1) * DK, ts]
            s = qk[h // 2][:, (h % 2) * n:(h % 2 + 1) * n] * dec
            st = s_ref[0, h]
            stb = st.astype(BF)
            v = prow[ts, HK + h * DV:HK + (h + 1) * DV].astype(BF)
            rhs = jnp.concatenate(
                [jnp.concatenate([v, zeros_v], axis=1)]
                + ([stb, zeros_s] if lo else [zeros_s, stb]), axis=0)
            lhs = jnp.concatenate([s.astype(BF), (qp * e_int).astype(BF)], axis=1)
            res = _dot(lhs, rhs)
            den = jnp.sum(s, axis=-1, keepdims=True) + res[:, DV:]
            hh = res[:, :DV] / jnp.maximum(jnp.abs(den), floor_b)
            hh = hh * lax.rsqrt(jnp.mean(hh * hh, axis=-1, keepdims=True) + EPS)
            y_scr[ts, h * DV:(h + 1) * DV] = hh
            ka = (kt * a[h:h + 1, :]).astype(BF)
            dc = decay[h:h + 1, :]
            s_ref[0, h] = (jnp.concatenate([dc, dc], axis=1) * st
                           + _dot(ka, jnp.concatenate([v, ones_v], axis=1)))

    o = prow[:, HK + D:]
    y = (y_scr[...] * gh_ref[...] * jax.nn.sigmoid(o)).astype(BF)
    xo_ref[0] = x + mod_ref[0, 2] * _dot(y, wout_ref[...])


def _mlstm_prompt(x, mod, g, w_row, w_col, b_g, g_head, w_out, *, tm):
    nb, t, _ = x.shape
    tok = pl.BlockSpec((1, tm, D), lambda b, i: (b, i, 0))
    return pl.pallas_call(
        functools.partial(_mlstm_prompt_kernel, tm=tm),
        grid=(nb, t // tm),
        in_specs=[
            tok,
            pl.BlockSpec((1, N_MOD, 1, D), lambda b, i: (b, 0, 0, 0)),
            _const_spec((1, D)),
            _const_spec((D, HK + 2 * D)),
            _const_spec((HK + 2 * LANES, D)),
            _const_spec((2 * H, tm)),
            _const_spec((1, D)),
            _const_spec((D, D)),
        ],
        out_specs=[
            tok,
            pl.BlockSpec((1, H, DK, 2 * DV), lambda b, i: (b, 0, 0, 0)),
            pl.BlockSpec((1, H, CHUNK), lambda b, i: (b, 0, 0)),
        ],
        out_shape=[
            jax.ShapeDtypeStruct(x.shape, F32),
            jax.ShapeDtypeStruct((nb, H, DK, 2 * DV), F32),
            jax.ShapeDtypeStruct((nb, H, CHUNK), F32),
        ],
        scratch_shapes=[pltpu.VMEM((tm, D), F32)],
        compiler_params=_params("arbitrary", "arbitrary"),
        name="mlstm_prompt",
    )(x, mod, g.reshape(1, D), w_row, w_col, jnp.broadcast_to(b_g.reshape(2 * H, 1), (2 * H, tm)),
      g_head.reshape(1, D), w_out)


def _mlstm_sample_pre_kernel(x_ref, mod_ref, g_ref, wrow_ref, wcol_ref, bg_ref, m0_ref, n0_ref,
                             gh_ref, seg_ref, e64_ref, e128_ref,
                             q_ref, ka_ref, v_ref, dec_ref, numi_ref, rd_ref, eint_ref, go_ref,
                             n1_ref, m1_ref, *, t, nb):
    def norm(xr, shift, scale):
        return _modulate(xr, g_ref[...], shift, scale)

    hb = _by_group(norm, (x_ref[0],), (mod_ref[0, 0], mod_ref[0, 1])).astype(BF)
    p = _dot(hb, wrow_ref[...])
    k_all = _dot_nt(hb, wcol_ref[0:HK, :])
    ig_all = _dot_nt(hb, wcol_ref[HK:HK + LANES, :]) + bg_ref[0:1]
    fg_all = _dot_nt(hb, wcol_ref[HK + LANES:HK + 2 * LANES, :]) + bg_ref[1:2]

    def rows(i):
        return slice(i * nb, (i + 1) * nb)

    q = [p[rows(i), 0:HK] for i in range(t)]
    k = [k_all[rows(i)] for i in range(t)]
    v = [p[rows(i), HK:HK + D] for i in range(t)]
    m0 = m0_ref[...]
    n0 = n0_ref[...]

    b, r, mx = [], [], []
    for i in range(t):
        lf = _log_sigmoid(fg_all[rows(i)])
        b.append(lf if i == 0 else b[-1] + lf)
        r.append(ig_all[rows(i)] - b[i])
        cm = r[i] if i == 0 else jnp.maximum(cm, r[i])
        mx.append(jnp.maximum(m0, cm))
    mx_last = mx[t - 1]
    decay = jnp.exp(m0 - mx_last)
    m1_ref[...] = b[t - 1] + mx_last
    dec_x = _split_dot(decay, e128_ref)
    n1 = _split_dot(decay, e64_ref) * n0

    for i in range(t):
        ka = k[i] * _split_dot(jnp.exp(r[i] - mx_last), e64_ref)
        n1 = n1 + ka
        seq_rows = pl.ds(i, nb, stride=t)
        for c in range(HK // LANES):
            cs = slice(c * LANES, (c + 1) * LANES)
            ka_ref[c, seq_rows, :] = ka[:, cs]
            q_ref[c, seq_rows, :] = q[i][:, cs]
        for c in range(D // LANES):
            cs = slice(c * LANES, (c + 1) * LANES)
            v_ref[c, seq_rows, :] = v[i][:, cs]
            dec_ref[c, seq_rows, :] = dec_x[:, cs]
        e_int = jnp.exp(m0 - mx[i])
        den = e_int * _split_dot(q[i] * n0, seg_ref)
        numi = jnp.zeros((nb, D), F32)
        for s in range(i + 1):
            w = _split_dot(q[i] * k[s], seg_ref) * jnp.exp(jnp.minimum(r[s] - mx[i], 0.0))
            den = den + w
            numi = numi + _split_dot(w, e128_ref) * v[s]
        numi_ref[rows(i), :] = numi
        rd_ref[rows(i), :] = 1.0 / jnp.maximum(jnp.abs(den), jnp.exp(-(b[i] + mx[i])))
        eint_ref[rows(i), :] = e_int
        go_ref[rows(i), :] = jax.nn.sigmoid(p[rows(i), HK + D:]) * gh_ref[...]
    n1_ref[...] = n1


def _mlstm_sample_pre(x, mod, layer, g, w_row, w_col, b_g, m0, n0, g_head, seg64, e64, e128, *, t):
    rows = x.shape[1]
    nb = rows // t

    def whole(shape):
        return pl.BlockSpec(shape, lambda i: (0,) * len(shape))

    outs = [(HK // LANES, rows, LANES), (HK // LANES, rows, LANES), (D // LANES, rows, LANES),
            (D // LANES, rows, LANES), (rows, D), (rows, FG), (rows, FG), (rows, D), (nb, HK),
            (nb, FG)]
    return pl.pallas_call(
        functools.partial(_mlstm_sample_pre_kernel, t=t, nb=nb),
        grid=(1,),
        in_specs=[
            whole((1, rows, D)),
            _group_mod_spec(layer, nb),
            _const_spec((1, D)),
            _const_spec((D, HK + 2 * D)),
            _const_spec((HK + 2 * LANES, D)),
            _const_spec((2, LANES)),
            whole((nb, FG)),
            whole((nb, HK)),
            _const_spec((1, D)),
            _const_spec((HK, FG)),
            _const_spec((FG, HK)),
            _const_spec((FG, D)),
        ],
        out_specs=[whole(s) for s in outs],
        out_shape=[jax.ShapeDtypeStruct(s, F32) for s in outs],
        compiler_params=_params("arbitrary"),
        name="mlstm_sample_pre",
    )(x, mod, g.reshape(1, D), w_row, w_col, b_g, m0, n0, g_head.reshape(1, D), seg64, e64, e128)


def _mlstm_sample_state_kernel(q_ref, ka_ref, v_ref, dec_ref, c0_ref, inter_ref, c1_ref, *, bb, t):
    per = SUBLANES // t
    seq_of_row = lax.broadcasted_iota(jnp.int32, (SUBLANES, 1), 0) // t
    zeros_c = jnp.zeros((DK, DV), BF)

    def tile(i, carry):
        rs = pl.ds(pl.multiple_of(i * SUBLANES, SUBLANES), SUBLANES)
        for h in range(H):
            lo = h % 2 == 0
            half = slice((h % 2) * DK, (h % 2 + 1) * DK)
            qp = q_ref[h // 2, rs, :].astype(BF)
            kap = ka_ref[h // 2, rs, :]
            vh = v_ref[h, rs, :].astype(BF)
            dech = dec_ref[h, rs, :]
            res = None
            for w in range(per):
                seq = i * per + w
                c0 = c0_ref[seq, h]
                c0b = c0.astype(BF)
                rw = _dot(qp, jnp.concatenate([c0b, zeros_c] if lo else [zeros_c, c0b], axis=0))
                res = rw if res is None else jnp.where(seq_of_row == w, rw, res)
                kaw = jnp.where(seq_of_row == w, kap, 0.0).astype(BF)
                c1_ref[seq, h] = dech[w * t:w * t + 1, :] * c0 + _dot_tn(kaw, vh)[half, :]
            inter_ref[h, rs, :] = res
        return carry

    lax.fori_loop(0, bb // per, tile, 0)


def _mlstm_sample_state(q, ka, v, dec, c0, *, t, bb):
    rows = q.shape[1]
    nb = rows // t
    assert SUBLANES % t == 0 and bb % (SUBLANES // t) == 0 and nb % bb == 0

    def tok(blocks):
        return pl.BlockSpec((blocks, bb * t, LANES), lambda i: (0, i, 0))

    state = pl.BlockSpec((bb, H, DK, DV), lambda i: (i, 0, 0, 0))
    return pl.pallas_call(
        functools.partial(_mlstm_sample_state_kernel, bb=bb, t=t),
        grid=(nb // bb,),
        in_specs=[tok(HK // LANES), tok(HK // LANES), tok(H), tok(H), state],
        out_specs=[tok(H), state],
        out_shape=[jax.ShapeDtypeStruct((H, rows, LANES), F32),
                   jax.ShapeDtypeStruct(c0.shape, F32)],
        compiler_params=_params("arbitrary"),
        name="mlstm_sample_state",
    )(q, ka, v, dec, c0)


def _mlstm_sample_post_kernel(inter_ref, numi_ref, rd_ref, eint_ref, go_ref, e128_ref, seg_ref,
                              y_ref, *, t, nb):
    for i in range(t):
        rs = slice(i * nb, (i + 1) * nb)
        inter = jnp.concatenate(
            [inter_ref[h, pl.ds(i, nb, stride=t), :] for h in range(H)], axis=1)
        num = numi_ref[rs, :] + _split_dot(eint_ref[rs, :], e128_ref) * inter
        hh = num * _split_dot(rd_ref[rs, :], e128_ref)
        ms = _split_dot(hh * hh, seg_ref) * (1.0 / DV)
        y_ref[rs, :] = hh * _split_dot(lax.rsqrt(ms + EPS), e128_ref) * go_ref[rs, :]


def _mlstm_sample_post(inter, numi, rd, eint, go, e128, seg128, *, t):
    rows = inter.shape[1]

    def whole(shape):
        return pl.BlockSpec(shape, lambda i: (0,) * len(shape))

    return pl.pallas_call(
        functools.partial(_mlstm_sample_post_kernel, t=t, nb=rows // t),
        grid=(1,),
        in_specs=[whole((H, rows, LANES)), whole((rows, D)), whole((rows, FG)), whole((rows, FG)),
                  whole((rows, D)), _const_spec((FG, D)), _const_spec((D, FG))],
        out_specs=whole((rows, D)),
        out_shape=jax.ShapeDtypeStruct((rows, D), F32),
        compiler_params=_params("arbitrary"),
        name="mlstm_sample_post",
    )(inter, numi, rd, eint, go, e128, seg128)


def _conv_taps(u, prev2, prev1, wc_ref):
    return wc_ref[0:1] * prev2 + wc_ref[1:2] * prev1 + wc_ref[2:3] * u


def _conv_prompt_kernel(x_ref, mod_ref, g_ref, win_ref, wc_ref, wout_ref,
                        xo_ref, st_ref, carry_scr):
    @pl.when(pl.program_id(1) == 0)
    def _():
        carry_scr[...] = jnp.zeros(carry_scr.shape, F32)

    x = x_ref[0]
    tm = x.shape[0]
    hb = _modulate(x, g_ref[...], mod_ref[0, 0], mod_ref[0, 1]).astype(BF)
    u = _dot(hb, win_ref[:, D:2 * D]) * _dot(hb, win_ref[:, 2 * D:])
    row = lax.broadcasted_iota(jnp.int32, (tm, D), 0)
    c0 = carry_scr[0:1]
    c1 = carry_scr[1:2]
    prev1 = jnp.where(row == 0, c1, pltpu.roll(u, 1, 0))
    prev2 = jnp.where(row == 0, c0, jnp.where(row == 1, c1, pltpu.roll(u, 2, 0)))
    y = _conv_taps(u, prev2, prev1, wc_ref)
    bg = _dot(hb, win_ref[:, :D])
    xo_ref[0] = x + mod_ref[0, 2] * _dot((bg * y).astype(BF), wout_ref[...])
    carry_scr[...] = u[tm - 2:tm]
    st_ref[0] = u[tm - 2:tm]


def _conv_prompt(x, mod, g, w_in, w_conv, w_out, *, tm):
    nb, t, _ = x.shape
    tok = pl.BlockSpec((1, tm, D), lambda b, i: (b, i, 0))
    return pl.pallas_call(
        _conv_prompt_kernel,
        grid=(nb, t // tm),
        in_specs=[
            tok,
            pl.BlockSpec((1, N_MOD, 1, D), lambda b, i: (b, 0, 0, 0)),
            _const_spec((1, D)),
            _const_spec((D, 3 * D)),
            _const_spec((CONV_W, D)),
            _const_spec((D, D)),
        ],
        out_specs=[tok, pl.BlockSpec((1, CONV_W - 1, D), lambda b, i: (b, 0, 0))],
        out_shape=[jax.ShapeDtypeStruct(x.shape, F32),
                   jax.ShapeDtypeStruct((nb, CONV_W - 1, D), F32)],
        scratch_shapes=[pltpu.VMEM((CONV_W - 1, D), F32)],
        compiler_params=_params("arbitrary", "arbitrary"),
        name="conv_prompt",
    )(x, mod, g.reshape(1, D), w_in, w_conv, w_out)


def _conv_sample_kernel(x_ref, mod_ref, g_ref, win_ref, wc_ref, wout_ref, buf_ref,
                        xo_ref, st_ref, *, t, nb):
    def norm(xr, shift, scale):
        return _modulate(xr, g_ref[...], shift, scale)

    def residual(xr, fr, gate):
        return xr + gate * fr

    x = x_ref[0]
    hb = _by_group(norm, (x,), (mod_ref[0, 0], mod_ref[0, 1])).astype(BF)
    u = _dot(hb, win_ref[:, D:2 * D]) * _dot(hb, win_ref[:, 2 * D:])
    full = [buf_ref[j] for j in range(CONV_W - 1)] + [u[i * nb:(i + 1) * nb] for i in range(t)]
    y = jnp.concatenate([_conv_taps(full[i + 2], full[i], full[i + 1], wc_ref) for i in range(t)],
                        axis=0)
    bg = _dot(hb, win_ref[:, :D])
    out = _dot((bg * y).astype(BF), wout_ref[...])
    xo_ref[0] = _by_group(residual, (x, out), (mod_ref[0, 2],))
    for j in range(CONV_W - 1):
        st_ref[j] = full[t + j]


def _conv_sample(x, mod, layer, g, w_in, w_conv, w_out, buf, *, t):
    _, rows, _ = x.shape
    nb = rows // t
    full = pl.BlockSpec((1, rows, D), lambda i: (0, 0, 0))
    state = pl.BlockSpec((CONV_W - 1, nb, D), lambda i: (0, 0, 0))
    return pl.pallas_call(
        functools.partial(_conv_sample_kernel, t=t, nb=nb),
        grid=(1,),
        in_specs=[
            full,
            _group_mod_spec(layer, nb),
            _const_spec((1, D)),
            _const_spec((D, 3 * D)),
            _const_spec((CONV_W, D)),
            _const_spec((D, D)),
            state,
        ],
        out_specs=[full, state],
        out_shape=[jax.ShapeDtypeStruct(x.shape, F32),
                   jax.ShapeDtypeStruct((CONV_W - 1, nb, D), F32)],
        compiler_params=_params("arbitrary"),
        name="conv_sample",
    )(x, mod, g.reshape(1, D), w_in, w_conv, w_out, buf)


def kernel(x_prompt, x_sample, c_prompt, c_sample, state_mlstm_C, state_mlstm_n, state_mlstm_m,
           state_conv, ada_w, ada_b, norm_g, final_norm_g, mlstm_w_in, mlstm_b_gates,
           mlstm_head_g, mlstm_w_out, conv_w_in, conv_w, conv_w_out, mlp_w_up, mlp_w_down):
    assert ada_w.shape[0] == 2 and mlstm_w_in.shape[0] == 1 and conv_w_in.shape[0] == 1
    bp, tp, _ = x_prompt.shape
    bs, ts, _ = x_sample.shape
    assert ts == 4 and tp % 512 == 0

    mod = _ada(jnp.concatenate([c_sample, c_prompt], axis=0), ada_w, ada_b)
    mod_p = jnp.transpose(mod[:, :, bs:], (0, 2, 1, 3))[:, :, :, None]

    w_in0 = mlstm_w_in[0]
    w_row = jnp.concatenate([w_in0[:, :HK], w_in0[:, 2 * HK:NP]], axis=1).astype(BF)
    gate_pad = jnp.zeros((D, LANES - H), F32)
    w_col = jnp.concatenate([w_in0[:, HK:2 * HK] * (DK ** -0.5), w_in0[:, NP:NP + H], gate_pad,
                             w_in0[:, NP + H:], gate_pad], axis=1).T.astype(BF)
    b_g = jnp.pad(mlstm_b_gates[0].reshape(2, H), ((0, 0), (0, LANES - H)))
    wout0 = mlstm_w_out[0].astype(BF)
    cwin = conv_w_in[0].astype(BF)
    cwout = conv_w_out[0].astype(BF)
    up = [mlp_w_up[i].astype(BF) for i in range(2)]
    down = [mlp_w_down[i].astype(BF) for i in range(2)]

    x, s_p, m_p = _mlstm_prompt(x_prompt, mod_p[0], norm_g[0, 0], w_row, w_col, mlstm_b_gates[0],
                                mlstm_head_g[0], wout0, tm=512)
    x = _mlp(x, mod_p[0], norm_g[0, 1], up[0], down[0], tm=512)
    x, conv_p = _conv_prompt(x, mod_p[1], norm_g[1, 0], cwin, conv_w[0], cwout, tm=512)
    y_prompt = _mlp(x, mod_p[1], norm_g[1, 1], up[1], down[1], tm=512, final_g=final_norm_g)
    prompt_c = s_p[None, :, :, :, :DV]
    prompt_n = s_p[None, :, :, :, DV]
    prompt_m = m_p[None, :, :, 0]
    prompt_conv = conv_p[None]

    xs = jnp.transpose(x_sample, (1, 0, 2)).reshape(1, ts * bs, D)
    head_of_v = jnp.arange(D, dtype=jnp.int32) // DV
    head_of_k = jnp.arange(HK, dtype=jnp.int32) // DK
    lanes = jnp.arange(FG, dtype=jnp.int32)
    e128 = (lanes[:, None] == head_of_v[None, :]).astype(BF)
    e64 = (lanes[:, None] == head_of_k[None, :]).astype(BF)
    m0 = jnp.pad(state_mlstm_m[0], ((0, 0), (0, FG - H)))
    n0 = state_mlstm_n[0].reshape(bs, HK)
    q, ka, v, dec, numi, rd, eint, go, n1, m1 = _mlstm_sample_pre(
        xs, mod, 0, norm_g[0, 0], w_row, w_col, b_g, m0, n0, mlstm_head_g[0], e64.T, e64, e128,
        t=ts)
    inter, c1 = _mlstm_sample_state(q, ka, v, dec, state_mlstm_C[0], t=ts, bb=16)
    y = _mlstm_sample_post(inter, numi, rd, eint, go, e128, e128.T, t=ts)
    xs = _mlp(xs, mod, norm_g[0, 1], up[0], down[0], tm=ts * bs, group=(0, bs),
              pre=(y.reshape(1, ts * bs, D), wout0))
    xs, conv_s = _conv_sample(xs, mod, 1, norm_g[1, 0], cwin, conv_w[0], cwout,
                              jnp.transpose(state_conv[0], (1, 0, 2)), t=ts)
    ys = _mlp(xs, mod, norm_g[1, 1], up[1], down[1], tm=ts * bs, group=(1, bs),
              final_g=final_norm_g)
    y_sample = jnp.transpose(ys.reshape(ts, bs, D), (1, 0, 2))
    sample_n = n1.reshape(bs, H, DK)[None]
    sample_m = m1[None, :, :H]
    sample_conv = jnp.transpose(conv_s, (1, 0, 2))[None]

    return (y_prompt, y_sample, prompt_c, prompt_n, prompt_m, prompt_conv,
            c1[None], sample_n, sample_m, sample_conv)
```

```python
import functools

import jax
import jax.numpy as jnp
from jax import lax
from jax.experimental import pallas as pl
from jax.experimental.pallas import tpu as pltpu

D = 1024
H = 8
DK = 64
DV = 128
HK = H * DK
NP = 2 * HK + 2 * D
GW = 256
FG = 128
LANES = 128
SUBLANES = 8
DFF = 4 * D
N_MOD = 6
CONV_W = 3
EPS = 1e-6
CHUNK = 128
FF_CHUNK = 1024
BF = jnp.bfloat16
F32 = jnp.float32
VMEM_LIMIT_BYTES = 56 * 1024 * 1024


def _dot(a, b):
    return jnp.dot(a, b, preferred_element_type=F32)


def _dot_nt(a, b):
    return lax.dot_general(a, b, (((1,), (1,)), ((), ())), preferred_element_type=F32)


def _dot_tn(a, b):
    return lax.dot_general(a, b, (((0,), (0,)), ((), ())), preferred_element_type=F32)


def _rms(x):
    return x * lax.rsqrt(jnp.mean(x * x, axis=-1, keepdims=True) + EPS)


def _modulate(x, g, shift, scale):
    return _rms(x) * g * (1.0 + scale) + shift


def _log_sigmoid(x):
    return jnp.minimum(x, 0.0) - jnp.log1p(jnp.exp(-jnp.abs(x)))


def _scan_lanes(x, op, ident, n):
    lane = lax.broadcasted_iota(jnp.int32, x.shape, 1) % n
    d = 1
    while d < n:
        x = op(x, jnp.where(lane >= d, pltpu.roll(x, d, 1), ident))
        d *= 2
    return x


def _by_group(fn, xs, mods):
    rows, r = xs[0].shape[0], mods[0].shape[0]
    if r == 1 or r == rows:
        return fn(*xs, *mods)
    return jnp.concatenate(
        [fn(*[x[i:i + r] for x in xs], *mods) for i in range(0, rows, r)], axis=0)


def _split_dot(x, m_ref):
    hi = x.astype(BF)
    lo = (x - hi.astype(F32)).astype(BF)
    m = m_ref[...]
    return _dot(hi, m) + _dot(lo, m)


def _cast_specs(jobs, nt, steps):
    ins, outs, shapes, args = [], [], [], []
    for w, layer in jobs:
        _, r, c = w.shape
        slab = r // steps
        assert slab * steps == r and slab % (2 * SUBLANES) == 0
        ins.append(pl.BlockSpec((1, slab, c), lambda b, i, layer=layer: (layer, b * nt + i, 0)))
        outs.append(pl.BlockSpec((slab, c), lambda b, i: (b * nt + i, 0)))
        shapes.append(jax.ShapeDtypeStruct((r, c), BF))
        args.append(w)
    return ins, outs, shapes, args


def _carry_casts(body, n_in, n_out, n_cast):
    def kernel(*refs):
        a, b = n_in, n_in + n_cast
        c, d = b + n_out, b + n_out + n_cast
        for src, dst in zip(refs[a:b], refs[c:d]):
            dst[...] = src[0].astype(BF)
        body(*refs[:a], *refs[b:c], *refs[d:])
    return kernel


def _const_spec(shape):
    nd = len(shape)
    return pl.BlockSpec(shape, lambda *_: (0,) * nd, pipeline_mode=pl.Buffered(1))


def _params(*sem):
    return pltpu.CompilerParams(dimension_semantics=sem, vmem_limit_bytes=VMEM_LIMIT_BYTES)


def _ada_kernel(c_ref, w_ref, b_ref, o_ref):
    c = c_ref[...]
    sc = (c * jax.nn.sigmoid(c)).astype(BF)
    o_ref[0, 0] = _dot(sc, w_ref[0].astype(BF)) + b_ref[0]


def _ada(c_all, ada_w, ada_b):
    depth, _, n = ada_w.shape
    nb = c_all.shape[0]
    return pl.pallas_call(
        _ada_kernel,
        grid=(depth, n // D),
        in_specs=[
            pl.BlockSpec((nb, D), lambda i, j: (0, 0)),
            pl.BlockSpec((1, D, D), lambda i, j: (i, 0, j)),
            pl.BlockSpec((1, 1, D), lambda i, j: (i, 0, j)),
        ],
        out_specs=pl.BlockSpec((1, 1, nb, D), lambda i, j: (i, j, 0, 0)),
        out_shape=jax.ShapeDtypeStruct((depth, n // D, nb, D), F32),
        compiler_params=_params("arbitrary", "arbitrary"),
        name="ada",
    )(c_all, ada_w, ada_b.reshape(depth, 1, n))


def _mlp_kernel(*refs, pre, final):
    refs = list(refs)
    x_ref = refs.pop(0)
    if pre:
        y_ref = refs.pop(0)
        wo_ref = refs.pop(0)
    mod_ref, g_ref, wu_ref, wd_ref = refs[:4]
    refs = refs[4:]
    if final:
        fg_ref = refs.pop(0)
    o_ref = refs.pop(0)

    def residual(xr, fr, gate):
        return xr + gate * fr

    def norm(xr, shift, scale):
        return _modulate(xr, g_ref[...], shift, scale)

    x = x_ref[0]
    if pre:
        x = _by_group(residual, (x, _dot(y_ref[0].astype(BF), wo_ref[...])), (mod_ref[0, 2],))
    hb = _by_group(norm, (x,), (mod_ref[0, 3], mod_ref[0, 4])).astype(BF)
    acc = jnp.zeros(x.shape, F32)
    for c in range(DFF // FF_CHUNK):
        cs = slice(c * FF_CHUNK, (c + 1) * FF_CHUNK)
        hid = jnp.maximum(_dot(hb, wu_ref[:, cs]), 0.0)
        acc = acc + _dot((hid * hid).astype(BF), wd_ref[cs, :])
    x = _by_group(residual, (x, acc), (mod_ref[0, 5],))
    if final:
        x = _rms(x) * fg_ref[...]
    o_ref[0] = x


def _group_mod_spec(layer, r):
    return pl.BlockSpec((1, N_MOD, r, D), lambda *_: (layer, 0, 0, 0))


def _mlp(x, mod, g, w_up, w_down, *, tm, pre=None, final_g=None, group=None, casts=()):
    nb, t, _ = x.shape
    tok = pl.BlockSpec((1, tm, D), lambda b, i: (b, i, 0))
    if group is None:
        mod_spec = pl.BlockSpec((1, N_MOD, 1, D), lambda b, i: (b, 0, 0, 0))
    else:
        assert tm == t and tm % group[1] == 0
        mod_spec = _group_mod_spec(*group)
    args, specs = [x], [tok]
    if pre is not None:
        y, w_out = pre
        args += [y, w_out]
        specs += [tok, _const_spec((D, D))]
    args += [mod, g.reshape(1, D), w_up, w_down]
    specs += [mod_spec, _const_spec((1, D)), _const_spec((D, DFF)), _const_spec((DFF, D))]
    if final_g is not None:
        args.append(final_g.reshape(1, D))
        specs.append(_const_spec((1, D)))
    nt = t // tm
    c_in, c_out, c_shapes, c_args = _cast_specs(casts, nt, nb * nt)
    body = functools.partial(_mlp_kernel, pre=pre is not None, final=final_g is not None)
    return pl.pallas_call(
        _carry_casts(body, len(args), 1, len(casts)),
        grid=(nb, nt),
        in_specs=specs + c_in,
        out_specs=[tok] + c_out,
        out_shape=[jax.ShapeDtypeStruct(x.shape, F32)] + c_shapes,
        compiler_params=_params("arbitrary", "arbitrary"),
        name="mlp",
    )(*args, *c_args)


def _mlstm_prompt_kernel(x_ref, mod_ref, g_ref, wrow_ref, wcol_ref, bg_ref, gh_ref, wout_ref,
                         xo_ref, s_ref, m_ref, y_scr, *, tm):
    @pl.when(pl.program_id(1) == 0)
    def _():
        s_ref[...] = jnp.zeros(s_ref.shape, F32)
        m_ref[...] = jnp.zeros(m_ref.shape, F32)

    x = x_ref[0]
    hb = _modulate(x, g_ref[...], mod_ref[0, 0], mod_ref[0, 1]).astype(BF)
    n = CHUNK
    ig_all = _dot_nt(wcol_ref[HK:HK + H, :], hb) + bg_ref[0:H]
    fg_all = _dot_nt(wcol_ref[HK + LANES:HK + LANES + H, :], hb) + bg_ref[H:2 * H]
    b_all = _scan_lanes(_log_sigmoid(fg_all), jnp.add, 0.0, n)
    r_all = ig_all - b_all
    cm_all = _scan_lanes(r_all, jnp.maximum, -jnp.inf, n)
    pcol = _dot_nt(wcol_ref[0:HK, :], hb)
    prow = _dot(hb, wrow_ref[...])
    causal = (lax.broadcasted_iota(jnp.int32, (n, n), 0)
              >= lax.broadcasted_iota(jnp.int32, (n, n), 1))
    zeros_k = jnp.zeros((DK, n), BF)
    zeros_s = jnp.zeros((DK, 2 * DV), BF)
    zeros_v = jnp.zeros((n, DV), BF)
    ones_v = jnp.ones((n, DV), BF)
    pad = jnp.zeros((n - 2 * H, n), F32)

    for c in range(tm // n):
        ts = slice(c * n, (c + 1) * n)
        r = r_all[:, ts]
        m0 = m_ref[0]
        mx = jnp.maximum(m0, cm_all[:, ts])
        m = b_all[:, ts] + mx
        e_neg = jnp.exp(-m)
        mx_last = jnp.broadcast_to(mx[:, n - 1:n], (H, n))
        a = jnp.exp(r - mx_last)
        decay = jnp.exp(m0 - mx_last)
        m_ref[0] = jnp.broadcast_to(m[:, n - 1:n], (H, n))
        cols = jnp.concatenate([mx, e_neg, pad], axis=0).T

        for h in range(H):
            lo = h % 2 == 0
            mx_b = jnp.broadcast_to(cols[:, h:h + 1], (n, n))
            floor_b = jnp.broadcast_to(cols[:, H + h:H + h + 1], (n, DV))
            dec = jnp.exp(jnp.where(causal, r[h:h + 1, :] - mx_b, -jnp.inf))
            e_int = jnp.exp(m0[h:h + 1, :] - mx_b)
            qp = prow[ts, (h // 2) * 2 * DK:(h // 2 + 1) * 2 * DK]
            kt = pcol[h * DK:(h + 1) * DK, ts]
            ktb = kt.astype(BF)
            s = _dot(qp.astype(BF),
                     jnp.concatenate([ktb, zeros_k] if lo else [zeros_k, ktb], axis=0)) * dec
            st = s_ref[0, h]
            stb = st.astype(BF)
            v = prow[ts, HK + h * DV:HK + (h + 1) * DV].astype(BF)
            rhs = jnp.concatenate(
                [jnp.concatenate([v, zeros_v], axis=1)]
                + ([stb, zeros_s] if lo else [zeros_s, stb]), axis=0)
            lhs = jnp.concatenate([s.astype(BF), (qp * e_int).astype(BF)], axis=1)
            res = _dot(lhs, rhs)
            den = jnp.sum(s, axis=-1, keepdims=True) + res[:, DV:]
            hh = res[:, :DV] / jnp.maximum(jnp.abs(den), floor_b)
            hh = hh * lax.rsqrt(jnp.mean(hh * hh, axis=-1, keepdims=True) + EPS)
            y_scr[ts, h * DV:(h + 1) * DV] = hh
            ka = (kt * a[h:h + 1, :]).astype(BF)
            dc = decay[h:h + 1, :]
            s_ref[0, h] = (jnp.concatenate([dc, dc], axis=1) * st
                           + _dot(ka, jnp.concatenate([v, ones_v], axis=1)))

    o = prow[:, HK + D:]
    y = (y_scr[...] * gh_ref[...] * jax.nn.sigmoid(o)).astype(BF)
    xo_ref[0] = x + mod_ref[0, 2] * _dot(y, wout_ref[...])


def _mlstm_prompt(x, mod, g, w_row, w_col, b_g, g_head, w_out, *, tm, casts=()):
    nb, t, _ = x.shape
    nt = t // tm
    tok = pl.BlockSpec((1, tm, D), lambda b, i: (b, i, 0))
    args = (x, mod, g.reshape(1, D), w_row, w_col,
            jnp.broadcast_to(b_g.reshape(2 * H, 1), (2 * H, tm)), g_head.reshape(1, D), w_out)
    c_in, c_out, c_shapes, c_args = _cast_specs(casts, nt, nb * nt)
    return pl.pallas_call(
        _carry_casts(functools.partial(_mlstm_prompt_kernel, tm=tm), len(args), 3, len(casts)),
        grid=(nb, nt),
        in_specs=[
            tok,
            pl.BlockSpec((1, N_MOD, 1, D), lambda b, i: (b, 0, 0, 0)),
            _const_spec((1, D)),
            _const_spec((D, HK + 2 * D)),
            _const_spec((HK + 2 * LANES, D)),
            _const_spec((2 * H, tm)),
            _const_spec((1, D)),
            _const_spec((D, D)),
        ] + c_in,
        out_specs=[
            tok,
            pl.BlockSpec((1, H, DK, 2 * DV), lambda b, i: (b, 0, 0, 0)),
            pl.BlockSpec((1, H, CHUNK), lambda b, i: (b, 0, 0)),
        ] + c_out,
        out_shape=[
            jax.ShapeDtypeStruct(x.shape, F32),
            jax.ShapeDtypeStruct((nb, H, DK, 2 * DV), F32),
            jax.ShapeDtypeStruct((nb, H, CHUNK), F32),
        ] + c_shapes,
        scratch_shapes=[pltpu.VMEM((tm, D), F32)],
        compiler_params=_params("arbitrary", "arbitrary"),
        name="mlstm_prompt",
    )(*args, *c_args)


def _mlstm_sample_pre_kernel(x_ref, mod_ref, g_ref, wrow_ref, wcol_ref, bg_ref, m0_ref, n0_ref,
                             gh_ref, seg_ref, e64_ref, e128_ref,
                             q_ref, ka_ref, v_ref, dec_ref, numi_ref, rd_ref, eint_ref, go_ref,
                             n1_ref, m1_ref, *, t, nb):
    def norm(xr, shift, scale):
        return _modulate(xr, g_ref[...], shift, scale)

    hb = _by_group(norm, (x_ref[0],), (mod_ref[0, 0], mod_ref[0, 1])).astype(BF)
    p = _dot(hb, wrow_ref[...])
    k_all = _dot_nt(hb, wcol_ref[0:HK, :])
    ig_all = _dot_nt(hb, wcol_ref[HK:HK + LANES, :]) + bg_ref[0:1]
    fg_all = _dot_nt(hb, wcol_ref[HK + LANES:HK + 2 * LANES, :]) + bg_ref[1:2]

    def rows(i):
        return slice(i * nb, (i + 1) * nb)

    q = [p[rows(i), 0:HK] for i in range(t)]
    k = [k_all[rows(i)] for i in range(t)]
    v = [p[rows(i), HK:HK + D] for i in range(t)]
    m0 = m0_ref[...]
    n0 = n0_ref[...]

    b, r, mx = [], [], []
    for i in range(t):
        lf = _log_sigmoid(fg_all[rows(i)])
        b.append(lf if i == 0 else b[-1] + lf)
        r.append(ig_all[rows(i)] - b[i])
        cm = r[i] if i == 0 else jnp.maximum(cm, r[i])
        mx.append(jnp.maximum(m0, cm))
    mx_last = mx[t - 1]
    decay = jnp.exp(m0 - mx_last)
    m1_ref[...] = b[t - 1] + mx_last
    dec_x = _split_dot(decay, e128_ref)
    n1 = _split_dot(decay, e64_ref) * n0

    for i in range(t):
        ka = k[i] * _split_dot(jnp.exp(r[i] - mx_last), e64_ref)
        n1 = n1 + ka
        seq_rows = pl.ds(i, nb, stride=t)
        for c in range(HK // LANES):
            cs = slice(c * LANES, (c + 1) * LANES)
            ka_ref[c, seq_rows, :] = ka[:, cs]
            q_ref[c, seq_rows, :] = q[i][:, cs]
        for c in range(D // LANES):
            cs = slice(c * LANES, (c + 1) * LANES)
            v_ref[c, seq_rows, :] = v[i][:, cs]
            dec_ref[c, seq_rows, :] = dec_x[:, cs]
        e_int = jnp.exp(m0 - mx[i])
        den = e_int * _split_dot(q[i] * n0, seg_ref)
        numi = jnp.zeros((nb, D), F32)
        for s in range(i + 1):
            w = _split_dot(q[i] * k[s], seg_ref) * jnp.exp(jnp.minimum(r[s] - mx[i], 0.0))
            den = den + w
            numi = numi + _split_dot(w, e128_ref) * v[s]
        numi_ref[rows(i), :] = numi
        rd_ref[rows(i), :] = 1.0 / jnp.maximum(jnp.abs(den), jnp.exp(-(b[i] + mx[i])))
        eint_ref[rows(i), :] = e_int
        go_ref[rows(i), :] = jax.nn.sigmoid(p[rows(i), HK + D:]) * gh_ref[...]
    n1_ref[...] = n1


def _mlstm_sample_pre(x, mod, layer, g, w_row, w_col, b_g, m0, n0, g_head, seg64, e64, e128, *, t):
    rows = x.shape[1]
    nb = rows // t

    def whole(shape):
        return pl.BlockSpec(shape, lambda i: (0,) * len(shape))

    outs = [(HK // LANES, rows, LANES), (HK // LANES, rows, LANES), (D // LANES, rows, LANES),
            (D // LANES, rows, LANES), (rows, D), (rows, FG), (rows, FG), (rows, D), (nb, HK),
            (nb, FG)]
    return pl.pallas_call(
        functools.partial(_mlstm_sample_pre_kernel, t=t, nb=nb),
        grid=(1,),
        in_specs=[
            whole((1, rows, D)),
            _group_mod_spec(layer, nb),
            _const_spec((1, D)),
            _const_spec((D, HK + 2 * D)),
            _const_spec((HK + 2 * LANES, D)),
            _const_spec((2, LANES)),
            whole((nb, FG)),
            whole((nb, HK)),
            _const_spec((1, D)),
            _const_spec((HK, FG)),
            _const_spec((FG, HK)),
            _const_spec((FG, D)),
        ],
        out_specs=[whole(s) for s in outs],
        out_shape=[jax.ShapeDtypeStruct(s, F32) for s in outs],
        compiler_params=_params("arbitrary"),
        name="mlstm_sample_pre",
    )(x, mod, g.reshape(1, D), w_row, w_col, b_g, m0, n0, g_head.reshape(1, D), seg64, e64, e128)


def _mlstm_sample_state_kernel(q_ref, ka_ref, v_ref, dec_ref, c0_ref, inter_ref, c1_ref, *, bb, t):
    per = SUBLANES // t
    seq_of_row = lax.broadcasted_iota(jnp.int32, (SUBLANES, 1), 0) // t
    zeros_c = jnp.zeros((DK, DV), BF)

    def tile(i, carry):
        rs = pl.ds(pl.multiple_of(i * SUBLANES, SUBLANES), SUBLANES)
        for h in range(H):
            lo = h % 2 == 0
            half = slice((h % 2) * DK, (h % 2 + 1) * DK)
            qp = q_ref[h // 2, rs, :].astype(BF)
            kap = ka_ref[h // 2, rs, :]
            vh = v_ref[h, rs, :].astype(BF)
            dech = dec_ref[h, rs, :]
            res = None
            for w in range(per):
                seq = i * per + w
                c0 = c0_ref[seq, h]
                c0b = c0.astype(BF)
                rw = _dot(qp, jnp.concatenate([c0b, zeros_c] if lo else [zeros_c, c0b], axis=0))
                res = rw if res is None else jnp.where(seq_of_row == w, rw, res)
                kaw = jnp.where(seq_of_row == w, kap, 0.0).astype(BF)
                c1_ref[seq, h] = dech[w * t:w * t + 1, :] * c0 + _dot_tn(kaw, vh)[half, :]
            inter_ref[h, rs, :] = res
        return carry

    lax.fori_loop(0, bb // per, tile, 0)


def _mlstm_sample_state(q, ka, v, dec, c0, *, t, bb):
    rows = q.shape[1]
    nb = rows // t
    assert SUBLANES % t == 0 and bb % (SUBLANES // t) == 0 and nb % bb == 0

    def tok(blocks):
        return pl.BlockSpec((blocks, bb * t, LANES), lambda i: (0, i, 0))

    state = pl.BlockSpec((bb, H, DK, DV), lambda i: (i, 0, 0, 0))
    return pl.pallas_call(
        functools.partial(_mlstm_sample_state_kernel, bb=bb, t=t),
        grid=(nb // bb,),
        in_specs=[tok(HK // LANES), tok(HK // LANES), tok(H), tok(H), state],
        out_specs=[tok(H), state],
        out_shape=[jax.ShapeDtypeStruct((H, rows, LANES), F32),
                   jax.ShapeDtypeStruct(c0.shape, F32)],
        compiler_params=_params("arbitrary"),
        name="mlstm_sample_state",
    )(q, ka, v, dec, c0)


def _mlstm_sample_post_kernel(inter_ref, numi_ref, rd_ref, eint_ref, go_ref, e128_ref, seg_ref,
                              y_ref, *, t, nb):
    for i in range(t):
        rs = slice(i * nb, (i + 1) * nb)
        inter = jnp.concatenate(
            [inter_ref[h, pl.ds(i, nb, stride=t), :] for h in range(H)], axis=1)
        num = numi_ref[rs, :] + _split_dot(eint_ref[rs, :], e128_ref) * inter
        hh = num * _split_dot(rd_ref[rs, :], e128_ref)
        ms = _split_dot(hh * hh, seg_ref) * (1.0 / DV)
        y_ref[rs, :] = hh * _split_dot(lax.rsqrt(ms + EPS), e128_ref) * go_ref[rs, :]


def _mlstm_sample_post(inter, numi, rd, eint, go, e128, seg128, *, t):
    rows = inter.shape[1]

    def whole(shape):
        return pl.BlockSpec(shape, lambda i: (0,) * len(shape))

    return pl.pallas_call(
        functools.partial(_mlstm_sample_post_kernel, t=t, nb=rows // t),
        grid=(1,),
        in_specs=[whole((H, rows, LANES)), whole((rows, D)), whole((rows, FG)), whole((rows, FG)),
                  whole((rows, D)), _const_spec((FG, D)), _const_spec((D, FG))],
        out_specs=whole((rows, D)),
        out_shape=jax.ShapeDtypeStruct((rows, D), F32),
        compiler_params=_params("arbitrary"),
        name="mlstm_sample_post",
    )(inter, numi, rd, eint, go, e128, seg128)


def _conv_taps(u, prev2, prev1, wc_ref):
    return wc_ref[0:1] * prev2 + wc_ref[1:2] * prev1 + wc_ref[2:3] * u


def _conv_prompt_kernel(x_ref, mod_ref, g_ref, win_ref, wc_ref, wout_ref,
                        xo_ref, st_ref, carry_scr):
    @pl.when(pl.program_id(1) == 0)
    def _():
        carry_scr[...] = jnp.zeros(carry_scr.shape, F32)

    x = x_ref[0]
    tm = x.shape[0]
    hb = _modulate(x, g_ref[...], mod_ref[0, 0], mod_ref[0, 1]).astype(BF)
    u = _dot(hb, win_ref[:, D:2 * D]) * _dot(hb, win_ref[:, 2 * D:])
    row = lax.broadcasted_iota(jnp.int32, (tm, D), 0)
    c0 = carry_scr[0:1]
    c1 = carry_scr[1:2]
    prev1 = jnp.where(row == 0, c1, pltpu.roll(u, 1, 0))
    prev2 = jnp.where(row == 0, c0, jnp.where(row == 1, c1, pltpu.roll(u, 2, 0)))
    y = _conv_taps(u, prev2, prev1, wc_ref)
    bg = _dot(hb, win_ref[:, :D])
    xo_ref[0] = x + mod_ref[0, 2] * _dot((bg * y).astype(BF), wout_ref[...])
    carry_scr[...] = u[tm - 2:tm]
    st_ref[0] = u[tm - 2:tm]


def _conv_prompt(x, mod, g, w_in, w_conv, w_out, *, tm, casts=()):
    nb, t, _ = x.shape
    nt = t // tm
    tok = pl.BlockSpec((1, tm, D), lambda b, i: (b, i, 0))
    args = (x, mod, g.reshape(1, D), w_in, w_conv, w_out)
    c_in, c_out, c_shapes, c_args = _cast_specs(casts, nt, nb * nt)
    return pl.pallas_call(
        _carry_casts(_conv_prompt_kernel, len(args), 2, len(casts)),
        grid=(nb, nt),
        in_specs=[
            tok,
            pl.BlockSpec((1, N_MOD, 1, D), lambda b, i: (b, 0, 0, 0)),
            _const_spec((1, D)),
            _const_spec((D, 3 * D)),
            _const_spec((CONV_W, D)),
            _const_spec((D, D)),
        ] + c_in,
        out_specs=[tok, pl.BlockSpec((1, CONV_W - 1, D), lambda b, i: (b, 0, 0))] + c_out,
        out_shape=[jax.ShapeDtypeStruct(x.shape, F32),
                   jax.ShapeDtypeStruct((nb, CONV_W - 1, D), F32)] + c_shapes,
        scratch_shapes=[pltpu.VMEM((CONV_W - 1, D), F32)],
        compiler_params=_params("arbitrary", "arbitrary"),
        name="conv_prompt",
    )(*args, *c_args)


def _conv_sample_kernel(x_ref, mod_ref, g_ref, win_ref, wc_ref, wout_ref, buf_ref,
                        xo_ref, st_ref, *, t, nb):
    def norm(xr, shift, scale):
        return _modulate(xr, g_ref[...], shift, scale)

    def residual(xr, fr, gate):
        return xr + gate * fr

    x = x_ref[0]
    hb = _by_group(norm, (x,), (mod_ref[0, 0], mod_ref[0, 1])).astype(BF)
    u = _dot(hb, win_ref[:, D:2 * D]) * _dot(hb, win_ref[:, 2 * D:])
    full = [buf_ref[j] for j in range(CONV_W - 1)] + [u[i * nb:(i + 1) * nb] for i in range(t)]
    y = jnp.concatenate([_conv_taps(full[i + 2], full[i], full[i + 1], wc_ref) for i in range(t)],
                        axis=0)
    bg = _dot(hb, win_ref[:, :D])
    out = _dot((bg * y).astype(BF), wout_ref[...])
    xo_ref[0] = _by_group(residual, (x, out), (mod_ref[0, 2],))
    for j in range(CONV_W - 1):
        st_ref[j] = full[t + j]


def _conv_sample(x, mod, layer, g, w_in, w_conv, w_out, buf, *, t):
    _, rows, _ = x.shape
    nb = rows // t
    full = pl.BlockSpec((1, rows, D), lambda i: (0, 0, 0))
    state = pl.BlockSpec((CONV_W - 1, nb, D), lambda i: (0, 0, 0))
    return pl.pallas_call(
        functools.partial(_conv_sample_kernel, t=t, nb=nb),
        grid=(1,),
        in_specs=[
            full,
            _group_mod_spec(layer, nb),
            _const_spec((1, D)),
            _const_spec((D, 3 * D)),
            _const_spec((CONV_W, D)),
            _const_spec((D, D)),
            state,
        ],
        out_specs=[full, state],
        out_shape=[jax.ShapeDtypeStruct(x.shape, F32),
                   jax.ShapeDtypeStruct((CONV_W - 1, nb, D), F32)],
        compiler_params=_params("arbitrary"),
        name="conv_sample",
    )(x, mod, g.reshape(1, D), w_in, w_conv, w_out, buf)


def kernel(x_prompt, x_sample, c_prompt, c_sample, state_mlstm_C, state_mlstm_n, state_mlstm_m,
           state_conv, ada_w, ada_b, norm_g, final_norm_g, mlstm_w_in, mlstm_b_gates,
           mlstm_head_g, mlstm_w_out, conv_w_in, conv_w, conv_w_out, mlp_w_up, mlp_w_down):
    assert ada_w.shape[0] == 2 and mlstm_w_in.shape[0] == 1 and conv_w_in.shape[0] == 1
    bp, tp, _ = x_prompt.shape
    bs, ts, _ = x_sample.shape
    assert ts == 4 and tp % 512 == 0

    mod = _ada(jnp.concatenate([c_sample, c_prompt], axis=0), ada_w, ada_b)
    mod_p = jnp.transpose(mod[:, :, bs:], (0, 2, 1, 3))[:, :, :, None]

    w_in0 = mlstm_w_in[0]
    w_row = jnp.concatenate([w_in0[:, :HK], w_in0[:, 2 * HK:NP]], axis=1).astype(BF)
    gate_pad = jnp.zeros((D, LANES - H), F32)
    w_col = jnp.concatenate([w_in0[:, HK:2 * HK] * (DK ** -0.5), w_in0[:, NP:NP + H], gate_pad,
                             w_in0[:, NP + H:], gate_pad], axis=1).T.astype(BF)
    b_g = jnp.pad(mlstm_b_gates[0].reshape(2, H), ((0, 0), (0, LANES - H)))
    wout0 = mlstm_w_out[0].astype(BF)

    x, s_p, m_p, up0, down0 = _mlstm_prompt(
        x_prompt, mod_p[0], norm_g[0, 0], w_row, w_col, mlstm_b_gates[0], mlstm_head_g[0], wout0,
        tm=512, casts=((mlp_w_up, 0), (mlp_w_down, 0)))
    x, cwin, cwout = _mlp(x, mod_p[0], norm_g[0, 1], up0, down0, tm=512,
                          casts=((conv_w_in, 0), (conv_w_out, 0)))
    x, conv_p, up1, down1 = _conv_prompt(x, mod_p[1], norm_g[1, 0], cwin, conv_w[0], cwout, tm=512,
                                         casts=((mlp_w_up, 1), (mlp_w_down, 1)))
    y_prompt, = _mlp(x, mod_p[1], norm_g[1, 1], up1, down1, tm=512, final_g=final_norm_g)
    prompt_c = s_p[None, :, :, :, :DV]
    prompt_n = s_p[None, :, :, :, DV]
    prompt_m = m_p[None, :, :, 0]
    prompt_conv = conv_p[None]

    xs = jnp.transpose(x_sample, (1, 0, 2)).reshape(1, ts * bs, D)
    head_of_v = jnp.arange(D, dtype=jnp.int32) // DV
    head_of_k = jnp.arange(HK, dtype=jnp.int32) // DK
    lanes = jnp.arange(FG, dtype=jnp.int32)
    e128 = (lanes[:, None] == head_of_v[None, :]).astype(BF)
    e64 = (lanes[:, None] == head_of_k[None, :]).astype(BF)
    m0 = jnp.pad(state_mlstm_m[0], ((0, 0), (0, FG - H)))
    n0 = state_mlstm_n[0].reshape(bs, HK)
    q, ka, v, dec, numi, rd, eint, go, n1, m1 = _mlstm_sample_pre(
        xs, mod, 0, norm_g[0, 0], w_row, w_col, b_g, m0, n0, mlstm_head_g[0], e64.T, e64, e128,
        t=ts)
    inter, c1 = _mlstm_sample_state(q, ka, v, dec, state_mlstm_C[0], t=ts, bb=16)
    y = _mlstm_sample_post(inter, numi, rd, eint, go, e128, e128.T, t=ts)
    xs, = _mlp(xs, mod, norm_g[0, 1], up0, down0, tm=ts * bs, group=(0, bs),
               pre=(y.reshape(1, ts * bs, D), wout0))
    xs, conv_s = _conv_sample(xs, mod, 1, norm_g[1, 0], cwin, conv_w[0], cwout,
                              jnp.transpose(state_conv[0], (1, 0, 2)), t=ts)
    ys, = _mlp(xs, mod, norm_g[1, 1], up1, down1, tm=ts * bs, group=(1, bs),
               final_g=final_norm_g)
    y_sample = jnp.transpose(ys.reshape(ts, bs, D), (1, 0, 2))
    sample_n = n1.reshape(bs, H, DK)[None]
    sample_m = m1[None, :, :H]
    sample_conv = jnp.transpose(conv_s, (1, 0, 2))[None]

    return (y_prompt, y_sample, prompt_c, prompt_n, prompt_m, prompt_conv,
            c1[None], sample_n, sample_m, sample_conv)
```

```python
import functools

import jax
import jax.numpy as jnp
from jax import lax
from jax.experimental import pallas as pl
from jax.experimental.pallas import tpu as pltpu

D = 1024
H = 8
DK = 64
DV = 128
HK = H * DK
NP = 2 * HK + 2 * D
FG = 128
LANES = 128
SUBLANES = 8
DFF = 4 * D
N_MOD = 6
CONV_W = 3
EPS = 1e-6
CHUNK = 128
FF_CHUNK = 1024
BF = jnp.bfloat16
F32 = jnp.float32
VMEM_LIMIT_BYTES = 56 * 1024 * 1024


def _dot(a, b):
    return jnp.dot(a, b, preferred_element_type=F32)


def _dot_nt(a, b):
    return lax.dot_general(a, b, (((1,), (1,)), ((), ())), preferred_element_type=F32)


def _dot_tn(a, b):
    return lax.dot_general(a, b, (((0,), (0,)), ((), ())), preferred_element_type=F32)


def _rms(x):
    return x * lax.rsqrt(jnp.mean(x * x, axis=-1, keepdims=True) + EPS)


def _modulate(x, g, shift, scale):
    return _rms(x) * g * (1.0 + scale) + shift


def _log_sigmoid(x):
    return jnp.minimum(x, 0.0) - jnp.log1p(jnp.exp(-jnp.abs(x)))


def _scan_lanes(x, op, ident, n):
    lane = lax.broadcasted_iota(jnp.int32, x.shape, 1) % n
    d = 1
    while d < n:
        x = op(x, jnp.where(lane >= d, pltpu.roll(x, d, 1), ident))
        d *= 2
    return x


def _by_group(fn, xs, mods):
    rows, r = xs[0].shape[0], mods[0].shape[0]
    if r == 1 or r == rows:
        return fn(*xs, *mods)
    return jnp.concatenate(
        [fn(*[x[i:i + r] for x in xs], *mods) for i in range(0, rows, r)], axis=0)


def _split_dot(x, m_ref):
    hi = x.astype(BF)
    lo = (x - hi.astype(F32)).astype(BF)
    m = m_ref[...]
    return _dot(hi, m) + _dot(lo, m)


def _cast_specs(jobs, steps, slab_of):
    ins, outs, shapes, args = [], [], [], []
    for w, layer in jobs:
        _, r, c = w.shape
        slab = r // steps
        assert slab * steps == r and slab % (2 * SUBLANES) == 0
        ins.append(pl.BlockSpec((1, slab, c), lambda *g, layer=layer: (layer, slab_of(*g), 0)))
        outs.append(pl.BlockSpec((slab, c), lambda *g: (slab_of(*g), 0)))
        shapes.append(jax.ShapeDtypeStruct((r, c), BF))
        args.append(w)
    return ins, outs, shapes, args


def _carry_casts(body, n_in, n_out, n_cast):
    def kernel(*refs):
        a, b = n_in, n_in + n_cast
        c, d = b + n_out, b + n_out + n_cast
        for src, dst in zip(refs[a:b], refs[c:d]):
            dst[...] = src[0].astype(BF)
        body(*refs[:a], *refs[b:c], *refs[d:])
    return kernel


def _const_spec(shape):
    nd = len(shape)
    return pl.BlockSpec(shape, lambda *_: (0,) * nd, pipeline_mode=pl.Buffered(1))


def _params(*sem):
    return pltpu.CompilerParams(dimension_semantics=sem, vmem_limit_bytes=VMEM_LIMIT_BYTES)


def _ada_kernel(c_ref, w_ref, b_ref, o_ref):
    c = c_ref[...]
    sc = (c * jax.nn.sigmoid(c)).astype(BF)
    o_ref[0, 0] = _dot(sc, w_ref[0].astype(BF)) + b_ref[0]


def _ada(c_all, ada_w, ada_b):
    depth, _, n = ada_w.shape
    nb = c_all.shape[0]
    return pl.pallas_call(
        _ada_kernel,
        grid=(depth, n // D),
        in_specs=[
            pl.BlockSpec((nb, D), lambda i, j: (0, 0)),
            pl.BlockSpec((1, D, D), lambda i, j: (i, 0, j)),
            pl.BlockSpec((1, 1, D), lambda i, j: (i, 0, j)),
        ],
        out_specs=pl.BlockSpec((1, 1, nb, D), lambda i, j: (i, j, 0, 0)),
        out_shape=jax.ShapeDtypeStruct((depth, n // D, nb, D), F32),
        compiler_params=_params("arbitrary", "arbitrary"),
        name="ada",
    )(c_all, ada_w, ada_b.reshape(depth, 1, n))


def _mlp_tile(x_ref, mod_ref, g_ref, wu_ref, wd_ref, fg_ref, o_ref, y_ref=None, wo_ref=None):
    def residual(xr, fr, gate):
        return xr + gate * fr

    def norm(xr, shift, scale):
        return _modulate(xr, g_ref[...], shift, scale)

    x = x_ref[0]
    if y_ref is not None:
        x = _by_group(residual, (x, _dot(y_ref[0].astype(BF), wo_ref[...])), (mod_ref[0, 2],))
    hb = _by_group(norm, (x,), (mod_ref[0, 3], mod_ref[0, 4])).astype(BF)
    acc = jnp.zeros(x.shape, F32)
    for c in range(DFF // FF_CHUNK):
        cs = slice(c * FF_CHUNK, (c + 1) * FF_CHUNK)
        hid = jnp.maximum(_dot(hb, wu_ref[:, cs]), 0.0)
        acc = acc + _dot((hid * hid).astype(BF), wd_ref[cs, :])
    x = _by_group(residual, (x, acc), (mod_ref[0, 5],))
    if fg_ref is not None:
        x = _rms(x) * fg_ref[...]
    o_ref[0] = x


def _mlp_kernel(*refs, n_prompt, pre, final):
    refs = list(refs)
    xp_ref, modp_ref, xs_ref = refs[:3]
    refs = refs[3:]
    ys_ref = wo_ref = fg_ref = None
    if pre:
        ys_ref, wo_ref = refs[:2]
        refs = refs[2:]
    mods_ref, g_ref, wu_ref, wd_ref = refs[:4]
    refs = refs[4:]
    if final:
        fg_ref = refs.pop(0)
    xpo_ref, xso_ref = refs
    step = pl.program_id(0)

    @pl.when(step < n_prompt)
    def _():
        _mlp_tile(xp_ref, modp_ref, g_ref, wu_ref, wd_ref, fg_ref, xpo_ref)

    @pl.when(step == n_prompt)
    def _():
        _mlp_tile(xs_ref, mods_ref, g_ref, wu_ref, wd_ref, fg_ref, xso_ref, ys_ref, wo_ref)


def _group_mod_spec(layer, r):
    return pl.BlockSpec((1, N_MOD, r, D), lambda *_: (layer, 0, 0, 0),
                        pipeline_mode=pl.Buffered(1))


def _prompt_tiles(nb, nt):
    def seq_tile(step):
        c = jnp.minimum(step, nb * nt - 1)
        return c // nt, c % nt

    def tok(tm):
        return pl.BlockSpec((1, tm, D), lambda s: (*seq_tile(s), 0))

    def per_seq(shape):
        return pl.BlockSpec((1,) + shape, lambda s: (seq_tile(s)[0],) + (0,) * len(shape))

    return tok, per_seq


def _whole_spec(shape):
    return pl.BlockSpec(shape, lambda *_: (0,) * len(shape))


def _mlp(xp, mod_p, xs, mod, layer, g, w_up, w_down, *, tm, seqs, pre=None, final_g=None,
         casts=()):
    nb, t, _ = xp.shape
    nt = t // tm
    n_prompt = nb * nt
    rows_s = xs.shape[1]
    tok, per_seq = _prompt_tiles(nb, nt)
    args = [xp, mod_p, xs]
    specs = [tok(tm), per_seq((N_MOD, 1, D)), _const_spec((1, rows_s, D))]
    if pre is not None:
        args += list(pre)
        specs += [_const_spec((1, rows_s, D)), _const_spec((D, D))]
    args += [mod, g.reshape(1, D), w_up, w_down]
    specs += [_group_mod_spec(layer, seqs), _const_spec((1, D)), _const_spec((D, DFF)),
              _const_spec((DFF, D))]
    if final_g is not None:
        args.append(final_g.reshape(1, D))
        specs.append(_const_spec((1, D)))
    c_in, c_out, c_shapes, c_args = _cast_specs(
        casts, n_prompt, lambda s: jnp.minimum(s, n_prompt - 1))
    body = functools.partial(_mlp_kernel, n_prompt=n_prompt, pre=pre is not None,
                             final=final_g is not None)
    return pl.pallas_call(
        _carry_casts(body, len(args), 2, len(casts)),
        grid=(n_prompt + 1,),
        in_specs=specs + c_in,
        out_specs=[tok(tm), _whole_spec((1, rows_s, D))] + c_out,
        out_shape=[jax.ShapeDtypeStruct(xp.shape, F32), jax.ShapeDtypeStruct(xs.shape, F32)]
        + c_shapes,
        compiler_params=_params("arbitrary"),
        name="mlp",
    )(*args, *c_args)


def _mlstm_prompt_kernel(x_ref, mod_ref, g_ref, wrow_ref, wcol_ref, bg_ref, gh_ref, wout_ref,
                         xo_ref, s_ref, m_ref, y_scr, *, tm):
    @pl.when(pl.program_id(1) == 0)
    def _():
        s_ref[...] = jnp.zeros(s_ref.shape, F32)
        m_ref[...] = jnp.zeros(m_ref.shape, F32)

    x = x_ref[0]
    hb = _modulate(x, g_ref[...], mod_ref[0, 0], mod_ref[0, 1]).astype(BF)
    n = CHUNK
    ig_all = _dot_nt(wcol_ref[HK:HK + H, :], hb) + bg_ref[0:H]
    fg_all = _dot_nt(wcol_ref[HK + LANES:HK + LANES + H, :], hb) + bg_ref[H:2 * H]
    b_all = _scan_lanes(_log_sigmoid(fg_all), jnp.add, 0.0, n)
    r_all = ig_all - b_all
    cm_all = _scan_lanes(r_all, jnp.maximum, -jnp.inf, n)
    pcol = _dot_nt(wcol_ref[0:HK, :], hb)
    prow = _dot(hb, wrow_ref[...])
    causal = (lax.broadcasted_iota(jnp.int32, (n, n), 0)
              >= lax.broadcasted_iota(jnp.int32, (n, n), 1))
    zeros_k = jnp.zeros((DK, n), BF)
    zeros_s = jnp.zeros((DK, 2 * DV), BF)
    zeros_v = jnp.zeros((n, DV), BF)
    ones_v = jnp.ones((n, DV), BF)
    pad = jnp.zeros((n - 2 * H, n), F32)

    for c in range(tm // n):
        ts = slice(c * n, (c + 1) * n)
        r = r_all[:, ts]
        m0 = m_ref[0]
        mx = jnp.maximum(m0, cm_all[:, ts])
        m = b_all[:, ts] + mx
        e_neg = jnp.exp(-m)
        mx_last = jnp.broadcast_to(mx[:, n - 1:n], (H, n))
        a = jnp.exp(r - mx_last)
        decay = jnp.exp(m0 - mx_last)
        m_ref[0] = jnp.broadcast_to(m[:, n - 1:n], (H, n))
        cols = jnp.concatenate([mx, e_neg, pad], axis=0).T

        for h in range(H):
            lo = h % 2 == 0
            mx_b = jnp.broadcast_to(cols[:, h:h + 1], (n, n))
            floor_b = jnp.broadcast_to(cols[:, H + h:H + h + 1], (n, DV))
            dec = jnp.exp(jnp.where(causal, r[h:h + 1, :] - mx_b, -jnp.inf))
            e_int = jnp.exp(m0[h:h + 1, :] - mx_b)
            qp = prow[ts, (h // 2) * 2 * DK:(h // 2 + 1) * 2 * DK]
            kt = pcol[h * DK:(h + 1) * DK, ts]
            ktb = kt.astype(BF)
            s = _dot(qp.astype(BF),
                     jnp.concatenate([ktb, zeros_k] if lo else [zeros_k, ktb], axis=0)) * dec
            st = s_ref[0, h]
            stb = st.astype(BF)
            v = prow[ts, HK + h * DV:HK + (h + 1) * DV].astype(BF)
            rhs = jnp.concatenate(
                [jnp.concatenate([v, zeros_v], axis=1)]
                + ([stb, zeros_s] if lo else [zeros_s, stb]), axis=0)
            lhs = jnp.concatenate([s.astype(BF), (qp * e_int).astype(BF)], axis=1)
            res = _dot(lhs, rhs)
            den = jnp.sum(s, axis=-1, keepdims=True) + res[:, DV:]
            hh = res[:, :DV] / jnp.maximum(jnp.abs(den), floor_b)
            hh = hh * lax.rsqrt(jnp.mean(hh * hh, axis=-1, keepdims=True) + EPS)
            y_scr[ts, h * DV:(h + 1) * DV] = hh
            ka = (kt * a[h:h + 1, :]).astype(BF)
            dc = decay[h:h + 1, :]
            s_ref[0, h] = (jnp.concatenate([dc, dc], axis=1) * st
                           + _dot(ka, jnp.concatenate([v, ones_v], axis=1)))

    o = prow[:, HK + D:]
    y = (y_scr[...] * gh_ref[...] * jax.nn.sigmoid(o)).astype(BF)
    xo_ref[0] = x + mod_ref[0, 2] * _dot(y, wout_ref[...])


def _mlstm_prompt(x, mod, g, w_row, w_col, b_g, g_head, w_out, *, tm, casts=()):
    nb, t, _ = x.shape
    nt = t // tm
    tok = pl.BlockSpec((1, tm, D), lambda b, i: (b, i, 0))
    args = (x, mod, g.reshape(1, D), w_row, w_col,
            jnp.broadcast_to(b_g.reshape(2 * H, 1), (2 * H, tm)), g_head.reshape(1, D), w_out)
    c_in, c_out, c_shapes, c_args = _cast_specs(casts, nb * nt, lambda b, i: b * nt + i)
    return pl.pallas_call(
        _carry_casts(functools.partial(_mlstm_prompt_kernel, tm=tm), len(args), 3, len(casts)),
        grid=(nb, nt),
        in_specs=[
            tok,
            pl.BlockSpec((1, N_MOD, 1, D), lambda b, i: (b, 0, 0, 0)),
            _const_spec((1, D)),
            _const_spec((D, HK + 2 * D)),
            _const_spec((HK + 2 * LANES, D)),
            _const_spec((2 * H, tm)),
            _const_spec((1, D)),
            _const_spec((D, D)),
        ] + c_in,
        out_specs=[
            tok,
            pl.BlockSpec((1, H, DK, 2 * DV), lambda b, i: (b, 0, 0, 0)),
            pl.BlockSpec((1, H, CHUNK), lambda b, i: (b, 0, 0)),
        ] + c_out,
        out_shape=[
            jax.ShapeDtypeStruct(x.shape, F32),
            jax.ShapeDtypeStruct((nb, H, DK, 2 * DV), F32),
            jax.ShapeDtypeStruct((nb, H, CHUNK), F32),
        ] + c_shapes,
        scratch_shapes=[pltpu.VMEM((tm, D), F32)],
        compiler_params=_params("arbitrary", "arbitrary"),
        name="mlstm_prompt",
    )(*args, *c_args)


def _mlstm_sample_pre_kernel(x_ref, mod_ref, g_ref, wrow_ref, wcol_ref, bg_ref, m0_ref, n0_ref,
                             gh_ref, seg_ref, e64_ref, e128_ref,
                             q_ref, ka_ref, v_ref, dec_ref, numi_ref, rd_ref, eint_ref, go_ref,
                             n1_ref, m1_ref, *, t, nb):
    def norm(xr, shift, scale):
        return _modulate(xr, g_ref[...], shift, scale)

    hb = _by_group(norm, (x_ref[0],), (mod_ref[0, 0], mod_ref[0, 1])).astype(BF)
    p = _dot(hb, wrow_ref[...])
    k_all = _dot_nt(hb, wcol_ref[0:HK, :])
    ig_all = _dot_nt(hb, wcol_ref[HK:HK + LANES, :]) + bg_ref[0:1]
    fg_all = _dot_nt(hb, wcol_ref[HK + LANES:HK + 2 * LANES, :]) + bg_ref[1:2]

    def rows(i):
        return slice(i * nb, (i + 1) * nb)

    q = [p[rows(i), 0:HK] for i in range(t)]
    k = [k_all[rows(i)] for i in range(t)]
    v = [p[rows(i), HK:HK + D] for i in range(t)]
    m0 = m0_ref[...]
    n0 = n0_ref[...]

    b, r, mx = [], [], []
    for i in range(t):
        lf = _log_sigmoid(fg_all[rows(i)])
        b.append(lf if i == 0 else b[-1] + lf)
        r.append(ig_all[rows(i)] - b[i])
        cm = r[i] if i == 0 else jnp.maximum(cm, r[i])
        mx.append(jnp.maximum(m0, cm))
    mx_last = mx[t - 1]
    decay = jnp.exp(m0 - mx_last)
    m1_ref[...] = b[t - 1] + mx_last
    dec_x = _split_dot(decay, e128_ref)
    n1 = _split_dot(decay, e64_ref) * n0

    for i in range(t):
        ka = k[i] * _split_dot(jnp.exp(r[i] - mx_last), e64_ref)
        n1 = n1 + ka
        seq_rows = pl.ds(i, nb, stride=t)
        for c in range(HK // LANES):
            cs = slice(c * LANES, (c + 1) * LANES)
            ka_ref[c, seq_rows, :] = ka[:, cs]
            q_ref[c, seq_rows, :] = q[i][:, cs]
        for c in range(D // LANES):
            cs = slice(c * LANES, (c + 1) * LANES)
            v_ref[c, seq_rows, :] = v[i][:, cs]
            dec_ref[c, seq_rows, :] = dec_x[:, cs]
        e_int = jnp.exp(m0 - mx[i])
        den = e_int * _split_dot(q[i] * n0, seg_ref)
        numi = jnp.zeros((nb, D), F32)
        for s in range(i + 1):
            w = _split_dot(q[i] * k[s], seg_ref) * jnp.exp(jnp.minimum(r[s] - mx[i], 0.0))
            den = den + w
            numi = numi + _split_dot(w, e128_ref) * v[s]
        numi_ref[rows(i), :] = numi
        rd_ref[rows(i), :] = 1.0 / jnp.maximum(jnp.abs(den), jnp.exp(-(b[i] + mx[i])))
        eint_ref[rows(i), :] = e_int
        go_ref[rows(i), :] = jax.nn.sigmoid(p[rows(i), HK + D:]) * gh_ref[...]
    n1_ref[...] = n1


def _mlstm_sample_pre(x, mod, layer, g, w_row, w_col, b_g, m0, n0, g_head, seg64, e64, e128, *, t):
    rows = x.shape[1]
    nb = rows // t

    def whole(shape):
        return pl.BlockSpec(shape, lambda i: (0,) * len(shape))

    outs = [(HK // LANES, rows, LANES), (HK // LANES, rows, LANES), (D // LANES, rows, LANES),
            (D // LANES, rows, LANES), (rows, D), (rows, FG), (rows, FG), (rows, D), (nb, HK),
            (nb, FG)]
    return pl.pallas_call(
        functools.partial(_mlstm_sample_pre_kernel, t=t, nb=nb),
        grid=(1,),
        in_specs=[
            whole((1, rows, D)),
            _group_mod_spec(layer, nb),
            _const_spec((1, D)),
            _const_spec((D, HK + 2 * D)),
            _const_spec((HK + 2 * LANES, D)),
            _const_spec((2, LANES)),
            whole((nb, FG)),
            whole((nb, HK)),
            _const_spec((1, D)),
            _const_spec((HK, FG)),
            _const_spec((FG, HK)),
            _const_spec((FG, D)),
        ],
        out_specs=[whole(s) for s in outs],
        out_shape=[jax.ShapeDtypeStruct(s, F32) for s in outs],
        compiler_params=_params("arbitrary"),
        name="mlstm_sample_pre",
    )(x, mod, g.reshape(1, D), w_row, w_col, b_g, m0, n0, g_head.reshape(1, D), seg64, e64, e128)


def _mlstm_sample_state_kernel(q_ref, ka_ref, v_ref, dec_ref, c0_ref, inter_ref, c1_ref, *, bb, t):
    per = SUBLANES // t
    seq_of_row = lax.broadcasted_iota(jnp.int32, (SUBLANES, 1), 0) // t
    zeros_c = jnp.zeros((DK, DV), BF)

    def tile(i, carry):
        rs = pl.ds(pl.multiple_of(i * SUBLANES, SUBLANES), SUBLANES)
        for h in range(H):
            lo = h % 2 == 0
            half = slice((h % 2) * DK, (h % 2 + 1) * DK)
            qp = q_ref[h // 2, rs, :].astype(BF)
            kap = ka_ref[h // 2, rs, :]
            vh = v_ref[h, rs, :].astype(BF)
            dech = dec_ref[h, rs, :]
            res = None
            for w in range(per):
                seq = i * per + w
                c0 = c0_ref[seq, h]
                c0b = c0.astype(BF)
                rw = _dot(qp, jnp.concatenate([c0b, zeros_c] if lo else [zeros_c, c0b], axis=0))
                res = rw if res is None else jnp.where(seq_of_row == w, rw, res)
                kaw = jnp.where(seq_of_row == w, kap, 0.0).astype(BF)
                c1_ref[seq, h] = dech[w * t:w * t + 1, :] * c0 + _dot_tn(kaw, vh)[half, :]
            inter_ref[h, rs, :] = res
        return carry

    lax.fori_loop(0, bb // per, tile, 0)


def _mlstm_sample_state(q, ka, v, dec, c0, *, t, bb):
    rows = q.shape[1]
    nb = rows // t
    assert SUBLANES % t == 0 and bb % (SUBLANES // t) == 0 and nb % bb == 0

    def tok(blocks):
        return pl.BlockSpec((blocks, bb * t, LANES), lambda i: (0, i, 0))

    state = pl.BlockSpec((bb, H, DK, DV), lambda i: (i, 0, 0, 0))
    return pl.pallas_call(
        functools.partial(_mlstm_sample_state_kernel, bb=bb, t=t),
        grid=(nb // bb,),
        in_specs=[tok(HK // LANES), tok(HK // LANES), tok(H), tok(H), state],
        out_specs=[tok(H), state],
        out_shape=[jax.ShapeDtypeStruct((H, rows, LANES), F32),
                   jax.ShapeDtypeStruct(c0.shape, F32)],
        compiler_params=_params("arbitrary"),
        name="mlstm_sample_state",
    )(q, ka, v, dec, c0)


def _mlstm_sample_post_kernel(inter_ref, numi_ref, rd_ref, eint_ref, go_ref, e128_ref, seg_ref,
                              y_ref, *, t, nb):
    for i in range(t):
        rs = slice(i * nb, (i + 1) * nb)
        inter = jnp.concatenate(
            [inter_ref[h, pl.ds(i, nb, stride=t), :] for h in range(H)], axis=1)
        num = numi_ref[rs, :] + _split_dot(eint_ref[rs, :], e128_ref) * inter
        hh = num * _split_dot(rd_ref[rs, :], e128_ref)
        ms = _split_dot(hh * hh, seg_ref) * (1.0 / DV)
        y_ref[rs, :] = hh * _split_dot(lax.rsqrt(ms + EPS), e128_ref) * go_ref[rs, :]


def _mlstm_sample_post(inter, numi, rd, eint, go, e128, seg128, *, t):
    rows = inter.shape[1]

    def whole(shape):
        return pl.BlockSpec(shape, lambda i: (0,) * len(shape))

    return pl.pallas_call(
        functools.partial(_mlstm_sample_post_kernel, t=t, nb=rows // t),
        grid=(1,),
        in_specs=[whole((H, rows, LANES)), whole((rows, D)), whole((rows, FG)), whole((rows, FG)),
                  whole((rows, D)), _const_spec((FG, D)), _const_spec((D, FG))],
        out_specs=whole((rows, D)),
        out_shape=jax.ShapeDtypeStruct((rows, D), F32),
        compiler_params=_params("arbitrary"),
        name="mlstm_sample_post",
    )(inter, numi, rd, eint, go, e128, seg128)


def _conv_taps(u, prev2, prev1, wc_ref):
    return wc_ref[0:1] * prev2 + wc_ref[1:2] * prev1 + wc_ref[2:3] * u


def _conv_prompt_tile(x_ref, mod_ref, g_ref, win_ref, wc_ref, wout_ref, xo_ref, st_ref, carry_scr,
                      first):
    @pl.when(first)
    def _():
        carry_scr[...] = jnp.zeros(carry_scr.shape, F32)

    x = x_ref[0]
    tm = x.shape[0]
    hb = _modulate(x, g_ref[...], mod_ref[0, 0], mod_ref[0, 1]).astype(BF)
    u = _dot(hb, win_ref[:, D:2 * D]) * _dot(hb, win_ref[:, 2 * D:])
    row = lax.broadcasted_iota(jnp.int32, (tm, D), 0)
    c0 = carry_scr[0:1]
    c1 = carry_scr[1:2]
    prev1 = jnp.where(row == 0, c1, pltpu.roll(u, 1, 0))
    prev2 = jnp.where(row == 0, c0, jnp.where(row == 1, c1, pltpu.roll(u, 2, 0)))
    y = _conv_taps(u, prev2, prev1, wc_ref)
    bg = _dot(hb, win_ref[:, :D])
    xo_ref[0] = x + mod_ref[0, 2] * _dot((bg * y).astype(BF), wout_ref[...])
    carry_scr[...] = u[tm - 2:tm]
    st_ref[0] = u[tm - 2:tm]


def _conv_sample_tile(x_ref, mod_ref, g_ref, win_ref, wc_ref, wout_ref, buf_ref, xo_ref, st_ref,
                      *, t, nb):
    def norm(xr, shift, scale):
        return _modulate(xr, g_ref[...], shift, scale)

    def residual(xr, fr, gate):
        return xr + gate * fr

    x = x_ref[0]
    hb = _by_group(norm, (x,), (mod_ref[0, 0], mod_ref[0, 1])).astype(BF)
    u = _dot(hb, win_ref[:, D:2 * D]) * _dot(hb, win_ref[:, 2 * D:])
    full = [buf_ref[j] for j in range(CONV_W - 1)] + [u[i * nb:(i + 1) * nb] for i in range(t)]
    y = jnp.concatenate([_conv_taps(full[i + 2], full[i], full[i + 1], wc_ref) for i in range(t)],
                        axis=0)
    bg = _dot(hb, win_ref[:, :D])
    out = _dot((bg * y).astype(BF), wout_ref[...])
    xo_ref[0] = _by_group(residual, (x, out), (mod_ref[0, 2],))
    for j in range(CONV_W - 1):
        st_ref[j] = full[t + j]


def _conv_kernel(xp_ref, modp_ref, xs_ref, mods_ref, g_ref, win_ref, wc_ref, wout_ref, buf_ref,
                 xpo_ref, stp_ref, xso_ref, sts_ref, carry_scr, *, n_prompt, nt, t, nb):
    step = pl.program_id(0)

    @pl.when(step < n_prompt)
    def _():
        _conv_prompt_tile(xp_ref, modp_ref, g_ref, win_ref, wc_ref, wout_ref, xpo_ref, stp_ref,
                          carry_scr, step % nt == 0)

    @pl.when(step == n_prompt)
    def _():
        _conv_sample_tile(xs_ref, mods_ref, g_ref, win_ref, wc_ref, wout_ref, buf_ref, xso_ref,
                          sts_ref, t=t, nb=nb)


def _conv(xp, mod_p, xs, mod, layer, g, w_in, w_conv, w_out, buf, *, tm, t, casts=()):
    nb, tp, _ = xp.shape
    nt = tp // tm
    n_prompt = nb * nt
    rows_s = xs.shape[1]
    seqs = rows_s // t
    tok, per_seq = _prompt_tiles(nb, nt)
    args = (xp, mod_p, xs, mod, g.reshape(1, D), w_in, w_conv, w_out, buf)
    c_in, c_out, c_shapes, c_args = _cast_specs(
        casts, n_prompt, lambda s: jnp.minimum(s, n_prompt - 1))
    body = functools.partial(_conv_kernel, n_prompt=n_prompt, nt=nt, t=t, nb=seqs)
    return pl.pallas_call(
        _carry_casts(body, len(args), 4, len(casts)),
        grid=(n_prompt + 1,),
        in_specs=[
            tok(tm),
            per_seq((N_MOD, 1, D)),
            _const_spec((1, rows_s, D)),
            _group_mod_spec(layer, seqs),
            _const_spec((1, D)),
            _const_spec((D, 3 * D)),
            _const_spec((CONV_W, D)),
            _const_spec((D, D)),
            _const_spec((CONV_W - 1, seqs, D)),
        ] + c_in,
        out_specs=[tok(tm), per_seq((CONV_W - 1, D)), _whole_spec((1, rows_s, D)),
                   _whole_spec((CONV_W - 1, seqs, D))] + c_out,
        out_shape=[jax.ShapeDtypeStruct(xp.shape, F32),
                   jax.ShapeDtypeStruct((nb, CONV_W - 1, D), F32),
                   jax.ShapeDtypeStruct(xs.shape, F32),
                   jax.ShapeDtypeStruct((CONV_W - 1, seqs, D), F32)] + c_shapes,
        scratch_shapes=[pltpu.VMEM((CONV_W - 1, D), F32)],
        compiler_params=_params("arbitrary"),
        name="conv",
    )(*args, *c_args)


def kernel(x_prompt, x_sample, c_prompt, c_sample, state_mlstm_C, state_mlstm_n, state_mlstm_m,
           state_conv, ada_w, ada_b, norm_g, final_norm_g, mlstm_w_in, mlstm_b_gates,
           mlstm_head_g, mlstm_w_out, conv_w_in, conv_w, conv_w_out, mlp_w_up, mlp_w_down):
    assert ada_w.shape[0] == 2 and mlstm_w_in.shape[0] == 1 and conv_w_in.shape[0] == 1
    bp, tp, _ = x_prompt.shape
    bs, ts, _ = x_sample.shape
    assert ts == 4 and tp % 512 == 0

    mod = _ada(jnp.concatenate([c_sample, c_prompt], axis=0), ada_w, ada_b)
    mod_p = jnp.transpose(mod[:, :, bs:], (0, 2, 1, 3))[:, :, :, None]

    w_in0 = mlstm_w_in[0]
    w_row = jnp.concatenate([w_in0[:, :HK], w_in0[:, 2 * HK:NP]], axis=1).astype(BF)
    gate_pad = jnp.zeros((D, LANES - H), F32)
    w_col = jnp.concatenate([w_in0[:, HK:2 * HK] * (DK ** -0.5), w_in0[:, NP:NP + H], gate_pad,
                             w_in0[:, NP + H:], gate_pad], axis=1).T.astype(BF)
    b_g = jnp.pad(mlstm_b_gates[0].reshape(2, H), ((0, 0), (0, LANES - H)))
    wout0 = mlstm_w_out[0].astype(BF)

    x, s_p, m_p, up0, down0 = _mlstm_prompt(
        x_prompt, mod_p[0], norm_g[0, 0], w_row, w_col, mlstm_b_gates[0], mlstm_head_g[0], wout0,
        tm=512, casts=((mlp_w_up, 0), (mlp_w_down, 0)))
    prompt_c = s_p[None, :, :, :, :DV]
    prompt_n = s_p[None, :, :, :, DV]
    prompt_m = m_p[None, :, :, 0]

    xs = jnp.transpose(x_sample, (1, 0, 2)).reshape(1, ts * bs, D)
    head_of_v = jnp.arange(D, dtype=jnp.int32) // DV
    head_of_k = jnp.arange(HK, dtype=jnp.int32) // DK
    lanes = jnp.arange(FG, dtype=jnp.int32)
    e128 = (lanes[:, None] == head_of_v[None, :]).astype(BF)
    e64 = (lanes[:, None] == head_of_k[None, :]).astype(BF)
    m0 = jnp.pad(state_mlstm_m[0], ((0, 0), (0, FG - H)))
    n0 = state_mlstm_n[0].reshape(bs, HK)
    q, ka, v, dec, numi, rd, eint, go, n1, m1 = _mlstm_sample_pre(
        xs, mod, 0, norm_g[0, 0], w_row, w_col, b_g, m0, n0, mlstm_head_g[0], e64.T, e64, e128,
        t=ts)
    inter, c1 = _mlstm_sample_state(q, ka, v, dec, state_mlstm_C[0], t=ts, bb=16)
    y = _mlstm_sample_post(inter, numi, rd, eint, go, e128, e128.T, t=ts)
    sample_n = n1.reshape(bs, H, DK)[None]
    sample_m = m1[None, :, :H]

    x, xs, cwin, cwout = _mlp(x, mod_p[0], xs, mod, 0, norm_g[0, 1], up0, down0, tm=512, seqs=bs,
                              pre=(y.reshape(1, ts * bs, D), wout0),
                              casts=((conv_w_in, 0), (conv_w_out, 0)))
    x, conv_p, xs, conv_s, up1, down1 = _conv(
        x, mod_p[1], xs, mod, 1, norm_g[1, 0], cwin, conv_w[0], cwout,
        jnp.transpose(state_conv[0], (1, 0, 2)), tm=512, t=ts,
        casts=((mlp_w_up, 1), (mlp_w_down, 1)))
    y_prompt, ys = _mlp(x, mod_p[1], xs, mod, 1, norm_g[1, 1], up1, down1, tm=512, seqs=bs,
                        final_g=final_norm_g)
    prompt_conv = conv_p[None]
    y_sample = jnp.transpose(ys.reshape(ts, bs, D), (1, 0, 2))
    sample_conv = jnp.transpose(conv_s, (1, 0, 2))[None]

    return (y_prompt, y_sample, prompt_c, prompt_n, prompt_m, prompt_conv,
            c1[None], sample_n, sample_m, sample_conv)
```

```python
import functools

import jax
import jax.numpy as jnp
from jax import lax
from jax.experimental import pallas as pl
from jax.experimental.pallas import tpu as pltpu

D = 1024
H = 8
DK = 64
DV = 128
HK = H * DK
NP = 2 * HK + 2 * D
FG = 128
LANES = 128
SUBLANES = 8
DFF = 4 * D
N_MOD = 6
CONV_W = 3
EPS = 1e-6
CHUNK = 128
FF_CHUNK = 1024
BF = jnp.bfloat16
F32 = jnp.float32
VMEM_LIMIT_BYTES = 56 * 1024 * 1024


def _dot(a, b):
    return jnp.dot(a, b, preferred_element_type=F32)


def _dot_nt(a, b):
    return lax.dot_general(a, b, (((1,), (1,)), ((), ())), preferred_element_type=F32)


def _dot_tn(a, b):
    return lax.dot_general(a, b, (((0,), (0,)), ((), ())), preferred_element_type=F32)


def _rms(x):
    return x * lax.rsqrt(jnp.mean(x * x, axis=-1, keepdims=True) + EPS)


def _modulate(x, g, shift, scale):
    return _rms(x) * g * (1.0 + scale) + shift


def _log_sigmoid(x):
    return jnp.minimum(x, 0.0) - jnp.log1p(jnp.exp(-jnp.abs(x)))


def _scan_lanes(x, op, ident, n):
    lane = lax.broadcasted_iota(jnp.int32, x.shape, 1) % n
    d = 1
    while d < n:
        x = op(x, jnp.where(lane >= d, pltpu.roll(x, d, 1), ident))
        d *= 2
    return x


def _by_group(fn, xs, mods):
    rows, r = xs[0].shape[0], mods[0].shape[0]
    if r == 1 or r == rows:
        return fn(*xs, *mods)
    return jnp.concatenate(
        [fn(*[x[i:i + r] for x in xs], *mods) for i in range(0, rows, r)], axis=0)


def _split_dot(x, m_ref):
    hi = x.astype(BF)
    lo = (x - hi.astype(F32)).astype(BF)
    m = m_ref[...]
    return _dot(hi, m) + _dot(lo, m)


def _cast_specs(jobs, steps, slab_of):
    ins, outs, shapes, args = [], [], [], []
    for w, layer in jobs:
        _, r, c = w.shape
        slab = r // steps
        assert slab * steps == r and slab % (2 * SUBLANES) == 0
        ins.append(pl.BlockSpec((1, slab, c), lambda *g, layer=layer: (layer, slab_of(*g), 0)))
        outs.append(pl.BlockSpec((slab, c), lambda *g: (slab_of(*g), 0)))
        shapes.append(jax.ShapeDtypeStruct((r, c), BF))
        args.append(w)
    return ins, outs, shapes, args


def _carry_casts(body, n_in, n_out, n_cast):
    def kernel(*refs):
        a, b = n_in, n_in + n_cast
        c, d = b + n_out, b + n_out + n_cast
        for src, dst in zip(refs[a:b], refs[c:d]):
            dst[...] = src[0].astype(BF)
        body(*refs[:a], *refs[b:c], *refs[d:])
    return kernel


def _const_spec(shape):
    nd = len(shape)
    return pl.BlockSpec(shape, lambda *_: (0,) * nd, pipeline_mode=pl.Buffered(1))


def _params(*sem):
    return pltpu.CompilerParams(dimension_semantics=sem, vmem_limit_bytes=VMEM_LIMIT_BYTES)


def _ada_kernel(c_ref, w_ref, b_ref, o_ref):
    k = pl.program_id(1)
    c = c_ref[...]
    sc = (c * jax.nn.sigmoid(c)).astype(BF)
    for j in range(N_MOD):
        cs = slice(j * D, (j + 1) * D)
        part = _dot(sc, w_ref[0, :, cs].astype(BF))

        @pl.when(k == 0)
        def _(j=j, cs=cs, part=part):
            o_ref[0, j] = part + b_ref[0, :, cs]

        @pl.when(k > 0)
        def _(j=j, part=part):
            o_ref[0, j] = o_ref[0, j] + part


def _ada(c_all, ada_w, ada_b, *, slab=256):
    depth, d_in, n = ada_w.shape
    nb = c_all.shape[0]
    return pl.pallas_call(
        _ada_kernel,
        grid=(depth, d_in // slab),
        in_specs=[
            pl.BlockSpec((nb, slab), lambda i, k: (0, k)),
            pl.BlockSpec((1, slab, n), lambda i, k: (i, k, 0)),
            pl.BlockSpec((1, 1, n), lambda i, k: (i, 0, 0)),
        ],
        out_specs=pl.BlockSpec((1, n // D, nb, D), lambda i, k: (i, 0, 0, 0)),
        out_shape=jax.ShapeDtypeStruct((depth, n // D, nb, D), F32),
        compiler_params=_params("arbitrary", "arbitrary"),
        name="ada",
    )(c_all, ada_w, ada_b.reshape(depth, 1, n))


def _mlp_tile(x_ref, mod_ref, g_ref, wu_ref, wd_ref, fg_ref, o_ref, y_ref=None, wo_ref=None):
    def residual(xr, fr, gate):
        return xr + gate * fr

    def norm(xr, shift, scale):
        return _modulate(xr, g_ref[...], shift, scale)

    x = x_ref[0]
    if y_ref is not None:
        x = _by_group(residual, (x, _dot(y_ref[0].astype(BF), wo_ref[...])), (mod_ref[0, 2],))
    hb = _by_group(norm, (x,), (mod_ref[0, 3], mod_ref[0, 4])).astype(BF)
    acc = jnp.zeros(x.shape, F32)
    for c in range(DFF // FF_CHUNK):
        cs = slice(c * FF_CHUNK, (c + 1) * FF_CHUNK)
        hid = jnp.maximum(_dot(hb, wu_ref[:, cs]), 0.0)
        acc = acc + _dot((hid * hid).astype(BF), wd_ref[cs, :])
    x = _by_group(residual, (x, acc), (mod_ref[0, 5],))
    if fg_ref is not None:
        x = _rms(x) * fg_ref[...]
    o_ref[0] = x


def _mlp_kernel(*refs, n_prompt, pre, final):
    refs = list(refs)
    xp_ref, modp_ref, xs_ref = refs[:3]
    refs = refs[3:]
    ys_ref = wo_ref = fg_ref = None
    if pre:
        ys_ref, wo_ref = refs[:2]
        refs = refs[2:]
    mods_ref, g_ref, wu_ref, wd_ref = refs[:4]
    refs = refs[4:]
    if final:
        fg_ref = refs.pop(0)
    xpo_ref, xso_ref = refs
    step = pl.program_id(0)

    @pl.when(step < n_prompt)
    def _():
        _mlp_tile(xp_ref, modp_ref, g_ref, wu_ref, wd_ref, fg_ref, xpo_ref)

    @pl.when(step == n_prompt)
    def _():
        _mlp_tile(xs_ref, mods_ref, g_ref, wu_ref, wd_ref, fg_ref, xso_ref, ys_ref, wo_ref)


def _group_mod_spec(layer, r):
    return pl.BlockSpec((1, N_MOD, r, D), lambda *_: (layer, 0, 0, 0),
                        pipeline_mode=pl.Buffered(1))


def _prompt_tiles(nb, nt):
    def seq_tile(step):
        c = jnp.minimum(step, nb * nt - 1)
        return c // nt, c % nt

    def tok(tm):
        return pl.BlockSpec((1, tm, D), lambda s: (*seq_tile(s), 0))

    def per_seq(shape):
        return pl.BlockSpec((1,) + shape, lambda s: (seq_tile(s)[0],) + (0,) * len(shape))

    return tok, per_seq


def _whole_spec(shape):
    return pl.BlockSpec(shape, lambda *_: (0,) * len(shape))


def _mlp(xp, mod_p, xs, mod, layer, g, w_up, w_down, *, tm, seqs, pre=None, final_g=None,
         casts=()):
    nb, t, _ = xp.shape
    nt = t // tm
    n_prompt = nb * nt
    rows_s = xs.shape[1]
    tok, per_seq = _prompt_tiles(nb, nt)
    args = [xp, mod_p, xs]
    specs = [tok(tm), per_seq((N_MOD, 1, D)), _const_spec((1, rows_s, D))]
    if pre is not None:
        args += list(pre)
        specs += [_const_spec((1, rows_s, D)), _const_spec((D, D))]
    args += [mod, g.reshape(1, D), w_up, w_down]
    specs += [_group_mod_spec(layer, seqs), _const_spec((1, D)), _const_spec((D, DFF)),
              _const_spec((DFF, D))]
    if final_g is not None:
        args.append(final_g.reshape(1, D))
        specs.append(_const_spec((1, D)))
    c_in, c_out, c_shapes, c_args = _cast_specs(
        casts, n_prompt, lambda s: jnp.minimum(s, n_prompt - 1))
    body = functools.partial(_mlp_kernel, n_prompt=n_prompt, pre=pre is not None,
                             final=final_g is not None)
    return pl.pallas_call(
        _carry_casts(body, len(args), 2, len(casts)),
        grid=(n_prompt + 1,),
        in_specs=specs + c_in,
        out_specs=[tok(tm), _whole_spec((1, rows_s, D))] + c_out,
        out_shape=[jax.ShapeDtypeStruct(xp.shape, F32), jax.ShapeDtypeStruct(xs.shape, F32)]
        + c_shapes,
        compiler_params=_params("arbitrary"),
        name="mlp",
    )(*args, *c_args)


def _mlstm_prompt_kernel(x_ref, mod_ref, g_ref, wrow_ref, wcol_ref, bg_ref, gh_ref, wout_ref,
                         xo_ref, s_ref, m_ref, y_scr, *, tm):
    @pl.when(pl.program_id(1) == 0)
    def _():
        s_ref[...] = jnp.zeros(s_ref.shape, F32)
        m_ref[...] = jnp.zeros(m_ref.shape, F32)

    x = x_ref[0]
    hb = _modulate(x, g_ref[...], mod_ref[0, 0], mod_ref[0, 1]).astype(BF)
    n = CHUNK
    ig_all = _dot_nt(wcol_ref[HK:HK + H, :], hb) + bg_ref[0:H]
    fg_all = _dot_nt(wcol_ref[HK + LANES:HK + LANES + H, :], hb) + bg_ref[H:2 * H]
    b_all = _scan_lanes(_log_sigmoid(fg_all), jnp.add, 0.0, n)
    r_all = ig_all - b_all
    cm_all = _scan_lanes(r_all, jnp.maximum, -jnp.inf, n)
    pcol = _dot_nt(wcol_ref[0:HK, :], hb)
    prow = _dot(hb, wrow_ref[...])
    causal = (lax.broadcasted_iota(jnp.int32, (n, n), 0)
              >= lax.broadcasted_iota(jnp.int32, (n, n), 1))
    zeros_k = jnp.zeros((DK, n), BF)
    zeros_s = jnp.zeros((DK, 2 * DV), BF)
    zeros_v = jnp.zeros((n, DV), BF)
    ones_v = jnp.ones((n, DV), BF)
    pad = jnp.zeros((n - 2 * H, n), F32)

    for c in range(tm // n):
        ts = slice(c * n, (c + 1) * n)
        r = r_all[:, ts]
        m0 = m_ref[0]
        mx = jnp.maximum(m0, cm_all[:, ts])
        m = b_all[:, ts] + mx
        e_neg = jnp.exp(-m)
        mx_last = jnp.broadcast_to(mx[:, n - 1:n], (H, n))
        a = jnp.exp(r - mx_last)
        decay = jnp.exp(m0 - mx_last)
        m_ref[0] = jnp.broadcast_to(m[:, n - 1:n], (H, n))
        cols = jnp.concatenate([mx, e_neg, pad], axis=0).T

        for h in range(H):
            lo = h % 2 == 0
            mx_b = jnp.broadcast_to(cols[:, h:h + 1], (n, n))
            floor_b = jnp.broadcast_to(cols[:, H + h:H + h + 1], (n, DV))
            dec = jnp.exp(jnp.where(causal, r[h:h + 1, :] - mx_b, -jnp.inf))
            e_int = jnp.exp(m0[h:h + 1, :] - mx_b)
            qp = prow[ts, (h // 2) * 2 * DK:(h // 2 + 1) * 2 * DK]
            kt = pcol[h * DK:(h + 1) * DK, ts]
            ktb = kt.astype(BF)
            s = _dot(qp.astype(BF),
                     jnp.concatenate([ktb, zeros_k] if lo else [zeros_k, ktb], axis=0)) * dec
            st = s_ref[0, h]
            stb = st.astype(BF)
            v = prow[ts, HK + h * DV:HK + (h + 1) * DV].astype(BF)
            rhs = jnp.concatenate(
                [jnp.concatenate([v, zeros_v], axis=1)]
                + ([stb, zeros_s] if lo else [zeros_s, stb]), axis=0)
            lhs = jnp.concatenate([s.astype(BF), (qp * e_int).astype(BF)], axis=1)
            res = _dot(lhs, rhs)
            den = jnp.sum(s, axis=-1, keepdims=True) + res[:, DV:]
            hh = res[:, :DV] / jnp.maximum(jnp.abs(den), floor_b)
            hh = hh * lax.rsqrt(jnp.mean(hh * hh, axis=-1, keepdims=True) + EPS)
            y_scr[ts, h * DV:(h + 1) * DV] = hh
            ka = (kt * a[h:h + 1, :]).astype(BF)
            dc = decay[h:h + 1, :]
            s_ref[0, h] = (jnp.concatenate([dc, dc], axis=1) * st
                           + _dot(ka, jnp.concatenate([v, ones_v], axis=1)))

    o = prow[:, HK + D:]
    y = (y_scr[...] * gh_ref[...] * jax.nn.sigmoid(o)).astype(BF)
    xo_ref[0] = x + mod_ref[0, 2] * _dot(y, wout_ref[...])


def _mlstm_prompt(x, mod, g, w_row, w_col, b_g, g_head, w_out, *, tm, casts=()):
    nb, t, _ = x.shape
    nt = t // tm
    tok = pl.BlockSpec((1, tm, D), lambda b, i: (b, i, 0))
    args = (x, mod, g.reshape(1, D), w_row, w_col,
            jnp.broadcast_to(b_g.reshape(2 * H, 1), (2 * H, tm)), g_head.reshape(1, D), w_out)
    c_in, c_out, c_shapes, c_args = _cast_specs(casts, nb * nt, lambda b, i: b * nt + i)
    return pl.pallas_call(
        _carry_casts(functools.partial(_mlstm_prompt_kernel, tm=tm), len(args), 3, len(casts)),
        grid=(nb, nt),
        in_specs=[
            tok,
            pl.BlockSpec((1, N_MOD, 1, D), lambda b, i: (b, 0, 0, 0)),
            _const_spec((1, D)),
            _const_spec((D, HK + 2 * D)),
            _const_spec((HK + 2 * LANES, D)),
            _const_spec((2 * H, tm)),
            _const_spec((1, D)),
            _const_spec((D, D)),
        ] + c_in,
        out_specs=[
            tok,
            pl.BlockSpec((1, H, DK, 2 * DV), lambda b, i: (b, 0, 0, 0)),
            pl.BlockSpec((1, H, CHUNK), lambda b, i: (b, 0, 0)),
        ] + c_out,
        out_shape=[
            jax.ShapeDtypeStruct(x.shape, F32),
            jax.ShapeDtypeStruct((nb, H, DK, 2 * DV), F32),
            jax.ShapeDtypeStruct((nb, H, CHUNK), F32),
        ] + c_shapes,
        scratch_shapes=[pltpu.VMEM((tm, D), F32)],
        compiler_params=_params("arbitrary", "arbitrary"),
        name="mlstm_prompt",
    )(*args, *c_args)


def _mlstm_sample_pre_kernel(x_ref, mod_ref, g_ref, wrow_ref, wcol_ref, bg_ref, m0_ref, n0_ref,
                             gh_ref, seg_ref, e64_ref, e128_ref,
                             q_ref, ka_ref, v_ref, dec_ref, numi_ref, rd_ref, eint_ref, go_ref,
                             n1_ref, m1_ref, *, t, nb):
    def norm(xr, shift, scale):
        return _modulate(xr, g_ref[...], shift, scale)

    hb = _by_group(norm, (x_ref[0],), (mod_ref[0, 0], mod_ref[0, 1])).astype(BF)
    p = _dot(hb, wrow_ref[...])
    k_all = _dot_nt(hb, wcol_ref[0:HK, :])
    ig_all = _dot_nt(hb, wcol_ref[HK:HK + LANES, :]) + bg_ref[0:1]
    fg_all = _dot_nt(hb, wcol_ref[HK + LANES:HK + 2 * LANES, :]) + bg_ref[1:2]

    def rows(i):
        return slice(i * nb, (i + 1) * nb)

    q = [p[rows(i), 0:HK] for i in range(t)]
    k = [k_all[rows(i)] for i in range(t)]
    v = [p[rows(i), HK:HK + D] for i in range(t)]
    m0 = m0_ref[...]
    n0 = n0_ref[...]

    b, r, mx = [], [], []
    for i in range(t):
        lf = _log_sigmoid(fg_all[rows(i)])
        b.append(lf if i == 0 else b[-1] + lf)
        r.append(ig_all[rows(i)] - b[i])
        cm = r[i] if i == 0 else jnp.maximum(cm, r[i])
        mx.append(jnp.maximum(m0, cm))
    mx_last = mx[t - 1]
    decay = jnp.exp(m0 - mx_last)
    m1_ref[...] = b[t - 1] + mx_last
    dec_x = _split_dot(decay, e128_ref)
    n1 = _split_dot(decay, e64_ref) * n0

    for i in range(t):
        ka = k[i] * _split_dot(jnp.exp(r[i] - mx_last), e64_ref)
        n1 = n1 + ka
        seq_rows = pl.ds(i, nb, stride=t)
        for c in range(HK // LANES):
            cs = slice(c * LANES, (c + 1) * LANES)
            ka_ref[c, seq_rows, :] = ka[:, cs]
            q_ref[c, seq_rows, :] = q[i][:, cs]
        for c in range(D // LANES):
            cs = slice(c * LANES, (c + 1) * LANES)
            v_ref[c, seq_rows, :] = v[i][:, cs]
            dec_ref[c, seq_rows, :] = dec_x[:, cs]
        e_int = jnp.exp(m0 - mx[i])
        den = e_int * _split_dot(q[i] * n0, seg_ref)
        numi = jnp.zeros((nb, D), F32)
        for s in range(i + 1):
            w = _split_dot(q[i] * k[s], seg_ref) * jnp.exp(jnp.minimum(r[s] - mx[i], 0.0))
            den = den + w
            numi = numi + _split_dot(w, e128_ref) * v[s]
        numi_ref[rows(i), :] = numi
        rd_ref[rows(i), :] = 1.0 / jnp.maximum(jnp.abs(den), jnp.exp(-(b[i] + mx[i])))
        eint_ref[rows(i), :] = e_int
        go_ref[rows(i), :] = jax.nn.sigmoid(p[rows(i), HK + D:]) * gh_ref[...]
    n1_ref[...] = n1


def _mlstm_sample_pre(x, mod, layer, g, w_row, w_col, b_g, m0, n0, g_head, seg64, e64, e128, *, t):
    rows = x.shape[1]
    nb = rows // t

    def whole(shape):
        return pl.BlockSpec(shape, lambda i: (0,) * len(shape))

    outs = [(HK // LANES, rows, LANES), (HK // LANES, rows, LANES), (D // LANES, rows, LANES),
            (D // LANES, rows, LANES), (rows, D), (rows, FG), (rows, FG), (rows, D), (nb, HK),
            (nb, FG)]
    return pl.pallas_call(
        functools.partial(_mlstm_sample_pre_kernel, t=t, nb=nb),
        grid=(1,),
        in_specs=[
            whole((1, rows, D)),
            _group_mod_spec(layer, nb),
            _const_spec((1, D)),
            _const_spec((D, HK + 2 * D)),
            _const_spec((HK + 2 * LANES, D)),
            _const_spec((2, LANES)),
            whole((nb, FG)),
            whole((nb, HK)),
            _const_spec((1, D)),
            _const_spec((HK, FG)),
            _const_spec((FG, HK)),
            _const_spec((FG, D)),
        ],
        out_specs=[whole(s) for s in outs],
        out_shape=[jax.ShapeDtypeStruct(s, F32) for s in outs],
        compiler_params=_params("arbitrary"),
        name="mlstm_sample_pre",
    )(x, mod, g.reshape(1, D), w_row, w_col, b_g, m0, n0, g_head.reshape(1, D), seg64, e64, e128)


def _mlstm_sample_state_kernel(q_ref, ka_ref, v_ref, dec_ref, c0_ref, inter_ref, c1_ref, *, bb, t):
    per = SUBLANES // t
    seq_of_row = lax.broadcasted_iota(jnp.int32, (SUBLANES, 1), 0) // t
    zeros_c = jnp.zeros((DK, DV), BF)

    def tile(i, carry):
        rs = pl.ds(pl.multiple_of(i * SUBLANES, SUBLANES), SUBLANES)
        for h in range(H):
            lo = h % 2 == 0
            half = slice((h % 2) * DK, (h % 2 + 1) * DK)
            qp = q_ref[h // 2, rs, :].astype(BF)
            kap = ka_ref[h // 2, rs, :]
            vh = v_ref[h, rs, :].astype(BF)
            dech = dec_ref[h, rs, :]
            res = None
            for w in range(per):
                seq = i * per + w
                c0 = c0_ref[seq, h]
                c0b = c0.astype(BF)
                rw = _dot(qp, jnp.concatenate([c0b, zeros_c] if lo else [zeros_c, c0b], axis=0))
                res = rw if res is None else jnp.where(seq_of_row == w, rw, res)
                kaw = jnp.where(seq_of_row == w, kap, 0.0).astype(BF)
                c1_ref[seq, h] = dech[w * t:w * t + 1, :] * c0 + _dot_tn(kaw, vh)[half, :]
            inter_ref[h, rs, :] = res
        return carry

    lax.fori_loop(0, bb // per, tile, 0, unroll=4)


def _mlstm_sample_state(q, ka, v, dec, c0, *, t, bb):
    rows = q.shape[1]
    nb = rows // t
    assert SUBLANES % t == 0 and bb % (SUBLANES // t) == 0 and nb % bb == 0

    def tok(blocks):
        return pl.BlockSpec((blocks, bb * t, LANES), lambda i: (0, i, 0))

    state = pl.BlockSpec((bb, H, DK, DV), lambda i: (i, 0, 0, 0))
    return pl.pallas_call(
        functools.partial(_mlstm_sample_state_kernel, bb=bb, t=t),
        grid=(nb // bb,),
        in_specs=[tok(HK // LANES), tok(HK // LANES), tok(H), tok(H), state],
        out_specs=[tok(H), state],
        out_shape=[jax.ShapeDtypeStruct((H, rows, LANES), F32),
                   jax.ShapeDtypeStruct(c0.shape, F32)],
        compiler_params=_params("arbitrary"),
        name="mlstm_sample_state",
    )(q, ka, v, dec, c0)


def _mlstm_sample_post_kernel(inter_ref, numi_ref, rd_ref, eint_ref, go_ref, e128_ref, seg_ref,
                              y_ref, *, t, nb):
    for i in range(t):
        rs = slice(i * nb, (i + 1) * nb)
        inter = jnp.concatenate(
            [inter_ref[h, pl.ds(i, nb, stride=t), :] for h in range(H)], axis=1)
        num = numi_ref[rs, :] + _split_dot(eint_ref[rs, :], e128_ref) * inter
        hh = num * _split_dot(rd_ref[rs, :], e128_ref)
        ms = _split_dot(hh * hh, seg_ref) * (1.0 / DV)
        y_ref[rs, :] = hh * _split_dot(lax.rsqrt(ms + EPS), e128_ref) * go_ref[rs, :]


def _mlstm_sample_post(inter, numi, rd, eint, go, e128, seg128, *, t):
    rows = inter.shape[1]

    def whole(shape):
        return pl.BlockSpec(shape, lambda i: (0,) * len(shape))

    return pl.pallas_call(
        functools.partial(_mlstm_sample_post_kernel, t=t, nb=rows // t),
        grid=(1,),
        in_specs=[whole((H, rows, LANES)), whole((rows, D)), whole((rows, FG)), whole((rows, FG)),
                  whole((rows, D)), _const_spec((FG, D)), _const_spec((D, FG))],
        out_specs=whole((rows, D)),
        out_shape=jax.ShapeDtypeStruct((rows, D), F32),
        compiler_params=_params("arbitrary"),
        name="mlstm_sample_post",
    )(inter, numi, rd, eint, go, e128, seg128)


def _conv_taps(u, prev2, prev1, wc_ref):
    return wc_ref[0:1] * prev2 + wc_ref[1:2] * prev1 + wc_ref[2:3] * u


def _conv_prompt_tile(x_ref, mod_ref, g_ref, win_ref, wc_ref, wout_ref, xo_ref, st_ref, carry_scr,
                      first):
    @pl.when(first)
    def _():
        carry_scr[...] = jnp.zeros(carry_scr.shape, F32)

    x = x_ref[0]
    tm = x.shape[0]
    hb = _modulate(x, g_ref[...], mod_ref[0, 0], mod_ref[0, 1]).astype(BF)
    u = _dot(hb, win_ref[:, D:2 * D]) * _dot(hb, win_ref[:, 2 * D:])
    row = lax.broadcasted_iota(jnp.int32, (tm, D), 0)
    c0 = carry_scr[0:1]
    c1 = carry_scr[1:2]
    prev1 = jnp.where(row == 0, c1, pltpu.roll(u, 1, 0))
    prev2 = jnp.where(row == 0, c0, jnp.where(row == 1, c1, pltpu.roll(u, 2, 0)))
    y = _conv_taps(u, prev2, prev1, wc_ref)
    bg = _dot(hb, win_ref[:, :D])
    xo_ref[0] = x + mod_ref[0, 2] * _dot((bg * y).astype(BF), wout_ref[...])
    carry_scr[...] = u[tm - 2:tm]
    st_ref[0] = u[tm - 2:tm]


def _conv_sample_tile(x_ref, mod_ref, g_ref, win_ref, wc_ref, wout_ref, buf_ref, xo_ref, st_ref,
                      *, t, nb):
    def norm(xr, shift, scale):
        return _modulate(xr, g_ref[...], shift, scale)

    def residual(xr, fr, gate):
        return xr + gate * fr

    x = x_ref[0]
    hb = _by_group(norm, (x,), (mod_ref[0, 0], mod_ref[0, 1])).astype(BF)
    u = _dot(hb, win_ref[:, D:2 * D]) * _dot(hb, win_ref[:, 2 * D:])
    full = [buf_ref[j] for j in range(CONV_W - 1)] + [u[i * nb:(i + 1) * nb] for i in range(t)]
    y = jnp.concatenate([_conv_taps(full[i + 2], full[i], full[i + 1], wc_ref) for i in range(t)],
                        axis=0)
    bg = _dot(hb, win_ref[:, :D])
    out = _dot((bg * y).astype(BF), wout_ref[...])
    xo_ref[0] = _by_group(residual, (x, out), (mod_ref[0, 2],))
    for j in range(CONV_W - 1):
        st_ref[j] = full[t + j]


def _conv_kernel(xp_ref, modp_ref, xs_ref, mods_ref, g_ref, win_ref, wc_ref, wout_ref, buf_ref,
                 xpo_ref, stp_ref, xso_ref, sts_ref, carry_scr, *, n_prompt, nt, t, nb):
    step = pl.program_id(0)

    @pl.when(step < n_prompt)
    def _():
        _conv_prompt_tile(xp_ref, modp_ref, g_ref, win_ref, wc_ref, wout_ref, xpo_ref, stp_ref,
                          carry_scr, step % nt == 0)

    @pl.when(step == n_prompt)
    def _():
        _conv_sample_tile(xs_ref, mods_ref, g_ref, win_ref, wc_ref, wout_ref, buf_ref, xso_ref,
                          sts_ref, t=t, nb=nb)


def _conv(xp, mod_p, xs, mod, layer, g, w_in, w_conv, w_out, buf, *, tm, t, casts=()):
    nb, tp, _ = xp.shape
    nt = tp // tm
    n_prompt = nb * nt
    rows_s = xs.shape[1]
    seqs = rows_s // t
    tok, per_seq = _prompt_tiles(nb, nt)
    args = (xp, mod_p, xs, mod, g.reshape(1, D), w_in, w_conv, w_out, buf)
    c_in, c_out, c_shapes, c_args = _cast_specs(
        casts, n_prompt, lambda s: jnp.minimum(s, n_prompt - 1))
    body = functools.partial(_conv_kernel, n_prompt=n_prompt, nt=nt, t=t, nb=seqs)
    return pl.pallas_call(
        _carry_casts(body, len(args), 4, len(casts)),
        grid=(n_prompt + 1,),
        in_specs=[
            tok(tm),
            per_seq((N_MOD, 1, D)),
            _const_spec((1, rows_s, D)),
            _group_mod_spec(layer, seqs),
            _const_spec((1, D)),
            _const_spec((D, 3 * D)),
            _const_spec((CONV_W, D)),
            _const_spec((D, D)),
            _const_spec((CONV_W - 1, seqs, D)),
        ] + c_in,
        out_specs=[tok(tm), per_seq((CONV_W - 1, D)), _whole_spec((1, rows_s, D)),
                   _whole_spec((CONV_W - 1, seqs, D))] + c_out,
        out_shape=[jax.ShapeDtypeStruct(xp.shape, F32),
                   jax.ShapeDtypeStruct((nb, CONV_W - 1, D), F32),
                   jax.ShapeDtypeStruct(xs.shape, F32),
                   jax.ShapeDtypeStruct((CONV_W - 1, seqs, D), F32)] + c_shapes,
        scratch_shapes=[pltpu.VMEM((CONV_W - 1, D), F32)],
        compiler_params=_params("arbitrary"),
        name="conv",
    )(*args, *c_args)


def kernel(x_prompt, x_sample, c_prompt, c_sample, state_mlstm_C, state_mlstm_n, state_mlstm_m,
           state_conv, ada_w, ada_b, norm_g, final_norm_g, mlstm_w_in, mlstm_b_gates,
           mlstm_head_g, mlstm_w_out, conv_w_in, conv_w, conv_w_out, mlp_w_up, mlp_w_down):
    assert ada_w.shape[0] == 2 and mlstm_w_in.shape[0] == 1 and conv_w_in.shape[0] == 1
    bp, tp, _ = x_prompt.shape
    bs, ts, _ = x_sample.shape
    assert ts == 4 and tp % 512 == 0

    mod = _ada(jnp.concatenate([c_sample, c_prompt], axis=0), ada_w, ada_b)
    mod_p = jnp.transpose(mod[:, :, bs:], (0, 2, 1, 3))[:, :, :, None]

    w_in0 = mlstm_w_in[0]
    w_row = jnp.concatenate([w_in0[:, :HK], w_in0[:, 2 * HK:NP]], axis=1).astype(BF)
    gate_pad = jnp.zeros((D, LANES - H), F32)
    w_col = jnp.concatenate([w_in0[:, HK:2 * HK] * (DK ** -0.5), w_in0[:, NP:NP + H], gate_pad,
                             w_in0[:, NP + H:], gate_pad], axis=1).T.astype(BF)
    b_g = jnp.pad(mlstm_b_gates[0].reshape(2, H), ((0, 0), (0, LANES - H)))
    wout0 = mlstm_w_out[0].astype(BF)

    x, s_p, m_p, up0, down0 = _mlstm_prompt(
        x_prompt, mod_p[0], norm_g[0, 0], w_row, w_col, mlstm_b_gates[0], mlstm_head_g[0], wout0,
        tm=512, casts=((mlp_w_up, 0), (mlp_w_down, 0)))
    prompt_c = s_p[None, :, :, :, :DV]
    prompt_n = s_p[None, :, :, :, DV]
    prompt_m = m_p[None, :, :, 0]

    xs = jnp.transpose(x_sample, (1, 0, 2)).reshape(1, ts * bs, D)
    head_of_v = jnp.arange(D, dtype=jnp.int32) // DV
    head_of_k = jnp.arange(HK, dtype=jnp.int32) // DK
    lanes = jnp.arange(FG, dtype=jnp.int32)
    e128 = (lanes[:, None] == head_of_v[None, :]).astype(BF)
    e64 = (lanes[:, None] == head_of_k[None, :]).astype(BF)
    m0 = jnp.pad(state_mlstm_m[0], ((0, 0), (0, FG - H)))
    n0 = state_mlstm_n[0].reshape(bs, HK)
    q, ka, v, dec, numi, rd, eint, go, n1, m1 = _mlstm_sample_pre(
        xs, mod, 0, norm_g[0, 0], w_row, w_col, b_g, m0, n0, mlstm_head_g[0], e64.T, e64, e128,
        t=ts)
    inter, c1 = _mlstm_sample_state(q, ka, v, dec, state_mlstm_C[0], t=ts, bb=16)
    y = _mlstm_sample_post(inter, numi, rd, eint, go, e128, e128.T, t=ts)
    sample_n = n1.reshape(bs, H, DK)[None]
    sample_m = m1[None, :, :H]

    x, xs, cwin, cwout = _mlp(x, mod_p[0], xs, mod, 0, norm_g[0, 1], up0, down0, tm=512, seqs=bs,
                              pre=(y.reshape(1, ts * bs, D), wout0),
                              casts=((conv_w_in, 0), (conv_w_out, 0)))
    x, conv_p, xs, conv_s, up1, down1 = _conv(
        x, mod_p[1], xs, mod, 1, norm_g[1, 0], cwin, conv_w[0], cwout,
        jnp.transpose(state_conv[0], (1, 0, 2)), tm=512, t=ts,
        casts=((mlp_w_up, 1), (mlp_w_down, 1)))
    y_prompt, ys = _mlp(x, mod_p[1], xs, mod, 1, norm_g[1, 1], up1, down1, tm=512, seqs=bs,
                        final_g=final_norm_g)
    prompt_conv = conv_p[None]
    y_sample = jnp.transpose(ys.reshape(ts, bs, D), (1, 0, 2))
    sample_conv = jnp.transpose(conv_s, (1, 0, 2))[None]

    return (y_prompt, y_sample, prompt_c, prompt_n, prompt_m, prompt_conv,
            c1[None], sample_n, sample_m, sample_conv)
```

```python
import functools

import jax
import jax.numpy as jnp
from jax import lax
from jax.experimental import pallas as pl
from jax.experimental.pallas import tpu as pltpu

D = 1024
H = 8
DK = 64
DV = 128
HK = H * DK
NP = 2 * HK + 2 * D
FG = 128
LANES = 128
SUBLANES = 8
DFF = 4 * D
N_MOD = 6
CONV_W = 3
EPS = 1e-6
CHUNK = 128
FF_CHUNK = 1024
BF = jnp.bfloat16
F32 = jnp.float32
VMEM_LIMIT_BYTES = 56 * 1024 * 1024


def _dot(a, b):
    return jnp.dot(a, b, preferred_element_type=F32)


def _dot_nt(a, b):
    return lax.dot_general(a, b, (((1,), (1,)), ((), ())), preferred_element_type=F32)


def _dot_tn(a, b):
    return lax.dot_general(a, b, (((0,), (0,)), ((), ())), preferred_element_type=F32)


def _rms(x):
    return x * lax.rsqrt(jnp.mean(x * x, axis=-1, keepdims=True) + EPS)


def _modulate(x, g, shift, scale):
    return _rms(x) * g * (1.0 + scale) + shift


def _log_sigmoid(x):
    return jnp.minimum(x, 0.0) - jnp.log1p(jnp.exp(-jnp.abs(x)))


def _scan_lanes(x, op, ident, n):
    lane = lax.broadcasted_iota(jnp.int32, x.shape, 1) % n
    d = 1
    while d < n:
        x = op(x, jnp.where(lane >= d, pltpu.roll(x, d, 1), ident))
        d *= 2
    return x


def _by_group(fn, xs, mods):
    rows, r = xs[0].shape[0], mods[0].shape[0]
    if r == 1 or r == rows:
        return fn(*xs, *mods)
    return jnp.concatenate(
        [fn(*[x[i:i + r] for x in xs], *mods) for i in range(0, rows, r)], axis=0)


def _split_dot(x, m_ref):
    hi = x.astype(BF)
    lo = (x - hi.astype(F32)).astype(BF)
    m = m_ref[...]
    return _dot(hi, m) + _dot(lo, m)


def _cast_specs(jobs, steps, slab_of):
    ins, outs, shapes, args = [], [], [], []
    for w, layer in jobs:
        _, r, c = w.shape
        slab = r // steps
        assert slab * steps == r and slab % (2 * SUBLANES) == 0
        ins.append(pl.BlockSpec((1, slab, c), lambda *g, layer=layer: (layer, slab_of(*g), 0)))
        outs.append(pl.BlockSpec((slab, c), lambda *g: (slab_of(*g), 0)))
        shapes.append(jax.ShapeDtypeStruct((r, c), BF))
        args.append(w)
    return ins, outs, shapes, args


def _carry_casts(body, n_in, n_out, n_cast):
    def kernel(*refs):
        a, b = n_in, n_in + n_cast
        c, d = b + n_out, b + n_out + n_cast
        for src, dst in zip(refs[a:b], refs[c:d]):
            dst[...] = src[0].astype(BF)
        body(*refs[:a], *refs[b:c], *refs[d:])
    return kernel


def _const_spec(shape):
    nd = len(shape)
    return pl.BlockSpec(shape, lambda *_: (0,) * nd, pipeline_mode=pl.Buffered(1))


def _params(*sem):
    return pltpu.CompilerParams(dimension_semantics=sem, vmem_limit_bytes=VMEM_LIMIT_BYTES)


def _ada_kernel(c_ref, w_ref, b_ref, o_ref):
    c = c_ref[...]
    sc = (c * jax.nn.sigmoid(c)).astype(BF)
    o_ref[0, 0] = _dot(sc, w_ref[0].astype(BF)) + b_ref[0]


def _ada(c_all, ada_w, ada_b, *, casts=(), cast_steps=8):
    depth, _, n = ada_w.shape
    nb = c_all.shape[0]
    nj = n // D
    assert cast_steps <= depth * nj
    c_in, c_out, c_shapes, c_args = _cast_specs(
        casts, cast_steps, lambda i, j: jnp.minimum(i * nj + j, cast_steps - 1))
    return pl.pallas_call(
        _carry_casts(_ada_kernel, 3, 1, len(casts)),
        grid=(depth, nj),
        in_specs=[
            pl.BlockSpec((nb, D), lambda i, j: (0, 0)),
            pl.BlockSpec((1, D, D), lambda i, j: (i, 0, j)),
            pl.BlockSpec((1, 1, D), lambda i, j: (i, 0, j)),
        ] + c_in,
        out_specs=[pl.BlockSpec((1, 1, nb, D), lambda i, j: (i, j, 0, 0))] + c_out,
        out_shape=[jax.ShapeDtypeStruct((depth, nj, nb, D), F32)] + c_shapes,
        compiler_params=_params("arbitrary", "arbitrary"),
        name="ada",
    )(c_all, ada_w, ada_b.reshape(depth, 1, n), *c_args)


def _mlp_tile(x_ref, mod_ref, g_ref, wu_ref, wd_ref, fg_ref, o_ref, y_ref=None, wo_ref=None):
    def residual(xr, fr, gate):
        return xr + gate * fr

    def norm(xr, shift, scale):
        return _modulate(xr, g_ref[...], shift, scale)

    x = x_ref[0]
    if y_ref is not None:
        x = _by_group(residual, (x, _dot(y_ref[0].astype(BF), wo_ref[...])), (mod_ref[0, 2],))
    hb = _by_group(norm, (x,), (mod_ref[0, 3], mod_ref[0, 4])).astype(BF)
    acc = jnp.zeros(x.shape, F32)
    for c in range(DFF // FF_CHUNK):
        cs = slice(c * FF_CHUNK, (c + 1) * FF_CHUNK)
        hid = jnp.maximum(_dot(hb, wu_ref[:, cs]), 0.0)
        acc = acc + _dot((hid * hid).astype(BF), wd_ref[cs, :])
    x = _by_group(residual, (x, acc), (mod_ref[0, 5],))
    if fg_ref is not None:
        x = _rms(x) * fg_ref[...]
    o_ref[0] = x


def _mlp_kernel(*refs, n_prompt, pre, final):
    refs = list(refs)
    xp_ref, modp_ref, xs_ref = refs[:3]
    refs = refs[3:]
    ys_ref = wo_ref = fg_ref = None
    if pre:
        ys_ref, wo_ref = refs[:2]
        refs = refs[2:]
    mods_ref, g_ref, wu_ref, wd_ref = refs[:4]
    refs = refs[4:]
    if final:
        fg_ref = refs.pop(0)
    xpo_ref, xso_ref = refs
    step = pl.program_id(0)

    @pl.when(step < n_prompt)
    def _():
        _mlp_tile(xp_ref, modp_ref, g_ref, wu_ref, wd_ref, fg_ref, xpo_ref)

    @pl.when(step == n_prompt)
    def _():
        _mlp_tile(xs_ref, mods_ref, g_ref, wu_ref, wd_ref, fg_ref, xso_ref, ys_ref, wo_ref)


def _group_mod_spec(layer, r):
    return pl.BlockSpec((1, N_MOD, r, D), lambda *_: (layer, 0, 0, 0),
                        pipeline_mode=pl.Buffered(1))


def _prompt_tiles(nb, nt):
    def seq_tile(step):
        c = jnp.minimum(step, nb * nt - 1)
        return c // nt, c % nt

    def tok(tm):
        return pl.BlockSpec((1, tm, D), lambda s: (*seq_tile(s), 0))

    def per_seq(shape):
        return pl.BlockSpec((1,) + shape, lambda s: (seq_tile(s)[0],) + (0,) * len(shape))

    return tok, per_seq


def _whole_spec(shape):
    return pl.BlockSpec(shape, lambda *_: (0,) * len(shape))


def _mlp(xp, mod_p, xs, mod, layer, g, w_up, w_down, *, tm, seqs, pre=None, final_g=None,
         casts=()):
    nb, t, _ = xp.shape
    nt = t // tm
    n_prompt = nb * nt
    rows_s = xs.shape[1]
    tok, per_seq = _prompt_tiles(nb, nt)
    args = [xp, mod_p, xs]
    specs = [tok(tm), per_seq((N_MOD, 1, D)), _const_spec((1, rows_s, D))]
    if pre is not None:
        args += list(pre)
        specs += [_const_spec((1, rows_s, D)), _const_spec((D, D))]
    args += [mod, g.reshape(1, D), w_up, w_down]
    specs += [_group_mod_spec(layer, seqs), _const_spec((1, D)), _const_spec((D, DFF)),
              _const_spec((DFF, D))]
    if final_g is not None:
        args.append(final_g.reshape(1, D))
        specs.append(_const_spec((1, D)))
    c_in, c_out, c_shapes, c_args = _cast_specs(
        casts, n_prompt, lambda s: jnp.minimum(s, n_prompt - 1))
    body = functools.partial(_mlp_kernel, n_prompt=n_prompt, pre=pre is not None,
                             final=final_g is not None)
    return pl.pallas_call(
        _carry_casts(body, len(args), 2, len(casts)),
        grid=(n_prompt + 1,),
        in_specs=specs + c_in,
        out_specs=[tok(tm), _whole_spec((1, rows_s, D))] + c_out,
        out_shape=[jax.ShapeDtypeStruct(xp.shape, F32), jax.ShapeDtypeStruct(xs.shape, F32)]
        + c_shapes,
        compiler_params=_params("arbitrary"),
        name="mlp",
    )(*args, *c_args)


def _mlstm_prompt_kernel(x_ref, mod_ref, g_ref, wrow_ref, wcol_ref, bg_ref, gh_ref, wout_ref,
                         xo_ref, s_ref, m_ref, y_scr, *, tm):
    @pl.when(pl.program_id(1) == 0)
    def _():
        s_ref[...] = jnp.zeros(s_ref.shape, F32)
        m_ref[...] = jnp.zeros(m_ref.shape, F32)

    x = x_ref[0]
    hb = _modulate(x, g_ref[...], mod_ref[0, 0], mod_ref[0, 1]).astype(BF)
    n = CHUNK
    ig_all = _dot_nt(wcol_ref[HK:HK + H, :], hb) + bg_ref[0:H]
    fg_all = _dot_nt(wcol_ref[HK + LANES:HK + LANES + H, :], hb) + bg_ref[H:2 * H]
    b_all = _scan_lanes(_log_sigmoid(fg_all), jnp.add, 0.0, n)
    r_all = ig_all - b_all
    cm_all = _scan_lanes(r_all, jnp.maximum, -jnp.inf, n)
    pcol = _dot_nt(wcol_ref[0:HK, :], hb)
    prow = jnp.concatenate([_dot(hb, wrow_ref[:, 0:HK]), _dot(hb, wrow_ref[:, 2 * HK:NP])],
                           axis=1)
    causal = (lax.broadcasted_iota(jnp.int32, (n, n), 0)
              >= lax.broadcasted_iota(jnp.int32, (n, n), 1))
    zeros_k = jnp.zeros((DK, n), BF)
    zeros_s = jnp.zeros((DK, 2 * DV), BF)
    zeros_v = jnp.zeros((n, DV), BF)
    ones_v = jnp.ones((n, DV), BF)
    pad = jnp.zeros((n - 2 * H, n), F32)

    for c in range(tm // n):
        ts = slice(c * n, (c + 1) * n)
        r = r_all[:, ts]
        m0 = m_ref[0]
        mx = jnp.maximum(m0, cm_all[:, ts])
        m = b_all[:, ts] + mx
        e_neg = jnp.exp(-m)
        mx_last = jnp.broadcast_to(mx[:, n - 1:n], (H, n))
        a = jnp.exp(r - mx_last)
        decay = jnp.exp(m0 - mx_last)
        m_ref[0] = jnp.broadcast_to(m[:, n - 1:n], (H, n))
        cols = jnp.concatenate([mx, e_neg, pad], axis=0).T

        for h in range(H):
            lo = h % 2 == 0
            mx_b = jnp.broadcast_to(cols[:, h:h + 1], (n, n))
            floor_b = jnp.broadcast_to(cols[:, H + h:H + h + 1], (n, DV))
            dec = jnp.exp(jnp.where(causal, r[h:h + 1, :] - mx_b, -jnp.inf))
            e_int = jnp.exp(m0[h:h + 1, :] - mx_b)
            qp = prow[ts, (h // 2) * 2 * DK:(h // 2 + 1) * 2 * DK]
            kt = pcol[h * DK:(h + 1) * DK, ts]
            ktb = kt.astype(BF)
            s = _dot(qp.astype(BF),
                     jnp.concatenate([ktb, zeros_k] if lo else [zeros_k, ktb], axis=0)) * dec
            st = s_ref[0, h]
            stb = st.astype(BF)
            v = prow[ts, HK + h * DV:HK + (h + 1) * DV].astype(BF)
            rhs = jnp.concatenate(
                [jnp.concatenate([v, zeros_v], axis=1)]
                + ([stb, zeros_s] if lo else [zeros_s, stb]), axis=0)
            lhs = jnp.concatenate([s.astype(BF), (qp * e_int).astype(BF)], axis=1)
            res = _dot(lhs, rhs)
            den = jnp.sum(s, axis=-1, keepdims=True) + res[:, DV:]
            hh = res[:, :DV] / jnp.maximum(jnp.abs(den), floor_b)
            hh = hh * lax.rsqrt(jnp.mean(hh * hh, axis=-1, keepdims=True) + EPS)
            y_scr[ts, h * DV:(h + 1) * DV] = hh
            ka = (kt * a[h:h + 1, :]).astype(BF)
            dc = decay[h:h + 1, :]
            s_ref[0, h] = (jnp.concatenate([dc, dc], axis=1) * st
                           + _dot(ka, jnp.concatenate([v, ones_v], axis=1)))

    o = prow[:, HK + D:]
    y = (y_scr[...] * gh_ref[...] * jax.nn.sigmoid(o)).astype(BF)
    xo_ref[0] = x + mod_ref[0, 2] * _dot(y, wout_ref[...])


def _mlstm_prompt(x, mod, g, w_row, w_col, b_g, g_head, w_out, *, tm, casts=()):
    nb, t, _ = x.shape
    nt = t // tm
    tok = pl.BlockSpec((1, tm, D), lambda b, i: (b, i, 0))
    args = (x, mod, g.reshape(1, D), w_row, w_col,
            jnp.broadcast_to(b_g.reshape(2 * H, 1), (2 * H, tm)), g_head.reshape(1, D), w_out)
    c_in, c_out, c_shapes, c_args = _cast_specs(casts, nb * nt, lambda b, i: b * nt + i)
    return pl.pallas_call(
        _carry_casts(functools.partial(_mlstm_prompt_kernel, tm=tm), len(args), 3, len(casts)),
        grid=(nb, nt),
        in_specs=[
            tok,
            pl.BlockSpec((1, N_MOD, 1, D), lambda b, i: (b, 0, 0, 0)),
            _const_spec((1, D)),
            _const_spec((D, NP + 2 * H)),
            _const_spec((HK + 2 * LANES, D)),
            _const_spec((2 * H, tm)),
            _const_spec((1, D)),
            _const_spec((D, D)),
        ] + c_in,
        out_specs=[
            tok,
            pl.BlockSpec((1, H, DK, 2 * DV), lambda b, i: (b, 0, 0, 0)),
            pl.BlockSpec((1, H, CHUNK), lambda b, i: (b, 0, 0)),
        ] + c_out,
        out_shape=[
            jax.ShapeDtypeStruct(x.shape, F32),
            jax.ShapeDtypeStruct((nb, H, DK, 2 * DV), F32),
            jax.ShapeDtypeStruct((nb, H, CHUNK), F32),
        ] + c_shapes,
        scratch_shapes=[pltpu.VMEM((tm, D), F32)],
        compiler_params=_params("arbitrary", "arbitrary"),
        name="mlstm_prompt",
    )(*args, *c_args)


def _mlstm_sample_pre_kernel(x_ref, mod_ref, g_ref, wrow_ref, wcol_ref, bg_ref, m0_ref, n0_ref,
                             gh_ref, seg_ref, e64_ref, e128_ref,
                             q_ref, ka_ref, v_ref, dec_ref, numi_ref, rd_ref, eint_ref, go_ref,
                             n1_ref, m1_ref, *, t, nb):
    def norm(xr, shift, scale):
        return _modulate(xr, g_ref[...], shift, scale)

    hb = _by_group(norm, (x_ref[0],), (mod_ref[0, 0], mod_ref[0, 1])).astype(BF)
    p = jnp.concatenate([_dot(hb, wrow_ref[:, 0:HK]), _dot(hb, wrow_ref[:, 2 * HK:NP])],
                        axis=1)
    k_all = _dot_nt(hb, wcol_ref[0:HK, :])
    ig_all = _dot_nt(hb, wcol_ref[HK:HK + LANES, :]) + bg_ref[0:1]
    fg_all = _dot_nt(hb, wcol_ref[HK + LANES:HK + 2 * LANES, :]) + bg_ref[1:2]

    def rows(i):
        return slice(i * nb, (i + 1) * nb)

    q = [p[rows(i), 0:HK] for i in range(t)]
    k = [k_all[rows(i)] for i in range(t)]
    v = [p[rows(i), HK:HK + D] for i in range(t)]
    m0 = m0_ref[...]
    n0 = n0_ref[...]

    b, r, mx = [], [], []
    for i in range(t):
        lf = _log_sigmoid(fg_all[rows(i)])
        b.append(lf if i == 0 else b[-1] + lf)
        r.append(ig_all[rows(i)] - b[i])
        cm = r[i] if i == 0 else jnp.maximum(cm, r[i])
        mx.append(jnp.maximum(m0, cm))
    mx_last = mx[t - 1]
    decay = jnp.exp(m0 - mx_last)
    m1_ref[...] = b[t - 1] + mx_last
    dec_x = _split_dot(decay, e128_ref)
    n1 = _split_dot(decay, e64_ref) * n0

    for i in range(t):
        ka = k[i] * _split_dot(jnp.exp(r[i] - mx_last), e64_ref)
        n1 = n1 + ka
        seq_rows = pl.ds(i, nb, stride=t)
        for c in range(HK // LANES):
            cs = slice(c * LANES, (c + 1) * LANES)
            ka_ref[c, seq_rows, :] = ka[:, cs]
            q_ref[c, seq_rows, :] = q[i][:, cs]
        for c in range(D // LANES):
            cs = slice(c * LANES, (c + 1) * LANES)
            v_ref[c, seq_rows, :] = v[i][:, cs]
            dec_ref[c, seq_rows, :] = dec_x[:, cs]
        e_int = jnp.exp(m0 - mx[i])
        den = e_int * _split_dot(q[i] * n0, seg_ref)
        numi = jnp.zeros((nb, D), F32)
        for s in range(i + 1):
            w = _split_dot(q[i] * k[s], seg_ref) * jnp.exp(jnp.minimum(r[s] - mx[i], 0.0))
            den = den + w
            numi = numi + _split_dot(w, e128_ref) * v[s]
        numi_ref[rows(i), :] = numi
        rd_ref[rows(i), :] = 1.0 / jnp.maximum(jnp.abs(den), jnp.exp(-(b[i] + mx[i])))
        eint_ref[rows(i), :] = e_int
        go_ref[rows(i), :] = jax.nn.sigmoid(p[rows(i), HK + D:]) * gh_ref[...]
    n1_ref[...] = n1


def _mlstm_sample_pre(x, mod, layer, g, w_row, w_col, b_g, m0, n0, g_head, seg64, e64, e128, *, t):
    rows = x.shape[1]
    nb = rows // t

    def whole(shape):
        return pl.BlockSpec(shape, lambda i: (0,) * len(shape))

    outs = [(HK // LANES, rows, LANES), (HK // LANES, rows, LANES), (D // LANES, rows, LANES),
            (D // LANES, rows, LANES), (rows, D), (rows, FG), (rows, FG), (rows, D), (nb, HK),
            (nb, FG)]
    return pl.pallas_call(
        functools.partial(_mlstm_sample_pre_kernel, t=t, nb=nb),
        grid=(1,),
        in_specs=[
            whole((1, rows, D)),
            _group_mod_spec(layer, nb),
            _const_spec((1, D)),
            _const_spec((D, NP + 2 * H)),
            _const_spec((HK + 2 * LANES, D)),
            _const_spec((2, LANES)),
            whole((nb, FG)),
            whole((nb, HK)),
            _const_spec((1, D)),
            _const_spec((HK, FG)),
            _const_spec((FG, HK)),
            _const_spec((FG, D)),
        ],
        out_specs=[whole(s) for s in outs],
        out_shape=[jax.ShapeDtypeStruct(s, F32) for s in outs],
        compiler_params=_params("arbitrary"),
        name="mlstm_sample_pre",
    )(x, mod, g.reshape(1, D), w_row, w_col, b_g, m0, n0, g_head.reshape(1, D), seg64, e64, e128)


def _mlstm_sample_state_kernel(q_ref, ka_ref, v_ref, dec_ref, c0_ref, inter_ref, c1_ref, *, bb, t):
    per = SUBLANES // t
    seq_of_row = lax.broadcasted_iota(jnp.int32, (SUBLANES, 1), 0) // t
    zeros_c = jnp.zeros((DK, DV), BF)

    def tile(i, carry):
        rs = pl.ds(pl.multiple_of(i * SUBLANES, SUBLANES), SUBLANES)
        for h in range(H):
            lo = h % 2 == 0
            half = slice((h % 2) * DK, (h % 2 + 1) * DK)
            qp = q_ref[h // 2, rs, :].astype(BF)
            kap = ka_ref[h // 2, rs, :]
            vh = v_ref[h, rs, :].astype(BF)
            dech = dec_ref[h, rs, :]
            res = None
            for w in range(per):
                seq = i * per + w
                c0 = c0_ref[seq, h]
                c0b = c0.astype(BF)
                rw = _dot(qp, jnp.concatenate([c0b, zeros_c] if lo else [zeros_c, c0b], axis=0))
                res = rw if res is None else jnp.where(seq_of_row == w, rw, res)
                kaw = jnp.where(seq_of_row == w, kap, 0.0).astype(BF)
                c1_ref[seq, h] = dech[w * t:w * t + 1, :] * c0 + _dot_tn(kaw, vh)[half, :]
            inter_ref[h, rs, :] = res
        return carry

    lax.fori_loop(0, bb // per, tile, 0, unroll=4)


def _mlstm_sample_state(q, ka, v, dec, c0, *, t, bb):
    rows = q.shape[1]
    nb = rows // t
    assert SUBLANES % t == 0 and bb % (SUBLANES // t) == 0 and nb % bb == 0

    def tok(blocks):
        return pl.BlockSpec((blocks, bb * t, LANES), lambda i: (0, i, 0))

    state = pl.BlockSpec((bb, H, DK, DV), lambda i: (i, 0, 0, 0))
    return pl.pallas_call(
        functools.partial(_mlstm_sample_state_kernel, bb=bb, t=t),
        grid=(nb // bb,),
        in_specs=[tok(HK // LANES), tok(HK // LANES), tok(H), tok(H), state],
        out_specs=[tok(H), state],
        out_shape=[jax.ShapeDtypeStruct((H, rows, LANES), F32),
                   jax.ShapeDtypeStruct(c0.shape, F32)],
        compiler_params=_params("arbitrary"),
        name="mlstm_sample_state",
    )(q, ka, v, dec, c0)


def _mlstm_sample_post_kernel(inter_ref, numi_ref, rd_ref, eint_ref, go_ref, e128_ref, seg_ref,
                              y_ref, *, t, nb):
    for i in range(t):
        rs = slice(i * nb, (i + 1) * nb)
        inter = jnp.concatenate(
            [inter_ref[h, pl.ds(i, nb, stride=t), :] for h in range(H)], axis=1)
        num = numi_ref[rs, :] + _split_dot(eint_ref[rs, :], e128_ref) * inter
        hh = num * _split_dot(rd_ref[rs, :], e128_ref)
        ms = _split_dot(hh * hh, seg_ref) * (1.0 / DV)
        y_ref[rs, :] = hh * _split_dot(lax.rsqrt(ms + EPS), e128_ref) * go_ref[rs, :]


def _mlstm_sample_post(inter, numi, rd, eint, go, e128, seg128, *, t):
    rows = inter.shape[1]

    def whole(shape):
        return pl.BlockSpec(shape, lambda i: (0,) * len(shape))

    return pl.pallas_call(
        functools.partial(_mlstm_sample_post_kernel, t=t, nb=rows // t),
        grid=(1,),
        in_specs=[whole((H, rows, LANES)), whole((rows, D)), whole((rows, FG)), whole((rows, FG)),
                  whole((rows, D)), _const_spec((FG, D)), _const_spec((D, FG))],
        out_specs=whole((rows, D)),
        out_shape=jax.ShapeDtypeStruct((rows, D), F32),
        compiler_params=_params("arbitrary"),
        name="mlstm_sample_post",
    )(inter, numi, rd, eint, go, e128, seg128)


def _conv_taps(u, prev2, prev1, wc_ref):
    return wc_ref[0:1] * prev2 + wc_ref[1:2] * prev1 + wc_ref[2:3] * u


def _conv_prompt_tile(x_ref, mod_ref, g_ref, win_ref, wc_ref, wout_ref, xo_ref, st_ref, carry_scr,
                      first):
    @pl.when(first)
    def _():
        carry_scr[...] = jnp.zeros(carry_scr.shape, F32)

    x = x_ref[0]
    tm = x.shape[0]
    hb = _modulate(x, g_ref[...], mod_ref[0, 0], mod_ref[0, 1]).astype(BF)
    u = _dot(hb, win_ref[:, D:2 * D]) * _dot(hb, win_ref[:, 2 * D:])
    row = lax.broadcasted_iota(jnp.int32, (tm, D), 0)
    c0 = carry_scr[0:1]
    c1 = carry_scr[1:2]
    prev1 = jnp.where(row == 0, c1, pltpu.roll(u, 1, 0))
    prev2 = jnp.where(row == 0, c0, jnp.where(row == 1, c1, pltpu.roll(u, 2, 0)))
    y = _conv_taps(u, prev2, prev1, wc_ref)
    bg = _dot(hb, win_ref[:, :D])
    xo_ref[0] = x + mod_ref[0, 2] * _dot((bg * y).astype(BF), wout_ref[...])
    carry_scr[...] = u[tm - 2:tm]
    st_ref[0] = u[tm - 2:tm]


def _conv_sample_tile(x_ref, mod_ref, g_ref, win_ref, wc_ref, wout_ref, buf_ref, xo_ref, st_ref,
                      *, t, nb):
    def norm(xr, shift, scale):
        return _modulate(xr, g_ref[...], shift, scale)

    def residual(xr, fr, gate):
        return xr + gate * fr

    x = x_ref[0]
    hb = _by_group(norm, (x,), (mod_ref[0, 0], mod_ref[0, 1])).astype(BF)
    u = _dot(hb, win_ref[:, D:2 * D]) * _dot(hb, win_ref[:, 2 * D:])
    full = [buf_ref[j] for j in range(CONV_W - 1)] + [u[i * nb:(i + 1) * nb] for i in range(t)]
    y = jnp.concatenate([_conv_taps(full[i + 2], full[i], full[i + 1], wc_ref) for i in range(t)],
                        axis=0)
    bg = _dot(hb, win_ref[:, :D])
    out = _dot((bg * y).astype(BF), wout_ref[...])
    xo_ref[0] = _by_group(residual, (x, out), (mod_ref[0, 2],))
    for j in range(CONV_W - 1):
        st_ref[j] = full[t + j]


def _conv_kernel(xp_ref, modp_ref, xs_ref, mods_ref, g_ref, win_ref, wc_ref, wout_ref, buf_ref,
                 xpo_ref, stp_ref, xso_ref, sts_ref, carry_scr, *, n_prompt, nt, t, nb):
    step = pl.program_id(0)

    @pl.when(step < n_prompt)
    def _():
        _conv_prompt_tile(xp_ref, modp_ref, g_ref, win_ref, wc_ref, wout_ref, xpo_ref, stp_ref,
                          carry_scr, step % nt == 0)

    @pl.when(step == n_prompt)
    def _():
        _conv_sample_tile(xs_ref, mods_ref, g_ref, win_ref, wc_ref, wout_ref, buf_ref, xso_ref,
                          sts_ref, t=t, nb=nb)


def _conv(xp, mod_p, xs, mod, layer, g, w_in, w_conv, w_out, buf, *, tm, t, casts=()):
    nb, tp, _ = xp.shape
    nt = tp // tm
    n_prompt = nb * nt
    rows_s = xs.shape[1]
    seqs = rows_s // t
    tok, per_seq = _prompt_tiles(nb, nt)
    args = (xp, mod_p, xs, mod, g.reshape(1, D), w_in, w_conv, w_out, buf)
    c_in, c_out, c_shapes, c_args = _cast_specs(
        casts, n_prompt, lambda s: jnp.minimum(s, n_prompt - 1))
    body = functools.partial(_conv_kernel, n_prompt=n_prompt, nt=nt, t=t, nb=seqs)
    return pl.pallas_call(
        _carry_casts(body, len(args), 4, len(casts)),
        grid=(n_prompt + 1,),
        in_specs=[
            tok(tm),
            per_seq((N_MOD, 1, D)),
            _const_spec((1, rows_s, D)),
            _group_mod_spec(layer, seqs),
            _const_spec((1, D)),
            _const_spec((D, 3 * D)),
            _const_spec((CONV_W, D)),
            _const_spec((D, D)),
            _const_spec((CONV_W - 1, seqs, D)),
        ] + c_in,
        out_specs=[tok(tm), per_seq((CONV_W - 1, D)), _whole_spec((1, rows_s, D)),
                   _whole_spec((CONV_W - 1, seqs, D))] + c_out,
        out_shape=[jax.ShapeDtypeStruct(xp.shape, F32),
                   jax.ShapeDtypeStruct((nb, CONV_W - 1, D), F32),
                   jax.ShapeDtypeStruct(xs.shape, F32),
                   jax.ShapeDtypeStruct((CONV_W - 1, seqs, D), F32)] + c_shapes,
        scratch_shapes=[pltpu.VMEM((CONV_W - 1, D), F32)],
        compiler_params=_params("arbitrary"),
        name="conv",
    )(*args, *c_args)


def kernel(x_prompt, x_sample, c_prompt, c_sample, state_mlstm_C, state_mlstm_n, state_mlstm_m,
           state_conv, ada_w, ada_b, norm_g, final_norm_g, mlstm_w_in, mlstm_b_gates,
           mlstm_head_g, mlstm_w_out, conv_w_in, conv_w, conv_w_out, mlp_w_up, mlp_w_down):
    assert ada_w.shape[0] == 2 and mlstm_w_in.shape[0] == 1 and conv_w_in.shape[0] == 1
    bp, tp, _ = x_prompt.shape
    bs, ts, _ = x_sample.shape
    assert ts == 4 and tp % 512 == 0

    mod, w_row = _ada(jnp.concatenate([c_sample, c_prompt], axis=0), ada_w, ada_b,
                      casts=((mlstm_w_in, 0),))
    mod_p = jnp.transpose(mod[:, :, bs:], (0, 2, 1, 3))[:, :, :, None]

    gate_pad = jnp.zeros((D, LANES - H), BF)
    w_col = jnp.concatenate([w_row[:, HK:2 * HK] * (DK ** -0.5), w_row[:, NP:NP + H], gate_pad,
                             w_row[:, NP + H:], gate_pad], axis=1).T
    b_g = jnp.pad(mlstm_b_gates[0].reshape(2, H), ((0, 0), (0, LANES - H)))
    wout0 = mlstm_w_out[0].astype(BF)

    x, s_p, m_p, up0, down0 = _mlstm_prompt(
        x_prompt, mod_p[0], norm_g[0, 0], w_row, w_col, mlstm_b_gates[0], mlstm_head_g[0], wout0,
        tm=512, casts=((mlp_w_up, 0), (mlp_w_down, 0)))
    prompt_c = s_p[None, :, :, :, :DV]
    prompt_n = s_p[None, :, :, :, DV]
    prompt_m = m_p[None, :, :, 0]

    xs = jnp.transpose(x_sample, (1, 0, 2)).reshape(1, ts * bs, D)
    head_of_v = jnp.arange(D, dtype=jnp.int32) // DV
    head_of_k = jnp.arange(HK, dtype=jnp.int32) // DK
    lanes = jnp.arange(FG, dtype=jnp.int32)
    e128 = (lanes[:, None] == head_of_v[None, :]).astype(BF)
    e64 = (lanes[:, None] == head_of_k[None, :]).astype(BF)
    m0 = jnp.pad(state_mlstm_m[0], ((0, 0), (0, FG - H)))
    n0 = state_mlstm_n[0].reshape(bs, HK)
    q, ka, v, dec, numi, rd, eint, go, n1, m1 = _mlstm_sample_pre(
        xs, mod, 0, norm_g[0, 0], w_row, w_col, b_g, m0, n0, mlstm_head_g[0], e64.T, e64, e128,
        t=ts)
    inter, c1 = _mlstm_sample_state(q, ka, v, dec, state_mlstm_C[0], t=ts, bb=16)
    y = _mlstm_sample_post(inter, numi, rd, eint, go, e128, e128.T, t=ts)
    sample_n = n1.reshape(bs, H, DK)[None]
    sample_m = m1[None, :, :H]

    x, xs, cwin, cwout = _mlp(x, mod_p[0], xs, mod, 0, norm_g[0, 1], up0, down0, tm=512, seqs=bs,
                              pre=(y.reshape(1, ts * bs, D), wout0),
                              casts=((conv_w_in, 0), (conv_w_out, 0)))
    x, conv_p, xs, conv_s, up1, down1 = _conv(
        x, mod_p[1], xs, mod, 1, norm_g[1, 0], cwin, conv_w[0], cwout,
        jnp.transpose(state_conv[0], (1, 0, 2)), tm=512, t=ts,
        casts=((mlp_w_up, 1), (mlp_w_down, 1)))
    y_prompt, ys = _mlp(x, mod_p[1], xs, mod, 1, norm_g[1, 1], up1, down1, tm=512, seqs=bs,
                        final_g=final_norm_g)
    prompt_conv = conv_p[None]
    y_sample = jnp.transpose(ys.reshape(ts, bs, D), (1, 0, 2))
    sample_conv = jnp.transpose(conv_s, (1, 0, 2))[None]

    return (y_prompt, y_sample, prompt_c, prompt_n, prompt_m, prompt_conv,
            c1[None], sample_n, sample_m, sample_conv)
```

```python
import functools

import jax
import jax.numpy as jnp
from jax import lax
from jax.experimental import pallas as pl
from jax.experimental.pallas import tpu as pltpu

D = 1024
H = 8
DK = 64
DV = 128
HK = H * DK
NP = 2 * HK + 2 * D
FG = 128
LANES = 128
SUBLANES = 8
DFF = 4 * D
N_MOD = 6
CONV_W = 3
EPS = 1e-6
CHUNK = 128
FF_CHUNK = 1024
BF = jnp.bfloat16
F32 = jnp.float32
VMEM_LIMIT_BYTES = 56 * 1024 * 1024


def _dot(a, b):
    return jnp.dot(a, b, preferred_element_type=F32)


def _dot_nt(a, b):
    return lax.dot_general(a, b, (((1,), (1,)), ((), ())), preferred_element_type=F32)


def _dot_tn(a, b):
    return lax.dot_general(a, b, (((0,), (0,)), ((), ())), preferred_element_type=F32)


def _rms(x):
    return x * lax.rsqrt(jnp.mean(x * x, axis=-1, keepdims=True) + EPS)


def _modulate(x, g, shift, scale):
    return _rms(x) * g * (1.0 + scale) + shift


def _log_sigmoid(x):
    return jnp.minimum(x, 0.0) - jnp.log1p(jnp.exp(-jnp.abs(x)))


def _scan_lanes(x, op, ident, n):
    lane = lax.broadcasted_iota(jnp.int32, x.shape, 1) % n
    d = 1
    while d < n:
        x = op(x, jnp.where(lane >= d, pltpu.roll(x, d, 1), ident))
        d *= 2
    return x


def _by_group(fn, xs, mods):
    rows, r = xs[0].shape[0], mods[0].shape[0]
    if r == 1 or r == rows:
        return fn(*xs, *mods)
    return jnp.concatenate(
        [fn(*[x[i:i + r] for x in xs], *mods) for i in range(0, rows, r)], axis=0)


def _split_dot(x, m_ref):
    hi = x.astype(BF)
    lo = (x - hi.astype(F32)).astype(BF)
    m = m_ref[...]
    return _dot(hi, m) + _dot(lo, m)


def _cast_specs(jobs, steps, slab_of):
    ins, outs, shapes, args = [], [], [], []
    for w, layer in jobs:
        _, r, c = w.shape
        slab = r // steps
        assert slab * steps == r and slab % (2 * SUBLANES) == 0
        ins.append(pl.BlockSpec((1, slab, c), lambda *g, layer=layer: (layer, slab_of(*g), 0)))
        outs.append(pl.BlockSpec((slab, c), lambda *g: (slab_of(*g), 0)))
        shapes.append(jax.ShapeDtypeStruct((r, c), BF))
        args.append(w)
    return ins, outs, shapes, args


def _carry_casts(body, n_in, n_out, n_cast):
    def kernel(*refs):
        a, b = n_in, n_in + n_cast
        c, d = b + n_out, b + n_out + n_cast
        for src, dst in zip(refs[a:b], refs[c:d]):
            dst[...] = src[0].astype(BF)
        body(*refs[:a], *refs[b:c], *refs[d:])
    return kernel


def _const_spec(shape):
    nd = len(shape)
    return pl.BlockSpec(shape, lambda *_: (0,) * nd, pipeline_mode=pl.Buffered(1))


def _params(*sem):
    return pltpu.CompilerParams(dimension_semantics=sem, vmem_limit_bytes=VMEM_LIMIT_BYTES)


def _ada_kernel(c_ref, w_ref, b_ref, o_ref):
    c = c_ref[...]
    sc = (c * jax.nn.sigmoid(c)).astype(BF)
    o_ref[0, 0] = _dot(sc, w_ref[0].astype(BF)) + b_ref[0]


def _ada(c_all, ada_w, ada_b):
    depth, _, n = ada_w.shape
    nb = c_all.shape[0]
    return pl.pallas_call(
        _ada_kernel,
        grid=(depth, n // D),
        in_specs=[
            pl.BlockSpec((nb, D), lambda i, j: (0, 0)),
            pl.BlockSpec((1, D, D), lambda i, j: (i, 0, j)),
            pl.BlockSpec((1, 1, D), lambda i, j: (i, 0, j)),
        ],
        out_specs=pl.BlockSpec((1, 1, nb, D), lambda i, j: (i, j, 0, 0)),
        out_shape=jax.ShapeDtypeStruct((depth, n // D, nb, D), F32),
        compiler_params=_params("arbitrary", "arbitrary"),
        name="ada",
    )(c_all, ada_w, ada_b.reshape(depth, 1, n))


def _mlp_tile(x_ref, mod_ref, g_ref, wu_ref, wd_ref, fg_ref, o_ref, y_ref=None, wo_ref=None):
    def residual(xr, fr, gate):
        return xr + gate * fr

    def norm(xr, shift, scale):
        return _modulate(xr, g_ref[...], shift, scale)

    x = x_ref[0]
    if y_ref is not None:
        x = _by_group(residual, (x, _dot(y_ref[0].astype(BF), wo_ref[...])), (mod_ref[0, 2],))
    hb = _by_group(norm, (x,), (mod_ref[0, 3], mod_ref[0, 4])).astype(BF)
    acc = jnp.zeros(x.shape, F32)
    for c in range(DFF // FF_CHUNK):
        cs = slice(c * FF_CHUNK, (c + 1) * FF_CHUNK)
        hid = jnp.maximum(_dot(hb, wu_ref[:, cs]), 0.0)
        acc = acc + _dot((hid * hid).astype(BF), wd_ref[cs, :])
    x = _by_group(residual, (x, acc), (mod_ref[0, 5],))
    if fg_ref is not None:
        x = _rms(x) * fg_ref[...]
    o_ref[0] = x


def _mlp_kernel(*refs, n_prompt, pre, final):
    refs = list(refs)
    xp_ref, modp_ref, xs_ref = refs[:3]
    refs = refs[3:]
    ys_ref = wo_ref = fg_ref = None
    if pre:
        ys_ref, wo_ref = refs[:2]
        refs = refs[2:]
    mods_ref, g_ref, wu_ref, wd_ref = refs[:4]
    refs = refs[4:]
    if final:
        fg_ref = refs.pop(0)
    xpo_ref, xso_ref = refs
    step = pl.program_id(0)

    @pl.when(step < n_prompt)
    def _():
        _mlp_tile(xp_ref, modp_ref, g_ref, wu_ref, wd_ref, fg_ref, xpo_ref)

    @pl.when(step == n_prompt)
    def _():
        _mlp_tile(xs_ref, mods_ref, g_ref, wu_ref, wd_ref, fg_ref, xso_ref, ys_ref, wo_ref)


def _group_mod_spec(layer, r):
    return pl.BlockSpec((1, N_MOD, r, D), lambda *_: (layer, 0, 0, 0),
                        pipeline_mode=pl.Buffered(1))


def _prompt_tiles(nb, nt):
    def seq_tile(step):
        c = jnp.minimum(step, nb * nt - 1)
        return c // nt, c % nt

    def tok(tm):
        return pl.BlockSpec((1, tm, D), lambda s: (*seq_tile(s), 0))

    def per_seq(shape):
        return pl.BlockSpec((1,) + shape, lambda s: (seq_tile(s)[0],) + (0,) * len(shape))

    return tok, per_seq


def _whole_spec(shape):
    return pl.BlockSpec(shape, lambda *_: (0,) * len(shape))


def _mlp(xp, mod_p, xs, mod, layer, g, w_up, w_down, *, tm, seqs, pre=None, final_g=None,
         casts=()):
    nb, t, _ = xp.shape
    nt = t // tm
    n_prompt = nb * nt
    rows_s = xs.shape[1]
    tok, per_seq = _prompt_tiles(nb, nt)
    args = [xp, mod_p, xs]
    specs = [tok(tm), per_seq((N_MOD, 1, D)), _const_spec((1, rows_s, D))]
    if pre is not None:
        args += list(pre)
        specs += [_const_spec((1, rows_s, D)), _const_spec((D, D))]
    args += [mod, g.reshape(1, D), w_up, w_down]
    specs += [_group_mod_spec(layer, seqs), _const_spec((1, D)), _const_spec((D, DFF)),
              _const_spec((DFF, D))]
    if final_g is not None:
        args.append(final_g.reshape(1, D))
        specs.append(_const_spec((1, D)))
    c_in, c_out, c_shapes, c_args = _cast_specs(
        casts, n_prompt, lambda s: jnp.minimum(s, n_prompt - 1))
    body = functools.partial(_mlp_kernel, n_prompt=n_prompt, pre=pre is not None,
                             final=final_g is not None)
    return pl.pallas_call(
        _carry_casts(body, len(args), 2, len(casts)),
        grid=(n_prompt + 1,),
        in_specs=specs + c_in,
        out_specs=[tok(tm), _whole_spec((1, rows_s, D))] + c_out,
        out_shape=[jax.ShapeDtypeStruct(xp.shape, F32), jax.ShapeDtypeStruct(xs.shape, F32)]
        + c_shapes,
        compiler_params=_params("arbitrary"),
        name="mlp",
    )(*args, *c_args)


def _mlstm_prompt_kernel(x_ref, mod_ref, g_ref, wrow_ref, wcol_ref, bg_ref, gh_ref, wout_ref,
                         xo_ref, s_ref, m_ref, y_scr, *, tm):
    @pl.when(pl.program_id(1) == 0)
    def _():
        s_ref[...] = jnp.zeros(s_ref.shape, F32)
        m_ref[...] = jnp.zeros(m_ref.shape, F32)

    x = x_ref[0]
    hb = _modulate(x, g_ref[...], mod_ref[0, 0], mod_ref[0, 1]).astype(BF)
    n = CHUNK
    ig_all = _dot_nt(wcol_ref[HK:HK + H, :], hb) + bg_ref[0:H]
    fg_all = _dot_nt(wcol_ref[HK + LANES:HK + LANES + H, :], hb) + bg_ref[H:2 * H]
    b_all = _scan_lanes(_log_sigmoid(fg_all), jnp.add, 0.0, n)
    r_all = ig_all - b_all
    cm_all = _scan_lanes(r_all, jnp.maximum, -jnp.inf, n)
    pcol = _dot_nt(wcol_ref[0:HK, :], hb)
    prow = _dot(hb, wrow_ref[...])
    causal = (lax.broadcasted_iota(jnp.int32, (n, n), 0)
              >= lax.broadcasted_iota(jnp.int32, (n, n), 1))
    zeros_k = jnp.zeros((DK, n), BF)
    zeros_s = jnp.zeros((DK, 2 * DV), BF)
    zeros_v = jnp.zeros((n, DV), BF)
    ones_v = jnp.ones((n, DV), BF)
    pad = jnp.zeros((n - 2 * H, n), F32)

    for c in range(tm // n):
        ts = slice(c * n, (c + 1) * n)
        r = r_all[:, ts]
        m0 = m_ref[0]
        mx = jnp.maximum(m0, cm_all[:, ts])
        m = b_all[:, ts] + mx
        e_neg = jnp.exp(-m)
        mx_last = jnp.broadcast_to(mx[:, n - 1:n], (H, n))
        a = jnp.exp(r - mx_last)
        decay = jnp.exp(m0 - mx_last)
        m_ref[0] = jnp.broadcast_to(m[:, n - 1:n], (H, n))
        cols = jnp.concatenate([mx, e_neg, pad], axis=0).T

        for h in range(H):
            lo = h % 2 == 0
            mx_b = jnp.broadcast_to(cols[:, h:h + 1], (n, n))
            floor_b = jnp.broadcast_to(cols[:, H + h:H + h + 1], (n, DV))
            dec = jnp.exp(jnp.where(causal, r[h:h + 1, :] - mx_b, -jnp.inf))
            e_int = jnp.exp(m0[h:h + 1, :] - mx_b)
            qp = prow[ts, (h // 2) * 2 * DK:(h // 2 + 1) * 2 * DK]
            kt = pcol[h * DK:(h + 1) * DK, ts]
            ktb = kt.astype(BF)
            s = _dot(qp.astype(BF),
                     jnp.concatenate([ktb, zeros_k] if lo else [zeros_k, ktb], axis=0)) * dec
            st = s_ref[0, h]
            stb = st.astype(BF)
            v = prow[ts, HK + h * DV:HK + (h + 1) * DV].astype(BF)
            rhs = jnp.concatenate(
                [jnp.concatenate([v, zeros_v], axis=1)]
                + ([stb, zeros_s] if lo else [zeros_s, stb]), axis=0)
            lhs = jnp.concatenate([s.astype(BF), (qp * e_int).astype(BF)], axis=1)
            res = _dot(lhs, rhs)
            den = jnp.sum(s, axis=-1, keepdims=True) + res[:, DV:]
            hh = res[:, :DV] / jnp.maximum(jnp.abs(den), floor_b)
            hh = hh * lax.rsqrt(jnp.mean(hh * hh, axis=-1, keepdims=True) + EPS)
            y_scr[ts, h * DV:(h + 1) * DV] = hh
            ka = (kt * a[h:h + 1, :]).astype(BF)
            dc = decay[h:h + 1, :]
            s_ref[0, h] = (jnp.concatenate([dc, dc], axis=1) * st
                           + _dot(ka, jnp.concatenate([v, ones_v], axis=1)))

    o = prow[:, HK + D:]
    y = (y_scr[...] * gh_ref[...] * jax.nn.sigmoid(o)).astype(BF)
    xo_ref[0] = x + mod_ref[0, 2] * _dot(y, wout_ref[...])


def _mlstm_prompt(x, mod, g, w_row, w_col, b_g, g_head, w_out, *, tm, casts=()):
    nb, t, _ = x.shape
    nt = t // tm
    tok = pl.BlockSpec((1, tm, D), lambda b, i: (b, i, 0))
    args = (x, mod, g.reshape(1, D), w_row, w_col,
            jnp.broadcast_to(b_g.reshape(2 * H, 1), (2 * H, tm)), g_head.reshape(1, D), w_out)
    c_in, c_out, c_shapes, c_args = _cast_specs(casts, nb * nt, lambda b, i: b * nt + i)
    return pl.pallas_call(
        _carry_casts(functools.partial(_mlstm_prompt_kernel, tm=tm), len(args), 3, len(casts)),
        grid=(nb, nt),
        in_specs=[
            tok,
            pl.BlockSpec((1, N_MOD, 1, D), lambda b, i: (b, 0, 0, 0)),
            _const_spec((1, D)),
            _const_spec((D, HK + 2 * D)),
            _const_spec((HK + 2 * LANES, D)),
            _const_spec((2 * H, tm)),
            _const_spec((1, D)),
            _const_spec((D, D)),
        ] + c_in,
        out_specs=[
            tok,
            pl.BlockSpec((1, H, DK, 2 * DV), lambda b, i: (b, 0, 0, 0)),
            pl.BlockSpec((1, H, CHUNK), lambda b, i: (b, 0, 0)),
        ] + c_out,
        out_shape=[
            jax.ShapeDtypeStruct(x.shape, F32),
            jax.ShapeDtypeStruct((nb, H, DK, 2 * DV), F32),
            jax.ShapeDtypeStruct((nb, H, CHUNK), F32),
        ] + c_shapes,
        scratch_shapes=[pltpu.VMEM((tm, D), F32)],
        compiler_params=_params("arbitrary", "arbitrary"),
        name="mlstm_prompt",
    )(*args, *c_args)


def _mlstm_sample_pre_kernel(x_ref, mod_ref, g_ref, wrow_ref, wcol_ref, bg_ref, m0_ref, n0_ref,
                             gh_ref, seg_ref, e64_ref, e128_ref,
                             q_ref, ka_ref, v_ref, dec_ref, numi_ref, rd_ref, eint_ref, go_ref,
                             n1_ref, m1_ref, *, t, nb):
    def norm(xr, shift, scale):
        return _modulate(xr, g_ref[...], shift, scale)

    hb = _by_group(norm, (x_ref[0],), (mod_ref[0, 0], mod_ref[0, 1])).astype(BF)
    p = _dot(hb, wrow_ref[...])
    k_all = _dot_nt(hb, wcol_ref[0:HK, :])
    ig_all = _dot_nt(hb, wcol_ref[HK:HK + LANES, :]) + bg_ref[0:1]
    fg_all = _dot_nt(hb, wcol_ref[HK + LANES:HK + 2 * LANES, :]) + bg_ref[1:2]

    def rows(i):
        return slice(i * nb, (i + 1) * nb)

    q = [p[rows(i), 0:HK] for i in range(t)]
    k = [k_all[rows(i)] for i in range(t)]
    v = [p[rows(i), HK:HK + D] for i in range(t)]
    m0 = m0_ref[...]
    n0 = n0_ref[...]

    b, r, mx = [], [], []
    for i in range(t):
        lf = _log_sigmoid(fg_all[rows(i)])
        b.append(lf if i == 0 else b[-1] + lf)
        r.append(ig_all[rows(i)] - b[i])
        cm = r[i] if i == 0 else jnp.maximum(cm, r[i])
        mx.append(jnp.maximum(m0, cm))
    mx_last = mx[t - 1]
    decay = jnp.exp(m0 - mx_last)
    m1_ref[...] = b[t - 1] + mx_last
    dec_x = _split_dot(decay, e128_ref)
    n1 = _split_dot(decay, e64_ref) * n0

    for i in range(t):
        ka = k[i] * _split_dot(jnp.exp(r[i] - mx_last), e64_ref)
        n1 = n1 + ka
        seq_rows = pl.ds(i, nb, stride=t)
        for c in range(HK // LANES):
            cs = slice(c * LANES, (c + 1) * LANES)
            ka_ref[c, seq_rows, :] = ka[:, cs]
            q_ref[c, seq_rows, :] = q[i][:, cs]
        for c in range(D // LANES):
            cs = slice(c * LANES, (c + 1) * LANES)
            v_ref[c, seq_rows, :] = v[i][:, cs]
            dec_ref[c, seq_rows, :] = dec_x[:, cs]
        e_int = jnp.exp(m0 - mx[i])
        den = e_int * _split_dot(q[i] * n0, seg_ref)
        numi = jnp.zeros((nb, D), F32)
        for s in range(i + 1):
            w = _split_dot(q[i] * k[s], seg_ref) * jnp.exp(jnp.minimum(r[s] - mx[i], 0.0))
            den = den + w
            numi = numi + _split_dot(w, e128_ref) * v[s]
        numi_ref[rows(i), :] = numi
        rd_ref[rows(i), :] = 1.0 / jnp.maximum(jnp.abs(den), jnp.exp(-(b[i] + mx[i])))
        eint_ref[rows(i), :] = e_int
        go_ref[rows(i), :] = jax.nn.sigmoid(p[rows(i), HK + D:]) * gh_ref[...]
    n1_ref[...] = n1


def _mlstm_sample_pre(x, mod, layer, g, w_row, w_col, b_g, m0, n0, g_head, seg64, e64, e128, *, t):
    rows = x.shape[1]
    nb = rows // t

    def whole(shape):
        return pl.BlockSpec(shape, lambda i: (0,) * len(shape))

    outs = [(HK // LANES, rows, LANES), (HK // LANES, rows, LANES), (D // LANES, rows, LANES),
            (D // LANES, rows, LANES), (rows, D), (rows, FG), (rows, FG), (rows, D), (nb, HK),
            (nb, FG)]
    return pl.pallas_call(
        functools.partial(_mlstm_sample_pre_kernel, t=t, nb=nb),
        grid=(1,),
        in_specs=[
            whole((1, rows, D)),
            _group_mod_spec(layer, nb),
            _const_spec((1, D)),
            _const_spec((D, HK + 2 * D)),
            _const_spec((HK + 2 * LANES, D)),
            _const_spec((2, LANES)),
            whole((nb, FG)),
            whole((nb, HK)),
            _const_spec((1, D)),
            _const_spec((HK, FG)),
            _const_spec((FG, HK)),
            _const_spec((FG, D)),
        ],
        out_specs=[whole(s) for s in outs],
        out_shape=[jax.ShapeDtypeStruct(s, F32) for s in outs],
        compiler_params=_params("arbitrary"),
        name="mlstm_sample_pre",
    )(x, mod, g.reshape(1, D), w_row, w_col, b_g, m0, n0, g_head.reshape(1, D), seg64, e64, e128)


def _mlstm_sample_state_kernel(q_ref, ka_ref, v_ref, dec_ref, c0_ref, inter_ref, c1_ref, *, bb, t):
    per = SUBLANES // t
    seq_of_row = lax.broadcasted_iota(jnp.int32, (SUBLANES, 1), 0) // t
    zeros_c = jnp.zeros((DK, DV), BF)

    def tile(i, carry):
        rs = pl.ds(pl.multiple_of(i * SUBLANES, SUBLANES), SUBLANES)
        for h in range(H):
            lo = h % 2 == 0
            half = slice((h % 2) * DK, (h % 2 + 1) * DK)
            qp = q_ref[h // 2, rs, :].astype(BF)
            kap = ka_ref[h // 2, rs, :]
            vh = v_ref[h, rs, :].astype(BF)
            dech = dec_ref[h, rs, :]
            res = None
            for w in range(per):
                seq = i * per + w
                c0 = c0_ref[seq, h]
                c0b = c0.astype(BF)
                rw = _dot(qp, jnp.concatenate([c0b, zeros_c] if lo else [zeros_c, c0b], axis=0))
                res = rw if res is None else jnp.where(seq_of_row == w, rw, res)
                kaw = jnp.where(seq_of_row == w, kap, 0.0).astype(BF)
                c1_ref[seq, h] = dech[w * t:w * t + 1, :] * c0 + _dot_tn(kaw, vh)[half, :]
            inter_ref[h, rs, :] = res
        return carry

    lax.fori_loop(0, bb // per, tile, 0, unroll=4)


def _mlstm_sample_state(q, ka, v, dec, c0, *, t, bb):
    rows = q.shape[1]
    nb = rows // t
    assert SUBLANES % t == 0 and bb % (SUBLANES // t) == 0 and nb % bb == 0

    def tok(blocks):
        return pl.BlockSpec((blocks, bb * t, LANES), lambda i: (0, i, 0))

    state = pl.BlockSpec((bb, H, DK, DV), lambda i: (i, 0, 0, 0))
    return pl.pallas_call(
        functools.partial(_mlstm_sample_state_kernel, bb=bb, t=t),
        grid=(nb // bb,),
        in_specs=[tok(HK // LANES), tok(HK // LANES), tok(H), tok(H), state],
        out_specs=[tok(H), state],
        out_shape=[jax.ShapeDtypeStruct((H, rows, LANES), F32),
                   jax.ShapeDtypeStruct(c0.shape, F32)],
        compiler_params=_params("arbitrary"),
        name="mlstm_sample_state",
    )(q, ka, v, dec, c0)


def _mlstm_sample_post_kernel(inter_ref, numi_ref, rd_ref, eint_ref, go_ref, e128_ref, seg_ref,
                              y_ref, *, t, nb):
    for i in range(t):
        rs = slice(i * nb, (i + 1) * nb)
        inter = jnp.concatenate(
            [inter_ref[h, pl.ds(i, nb, stride=t), :] for h in range(H)], axis=1)
        num = numi_ref[rs, :] + _split_dot(eint_ref[rs, :], e128_ref) * inter
        hh = num * _split_dot(rd_ref[rs, :], e128_ref)
        ms = _split_dot(hh * hh, seg_ref) * (1.0 / DV)
        y_ref[rs, :] = hh * _split_dot(lax.rsqrt(ms + EPS), e128_ref) * go_ref[rs, :]


def _mlstm_sample_post(inter, numi, rd, eint, go, e128, seg128, *, t):
    rows = inter.shape[1]

    def whole(shape):
        return pl.BlockSpec(shape, lambda i: (0,) * len(shape))

    return pl.pallas_call(
        functools.partial(_mlstm_sample_post_kernel, t=t, nb=rows // t),
        grid=(1,),
        in_specs=[whole((H, rows, LANES)), whole((rows, D)), whole((rows, FG)), whole((rows, FG)),
                  whole((rows, D)), _const_spec((FG, D)), _const_spec((D, FG))],
        out_specs=whole((rows, D)),
        out_shape=jax.ShapeDtypeStruct((rows, D), F32),
        compiler_params=_params("arbitrary"),
        name="mlstm_sample_post",
    )(inter, numi, rd, eint, go, e128, seg128)


def _conv_taps(u, prev2, prev1, wc_ref):
    return wc_ref[0:1] * prev2 + wc_ref[1:2] * prev1 + wc_ref[2:3] * u


def _conv_prompt_tile(x_ref, mod_ref, g_ref, win_ref, wc_ref, wout_ref, xo_ref, st_ref, carry_scr,
                      first):
    @pl.when(first)
    def _():
        carry_scr[...] = jnp.zeros(carry_scr.shape, F32)

    x = x_ref[0]
    tm = x.shape[0]
    hb = _modulate(x, g_ref[...], mod_ref[0, 0], mod_ref[0, 1]).astype(BF)
    u = _dot(hb, win_ref[:, D:2 * D]) * _dot(hb, win_ref[:, 2 * D:])
    row = lax.broadcasted_iota(jnp.int32, (tm, D), 0)
    c0 = carry_scr[0:1]
    c1 = carry_scr[1:2]
    prev1 = jnp.where(row == 0, c1, pltpu.roll(u, 1, 0))
    prev2 = jnp.where(row == 0, c0, jnp.where(row == 1, c1, pltpu.roll(u, 2, 0)))
    y = _conv_taps(u, prev2, prev1, wc_ref)
    bg = _dot(hb, win_ref[:, :D])
    xo_ref[0] = x + mod_ref[0, 2] * _dot((bg * y).astype(BF), wout_ref[...])
    carry_scr[...] = u[tm - 2:tm]
    st_ref[0] = u[tm - 2:tm]


def _conv_sample_tile(x_ref, mod_ref, g_ref, win_ref, wc_ref, wout_ref, buf_ref, xo_ref, st_ref,
                      *, t, nb):
    def norm(xr, shift, scale):
        return _modulate(xr, g_ref[...], shift, scale)

    def residual(xr, fr, gate):
        return xr + gate * fr

    x = x_ref[0]
    hb = _by_group(norm, (x,), (mod_ref[0, 0], mod_ref[0, 1])).astype(BF)
    u = _dot(hb, win_ref[:, D:2 * D]) * _dot(hb, win_ref[:, 2 * D:])
    full = [buf_ref[j] for j in range(CONV_W - 1)] + [u[i * nb:(i + 1) * nb] for i in range(t)]
    y = jnp.concatenate([_conv_taps(full[i + 2], full[i], full[i + 1], wc_ref) for i in range(t)],
                        axis=0)
    bg = _dot(hb, win_ref[:, :D])
    out = _dot((bg * y).astype(BF), wout_ref[...])
    xo_ref[0] = _by_group(residual, (x, out), (mod_ref[0, 2],))
    for j in range(CONV_W - 1):
        st_ref[j] = full[t + j]


def _conv_kernel(xp_ref, modp_ref, xs_ref, mods_ref, g_ref, win_ref, wc_ref, wout_ref, buf_ref,
                 xpo_ref, stp_ref, xso_ref, sts_ref, carry_scr, *, n_prompt, nt, t, nb):
    step = pl.program_id(0)

    @pl.when(step < n_prompt)
    def _():
        _conv_prompt_tile(xp_ref, modp_ref, g_ref, win_ref, wc_ref, wout_ref, xpo_ref, stp_ref,
                          carry_scr, step % nt == 0)

    @pl.when(step == n_prompt)
    def _():
        _conv_sample_tile(xs_ref, mods_ref, g_ref, win_ref, wc_ref, wout_ref, buf_ref, xso_ref,
                          sts_ref, t=t, nb=nb)


def _conv(xp, mod_p, xs, mod, layer, g, w_in, w_conv, w_out, buf, *, tm, t, casts=()):
    nb, tp, _ = xp.shape
    nt = tp // tm
    n_prompt = nb * nt
    rows_s = xs.shape[1]
    seqs = rows_s // t
    tok, per_seq = _prompt_tiles(nb, nt)
    args = (xp, mod_p, xs, mod, g.reshape(1, D), w_in, w_conv, w_out, buf)
    c_in, c_out, c_shapes, c_args = _cast_specs(
        casts, n_prompt, lambda s: jnp.minimum(s, n_prompt - 1))
    body = functools.partial(_conv_kernel, n_prompt=n_prompt, nt=nt, t=t, nb=seqs)
    return pl.pallas_call(
        _carry_casts(body, len(args), 4, len(casts)),
        grid=(n_prompt + 1,),
        in_specs=[
            tok(tm),
            per_seq((N_MOD, 1, D)),
            _const_spec((1, rows_s, D)),
            _group_mod_spec(layer, seqs),
            _const_spec((1, D)),
            _const_spec((D, 3 * D)),
            _const_spec((CONV_W, D)),
            _const_spec((D, D)),
            _const_spec((CONV_W - 1, seqs, D)),
        ] + c_in,
        out_specs=[tok(tm), per_seq((CONV_W - 1, D)), _whole_spec((1, rows_s, D)),
                   _whole_spec((CONV_W - 1, seqs, D))] + c_out,
        out_shape=[jax.ShapeDtypeStruct(xp.shape, F32),
                   jax.ShapeDtypeStruct((nb, CONV_W - 1, D), F32),
                   jax.ShapeDtypeStruct(xs.shape, F32),
                   jax.ShapeDtypeStruct((CONV_W - 1, seqs, D), F32)] + c_shapes,
        scratch_shapes=[pltpu.VMEM((CONV_W - 1, D), F32)],
        compiler_params=_params("arbitrary"),
        name="conv",
    )(*args, *c_args)


def kernel(x_prompt, x_sample, c_prompt, c_sample, state_mlstm_C, state_mlstm_n, state_mlstm_m,
           state_conv, ada_w, ada_b, norm_g, final_norm_g, mlstm_w_in, mlstm_b_gates,
           mlstm_head_g, mlstm_w_out, conv_w_in, conv_w, conv_w_out, mlp_w_up, mlp_w_down):
    assert ada_w.shape[0] == 2 and mlstm_w_in.shape[0] == 1 and conv_w_in.shape[0] == 1
    bp, tp, _ = x_prompt.shape
    bs, ts, _ = x_sample.shape
    assert ts == 4 and tp % 512 == 0

    mod = _ada(jnp.concatenate([c_sample, c_prompt], axis=0), ada_w, ada_b)
    mod_p = jnp.transpose(mod[:, :, bs:], (0, 2, 1, 3))[:, :, :, None]

    w_in0 = mlstm_w_in[0]
    w_row = jnp.concatenate([w_in0[:, :HK], w_in0[:, 2 * HK:NP]], axis=1).astype(BF)
    gate_pad = jnp.zeros((D, LANES - H), F32)
    w_col = jnp.concatenate([w_in0[:, HK:2 * HK] * (DK ** -0.5), w_in0[:, NP:NP + H], gate_pad,
                             w_in0[:, NP + H:], gate_pad], axis=1).T.astype(BF)
    b_g = jnp.pad(mlstm_b_gates[0].reshape(2, H), ((0, 0), (0, LANES - H)))
    wout0 = mlstm_w_out[0].astype(BF)

    x, s_p, m_p, up0, down0 = _mlstm_prompt(
        x_prompt, mod_p[0], norm_g[0, 0], w_row, w_col, mlstm_b_gates[0], mlstm_head_g[0], wout0,
        tm=512, casts=((mlp_w_up, 0), (mlp_w_down, 0)))
    prompt_c = s_p[None, :, :, :, :DV]
    prompt_n = s_p[None, :, :, :, DV]
    prompt_m = m_p[None, :, :, 0]

    xs = jnp.transpose(x_sample, (1, 0, 2)).reshape(1, ts * bs, D)
    head_of_v = jnp.arange(D, dtype=jnp.int32) // DV
    head_of_k = jnp.arange(HK, dtype=jnp.int32) // DK
    lanes = jnp.arange(FG, dtype=jnp.int32)
    e128 = (lanes[:, None] == head_of_v[None, :]).astype(BF)
    e64 = (lanes[:, None] == head_of_k[None, :]).astype(BF)
    m0 = jnp.pad(state_mlstm_m[0], ((0, 0), (0, FG - H)))
    n0 = state_mlstm_n[0].reshape(bs, HK)
    q, ka, v, dec, numi, rd, eint, go, n1, m1 = _mlstm_sample_pre(
        xs, mod, 0, norm_g[0, 0], w_row, w_col, b_g, m0, n0, mlstm_head_g[0], e64.T, e64, e128,
        t=ts)
    inter, c1 = _mlstm_sample_state(q, ka, v, dec, state_mlstm_C[0], t=ts, bb=16)
    y = _mlstm_sample_post(inter, numi, rd, eint, go, e128, e128.T, t=ts)
    sample_n = n1.reshape(bs, H, DK)[None]
    sample_m = m1[None, :, :H]

    x, xs, cwin, cwout = _mlp(x, mod_p[0], xs, mod, 0, norm_g[0, 1], up0, down0, tm=512, seqs=bs,
                              pre=(y.reshape(1, ts * bs, D), wout0),
                              casts=((conv_w_in, 0), (conv_w_out, 0)))
    x, conv_p, xs, conv_s, up1, down1 = _conv(
        x, mod_p[1], xs, mod, 1, norm_g[1, 0], cwin, conv_w[0], cwout,
        jnp.transpose(state_conv[0], (1, 0, 2)), tm=512, t=ts,
        casts=((mlp_w_up, 1), (mlp_w_down, 1)))
    y_prompt, ys = _mlp(x, mod_p[1], xs, mod, 1, norm_g[1, 1], up1, down1, tm=512, seqs=bs,
                        final_g=final_norm_g)
    prompt_conv = conv_p[None]
    y_sample = jnp.transpose(ys.reshape(ts, bs, D), (1, 0, 2))
    sample_conv = jnp.transpose(conv_s, (1, 0, 2))[None]

    return (y_prompt, y_sample, prompt_c, prompt_n, prompt_m, prompt_conv,
            c1[None], sample_n, sample_m, sample_conv)
```

```python
import functools

import jax
import jax.numpy as jnp
from jax import lax
from jax.experimental import pallas as pl
from jax.experimental.pallas import tpu as pltpu

D = 1024
H = 8
DK = 64
DV = 128
HK = H * DK
NP = 2 * HK + 2 * D
FG = 128
LANES = 128
SUBLANES = 8
DFF = 4 * D
N_MOD = 6
CONV_W = 3
EPS = 1e-6
CHUNK = 128
FF_CHUNK = 1024
BF = jnp.bfloat16
F32 = jnp.float32
VMEM_LIMIT_BYTES = 56 * 1024 * 1024


def _dot(a, b):
    return jnp.dot(a, b, preferred_element_type=F32)


def _dot_nt(a, b):
    return lax.dot_general(a, b, (((1,), (1,)), ((), ())), preferred_element_type=F32)


def _dot_tn(a, b):
    return lax.dot_general(a, b, (((0,), (0,)), ((), ())), preferred_element_type=F32)


def _rms(x):
    return x * lax.rsqrt(jnp.mean(x * x, axis=-1, keepdims=True) + EPS)


def _modulate(x, g, shift, scale):
    return _rms(x) * g * (1.0 + scale) + shift


def _log_sigmoid(x):
    return jnp.minimum(x, 0.0) - jnp.log1p(jnp.exp(-jnp.abs(x)))


def _scan_lanes(x, op, ident, n):
    lane = lax.broadcasted_iota(jnp.int32, x.shape, 1) % n
    d = 1
    while d < n:
        x = op(x, jnp.where(lane >= d, pltpu.roll(x, d, 1), ident))
        d *= 2
    return x


def _by_group(fn, xs, mods):
    rows, r = xs[0].shape[0], mods[0].shape[0]
    if r == 1 or r == rows:
        return fn(*xs, *mods)
    return jnp.concatenate(
        [fn(*[x[i:i + r] for x in xs], *mods) for i in range(0, rows, r)], axis=0)


def _split_dot(x, m_ref):
    hi = x.astype(BF)
    lo = (x - hi.astype(F32)).astype(BF)
    m = m_ref[...]
    return _dot(hi, m) + _dot(lo, m)


def _cast_specs(jobs, steps, slab_of):
    ins, outs, shapes, args = [], [], [], []
    for w, layer in jobs:
        _, r, c = w.shape
        slab = r // steps
        assert slab * steps == r and slab % (2 * SUBLANES) == 0
        ins.append(pl.BlockSpec((1, slab, c), lambda *g, layer=layer: (layer, slab_of(*g), 0)))
        outs.append(pl.BlockSpec((slab, c), lambda *g: (slab_of(*g), 0)))
        shapes.append(jax.ShapeDtypeStruct((r, c), BF))
        args.append(w)
    return ins, outs, shapes, args


def _carry_casts(body, n_in, n_out, n_cast):
    def kernel(*refs):
        a, b = n_in, n_in + n_cast
        c, d = b + n_out, b + n_out + n_cast
        for src, dst in zip(refs[a:b], refs[c:d]):
            dst[...] = src[0].astype(BF)
        body(*refs[:a], *refs[b:c], *refs[d:])
    return kernel


def _const_spec(shape):
    nd = len(shape)
    return pl.BlockSpec(shape, lambda *_: (0,) * nd, pipeline_mode=pl.Buffered(1))


def _params(*sem):
    return pltpu.CompilerParams(dimension_semantics=sem, vmem_limit_bytes=VMEM_LIMIT_BYTES)


def _ada_kernel(c_ref, w_ref, b_ref, o_ref):
    c = c_ref[...]
    sc = (c * jax.nn.sigmoid(c)).astype(BF)
    o_ref[0, 0] = _dot(sc, w_ref[0].astype(BF)) + b_ref[0]


def _ada(c_all, ada_w, ada_b):
    depth, _, n = ada_w.shape
    nb = c_all.shape[0]
    return pl.pallas_call(
        _ada_kernel,
        grid=(depth, n // D),
        in_specs=[
            pl.BlockSpec((nb, D), lambda i, j: (0, 0)),
            pl.BlockSpec((1, D, D), lambda i, j: (i, 0, j)),
            pl.BlockSpec((1, 1, D), lambda i, j: (i, 0, j)),
        ],
        out_specs=pl.BlockSpec((1, 1, nb, D), lambda i, j: (i, j, 0, 0)),
        out_shape=jax.ShapeDtypeStruct((depth, n // D, nb, D), F32),
        compiler_params=_params("arbitrary", "arbitrary"),
        name="ada",
    )(c_all, ada_w, ada_b.reshape(depth, 1, n))


def _mlp_tile(x_ref, mod_ref, g_ref, wu_ref, wd_ref, fg_ref, o_ref, y_ref=None, wo_ref=None):
    def residual(xr, fr, gate):
        return xr + gate * fr

    def norm(xr, shift, scale):
        return _modulate(xr, g_ref[...], shift, scale)

    x = x_ref[0]
    if y_ref is not None:
        x = _by_group(residual, (x, _dot(y_ref[0].astype(BF), wo_ref[...])), (mod_ref[0, 2],))
    hb = _by_group(norm, (x,), (mod_ref[0, 3], mod_ref[0, 4])).astype(BF)
    acc = jnp.zeros(x.shape, F32)
    for c in range(DFF // FF_CHUNK):
        cs = slice(c * FF_CHUNK, (c + 1) * FF_CHUNK)
        hid = jnp.maximum(_dot(hb, wu_ref[:, cs]), 0.0)
        acc = acc + _dot((hid * hid).astype(BF), wd_ref[cs, :])
    x = _by_group(residual, (x, acc), (mod_ref[0, 5],))
    if fg_ref is not None:
        x = _rms(x) * fg_ref[...]
    o_ref[0] = x


def _mlp_kernel(*refs, n_prompt, pre, final):
    refs = list(refs)
    xp_ref, modp_ref, xs_ref = refs[:3]
    refs = refs[3:]
    ys_ref = wo_ref = fg_ref = None
    if pre:
        ys_ref, wo_ref = refs[:2]
        refs = refs[2:]
    mods_ref, g_ref, wu_ref, wd_ref = refs[:4]
    refs = refs[4:]
    if final:
        fg_ref = refs.pop(0)
    xpo_ref, xso_ref = refs
    step = pl.program_id(0)

    @pl.when(step < n_prompt)
    def _():
        _mlp_tile(xp_ref, modp_ref, g_ref, wu_ref, wd_ref, fg_ref, xpo_ref)

    @pl.when(step == n_prompt)
    def _():
        _mlp_tile(xs_ref, mods_ref, g_ref, wu_ref, wd_ref, fg_ref, xso_ref, ys_ref, wo_ref)


def _group_mod_spec(layer, r):
    return pl.BlockSpec((1, N_MOD, r, D), lambda *_: (layer, 0, 0, 0),
                        pipeline_mode=pl.Buffered(1))


def _prompt_tiles(nb, nt):
    def seq_tile(step):
        c = jnp.minimum(step, nb * nt - 1)
        return c // nt, c % nt

    def tok(tm):
        return pl.BlockSpec((1, tm, D), lambda s: (*seq_tile(s), 0))

    def per_seq(shape):
        return pl.BlockSpec((1,) + shape, lambda s: (seq_tile(s)[0],) + (0,) * len(shape))

    return tok, per_seq


def _whole_spec(shape):
    return pl.BlockSpec(shape, lambda *_: (0,) * len(shape))


def _mlp(xp, mod_p, xs, mod, layer, g, w_up, w_down, *, tm, seqs, pre=None, final_g=None,
         casts=()):
    nb, t, _ = xp.shape
    nt = t // tm
    n_prompt = nb * nt
    rows_s = xs.shape[1]
    tok, per_seq = _prompt_tiles(nb, nt)
    args = [xp, mod_p, xs]
    specs = [tok(tm), per_seq((N_MOD, 1, D)), _const_spec((1, rows_s, D))]
    if pre is not None:
        args += list(pre)
        specs += [_const_spec((1, rows_s, D)), _const_spec((D, D))]
    args += [mod, g.reshape(1, D), w_up, w_down]
    specs += [_group_mod_spec(layer, seqs), _const_spec((1, D)), _const_spec((D, DFF)),
              _const_spec((DFF, D))]
    if final_g is not None:
        args.append(final_g.reshape(1, D))
        specs.append(_const_spec((1, D)))
    c_in, c_out, c_shapes, c_args = _cast_specs(
        casts, n_prompt, lambda s: jnp.minimum(s, n_prompt - 1))
    body = functools.partial(_mlp_kernel, n_prompt=n_prompt, pre=pre is not None,
                             final=final_g is not None)
    return pl.pallas_call(
        _carry_casts(body, len(args), 2, len(casts)),
        grid=(n_prompt + 1,),
        in_specs=specs + c_in,
        out_specs=[tok(tm), _whole_spec((1, rows_s, D))] + c_out,
        out_shape=[jax.ShapeDtypeStruct(xp.shape, F32), jax.ShapeDtypeStruct(xs.shape, F32)]
        + c_shapes,
        compiler_params=_params("arbitrary"),
        name="mlp",
    )(*args, *c_args)


def _normed_bf16(x_ref, mod_ref, g_ref):
    return _modulate(x_ref[0], g_ref[...], mod_ref[0, 0], mod_ref[0, 1]).astype(BF)


def _mlstm_prompt_kernel(x_ref, xn_ref, mod_ref, modn_ref, g_ref, *rest, tm):
    *rest, hb_a, hb_b = rest
    step = pl.program_id(0) * pl.num_programs(1) + pl.program_id(1)

    @pl.when(step == 0)
    def _():
        hb_a[...] = _normed_bf16(x_ref, mod_ref, g_ref)

    for parity, (cur, nxt) in enumerate(((hb_a, hb_b), (hb_b, hb_a))):
        @pl.when(step % 2 == parity)
        def _(cur=cur, nxt=nxt):
            _mlstm_prompt_tile(cur, nxt, x_ref, xn_ref, mod_ref, modn_ref, g_ref, *rest, tm=tm)


def _mlstm_prompt_tile(hb_ref, hbn_ref, x_ref, xn_ref, mod_ref, modn_ref, g_ref, wrow_ref, wcol_ref,
                       bg_ref, gh_ref, wout_ref, xo_ref, s_ref, m_ref, y_scr, *, tm):
    @pl.when(pl.program_id(1) == 0)
    def _():
        s_ref[...] = jnp.zeros(s_ref.shape, F32)
        m_ref[...] = jnp.zeros(m_ref.shape, F32)

    x = x_ref[0]
    hb = hb_ref[...]
    n = CHUNK
    w_t = jnp.concatenate([wcol_ref[0:HK, :], wcol_ref[HK:HK + H, :],
                           wcol_ref[HK + LANES:HK + LANES + H, :]], axis=0)
    pcol = _dot_nt(w_t, hb)
    ig_all = pcol[HK:HK + H] + bg_ref[0:H]
    fg_all = pcol[HK + H:HK + 2 * H] + bg_ref[H:2 * H]
    b_all = _scan_lanes(_log_sigmoid(fg_all), jnp.add, 0.0, n)
    r_all = ig_all - b_all
    cm_all = _scan_lanes(r_all, jnp.maximum, -jnp.inf, n)
    prow = _dot(hb, wrow_ref[...])
    causal = (lax.broadcasted_iota(jnp.int32, (n, n), 0)
              >= lax.broadcasted_iota(jnp.int32, (n, n), 1))
    zeros_k = jnp.zeros((DK, n), BF)
    zeros_s = jnp.zeros((DK, 2 * DV), BF)
    zeros_v = jnp.zeros((n, DV), BF)
    ones_v = jnp.ones((n, DV), BF)
    pad = jnp.zeros((n - 2 * H, n), F32)

    for c in range(tm // n):
        ts = slice(c * n, (c + 1) * n)
        r = r_all[:, ts]
        m0 = m_ref[0]
        mx = jnp.maximum(m0, cm_all[:, ts])
        m = b_all[:, ts] + mx
        e_neg = jnp.exp(-m)
        mx_last = jnp.broadcast_to(mx[:, n - 1:n], (H, n))
        a = jnp.exp(r - mx_last)
        decay = jnp.exp(m0 - mx_last)
        m_ref[0] = jnp.broadcast_to(m[:, n - 1:n], (H, n))
        cols = jnp.concatenate([mx, e_neg, pad], axis=0).T

        for h in range(H):
            lo = h % 2 == 0
            mx_b = jnp.broadcast_to(cols[:, h:h + 1], (n, n))
            floor_b = jnp.broadcast_to(cols[:, H + h:H + h + 1], (n, DV))
            dec = jnp.exp(jnp.where(causal, r[h:h + 1, :] - mx_b, -jnp.inf))
            e_int = jnp.exp(m0[h:h + 1, :] - mx_b)
            qp = prow[ts, (h // 2) * 2 * DK:(h // 2 + 1) * 2 * DK]
            kt = pcol[h * DK:(h + 1) * DK, ts]
            ktb = kt.astype(BF)
            s = _dot(qp.astype(BF),
                     jnp.concatenate([ktb, zeros_k] if lo else [zeros_k, ktb], axis=0)) * dec
            st = s_ref[0, h]
            stb = st.astype(BF)
            v = prow[ts, HK + h * DV:HK + (h + 1) * DV].astype(BF)
            rhs = jnp.concatenate(
                [jnp.concatenate([v, zeros_v], axis=1)]
                + ([stb, zeros_s] if lo else [zeros_s, stb]), axis=0)
            lhs = jnp.concatenate([s.astype(BF), (qp * e_int).astype(BF)], axis=1)
            res = _dot(lhs, rhs)
            den = jnp.sum(s, axis=-1, keepdims=True) + res[:, DV:]
            hh = res[:, :DV] / jnp.maximum(jnp.abs(den), floor_b)
            hh = hh * lax.rsqrt(jnp.mean(hh * hh, axis=-1, keepdims=True) + EPS)
            y_scr[ts, h * DV:(h + 1) * DV] = hh
            ka = (kt * a[h:h + 1, :]).astype(BF)
            dc = decay[h:h + 1, :]
            s_ref[0, h] = (jnp.concatenate([dc, dc], axis=1) * st
                           + _dot(ka, jnp.concatenate([v, ones_v], axis=1)))

    o = prow[:, HK + D:]
    y = (y_scr[...] * gh_ref[...] * jax.nn.sigmoid(o)).astype(BF)
    out = _dot(y, wout_ref[...])
    hbn_ref[...] = _normed_bf16(xn_ref, modn_ref, g_ref)
    xo_ref[0] = x + mod_ref[0, 2] * out


def _mlstm_prompt(x, mod, g, w_row, w_col, b_g, g_head, w_out, *, tm, casts=()):
    nb, t, _ = x.shape
    nt = t // tm
    tok = pl.BlockSpec((1, tm, D), lambda b, i: (b, i, 0))
    per_seq = pl.BlockSpec((1, N_MOD, 1, D), lambda b, i: (b, 0, 0, 0))

    def next_tile(b, i):
        c = jnp.minimum(b * nt + i + 1, nb * nt - 1)
        return c // nt, c % nt

    tok_next = pl.BlockSpec((1, tm, D), lambda b, i: (*next_tile(b, i), 0))
    per_seq_next = pl.BlockSpec((1, N_MOD, 1, D), lambda b, i: (next_tile(b, i)[0], 0, 0, 0))
    args = (x, x, mod, mod, g.reshape(1, D), w_row, w_col,
            jnp.broadcast_to(b_g.reshape(2 * H, 1), (2 * H, tm)), g_head.reshape(1, D), w_out)
    c_in, c_out, c_shapes, c_args = _cast_specs(casts, nb * nt, lambda b, i: b * nt + i)
    return pl.pallas_call(
        _carry_casts(functools.partial(_mlstm_prompt_kernel, tm=tm), len(args), 3, len(casts)),
        grid=(nb, nt),
        in_specs=[
            tok,
            tok_next,
            per_seq,
            per_seq_next,
            _const_spec((1, D)),
            _const_spec((D, HK + 2 * D)),
            _const_spec((HK + 2 * LANES, D)),
            _const_spec((2 * H, tm)),
            _const_spec((1, D)),
            _const_spec((D, D)),
        ] + c_in,
        out_specs=[
            tok,
            pl.BlockSpec((1, H, DK, 2 * DV), lambda b, i: (b, 0, 0, 0)),
            pl.BlockSpec((1, H, CHUNK), lambda b, i: (b, 0, 0)),
        ] + c_out,
        out_shape=[
            jax.ShapeDtypeStruct(x.shape, F32),
            jax.ShapeDtypeStruct((nb, H, DK, 2 * DV), F32),
            jax.ShapeDtypeStruct((nb, H, CHUNK), F32),
        ] + c_shapes,
        scratch_shapes=[pltpu.VMEM((tm, D), F32), pltpu.VMEM((tm, D), BF),
                        pltpu.VMEM((tm, D), BF)],
        compiler_params=_params("arbitrary", "arbitrary"),
        name="mlstm_prompt",
    )(*args, *c_args)


def _mlstm_sample_pre_kernel(x_ref, mod_ref, g_ref, wrow_ref, wcol_ref, bg_ref, m0_ref, n0_ref,
                             gh_ref, seg_ref, e64_ref, e128_ref,
                             q_ref, ka_ref, v_ref, dec_ref, numi_ref, rd_ref, eint_ref, go_ref,
                             n1_ref, m1_ref, *, t, nb):
    def norm(xr, shift, scale):
        return _modulate(xr, g_ref[...], shift, scale)

    hb = _by_group(norm, (x_ref[0],), (mod_ref[0, 0], mod_ref[0, 1])).astype(BF)
    p = _dot(hb, wrow_ref[...])
    k_all = _dot_nt(hb, wcol_ref[0:HK, :])
    ig_all = _dot_nt(hb, wcol_ref[HK:HK + LANES, :]) + bg_ref[0:1]
    fg_all = _dot_nt(hb, wcol_ref[HK + LANES:HK + 2 * LANES, :]) + bg_ref[1:2]

    def rows(i):
        return slice(i * nb, (i + 1) * nb)

    q = [p[rows(i), 0:HK] for i in range(t)]
    k = [k_all[rows(i)] for i in range(t)]
    v = [p[rows(i), HK:HK + D] for i in range(t)]
    m0 = m0_ref[...]
    n0 = n0_ref[...]

    b, r, mx = [], [], []
    for i in range(t):
        lf = _log_sigmoid(fg_all[rows(i)])
        b.append(lf if i == 0 else b[-1] + lf)
        r.append(ig_all[rows(i)] - b[i])
        cm = r[i] if i == 0 else jnp.maximum(cm, r[i])
        mx.append(jnp.maximum(m0, cm))
    mx_last = mx[t - 1]
    decay = jnp.exp(m0 - mx_last)
    m1_ref[...] = b[t - 1] + mx_last
    dec_x = _split_dot(decay, e128_ref)
    n1 = _split_dot(decay, e64_ref) * n0

    for i in range(t):
        ka = k[i] * _split_dot(jnp.exp(r[i] - mx_last), e64_ref)
        n1 = n1 + ka
        seq_rows = pl.ds(i, nb, stride=t)
        for c in range(HK // LANES):
            cs = slice(c * LANES, (c + 1) * LANES)
            ka_ref[c, seq_rows, :] = ka[:, cs]
            q_ref[c, seq_rows, :] = q[i][:, cs]
        for c in range(D // LANES):
            cs = slice(c * LANES, (c + 1) * LANES)
            v_ref[c, seq_rows, :] = v[i][:, cs]
            dec_ref[c, seq_rows, :] = dec_x[:, cs]
        e_int = jnp.exp(m0 - mx[i])
        den = e_int * _split_dot(q[i] * n0, seg_ref)
        numi = jnp.zeros((nb, D), F32)
        for s in range(i + 1):
            w = _split_dot(q[i] * k[s], seg_ref) * jnp.exp(jnp.minimum(r[s] - mx[i], 0.0))
            den = den + w
            numi = numi + _split_dot(w, e128_ref) * v[s]
        numi_ref[rows(i), :] = numi
        rd_ref[rows(i), :] = 1.0 / jnp.maximum(jnp.abs(den), jnp.exp(-(b[i] + mx[i])))
        eint_ref[rows(i), :] = e_int
        go_ref[rows(i), :] = jax.nn.sigmoid(p[rows(i), HK + D:]) * gh_ref[...]
    n1_ref[...] = n1


def _mlstm_sample_pre(x, mod, layer, g, w_row, w_col, b_g, m0, n0, g_head, seg64, e64, e128, *, t):
    rows = x.shape[1]
    nb = rows // t

    def whole(shape):
        return pl.BlockSpec(shape, lambda i: (0,) * len(shape))

    outs = [(HK // LANES, rows, LANES), (HK // LANES, rows, LANES), (D // LANES, rows, LANES),
            (D // LANES, rows, LANES), (rows, D), (rows, FG), (rows, FG), (rows, D), (nb, HK),
            (nb, FG)]
    return pl.pallas_call(
        functools.partial(_mlstm_sample_pre_kernel, t=t, nb=nb),
        grid=(1,),
        in_specs=[
            whole((1, rows, D)),
            _group_mod_spec(layer, nb),
            _const_spec((1, D)),
            _const_spec((D, HK + 2 * D)),
            _const_spec((HK + 2 * LANES, D)),
            _const_spec((2, LANES)),
            whole((nb, FG)),
            whole((nb, HK)),
            _const_spec((1, D)),
            _const_spec((HK, FG)),
            _const_spec((FG, HK)),
            _const_spec((FG, D)),
        ],
        out_specs=[whole(s) for s in outs],
        out_shape=[jax.ShapeDtypeStruct(s, F32) for s in outs],
        compiler_params=_params("arbitrary"),
        name="mlstm_sample_pre",
    )(x, mod, g.reshape(1, D), w_row, w_col, b_g, m0, n0, g_head.reshape(1, D), seg64, e64, e128)


def _mlstm_sample_state_kernel(q_ref, ka_ref, v_ref, dec_ref, c0_ref, inter_ref, c1_ref, *, bb, t):
    per = SUBLANES // t
    seq_of_row = lax.broadcasted_iota(jnp.int32, (SUBLANES, 1), 0) // t
    zeros_c = jnp.zeros((DK, DV), BF)

    def tile(i, carry):
        rs = pl.ds(pl.multiple_of(i * SUBLANES, SUBLANES), SUBLANES)
        for h in range(H):
            lo = h % 2 == 0
            half = slice((h % 2) * DK, (h % 2 + 1) * DK)
            qp = q_ref[h // 2, rs, :].astype(BF)
            kap = ka_ref[h // 2, rs, :]
            vh = v_ref[h, rs, :].astype(BF)
            dech = dec_ref[h, rs, :]
            res = None
            for w in range(per):
                seq = i * per + w
                c0 = c0_ref[seq, h]
                c0b = c0.astype(BF)
                rw = _dot(qp, jnp.concatenate([c0b, zeros_c] if lo else [zeros_c, c0b], axis=0))
                res = rw if res is None else jnp.where(seq_of_row == w, rw, res)
                kaw = jnp.where(seq_of_row == w, kap, 0.0).astype(BF)
                c1_ref[seq, h] = dech[w * t:w * t + 1, :] * c0 + _dot_tn(kaw, vh)[half, :]
            inter_ref[h, rs, :] = res
        return carry

    lax.fori_loop(0, bb // per, tile, 0, unroll=4)


def _mlstm_sample_state(q, ka, v, dec, c0, *, t, bb):
    rows = q.shape[1]
    nb = rows // t
    assert SUBLANES % t == 0 and bb % (SUBLANES // t) == 0 and nb % bb == 0

    def tok(blocks):
        return pl.BlockSpec((blocks, bb * t, LANES), lambda i: (0, i, 0))

    state = pl.BlockSpec((bb, H, DK, DV), lambda i: (i, 0, 0, 0))
    return pl.pallas_call(
        functools.partial(_mlstm_sample_state_kernel, bb=bb, t=t),
        grid=(nb // bb,),
        in_specs=[tok(HK // LANES), tok(HK // LANES), tok(H), tok(H), state],
        out_specs=[tok(H), state],
        out_shape=[jax.ShapeDtypeStruct((H, rows, LANES), F32),
                   jax.ShapeDtypeStruct(c0.shape, F32)],
        compiler_params=_params("arbitrary"),
        name="mlstm_sample_state",
    )(q, ka, v, dec, c0)


def _mlstm_sample_post_kernel(inter_ref, numi_ref, rd_ref, eint_ref, go_ref, e128_ref, seg_ref,
                              y_ref, *, t, nb):
    for i in range(t):
        rs = slice(i * nb, (i + 1) * nb)
        inter = jnp.concatenate(
            [inter_ref[h, pl.ds(i, nb, stride=t), :] for h in range(H)], axis=1)
        num = numi_ref[rs, :] + _split_dot(eint_ref[rs, :], e128_ref) * inter
        hh = num * _split_dot(rd_ref[rs, :], e128_ref)
        ms = _split_dot(hh * hh, seg_ref) * (1.0 / DV)
        y_ref[rs, :] = hh * _split_dot(lax.rsqrt(ms + EPS), e128_ref) * go_ref[rs, :]


def _mlstm_sample_post(inter, numi, rd, eint, go, e128, seg128, *, t):
    rows = inter.shape[1]

    def whole(shape):
        return pl.BlockSpec(shape, lambda i: (0,) * len(shape))

    return pl.pallas_call(
        functools.partial(_mlstm_sample_post_kernel, t=t, nb=rows // t),
        grid=(1,),
        in_specs=[whole((H, rows, LANES)), whole((rows, D)), whole((rows, FG)), whole((rows, FG)),
                  whole((rows, D)), _const_spec((FG, D)), _const_spec((D, FG))],
        out_specs=whole((rows, D)),
        out_shape=jax.ShapeDtypeStruct((rows, D), F32),
        compiler_params=_params("arbitrary"),
        name="mlstm_sample_post",
    )(inter, numi, rd, eint, go, e128, seg128)


def _conv_taps(u, prev2, prev1, wc_ref):
    return wc_ref[0:1] * prev2 + wc_ref[1:2] * prev1 + wc_ref[2:3] * u


def _conv_prompt_tile(x_ref, mod_ref, g_ref, win_ref, wc_ref, wout_ref, xo_ref, st_ref, carry_scr,
                      first):
    @pl.when(first)
    def _():
        carry_scr[...] = jnp.zeros(carry_scr.shape, F32)

    x = x_ref[0]
    tm = x.shape[0]
    hb = _modulate(x, g_ref[...], mod_ref[0, 0], mod_ref[0, 1]).astype(BF)
    u = _dot(hb, win_ref[:, D:2 * D]) * _dot(hb, win_ref[:, 2 * D:])
    row = lax.broadcasted_iota(jnp.int32, (tm, D), 0)
    c0 = carry_scr[0:1]
    c1 = carry_scr[1:2]
    prev1 = jnp.where(row == 0, c1, pltpu.roll(u, 1, 0))
    prev2 = jnp.where(row == 0, c0, jnp.where(row == 1, c1, pltpu.roll(u, 2, 0)))
    y = _conv_taps(u, prev2, prev1, wc_ref)
    bg = _dot(hb, win_ref[:, :D])
    xo_ref[0] = x + mod_ref[0, 2] * _dot((bg * y).astype(BF), wout_ref[...])
    carry_scr[...] = u[tm - 2:tm]
    st_ref[0] = u[tm - 2:tm]


def _conv_sample_tile(x_ref, mod_ref, g_ref, win_ref, wc_ref, wout_ref, buf_ref, xo_ref, st_ref,
                      *, t, nb):
    def norm(xr, shift, scale):
        return _modulate(xr, g_ref[...], shift, scale)

    def residual(xr, fr, gate):
        return xr + gate * fr

    x = x_ref[0]
    hb = _by_group(norm, (x,), (mod_ref[0, 0], mod_ref[0, 1])).astype(BF)
    u = _dot(hb, win_ref[:, D:2 * D]) * _dot(hb, win_ref[:, 2 * D:])
    full = [buf_ref[j] for j in range(CONV_W - 1)] + [u[i * nb:(i + 1) * nb] for i in range(t)]
    y = jnp.concatenate([_conv_taps(full[i + 2], full[i], full[i + 1], wc_ref) for i in range(t)],
                        axis=0)
    bg = _dot(hb, win_ref[:, :D])
    out = _dot((bg * y).astype(BF), wout_ref[...])
    xo_ref[0] = _by_group(residual, (x, out), (mod_ref[0, 2],))
    for j in range(CONV_W - 1):
        st_ref[j] = full[t + j]


def _conv_kernel(xp_ref, modp_ref, xs_ref, mods_ref, g_ref, win_ref, wc_ref, wout_ref, buf_ref,
                 xpo_ref, stp_ref, xso_ref, sts_ref, carry_scr, *, n_prompt, nt, t, nb):
    step = pl.program_id(0)

    @pl.when(step < n_prompt)
    def _():
        _conv_prompt_tile(xp_ref, modp_ref, g_ref, win_ref, wc_ref, wout_ref, xpo_ref, stp_ref,
                          carry_scr, step % nt == 0)

    @pl.when(step == n_prompt)
    def _():
        _conv_sample_tile(xs_ref, mods_ref, g_ref, win_ref, wc_ref, wout_ref, buf_ref, xso_ref,
                          sts_ref, t=t, nb=nb)


def _conv(xp, mod_p, xs, mod, layer, g, w_in, w_conv, w_out, buf, *, tm, t, casts=()):
    nb, tp, _ = xp.shape
    nt = tp // tm
    n_prompt = nb * nt
    rows_s = xs.shape[1]
    seqs = rows_s // t
    tok, per_seq = _prompt_tiles(nb, nt)
    args = (xp, mod_p, xs, mod, g.reshape(1, D), w_in, w_conv, w_out, buf)
    c_in, c_out, c_shapes, c_args = _cast_specs(
        casts, n_prompt, lambda s: jnp.minimum(s, n_prompt - 1))
    body = functools.partial(_conv_kernel, n_prompt=n_prompt, nt=nt, t=t, nb=seqs)
    return pl.pallas_call(
        _carry_casts(body, len(args), 4, len(casts)),
        grid=(n_prompt + 1,),
        in_specs=[
            tok(tm),
            per_seq((N_MOD, 1, D)),
            _const_spec((1, rows_s, D)),
            _group_mod_spec(layer, seqs),
            _const_spec((1, D)),
            _const_spec((D, 3 * D)),
            _const_spec((CONV_W, D)),
            _const_spec((D, D)),
            _const_spec((CONV_W - 1, seqs, D)),
        ] + c_in,
        out_specs=[tok(tm), per_seq((CONV_W - 1, D)), _whole_spec((1, rows_s, D)),
                   _whole_spec((CONV_W - 1, seqs, D))] + c_out,
        out_shape=[jax.ShapeDtypeStruct(xp.shape, F32),
                   jax.ShapeDtypeStruct((nb, CONV_W - 1, D), F32),
                   jax.ShapeDtypeStruct(xs.shape, F32),
                   jax.ShapeDtypeStruct((CONV_W - 1, seqs, D), F32)] + c_shapes,
        scratch_shapes=[pltpu.VMEM((CONV_W - 1, D), F32)],
        compiler_params=_params("arbitrary"),
        name="conv",
    )(*args, *c_args)


def kernel(x_prompt, x_sample, c_prompt, c_sample, state_mlstm_C, state_mlstm_n, state_mlstm_m,
           state_conv, ada_w, ada_b, norm_g, final_norm_g, mlstm_w_in, mlstm_b_gates,
           mlstm_head_g, mlstm_w_out, conv_w_in, conv_w, conv_w_out, mlp_w_up, mlp_w_down):
    assert ada_w.shape[0] == 2 and mlstm_w_in.shape[0] == 1 and conv_w_in.shape[0] == 1
    bp, tp, _ = x_prompt.shape
    bs, ts, _ = x_sample.shape
    assert ts == 4 and tp % 512 == 0

    mod = _ada(jnp.concatenate([c_sample, c_prompt], axis=0), ada_w, ada_b)
    mod_p = jnp.transpose(mod[:, :, bs:], (0, 2, 1, 3))[:, :, :, None]

    w_in0 = mlstm_w_in[0]
    w_row = jnp.concatenate([w_in0[:, :HK], w_in0[:, 2 * HK:NP]], axis=1).astype(BF)
    gate_pad = jnp.zeros((D, LANES - H), F32)
    w_col = jnp.concatenate([w_in0[:, HK:2 * HK] * (DK ** -0.5), w_in0[:, NP:NP + H], gate_pad,
                             w_in0[:, NP + H:], gate_pad], axis=1).T.astype(BF)
    b_g = jnp.pad(mlstm_b_gates[0].reshape(2, H), ((0, 0), (0, LANES - H)))
    wout0 = mlstm_w_out[0].astype(BF)

    x, s_p, m_p, up0, down0 = _mlstm_prompt(
        x_prompt, mod_p[0], norm_g[0, 0], w_row, w_col, mlstm_b_gates[0], mlstm_head_g[0], wout0,
        tm=512, casts=((mlp_w_up, 0), (mlp_w_down, 0)))
    prompt_c = s_p[None, :, :, :, :DV]
    prompt_n = s_p[None, :, :, :, DV]
    prompt_m = m_p[None, :, :, 0]

    xs = jnp.transpose(x_sample, (1, 0, 2)).reshape(1, ts * bs, D)
    head_of_v = jnp.arange(D, dtype=jnp.int32) // DV
    head_of_k = jnp.arange(HK, dtype=jnp.int32) // DK
    lanes = jnp.arange(FG, dtype=jnp.int32)
    e128 = (lanes[:, None] == head_of_v[None, :]).astype(BF)
    e64 = (lanes[:, None] == head_of_k[None, :]).astype(BF)
    m0 = jnp.pad(state_mlstm_m[0], ((0, 0), (0, FG - H)))
    n0 = state_mlstm_n[0].reshape(bs, HK)
    q, ka, v, dec, numi, rd, eint, go, n1, m1 = _mlstm_sample_pre(
        xs, mod, 0, norm_g[0, 0], w_row, w_col, b_g, m0, n0, mlstm_head_g[0], e64.T, e64, e128,
        t=ts)
    inter, c1 = _mlstm_sample_state(q, ka, v, dec, state_mlstm_C[0], t=ts, bb=16)
    y = _mlstm_sample_post(inter, numi, rd, eint, go, e128, e128.T, t=ts)
    sample_n = n1.reshape(bs, H, DK)[None]
    sample_m = m1[None, :, :H]

    x, xs, cwin, cwout = _mlp(x, mod_p[0], xs, mod, 0, norm_g[0, 1], up0, down0, tm=512, seqs=bs,
                              pre=(y.reshape(1, ts * bs, D), wout0),
                              casts=((conv_w_in, 0), (conv_w_out, 0)))
    x, conv_p, xs, conv_s, up1, down1 = _conv(
        x, mod_p[1], xs, mod, 1, norm_g[1, 0], cwin, conv_w[0], cwout,
        jnp.transpose(state_conv[0], (1, 0, 2)), tm=512, t=ts,
        casts=((mlp_w_up, 1), (mlp_w_down, 1)))
    y_prompt, ys = _mlp(x, mod_p[1], xs, mod, 1, norm_g[1, 1], up1, down1, tm=512, seqs=bs,
                        final_g=final_norm_g)
    prompt_conv = conv_p[None]
    y_sample = jnp.transpose(ys.reshape(ts, bs, D), (1, 0, 2))
    sample_conv = jnp.transpose(conv_s, (1, 0, 2))[None]

    return (y_prompt, y_sample, prompt_c, prompt_n, prompt_m, prompt_conv,
            c1[None], sample_n, sample_m, sample_conv)
```

```python
import functools

import jax
import jax.numpy as jnp
from jax import lax
from jax.experimental import pallas as pl
from jax.experimental.pallas import tpu as pltpu

D = 1024
H = 8
DK = 64
DV = 128
HK = H * DK
NP = 2 * HK + 2 * D
FG = 128
LANES = 128
SUBLANES = 8
DFF = 4 * D
N_MOD = 6
CONV_W = 3
EPS = 1e-6
CHUNK = 128
FF_CHUNK = 1024
BF = jnp.bfloat16
F32 = jnp.float32
VMEM_LIMIT_BYTES = 56 * 1024 * 1024


def _dot(a, b):
    return jnp.dot(a, b, preferred_element_type=F32)


def _dot_nt(a, b):
    return lax.dot_general(a, b, (((1,), (1,)), ((), ())), preferred_element_type=F32)


def _dot_tn(a, b):
    return lax.dot_general(a, b, (((0,), (0,)), ((), ())), preferred_element_type=F32)


def _rms(x):
    return x * lax.rsqrt(jnp.mean(x * x, axis=-1, keepdims=True) + EPS)


def _modulate(x, g, shift, scale):
    return _rms(x) * g * (1.0 + scale) + shift


def _log_sigmoid(x):
    return jnp.minimum(x, 0.0) - jnp.log1p(jnp.exp(-jnp.abs(x)))


def _scan_lanes(x, op, ident, n):
    lane = lax.broadcasted_iota(jnp.int32, x.shape, 1) % n
    d = 1
    while d < n:
        x = op(x, jnp.where(lane >= d, pltpu.roll(x, d, 1), ident))
        d *= 2
    return x


def _by_group(fn, xs, mods):
    rows, r = xs[0].shape[0], mods[0].shape[0]
    if r == 1 or r == rows:
        return fn(*xs, *mods)
    return jnp.concatenate(
        [fn(*[x[i:i + r] for x in xs], *mods) for i in range(0, rows, r)], axis=0)


def _split_dot(x, m_ref):
    hi = x.astype(BF)
    lo = (x - hi.astype(F32)).astype(BF)
    m = m_ref[...]
    return _dot(hi, m) + _dot(lo, m)


def _cast_specs(jobs, steps, slab_of):
    ins, outs, shapes, args = [], [], [], []
    for w, layer in jobs:
        _, r, c = w.shape
        slab = r // steps
        assert slab * steps == r and slab % (2 * SUBLANES) == 0
        ins.append(pl.BlockSpec((1, slab, c), lambda *g, layer=layer: (layer, slab_of(*g), 0)))
        outs.append(pl.BlockSpec((slab, c), lambda *g: (slab_of(*g), 0)))
        shapes.append(jax.ShapeDtypeStruct((r, c), BF))
        args.append(w)
    return ins, outs, shapes, args


def _carry_casts(body, n_in, n_out, n_cast):
    def kernel(*refs):
        a, b = n_in, n_in + n_cast
        c, d = b + n_out, b + n_out + n_cast
        for src, dst in zip(refs[a:b], refs[c:d]):
            dst[...] = src[0].astype(BF)
        body(*refs[:a], *refs[b:c], *refs[d:])
    return kernel


def _const_spec(shape):
    nd = len(shape)
    return pl.BlockSpec(shape, lambda *_: (0,) * nd, pipeline_mode=pl.Buffered(1))


def _params(*sem):
    return pltpu.CompilerParams(dimension_semantics=sem, vmem_limit_bytes=VMEM_LIMIT_BYTES)


def _ada_kernel(c_ref, w_ref, b_ref, o_ref):
    c = c_ref[...]
    sc = (c * jax.nn.sigmoid(c)).astype(BF)
    o_ref[0, 0] = _dot(sc, w_ref[0].astype(BF)) + b_ref[0]


def _ada(c_all, ada_w, ada_b):
    depth, _, n = ada_w.shape
    nb = c_all.shape[0]
    return pl.pallas_call(
        _ada_kernel,
        grid=(depth, n // D),
        in_specs=[
            pl.BlockSpec((nb, D), lambda i, j: (0, 0)),
            pl.BlockSpec((1, D, D), lambda i, j: (i, 0, j)),
            pl.BlockSpec((1, 1, D), lambda i, j: (i, 0, j)),
        ],
        out_specs=pl.BlockSpec((1, 1, nb, D), lambda i, j: (i, j, 0, 0)),
        out_shape=jax.ShapeDtypeStruct((depth, n // D, nb, D), F32),
        compiler_params=_params("arbitrary", "arbitrary"),
        name="ada",
    )(c_all, ada_w, ada_b.reshape(depth, 1, n))


def _mlp_tile(x_ref, mod_ref, g_ref, wu_ref, wd_ref, fg_ref, o_ref, y_ref=None, wo_ref=None):
    def residual(xr, fr, gate):
        return xr + gate * fr

    def norm(xr, shift, scale):
        return _modulate(xr, g_ref[...], shift, scale)

    x = x_ref[0]
    if y_ref is not None:
        x = _by_group(residual, (x, _dot(y_ref[0].astype(BF), wo_ref[...])), (mod_ref[0, 2],))
    hb = _by_group(norm, (x,), (mod_ref[0, 3], mod_ref[0, 4])).astype(BF)
    acc = jnp.zeros(x.shape, F32)
    for c in range(DFF // FF_CHUNK):
        cs = slice(c * FF_CHUNK, (c + 1) * FF_CHUNK)
        hid = jnp.maximum(_dot(hb, wu_ref[:, cs]), 0.0)
        acc = acc + _dot((hid * hid).astype(BF), wd_ref[cs, :])
    x = _by_group(residual, (x, acc), (mod_ref[0, 5],))
    if fg_ref is not None:
        x = _rms(x) * fg_ref[...]
    o_ref[0] = x


def _mlp_kernel(*refs, n_prompt, pre, final):
    refs = list(refs)
    xp_ref, modp_ref, xs_ref = refs[:3]
    refs = refs[3:]
    ys_ref = wo_ref = fg_ref = None
    if pre:
        ys_ref, wo_ref = refs[:2]
        refs = refs[2:]
    mods_ref, g_ref, wu_ref, wd_ref = refs[:4]
    refs = refs[4:]
    if final:
        fg_ref = refs.pop(0)
    xpo_ref, xso_ref = refs
    step = pl.program_id(0)

    @pl.when(step < n_prompt)
    def _():
        _mlp_tile(xp_ref, modp_ref, g_ref, wu_ref, wd_ref, fg_ref, xpo_ref)

    @pl.when(step == n_prompt)
    def _():
        _mlp_tile(xs_ref, mods_ref, g_ref, wu_ref, wd_ref, fg_ref, xso_ref, ys_ref, wo_ref)


def _group_mod_spec(layer, r):
    return pl.BlockSpec((1, N_MOD, r, D), lambda *_: (layer, 0, 0, 0),
                        pipeline_mode=pl.Buffered(1))


def _prompt_tiles(nb, nt):
    def seq_tile(step):
        c = jnp.minimum(step, nb * nt - 1)
        return c // nt, c % nt

    def tok(tm):
        return pl.BlockSpec((1, tm, D), lambda s: (*seq_tile(s), 0))

    def per_seq(shape):
        return pl.BlockSpec((1,) + shape, lambda s: (seq_tile(s)[0],) + (0,) * len(shape))

    return tok, per_seq


def _whole_spec(shape):
    return pl.BlockSpec(shape, lambda *_: (0,) * len(shape))


def _mlp(xp, mod_p, xs, mod, layer, g, w_up, w_down, *, tm, seqs, pre=None, final_g=None,
         casts=()):
    nb, t, _ = xp.shape
    nt = t // tm
    n_prompt = nb * nt
    rows_s = xs.shape[1]
    tok, per_seq = _prompt_tiles(nb, nt)
    args = [xp, mod_p, xs]
    specs = [tok(tm), per_seq((N_MOD, 1, D)), _const_spec((1, rows_s, D))]
    if pre is not None:
        args += list(pre)
        specs += [_const_spec((1, rows_s, D)), _const_spec((D, D))]
    args += [mod, g.reshape(1, D), w_up, w_down]
    specs += [_group_mod_spec(layer, seqs), _const_spec((1, D)), _const_spec((D, DFF)),
              _const_spec((DFF, D))]
    if final_g is not None:
        args.append(final_g.reshape(1, D))
        specs.append(_const_spec((1, D)))
    c_in, c_out, c_shapes, c_args = _cast_specs(
        casts, n_prompt, lambda s: jnp.minimum(s, n_prompt - 1))
    body = functools.partial(_mlp_kernel, n_prompt=n_prompt, pre=pre is not None,
                             final=final_g is not None)
    return pl.pallas_call(
        _carry_casts(body, len(args), 2, len(casts)),
        grid=(n_prompt + 1,),
        in_specs=specs + c_in,
        out_specs=[tok(tm), _whole_spec((1, rows_s, D))] + c_out,
        out_shape=[jax.ShapeDtypeStruct(xp.shape, F32), jax.ShapeDtypeStruct(xs.shape, F32)]
        + c_shapes,
        compiler_params=_params("arbitrary"),
        name="mlp",
    )(*args, *c_args)


def _mlstm_prompt_kernel(x_ref, mod_ref, g_ref, wrow_ref, wcol_ref, bg_ref, gh_ref, wout_ref,
                         xo_ref, s_ref, m_ref, y_scr, *, tm):
    @pl.when(pl.program_id(1) == 0)
    def _():
        s_ref[...] = jnp.zeros(s_ref.shape, F32)
        m_ref[...] = jnp.zeros(m_ref.shape, F32)

    x = x_ref[0]
    hb = _modulate(x, g_ref[...], mod_ref[0, 0], mod_ref[0, 1]).astype(BF)
    n = CHUNK
    w_t = jnp.concatenate([wcol_ref[0:HK, :], wcol_ref[HK:HK + H, :],
                           wcol_ref[HK + LANES:HK + LANES + H, :]], axis=0)
    pcol = _dot_nt(w_t, hb)
    ig_all = pcol[HK:HK + H] + bg_ref[0:H]
    fg_all = pcol[HK + H:HK + 2 * H] + bg_ref[H:2 * H]
    b_all = _scan_lanes(_log_sigmoid(fg_all), jnp.add, 0.0, n)
    r_all = ig_all - b_all
    cm_all = _scan_lanes(r_all, jnp.maximum, -jnp.inf, n)
    prow = _dot(hb, wrow_ref[...])
    causal = (lax.broadcasted_iota(jnp.int32, (n, n), 0)
              >= lax.broadcasted_iota(jnp.int32, (n, n), 1))
    zeros_k = jnp.zeros((DK, n), BF)
    zeros_s = jnp.zeros((DK, 2 * DV), BF)
    zeros_v = jnp.zeros((n, DV), BF)
    ones_v = jnp.ones((n, DV), BF)
    pad = jnp.zeros((n - 2 * H, n), F32)

    for c in range(tm // n):
        ts = slice(c * n, (c + 1) * n)
        r = r_all[:, ts]
        m0 = m_ref[0]
        mx = jnp.maximum(m0, cm_all[:, ts])
        m = b_all[:, ts] + mx
        e_neg = jnp.exp(-m)
        mx_last = jnp.broadcast_to(mx[:, n - 1:n], (H, n))
        a = jnp.exp(r - mx_last)
        decay = jnp.exp(m0 - mx_last)
        m_ref[0] = jnp.broadcast_to(m[:, n - 1:n], (H, n))
        cols = jnp.concatenate([mx, e_neg, pad], axis=0).T

        for h in range(H):
            lo = h % 2 == 0
            mx_b = jnp.broadcast_to(cols[:, h:h + 1], (n, n))
            floor_b = jnp.broadcast_to(cols[:, H + h:H + h + 1], (n, DV))
            dec = jnp.exp(jnp.where(causal, r[h:h + 1, :] - mx_b, -jnp.inf))
            e_int = jnp.exp(m0[h:h + 1, :] - mx_b)
            qp = prow[ts, (h // 2) * 2 * DK:(h // 2 + 1) * 2 * DK]
            kt = pcol[h * DK:(h + 1) * DK, ts]
            ktb = kt.astype(BF)
            s = _dot(qp.astype(BF),
                     jnp.concatenate([ktb, zeros_k] if lo else [zeros_k, ktb], axis=0)) * dec
            st = s_ref[0, h]
            stb = st.astype(BF)
            v = prow[ts, HK + h * DV:HK + (h + 1) * DV].astype(BF)
            rhs = jnp.concatenate(
                [jnp.concatenate([v, zeros_v], axis=1)]
                + ([stb, zeros_s] if lo else [zeros_s, stb]), axis=0)
            lhs = jnp.concatenate([s.astype(BF), (qp * e_int).astype(BF)], axis=1)
            res = _dot(lhs, rhs)
            den = jnp.sum(s, axis=-1, keepdims=True) + res[:, DV:]
            hh = res[:, :DV] / jnp.maximum(jnp.abs(den), floor_b)
            hh = hh * lax.rsqrt(jnp.mean(hh * hh, axis=-1, keepdims=True) + EPS)
            y_scr[ts, h * DV:(h + 1) * DV] = hh
            ka = (kt * a[h:h + 1, :]).astype(BF)
            dc = decay[h:h + 1, :]
            s_ref[0, h] = (jnp.concatenate([dc, dc], axis=1) * st
                           + _dot(ka, jnp.concatenate([v, ones_v], axis=1)))

    o = prow[:, HK + D:]
    y = (y_scr[...] * gh_ref[...] * jax.nn.sigmoid(o)).astype(BF)
    xo_ref[0] = x + mod_ref[0, 2] * _dot(y, wout_ref[...])


def _mlstm_prompt(x, mod, g, w_row, w_col, b_g, g_head, w_out, *, tm, casts=()):
    nb, t, _ = x.shape
    nt = t // tm
    tok = pl.BlockSpec((1, tm, D), lambda b, i: (b, i, 0))
    args = (x, mod, g.reshape(1, D), w_row, w_col,
            jnp.broadcast_to(b_g.reshape(2 * H, 1), (2 * H, tm)), g_head.reshape(1, D), w_out)
    c_in, c_out, c_shapes, c_args = _cast_specs(casts, nb * nt, lambda b, i: b * nt + i)
    return pl.pallas_call(
        _carry_casts(functools.partial(_mlstm_prompt_kernel, tm=tm), len(args), 3, len(casts)),
        grid=(nb, nt),
        in_specs=[
            tok,
            pl.BlockSpec((1, N_MOD, 1, D), lambda b, i: (b, 0, 0, 0)),
            _const_spec((1, D)),
            _const_spec((D, HK + 2 * D)),
            _const_spec((HK + 2 * LANES, D)),
            _const_spec((2 * H, tm)),
            _const_spec((1, D)),
            _const_spec((D, D)),
        ] + c_in,
        out_specs=[
            tok,
            pl.BlockSpec((1, H, DK, 2 * DV), lambda b, i: (b, 0, 0, 0)),
            pl.BlockSpec((1, H, CHUNK), lambda b, i: (b, 0, 0)),
        ] + c_out,
        out_shape=[
            jax.ShapeDtypeStruct(x.shape, F32),
            jax.ShapeDtypeStruct((nb, H, DK, 2 * DV), F32),
            jax.ShapeDtypeStruct((nb, H, CHUNK), F32),
        ] + c_shapes,
        scratch_shapes=[pltpu.VMEM((tm, D), F32)],
        compiler_params=_params("arbitrary", "arbitrary"),
        name="mlstm_prompt",
    )(*args, *c_args)


def _mlstm_sample_pre_kernel(x_ref, mod_ref, g_ref, wrow_ref, wcol_ref, bg_ref, m0_ref, n0_ref,
                             gh_ref, seg_ref, e64_ref, e128_ref,
                             q_ref, ka_ref, v_ref, dec_ref, numi_ref, rd_ref, eint_ref, go_ref,
                             n1_ref, m1_ref, *, t, nb):
    def norm(xr, shift, scale):
        return _modulate(xr, g_ref[...], shift, scale)

    hb = _by_group(norm, (x_ref[0],), (mod_ref[0, 0], mod_ref[0, 1])).astype(BF)
    p = _dot(hb, wrow_ref[...])
    k_all = _dot_nt(hb, wcol_ref[0:HK, :])
    ig_all = _dot_nt(hb, wcol_ref[HK:HK + LANES, :]) + bg_ref[0:1]
    fg_all = _dot_nt(hb, wcol_ref[HK + LANES:HK + 2 * LANES, :]) + bg_ref[1:2]

    def rows(i):
        return slice(i * nb, (i + 1) * nb)

    q = [p[rows(i), 0:HK] for i in range(t)]
    k = [k_all[rows(i)] for i in range(t)]
    v = [p[rows(i), HK:HK + D] for i in range(t)]
    m0 = m0_ref[...]
    n0 = n0_ref[...]

    b, r, mx = [], [], []
    for i in range(t):
        lf = _log_sigmoid(fg_all[rows(i)])
        b.append(lf if i == 0 else b[-1] + lf)
        r.append(ig_all[rows(i)] - b[i])
        cm = r[i] if i == 0 else jnp.maximum(cm, r[i])
        mx.append(jnp.maximum(m0, cm))
    mx_last = mx[t - 1]
    decay = jnp.exp(m0 - mx_last)
    m1_ref[...] = b[t - 1] + mx_last
    dec_x = _split_dot(decay, e128_ref)
    n1 = _split_dot(decay, e64_ref) * n0

    for i in range(t):
        ka = k[i] * _split_dot(jnp.exp(r[i] - mx_last), e64_ref)
        n1 = n1 + ka
        seq_rows = pl.ds(i, nb, stride=t)
        for c in range(HK // LANES):
            cs = slice(c * LANES, (c + 1) * LANES)
            ka_ref[c, seq_rows, :] = ka[:, cs]
            q_ref[c, seq_rows, :] = q[i][:, cs]
        for c in range(D // LANES):
            cs = slice(c * LANES, (c + 1) * LANES)
            v_ref[c, seq_rows, :] = v[i][:, cs]
            dec_ref[c, seq_rows, :] = dec_x[:, cs]
        e_int = jnp.exp(m0 - mx[i])
        den = e_int * _split_dot(q[i] * n0, seg_ref)
        numi = jnp.zeros((nb, D), F32)
        for s in range(i + 1):
            w = _split_dot(q[i] * k[s], seg_ref) * jnp.exp(jnp.minimum(r[s] - mx[i], 0.0))
            den = den + w
            numi = numi + _split_dot(w, e128_ref) * v[s]
        numi_ref[rows(i), :] = numi
        rd_ref[rows(i), :] = 1.0 / jnp.maximum(jnp.abs(den), jnp.exp(-(b[i] + mx[i])))
        eint_ref[rows(i), :] = e_int
        go_ref[rows(i), :] = jax.nn.sigmoid(p[rows(i), HK + D:]) * gh_ref[...]
    n1_ref[...] = n1


def _mlstm_sample_pre(x, mod, layer, g, w_row, w_col, b_g, m0, n0, g_head, seg64, e64, e128, *, t):
    rows = x.shape[1]
    nb = rows // t

    def whole(shape):
        return pl.BlockSpec(shape, lambda i: (0,) * len(shape))

    outs = [(HK // LANES, rows, LANES), (HK // LANES, rows, LANES), (D // LANES, rows, LANES),
            (D // LANES, rows, LANES), (rows, D), (rows, FG), (rows, FG), (rows, D), (nb, HK),
            (nb, FG)]
    return pl.pallas_call(
        functools.partial(_mlstm_sample_pre_kernel, t=t, nb=nb),
        grid=(1,),
        in_specs=[
            whole((1, rows, D)),
            _group_mod_spec(layer, nb),
            _const_spec((1, D)),
            _const_spec((D, HK + 2 * D)),
            _const_spec((HK + 2 * LANES, D)),
            _const_spec((2, LANES)),
            whole((nb, FG)),
            whole((nb, HK)),
            _const_spec((1, D)),
            _const_spec((HK, FG)),
            _const_spec((FG, HK)),
            _const_spec((FG, D)),
        ],
        out_specs=[whole(s) for s in outs],
        out_shape=[jax.ShapeDtypeStruct(s, F32) for s in outs],
        compiler_params=_params("arbitrary"),
        name="mlstm_sample_pre",
    )(x, mod, g.reshape(1, D), w_row, w_col, b_g, m0, n0, g_head.reshape(1, D), seg64, e64, e128)


def _mlstm_sample_state_kernel(q_ref, ka_ref, v_ref, dec_ref, c0_ref, inter_ref, c1_ref, *, bb, t):
    per = SUBLANES // t
    seq_of_row = lax.broadcasted_iota(jnp.int32, (SUBLANES, 1), 0) // t
    zeros_c = jnp.zeros((DK, DV), BF)

    def tile(i, carry):
        rs = pl.ds(pl.multiple_of(i * SUBLANES, SUBLANES), SUBLANES)
        for h in range(H):
            lo = h % 2 == 0
            half = slice((h % 2) * DK, (h % 2 + 1) * DK)
            qp = q_ref[h // 2, rs, :].astype(BF)
            kap = ka_ref[h // 2, rs, :]
            vh = v_ref[h, rs, :].astype(BF)
            dech = dec_ref[h, rs, :]
            res = None
            for w in range(per):
                seq = i * per + w
                c0 = c0_ref[seq, h]
                c0b = c0.astype(BF)
                rw = _dot(qp, jnp.concatenate([c0b, zeros_c] if lo else [zeros_c, c0b], axis=0))
                res = rw if res is None else jnp.where(seq_of_row == w, rw, res)
                kaw = jnp.where(seq_of_row == w, kap, 0.0).astype(BF)
                c1_ref[seq, h] = dech[w * t:w * t + 1, :] * c0 + _dot_tn(kaw, vh)[half, :]
            inter_ref[h, rs, :] = res
        return carry

    lax.fori_loop(0, bb // per, tile, 0, unroll=4)


def _mlstm_sample_state(q, ka, v, dec, c0, *, t, bb):
    rows = q.shape[1]
    nb = rows // t
    assert SUBLANES % t == 0 and bb % (SUBLANES // t) == 0 and nb % bb == 0

    def tok(blocks):
        return pl.BlockSpec((blocks, bb * t, LANES), lambda i: (0, i, 0))

    state = pl.BlockSpec((bb, H, DK, DV), lambda i: (i, 0, 0, 0))
    return pl.pallas_call(
        functools.partial(_mlstm_sample_state_kernel, bb=bb, t=t),
        grid=(nb // bb,),
        in_specs=[tok(HK // LANES), tok(HK // LANES), tok(H), tok(H), state],
        out_specs=[tok(H), state],
        out_shape=[jax.ShapeDtypeStruct((H, rows, LANES), F32),
                   jax.ShapeDtypeStruct(c0.shape, F32)],
        compiler_params=_params("arbitrary"),
        name="mlstm_sample_state",
    )(q, ka, v, dec, c0)


def _mlstm_sample_post_kernel(inter_ref, numi_ref, rd_ref, eint_ref, go_ref, e128_ref, seg_ref,
                              y_ref, *, t, nb):
    for i in range(t):
        rs = slice(i * nb, (i + 1) * nb)
        inter = jnp.concatenate(
            [inter_ref[h, pl.ds(i, nb, stride=t), :] for h in range(H)], axis=1)
        num = numi_ref[rs, :] + _split_dot(eint_ref[rs, :], e128_ref) * inter
        hh = num * _split_dot(rd_ref[rs, :], e128_ref)
        ms = _split_dot(hh * hh, seg_ref) * (1.0 / DV)
        y_ref[rs, :] = hh * _split_dot(lax.rsqrt(ms + EPS), e128_ref) * go_ref[rs, :]


def _mlstm_sample_post(inter, numi, rd, eint, go, e128, seg128, *, t):
    rows = inter.shape[1]

    def whole(shape):
        return pl.BlockSpec(shape, lambda i: (0,) * len(shape))

    return pl.pallas_call(
        functools.partial(_mlstm_sample_post_kernel, t=t, nb=rows // t),
        grid=(1,),
        in_specs=[whole((H, rows, LANES)), whole((rows, D)), whole((rows, FG)), whole((rows, FG)),
                  whole((rows, D)), _const_spec((FG, D)), _const_spec((D, FG))],
        out_specs=whole((rows, D)),
        out_shape=jax.ShapeDtypeStruct((rows, D), F32),
        compiler_params=_params("arbitrary"),
        name="mlstm_sample_post",
    )(inter, numi, rd, eint, go, e128, seg128)


def _conv_taps(u, prev2, prev1, wc_ref):
    return wc_ref[0:1] * prev2 + wc_ref[1:2] * prev1 + wc_ref[2:3] * u


def _conv_prompt_tile(x_ref, mod_ref, g_ref, win_ref, wc_ref, wout_ref, xo_ref, st_ref, carry_scr,
                      first):
    @pl.when(first)
    def _():
        carry_scr[...] = jnp.zeros(carry_scr.shape, F32)

    x = x_ref[0]
    tm = x.shape[0]
    hb = _modulate(x, g_ref[...], mod_ref[0, 0], mod_ref[0, 1]).astype(BF)
    u = _dot(hb, win_ref[:, D:2 * D]) * _dot(hb, win_ref[:, 2 * D:])
    row = lax.broadcasted_iota(jnp.int32, (tm, D), 0)
    c0 = carry_scr[0:1]
    c1 = carry_scr[1:2]
    prev1 = jnp.where(row == 0, c1, pltpu.roll(u, 1, 0))
    prev2 = jnp.where(row == 0, c0, jnp.where(row == 1, c1, pltpu.roll(u, 2, 0)))
    y = _conv_taps(u, prev2, prev1, wc_ref)
    bg = _dot(hb, win_ref[:, :D])
    xo_ref[0] = x + mod_ref[0, 2] * _dot((bg * y).astype(BF), wout_ref[...])
    carry_scr[...] = u[tm - 2:tm]
    st_ref[0] = u[tm - 2:tm]


def _conv_sample_tile(x_ref, mod_ref, g_ref, win_ref, wc_ref, wout_ref, buf_ref, xo_ref, st_ref,
                      *, t, nb):
    def norm(xr, shift, scale):
        return _modulate(xr, g_ref[...], shift, scale)

    def residual(xr, fr, gate):
        return xr + gate * fr

    x = x_ref[0]
    hb = _by_group(norm, (x,), (mod_ref[0, 0], mod_ref[0, 1])).astype(BF)
    u = _dot(hb, win_ref[:, D:2 * D]) * _dot(hb, win_ref[:, 2 * D:])
    full = [buf_ref[j] for j in range(CONV_W - 1)] + [u[i * nb:(i + 1) * nb] for i in range(t)]
    y = jnp.concatenate([_conv_taps(full[i + 2], full[i], full[i + 1], wc_ref) for i in range(t)],
                        axis=0)
    bg = _dot(hb, win_ref[:, :D])
    out = _dot((bg * y).astype(BF), wout_ref[...])
    xo_ref[0] = _by_group(residual, (x, out), (mod_ref[0, 2],))
    for j in range(CONV_W - 1):
        st_ref[j] = full[t + j]


def _conv_kernel(xp_ref, modp_ref, xs_ref, mods_ref, g_ref, win_ref, wc_ref, wout_ref, buf_ref,
                 xpo_ref, stp_ref, xso_ref, sts_ref, carry_scr, *, n_prompt, nt, t, nb):
    step = pl.program_id(0)

    @pl.when(step < n_prompt)
    def _():
        _conv_prompt_tile(xp_ref, modp_ref, g_ref, win_ref, wc_ref, wout_ref, xpo_ref, stp_ref,
                          carry_scr, step % nt == 0)

    @pl.when(step == n_prompt)
    def _():
        _conv_sample_tile(xs_ref, mods_ref, g_ref, win_ref, wc_ref, wout_ref, buf_ref, xso_ref,
                          sts_ref, t=t, nb=nb)


def _conv(xp, mod_p, xs, mod, layer, g, w_in, w_conv, w_out, buf, *, tm, t, casts=()):
    nb, tp, _ = xp.shape
    nt = tp // tm
    n_prompt = nb * nt
    rows_s = xs.shape[1]
    seqs = rows_s // t
    tok, per_seq = _prompt_tiles(nb, nt)
    args = (xp, mod_p, xs, mod, g.reshape(1, D), w_in, w_conv, w_out, buf)
    c_in, c_out, c_shapes, c_args = _cast_specs(
        casts, n_prompt, lambda s: jnp.minimum(s, n_prompt - 1))
    body = functools.partial(_conv_kernel, n_prompt=n_prompt, nt=nt, t=t, nb=seqs)
    return pl.pallas_call(
        _carry_casts(body, len(args), 4, len(casts)),
        grid=(n_prompt + 1,),
        in_specs=[
            tok(tm),
            per_seq((N_MOD, 1, D)),
            _const_spec((1, rows_s, D)),
            _group_mod_spec(layer, seqs),
            _const_spec((1, D)),
            _const_spec((D, 3 * D)),
            _const_spec((CONV_W, D)),
            _const_spec((D, D)),
            _const_spec((CONV_W - 1, seqs, D)),
        ] + c_in,
        out_specs=[tok(tm), per_seq((CONV_W - 1, D)), _whole_spec((1, rows_s, D)),
                   _whole_spec((CONV_W - 1, seqs, D))] + c_out,
        out_shape=[jax.ShapeDtypeStruct(xp.shape, F32),
                   jax.ShapeDtypeStruct((nb, CONV_W - 1, D), F32),
                   jax.ShapeDtypeStruct(xs.shape, F32),
                   jax.ShapeDtypeStruct((CONV_W - 1, seqs, D), F32)] + c_shapes,
        scratch_shapes=[pltpu.VMEM((CONV_W - 1, D), F32)],
        compiler_params=_params("arbitrary"),
        name="conv",
    )(*args, *c_args)


def kernel(x_prompt, x_sample, c_prompt, c_sample, state_mlstm_C, state_mlstm_n, state_mlstm_m,
           state_conv, ada_w, ada_b, norm_g, final_norm_g, mlstm_w_in, mlstm_b_gates,
           mlstm_head_g, mlstm_w_out, conv_w_in, conv_w, conv_w_out, mlp_w_up, mlp_w_down):
    assert ada_w.shape[0] == 2 and mlstm_w_in.shape[0] == 1 and conv_w_in.shape[0] == 1
    bp, tp, _ = x_prompt.shape
    bs, ts, _ = x_sample.shape
    assert ts == 4 and tp % 512 == 0

    mod = _ada(jnp.concatenate([c_sample, c_prompt], axis=0), ada_w, ada_b)
    mod_p = jnp.transpose(mod[:, :, bs:], (0, 2, 1, 3))[:, :, :, None]

    w_in0 = mlstm_w_in[0]
    w_row = jnp.concatenate([w_in0[:, :HK], w_in0[:, 2 * HK:NP]], axis=1).astype(BF)
    gate_pad = jnp.zeros((D, LANES - H), F32)
    w_col = jnp.concatenate([w_in0[:, HK:2 * HK] * (DK ** -0.5), w_in0[:, NP:NP + H], gate_pad,
                             w_in0[:, NP + H:], gate_pad], axis=1).T.astype(BF)
    b_g = jnp.pad(mlstm_b_gates[0].reshape(2, H), ((0, 0), (0, LANES - H)))
    wout0 = mlstm_w_out[0].astype(BF)

    x, s_p, m_p, up0, down0 = _mlstm_prompt(
        x_prompt, mod_p[0], norm_g[0, 0], w_row, w_col, mlstm_b_gates[0], mlstm_head_g[0], wout0,
        tm=512, casts=((mlp_w_up, 0), (mlp_w_down, 0)))
    prompt_c = s_p[None, :, :, :, :DV]
    prompt_n = s_p[None, :, :, :, DV]
    prompt_m = m_p[None, :, :, 0]

    xs = jnp.transpose(x_sample, (1, 0, 2)).reshape(1, ts * bs, D)
    head_of_v = jnp.arange(D, dtype=jnp.int32) // DV
    head_of_k = jnp.arange(HK, dtype=jnp.int32) // DK
    lanes = jnp.arange(FG, dtype=jnp.int32)
    e128 = (lanes[:, None] == head_of_v[None, :]).astype(BF)
    e64 = (lanes[:, None] == head_of_k[None, :]).astype(BF)
    m0 = jnp.pad(state_mlstm_m[0], ((0, 0), (0, FG - H)))
    n0 = state_mlstm_n[0].reshape(bs, HK)
    q, ka, v, dec, numi, rd, eint, go, n1, m1 = _mlstm_sample_pre(
        xs, mod, 0, norm_g[0, 0], w_row, w_col, b_g, m0, n0, mlstm_head_g[0], e64.T, e64, e128,
        t=ts)
    inter, c1 = _mlstm_sample_state(q, ka, v, dec, state_mlstm_C[0], t=ts, bb=32)
    y = _mlstm_sample_post(inter, numi, rd, eint, go, e128, e128.T, t=ts)
    sample_n = n1.reshape(bs, H, DK)[None]
    sample_m = m1[None, :, :H]

    x, xs, cwin, cwout = _mlp(x, mod_p[0], xs, mod, 0, norm_g[0, 1], up0, down0, tm=512, seqs=bs,
                              pre=(y.reshape(1, ts * bs, D), wout0),
                              casts=((conv_w_in, 0), (conv_w_out, 0)))
    x, conv_p, xs, conv_s, up1, down1 = _conv(
        x, mod_p[1], xs, mod, 1, norm_g[1, 0], cwin, conv_w[0], cwout,
        jnp.transpose(state_conv[0], (1, 0, 2)), tm=512, t=ts,
        casts=((mlp_w_up, 1), (mlp_w_down, 1)))
    y_prompt, ys = _mlp(x, mod_p[1], xs, mod, 1, norm_g[1, 1], up1, down1, tm=512, seqs=bs,
                        final_g=final_norm_g)
    prompt_conv = conv_p[None]
    y_sample = jnp.transpose(ys.reshape(ts, bs, D), (1, 0, 2))
    sample_conv = jnp.transpose(conv_s, (1, 0, 2))[None]

    return (y_prompt, y_sample, prompt_c, prompt_n, prompt_m, prompt_conv,
            c1[None], sample_n, sample_m, sample_conv)
```

```python
import functools

import jax
import jax.numpy as jnp
from jax import lax
from jax.experimental import pallas as pl
from jax.experimental.pallas import tpu as pltpu

D = 1024
H = 8
DK = 64
DV = 128
HK = H * DK
NP = 2 * HK + 2 * D
FG = 128
LANES = 128
SUBLANES = 8
DFF = 4 * D
N_MOD = 6
CONV_W = 3
EPS = 1e-6
CHUNK = 128
FF_CHUNK = 1024
BF = jnp.bfloat16
F32 = jnp.float32
VMEM_LIMIT_BYTES = 56 * 1024 * 1024


def _dot(a, b):
    return jnp.dot(a, b, preferred_element_type=F32)


def _dot_nt(a, b):
    return lax.dot_general(a, b, (((1,), (1,)), ((), ())), preferred_element_type=F32)


def _dot_tn(a, b):
    return lax.dot_general(a, b, (((0,), (0,)), ((), ())), preferred_element_type=F32)


def _rms(x):
    return x * lax.rsqrt(jnp.mean(x * x, axis=-1, keepdims=True) + EPS)


def _modulate(x, g, shift, scale):
    return _rms(x) * g * (1.0 + scale) + shift


def _log_sigmoid(x):
    return jnp.minimum(x, 0.0) - jnp.log1p(jnp.exp(-jnp.abs(x)))


def _scan_lanes(x, op, ident, n):
    lane = lax.broadcasted_iota(jnp.int32, x.shape, 1) % n
    d = 1
    while d < n:
        x = op(x, jnp.where(lane >= d, pltpu.roll(x, d, 1), ident))
        d *= 2
    return x


def _by_group(fn, xs, mods):
    rows, r = xs[0].shape[0], mods[0].shape[0]
    if r == 1 or r == rows:
        return fn(*xs, *mods)
    return jnp.concatenate(
        [fn(*[x[i:i + r] for x in xs], *mods) for i in range(0, rows, r)], axis=0)


def _split_dot(x, m_ref):
    hi = x.astype(BF)
    lo = (x - hi.astype(F32)).astype(BF)
    m = m_ref[...]
    return _dot(hi, m) + _dot(lo, m)


def _cast_specs(jobs, steps, slab_of):
    ins, outs, shapes, args = [], [], [], []
    for w, layer in jobs:
        _, r, c = w.shape
        slab = r // steps
        assert slab * steps == r and slab % (2 * SUBLANES) == 0
        ins.append(pl.BlockSpec((1, slab, c), lambda *g, layer=layer: (layer, slab_of(*g), 0)))
        outs.append(pl.BlockSpec((slab, c), lambda *g: (slab_of(*g), 0)))
        shapes.append(jax.ShapeDtypeStruct((r, c), BF))
        args.append(w)
    return ins, outs, shapes, args


def _carry_casts(body, n_in, n_out, n_cast):
    def kernel(*refs):
        a, b = n_in, n_in + n_cast
        c, d = b + n_out, b + n_out + n_cast
        for src, dst in zip(refs[a:b], refs[c:d]):
            dst[...] = src[0].astype(BF)
        body(*refs[:a], *refs[b:c], *refs[d:])
    return kernel


def _const_spec(shape):
    nd = len(shape)
    return pl.BlockSpec(shape, lambda *_: (0,) * nd, pipeline_mode=pl.Buffered(1))


def _params(*sem):
    return pltpu.CompilerParams(dimension_semantics=sem, vmem_limit_bytes=VMEM_LIMIT_BYTES)


def _ada_kernel(c_ref, w_ref, b_ref, o_ref):
    c = c_ref[...]
    sc = (c * jax.nn.sigmoid(c)).astype(BF)
    o_ref[0, 0] = _dot(sc, w_ref[0].astype(BF)) + b_ref[0]


def _ada(c_all, ada_w, ada_b):
    depth, _, n = ada_w.shape
    nb = c_all.shape[0]
    return pl.pallas_call(
        _ada_kernel,
        grid=(depth, n // D),
        in_specs=[
            pl.BlockSpec((nb, D), lambda i, j: (0, 0)),
            pl.BlockSpec((1, D, D), lambda i, j: (i, 0, j)),
            pl.BlockSpec((1, 1, D), lambda i, j: (i, 0, j)),
        ],
        out_specs=pl.BlockSpec((1, 1, nb, D), lambda i, j: (i, j, 0, 0)),
        out_shape=jax.ShapeDtypeStruct((depth, n // D, nb, D), F32),
        compiler_params=_params("arbitrary", "arbitrary"),
        name="ada",
    )(c_all, ada_w, ada_b.reshape(depth, 1, n))


def _mlp_tile(x_ref, mod_ref, g_ref, wu_ref, wd_ref, fg_ref, o_ref, y_ref=None, wo_ref=None):
    def residual(xr, fr, gate):
        return xr + gate * fr

    def norm(xr, shift, scale):
        return _modulate(xr, g_ref[...], shift, scale)

    x = x_ref[0]
    if y_ref is not None:
        x = _by_group(residual, (x, _dot(y_ref[0].astype(BF), wo_ref[...])), (mod_ref[0, 2],))
    hb = _by_group(norm, (x,), (mod_ref[0, 3], mod_ref[0, 4])).astype(BF)
    acc = jnp.zeros(x.shape, F32)
    for c in range(DFF // FF_CHUNK):
        cs = slice(c * FF_CHUNK, (c + 1) * FF_CHUNK)
        hid = jnp.maximum(_dot(hb, wu_ref[:, cs]), 0.0)
        acc = acc + _dot((hid * hid).astype(BF), wd_ref[cs, :])
    x = _by_group(residual, (x, acc), (mod_ref[0, 5],))
    if fg_ref is not None:
        x = _rms(x) * fg_ref[...]
    o_ref[0] = x


def _mlp_kernel(*refs, n_prompt, pre, final):
    refs = list(refs)
    xp_ref, modp_ref, xs_ref = refs[:3]
    refs = refs[3:]
    ys_ref = wo_ref = fg_ref = None
    if pre:
        ys_ref, wo_ref = refs[:2]
        refs = refs[2:]
    mods_ref, g_ref, wu_ref, wd_ref = refs[:4]
    refs = refs[4:]
    if final:
        fg_ref = refs.pop(0)
    xpo_ref, xso_ref = refs
    step = pl.program_id(0)

    @pl.when(step < n_prompt)
    def _():
        _mlp_tile(xp_ref, modp_ref, g_ref, wu_ref, wd_ref, fg_ref, xpo_ref)

    @pl.when(step == n_prompt)
    def _():
        _mlp_tile(xs_ref, mods_ref, g_ref, wu_ref, wd_ref, fg_ref, xso_ref, ys_ref, wo_ref)


def _group_mod_spec(layer, r):
    return pl.BlockSpec((1, N_MOD, r, D), lambda *_: (layer, 0, 0, 0),
                        pipeline_mode=pl.Buffered(1))


def _prompt_tiles(nb, nt):
    def seq_tile(step):
        c = jnp.minimum(step, nb * nt - 1)
        return c // nt, c % nt

    def tok(tm):
        return pl.BlockSpec((1, tm, D), lambda s: (*seq_tile(s), 0))

    def per_seq(shape):
        return pl.BlockSpec((1,) + shape, lambda s: (seq_tile(s)[0],) + (0,) * len(shape))

    return tok, per_seq


def _whole_spec(shape):
    return pl.BlockSpec(shape, lambda *_: (0,) * len(shape))


def _mlp(xp, mod_p, xs, mod, layer, g, w_up, w_down, *, tm, seqs, pre=None, final_g=None,
         casts=()):
    nb, t, _ = xp.shape
    nt = t // tm
    n_prompt = nb * nt
    rows_s = xs.shape[1]
    tok, per_seq = _prompt_tiles(nb, nt)
    args = [xp, mod_p, xs]
    specs = [tok(tm), per_seq((N_MOD, 1, D)), _const_spec((1, rows_s, D))]
    if pre is not None:
        args += list(pre)
        specs += [_const_spec((1, rows_s, D)), _const_spec((D, D))]
    args += [mod, g.reshape(1, D), w_up, w_down]
    specs += [_group_mod_spec(layer, seqs), _const_spec((1, D)), _const_spec((D, DFF)),
              _const_spec((DFF, D))]
    if final_g is not None:
        args.append(final_g.reshape(1, D))
        specs.append(_const_spec((1, D)))
    c_in, c_out, c_shapes, c_args = _cast_specs(
        casts, n_prompt, lambda s: jnp.minimum(s, n_prompt - 1))
    body = functools.partial(_mlp_kernel, n_prompt=n_prompt, pre=pre is not None,
                             final=final_g is not None)
    return pl.pallas_call(
        _carry_casts(body, len(args), 2, len(casts)),
        grid=(n_prompt + 1,),
        in_specs=specs + c_in,
        out_specs=[tok(tm), _whole_spec((1, rows_s, D))] + c_out,
        out_shape=[jax.ShapeDtypeStruct(xp.shape, F32), jax.ShapeDtypeStruct(xs.shape, F32)]
        + c_shapes,
        compiler_params=_params("arbitrary"),
        name="mlp",
    )(*args, *c_args)


def _mlstm_prompt_kernel(x_ref, mod_ref, g_ref, wrow_ref, wcol_ref, bg_ref, gh_ref, wout_ref,
                         xo_ref, s_ref, m_ref, y_scr, *, tm):
    @pl.when(pl.program_id(1) == 0)
    def _():
        s_ref[...] = jnp.zeros(s_ref.shape, F32)
        m_ref[...] = jnp.zeros(m_ref.shape, F32)

    x = x_ref[0]
    hb = _modulate(x, g_ref[...], mod_ref[0, 0], mod_ref[0, 1]).astype(BF)
    n = CHUNK
    w_t = jnp.concatenate([wcol_ref[0:HK, :], wcol_ref[HK:HK + H, :],
                           wcol_ref[HK + LANES:HK + LANES + H, :]], axis=0)
    pcol = _dot_nt(w_t, hb)
    ig_all = pcol[HK:HK + H] + bg_ref[0:H]
    fg_all = pcol[HK + H:HK + 2 * H] + bg_ref[H:2 * H]
    b_all = _scan_lanes(_log_sigmoid(fg_all), jnp.add, 0.0, n)
    r_all = ig_all - b_all
    cm_all = _scan_lanes(r_all, jnp.maximum, -jnp.inf, n)
    prow = _dot(hb, wrow_ref[...])
    causal = (lax.broadcasted_iota(jnp.int32, (n, n), 0)
              >= lax.broadcasted_iota(jnp.int32, (n, n), 1))
    zeros_k = jnp.zeros((DK, n), BF)
    zeros_s = jnp.zeros((DK, 2 * DV), BF)
    zeros_v = jnp.zeros((n, DV), BF)
    ones_v = jnp.ones((n, DV), BF)
    pad = jnp.zeros((n - 2 * H, n), F32)

    for c in range(tm // n):
        ts = slice(c * n, (c + 1) * n)
        r = r_all[:, ts]
        m0 = m_ref[0]
        mx = jnp.maximum(m0, cm_all[:, ts])
        m = b_all[:, ts] + mx
        e_neg = jnp.exp(-m)
        mx_last = jnp.broadcast_to(mx[:, n - 1:n], (H, n))
        a = jnp.exp(r - mx_last)
        decay = jnp.exp(m0 - mx_last)
        m_ref[0] = jnp.broadcast_to(m[:, n - 1:n], (H, n))
        cols = jnp.concatenate([mx, e_neg, pad], axis=0).T

        for h in range(H):
            lo = h % 2 == 0
            mx_b = jnp.broadcast_to(cols[:, h:h + 1], (n, n))
            floor_b = jnp.broadcast_to(cols[:, H + h:H + h + 1], (n, DV))
            dec = jnp.exp(jnp.where(causal, r[h:h + 1, :] - mx_b, -jnp.inf))
            e_int = jnp.exp(m0[h:h + 1, :] - mx_b)
            qp = prow[ts, (h // 2) * 2 * DK:(h // 2 + 1) * 2 * DK]
            kt = pcol[h * DK:(h + 1) * DK, ts]
            ktb = kt.astype(BF)
            s = _dot(qp.astype(BF),
                     jnp.concatenate([ktb, zeros_k] if lo else [zeros_k, ktb], axis=0)) * dec
            st = s_ref[0, h]
            stb = st.astype(BF)
            v = prow[ts, HK + h * DV:HK + (h + 1) * DV].astype(BF)
            rhs = jnp.concatenate(
                [jnp.concatenate([v, zeros_v], axis=1)]
                + ([stb, zeros_s] if lo else [zeros_s, stb]), axis=0)
            lhs = jnp.concatenate([s.astype(BF), (qp * e_int).astype(BF)], axis=1)
            res = _dot(lhs, rhs)
            den = jnp.sum(s, axis=-1, keepdims=True) + res[:, DV:]
            hh = res[:, :DV] / jnp.maximum(jnp.abs(den), floor_b)
            hh = hh * lax.rsqrt(jnp.mean(hh * hh, axis=-1, keepdims=True) + EPS)
            y_scr[ts, h * DV:(h + 1) * DV] = hh
            ka = (kt * a[h:h + 1, :]).astype(BF)
            dc = decay[h:h + 1, :]
            s_ref[0, h] = (jnp.concatenate([dc, dc], axis=1) * st
                           + _dot(ka, jnp.concatenate([v, ones_v], axis=1)))

    o = prow[:, HK + D:]
    y = (y_scr[...] * gh_ref[...] * jax.nn.sigmoid(o)).astype(BF)
    xo_ref[0] = x + mod_ref[0, 2] * _dot(y, wout_ref[...])


def _mlstm_prompt(x, mod, g, w_row, w_col, b_g, g_head, w_out, *, tm, casts=()):
    nb, t, _ = x.shape
    nt = t // tm
    tok = pl.BlockSpec((1, tm, D), lambda b, i: (b, i, 0))
    args = (x, mod, g.reshape(1, D), w_row, w_col,
            jnp.broadcast_to(b_g.reshape(2 * H, 1), (2 * H, tm)), g_head.reshape(1, D), w_out)
    c_in, c_out, c_shapes, c_args = _cast_specs(casts, nb * nt, lambda b, i: b * nt + i)
    return pl.pallas_call(
        _carry_casts(functools.partial(_mlstm_prompt_kernel, tm=tm), len(args), 3, len(casts)),
        grid=(nb, nt),
        in_specs=[
            tok,
            pl.BlockSpec((1, N_MOD, 1, D), lambda b, i: (b, 0, 0, 0)),
            _const_spec((1, D)),
            _const_spec((D, HK + 2 * D)),
            _const_spec((HK + 2 * LANES, D)),
            _const_spec((2 * H, tm)),
            _const_spec((1, D)),
            _const_spec((D, D)),
        ] + c_in,
        out_specs=[
            tok,
            pl.BlockSpec((1, H, DK, 2 * DV), lambda b, i: (b, 0, 0, 0)),
            pl.BlockSpec((1, H, CHUNK), lambda b, i: (b, 0, 0)),
        ] + c_out,
        out_shape=[
            jax.ShapeDtypeStruct(x.shape, F32),
            jax.ShapeDtypeStruct((nb, H, DK, 2 * DV), F32),
            jax.ShapeDtypeStruct((nb, H, CHUNK), F32),
        ] + c_shapes,
        scratch_shapes=[pltpu.VMEM((tm, D), F32)],
        compiler_params=_params("arbitrary", "arbitrary"),
        name="mlstm_prompt",
    )(*args, *c_args)


def _mlstm_sample_pre_kernel(x_ref, mod_ref, g_ref, wrow_ref, wcol_ref, bg_ref, m0_ref, n0_ref,
                             gh_ref, seg_ref, e64_ref, e128_ref,
                             q_ref, ka_ref, v_ref, dec_ref, numi_ref, rd_ref, eint_ref, go_ref,
                             n1_ref, m1_ref, *, t, nb):
    def norm(xr, shift, scale):
        return _modulate(xr, g_ref[...], shift, scale)

    hb = _by_group(norm, (x_ref[0],), (mod_ref[0, 0], mod_ref[0, 1])).astype(BF)
    p = _dot(hb, wrow_ref[...])
    k_all = _dot_nt(hb, wcol_ref[0:HK, :])
    ig_all = _dot_nt(hb, wcol_ref[HK:HK + LANES, :]) + bg_ref[0:1]
    fg_all = _dot_nt(hb, wcol_ref[HK + LANES:HK + 2 * LANES, :]) + bg_ref[1:2]

    def rows(i):
        return slice(i * nb, (i + 1) * nb)

    q = [p[rows(i), 0:HK] for i in range(t)]
    k = [k_all[rows(i)] for i in range(t)]
    v = [p[rows(i), HK:HK + D] for i in range(t)]
    m0 = m0_ref[...]
    n0 = n0_ref[...]

    b, r, mx = [], [], []
    for i in range(t):
        lf = _log_sigmoid(fg_all[rows(i)])
        b.append(lf if i == 0 else b[-1] + lf)
        r.append(ig_all[rows(i)] - b[i])
        cm = r[i] if i == 0 else jnp.maximum(cm, r[i])
        mx.append(jnp.maximum(m0, cm))
    mx_last = mx[t - 1]
    decay = jnp.exp(m0 - mx_last)
    m1_ref[...] = b[t - 1] + mx_last
    dec_x = _split_dot(decay, e128_ref)
    n1 = _split_dot(decay, e64_ref) * n0

    for i in range(t):
        ka = k[i] * _split_dot(jnp.exp(r[i] - mx_last), e64_ref)
        n1 = n1 + ka
        seq_rows = pl.ds(i, nb, stride=t)
        for c in range(HK // LANES):
            cs = slice(c * LANES, (c + 1) * LANES)
            ka_ref[c, seq_rows, :] = ka[:, cs]
            q_ref[c, seq_rows, :] = q[i][:, cs]
        for c in range(D // LANES):
            cs = slice(c * LANES, (c + 1) * LANES)
            v_ref[c, seq_rows, :] = v[i][:, cs]
            dec_ref[c, seq_rows, :] = dec_x[:, cs]
        e_int = jnp.exp(m0 - mx[i])
        den = e_int * _split_dot(q[i] * n0, seg_ref)
        numi = jnp.zeros((nb, D), F32)
        for s in range(i + 1):
            w = _split_dot(q[i] * k[s], seg_ref) * jnp.exp(jnp.minimum(r[s] - mx[i], 0.0))
            den = den + w
            numi = numi + _split_dot(w, e128_ref) * v[s]
        numi_ref[rows(i), :] = numi
        rd_ref[rows(i), :] = 1.0 / jnp.maximum(jnp.abs(den), jnp.exp(-(b[i] + mx[i])))
        eint_ref[rows(i), :] = e_int
        go_ref[rows(i), :] = jax.nn.sigmoid(p[rows(i), HK + D:]) * gh_ref[...]
    n1_ref[...] = n1


def _mlstm_sample_pre(x, mod, layer, g, w_row, w_col, b_g, m0, n0, g_head, seg64, e64, e128, *, t):
    rows = x.shape[1]
    nb = rows // t

    def whole(shape):
        return pl.BlockSpec(shape, lambda i: (0,) * len(shape))

    outs = [(HK // LANES, rows, LANES), (HK // LANES, rows, LANES), (D // LANES, rows, LANES),
            (D // LANES, rows, LANES), (rows, D), (rows, FG), (rows, FG), (rows, D), (nb, HK),
            (nb, FG)]
    return pl.pallas_call(
        functools.partial(_mlstm_sample_pre_kernel, t=t, nb=nb),
        grid=(1,),
        in_specs=[
            whole((1, rows, D)),
            _group_mod_spec(layer, nb),
            _const_spec((1, D)),
            _const_spec((D, HK + 2 * D)),
            _const_spec((HK + 2 * LANES, D)),
            _const_spec((2, LANES)),
            whole((nb, FG)),
            whole((nb, HK)),
            _const_spec((1, D)),
            _const_spec((HK, FG)),
            _const_spec((FG, HK)),
            _const_spec((FG, D)),
        ],
        out_specs=[whole(s) for s in outs],
        out_shape=[jax.ShapeDtypeStruct(s, F32) for s in outs],
        compiler_params=_params("arbitrary"),
        name="mlstm_sample_pre",
    )(x, mod, g.reshape(1, D), w_row, w_col, b_g, m0, n0, g_head.reshape(1, D), seg64, e64, e128)


def _mlstm_sample_state_kernel(q_ref, ka_ref, v_ref, dec_ref, c0_ref, inter_ref, c1_ref, *, bb, t):
    per = SUBLANES // t
    seq_of_row = lax.broadcasted_iota(jnp.int32, (SUBLANES, 1), 0) // t
    zeros_c = jnp.zeros((DK, DV), BF)

    def tile(i, carry):
        rs = pl.ds(pl.multiple_of(i * SUBLANES, SUBLANES), SUBLANES)
        for h in range(H):
            lo = h % 2 == 0
            half = slice((h % 2) * DK, (h % 2 + 1) * DK)
            qp = q_ref[h // 2, rs, :].astype(BF)
            kap = ka_ref[h // 2, rs, :]
            vh = v_ref[h, rs, :].astype(BF)
            dech = dec_ref[h, rs, :]
            res = None
            for w in range(per):
                seq = i * per + w
                c0 = c0_ref[seq, h]
                c0b = c0.astype(BF)
                rw = _dot(qp, jnp.concatenate([c0b, zeros_c] if lo else [zeros_c, c0b], axis=0))
                res = rw if res is None else jnp.where(seq_of_row == w, rw, res)
                kaw = jnp.where(seq_of_row == w, kap, 0.0).astype(BF)
                c1_ref[seq, h] = dech[w * t:w * t + 1, :] * c0 + _dot_tn(kaw, vh)[half, :]
            inter_ref[h, rs, :] = res
        return carry

    lax.fori_loop(0, bb // per, tile, 0, unroll=4)


def _mlstm_sample_state(q, ka, v, dec, c0, *, t, bb):
    rows = q.shape[1]
    nb = rows // t
    assert SUBLANES % t == 0 and bb % (SUBLANES // t) == 0 and nb % bb == 0

    def tok(blocks):
        return pl.BlockSpec((blocks, bb * t, LANES), lambda i: (0, i, 0))

    state = pl.BlockSpec((bb, H, DK, DV), lambda i: (i, 0, 0, 0))
    return pl.pallas_call(
        functools.partial(_mlstm_sample_state_kernel, bb=bb, t=t),
        grid=(nb // bb,),
        in_specs=[tok(HK // LANES), tok(HK // LANES), tok(H), tok(H), state],
        out_specs=[tok(H), state],
        out_shape=[jax.ShapeDtypeStruct((H, rows, LANES), F32),
                   jax.ShapeDtypeStruct(c0.shape, F32)],
        compiler_params=_params("arbitrary"),
        name="mlstm_sample_state",
    )(q, ka, v, dec, c0)


def _mlstm_sample_post_kernel(inter_ref, numi_ref, rd_ref, eint_ref, go_ref, e128_ref, seg_ref,
                              y_ref, *, t, nb):
    for i in range(t):
        rs = slice(i * nb, (i + 1) * nb)
        inter = jnp.concatenate(
            [inter_ref[h, pl.ds(i, nb, stride=t), :] for h in range(H)], axis=1)
        num = numi_ref[rs, :] + _split_dot(eint_ref[rs, :], e128_ref) * inter
        hh = num * _split_dot(rd_ref[rs, :], e128_ref)
        ms = _split_dot(hh * hh, seg_ref) * (1.0 / DV)
        y_ref[rs, :] = hh * _split_dot(lax.rsqrt(ms + EPS), e128_ref) * go_ref[rs, :]


def _mlstm_sample_post(inter, numi, rd, eint, go, e128, seg128, *, t):
    rows = inter.shape[1]

    def whole(shape):
        return pl.BlockSpec(shape, lambda i: (0,) * len(shape))

    return pl.pallas_call(
        functools.partial(_mlstm_sample_post_kernel, t=t, nb=rows // t),
        grid=(1,),
        in_specs=[whole((H, rows, LANES)), whole((rows, D)), whole((rows, FG)), whole((rows, FG)),
                  whole((rows, D)), _const_spec((FG, D)), _const_spec((D, FG))],
        out_specs=whole((rows, D)),
        out_shape=jax.ShapeDtypeStruct((rows, D), F32),
        compiler_params=_params("arbitrary"),
        name="mlstm_sample_post",
    )(inter, numi, rd, eint, go, e128, seg128)


def _conv_taps(u, prev2, prev1, wc_ref):
    return wc_ref[0:1] * prev2 + wc_ref[1:2] * prev1 + wc_ref[2:3] * u


def _conv_prompt_tile(x_ref, mod_ref, g_ref, win_ref, wc_ref, wout_ref, xo_ref, st_ref, carry_scr,
                      first):
    @pl.when(first)
    def _():
        carry_scr[...] = jnp.zeros(carry_scr.shape, F32)

    x = x_ref[0]
    tm = x.shape[0]
    hb = _modulate(x, g_ref[...], mod_ref[0, 0], mod_ref[0, 1]).astype(BF)
    u = _dot(hb, win_ref[:, D:2 * D]) * _dot(hb, win_ref[:, 2 * D:])
    row = lax.broadcasted_iota(jnp.int32, (tm, D), 0)
    c0 = carry_scr[0:1]
    c1 = carry_scr[1:2]
    prev1 = jnp.where(row == 0, c1, pltpu.roll(u, 1, 0))
    prev2 = jnp.where(row == 0, c0, jnp.where(row == 1, c1, pltpu.roll(u, 2, 0)))
    y = _conv_taps(u, prev2, prev1, wc_ref)
    bg = _dot(hb, win_ref[:, :D])
    xo_ref[0] = x + mod_ref[0, 2] * _dot((bg * y).astype(BF), wout_ref[...])
    carry_scr[...] = u[tm - 2:tm]
    st_ref[0] = u[tm - 2:tm]


def _conv_sample_tile(x_ref, mod_ref, g_ref, win_ref, wc_ref, wout_ref, buf_ref, xo_ref, st_ref,
                      *, t, nb):
    def norm(xr, shift, scale):
        return _modulate(xr, g_ref[...], shift, scale)

    def residual(xr, fr, gate):
        return xr + gate * fr

    x = x_ref[0]
    hb = _by_group(norm, (x,), (mod_ref[0, 0], mod_ref[0, 1])).astype(BF)
    u = _dot(hb, win_ref[:, D:2 * D]) * _dot(hb, win_ref[:, 2 * D:])
    full = [buf_ref[j] for j in range(CONV_W - 1)] + [u[i * nb:(i + 1) * nb] for i in range(t)]
    y = jnp.concatenate([_conv_taps(full[i + 2], full[i], full[i + 1], wc_ref) for i in range(t)],
                        axis=0)
    bg = _dot(hb, win_ref[:, :D])
    out = _dot((bg * y).astype(BF), wout_ref[...])
    xo_ref[0] = _by_group(residual, (x, out), (mod_ref[0, 2],))
    for j in range(CONV_W - 1):
        st_ref[j] = full[t + j]


def _conv_kernel(xp_ref, modp_ref, xs_ref, mods_ref, g_ref, win_ref, wc_ref, wout_ref, buf_ref,
                 xpo_ref, stp_ref, xso_ref, sts_ref, carry_scr, *, n_prompt, nt, t, nb):
    step = pl.program_id(0)

    @pl.when(step < n_prompt)
    def _():
        _conv_prompt_tile(xp_ref, modp_ref, g_ref, win_ref, wc_ref, wout_ref, xpo_ref, stp_ref,
                          carry_scr, step % nt == 0)

    @pl.when(step == n_prompt)
    def _():
        _conv_sample_tile(xs_ref, mods_ref, g_ref, win_ref, wc_ref, wout_ref, buf_ref, xso_ref,
                          sts_ref, t=t, nb=nb)


def _conv(xp, mod_p, xs, mod, layer, g, w_in, w_conv, w_out, buf, *, tm, t, casts=()):
    nb, tp, _ = xp.shape
    nt = tp // tm
    n_prompt = nb * nt
    rows_s = xs.shape[1]
    seqs = rows_s // t
    tok, per_seq = _prompt_tiles(nb, nt)
    args = (xp, mod_p, xs, mod, g.reshape(1, D), w_in, w_conv, w_out, buf)
    c_in, c_out, c_shapes, c_args = _cast_specs(
        casts, n_prompt, lambda s: jnp.minimum(s, n_prompt - 1))
    body = functools.partial(_conv_kernel, n_prompt=n_prompt, nt=nt, t=t, nb=seqs)
    return pl.pallas_call(
        _carry_casts(body, len(args), 4, len(casts)),
        grid=(n_prompt + 1,),
        in_specs=[
            tok(tm),
            per_seq((N_MOD, 1, D)),
            _const_spec((1, rows_s, D)),
            _group_mod_spec(layer, seqs),
            _const_spec((1, D)),
            _const_spec((D, 3 * D)),
            _const_spec((CONV_W, D)),
            _const_spec((D, D)),
            _const_spec((CONV_W - 1, seqs, D)),
        ] + c_in,
        out_specs=[tok(tm), per_seq((CONV_W - 1, D)), _whole_spec((1, rows_s, D)),
                   _whole_spec((CONV_W - 1, seqs, D))] + c_out,
        out_shape=[jax.ShapeDtypeStruct(xp.shape, F32),
                   jax.ShapeDtypeStruct((nb, CONV_W - 1, D), F32),
                   jax.ShapeDtypeStruct(xs.shape, F32),
                   jax.ShapeDtypeStruct((CONV_W - 1, seqs, D), F32)] + c_shapes,
        scratch_shapes=[pltpu.VMEM((CONV_W - 1, D), F32)],
        compiler_params=_params("arbitrary"),
        name="conv",
    )(*args, *c_args)


def kernel(x_prompt, x_sample, c_prompt, c_sample, state_mlstm_C, state_mlstm_n, state_mlstm_m,
           state_conv, ada_w, ada_b, norm_g, final_norm_g, mlstm_w_in, mlstm_b_gates,
           mlstm_head_g, mlstm_w_out, conv_w_in, conv_w, conv_w_out, mlp_w_up, mlp_w_down):
    assert ada_w.shape[0] == 2 and mlstm_w_in.shape[0] == 1 and conv_w_in.shape[0] == 1
    bp, tp, _ = x_prompt.shape
    bs, ts, _ = x_sample.shape
    assert ts == 4 and tp % 512 == 0

    mod = _ada(jnp.concatenate([c_sample, c_prompt], axis=0), ada_w, ada_b)
    mod_p = jnp.transpose(mod[:, :, bs:], (0, 2, 1, 3))[:, :, :, None]

    w_in0 = mlstm_w_in[0]
    w_row = jnp.concatenate([w_in0[:, :HK], w_in0[:, 2 * HK:NP]], axis=1).astype(BF)
    gate_pad = jnp.zeros((D, LANES - H), F32)
    w_col = jnp.concatenate([w_in0[:, HK:2 * HK] * (DK ** -0.5), w_in0[:, NP:NP + H], gate_pad,
                             w_in0[:, NP + H:], gate_pad], axis=1).T.astype(BF)
    b_g = jnp.pad(mlstm_b_gates[0].reshape(2, H), ((0, 0), (0, LANES - H)))
    wout0 = mlstm_w_out[0].astype(BF)

    x, s_p, m_p, up0, down0 = _mlstm_prompt(
        x_prompt, mod_p[0], norm_g[0, 0], w_row, w_col, mlstm_b_gates[0], mlstm_head_g[0], wout0,
        tm=1024, casts=((mlp_w_up, 0), (mlp_w_down, 0)))
    prompt_c = s_p[None, :, :, :, :DV]
    prompt_n = s_p[None, :, :, :, DV]
    prompt_m = m_p[None, :, :, 0]

    xs = jnp.transpose(x_sample, (1, 0, 2)).reshape(1, ts * bs, D)
    head_of_v = jnp.arange(D, dtype=jnp.int32) // DV
    head_of_k = jnp.arange(HK, dtype=jnp.int32) // DK
    lanes = jnp.arange(FG, dtype=jnp.int32)
    e128 = (lanes[:, None] == head_of_v[None, :]).astype(BF)
    e64 = (lanes[:, None] == head_of_k[None, :]).astype(BF)
    m0 = jnp.pad(state_mlstm_m[0], ((0, 0), (0, FG - H)))
    n0 = state_mlstm_n[0].reshape(bs, HK)
    q, ka, v, dec, numi, rd, eint, go, n1, m1 = _mlstm_sample_pre(
        xs, mod, 0, norm_g[0, 0], w_row, w_col, b_g, m0, n0, mlstm_head_g[0], e64.T, e64, e128,
        t=ts)
    inter, c1 = _mlstm_sample_state(q, ka, v, dec, state_mlstm_C[0], t=ts, bb=32)
    y = _mlstm_sample_post(inter, numi, rd, eint, go, e128, e128.T, t=ts)
    sample_n = n1.reshape(bs, H, DK)[None]
    sample_m = m1[None, :, :H]

    x, xs, cwin, cwout = _mlp(x, mod_p[0], xs, mod, 0, norm_g[0, 1], up0, down0, tm=512, seqs=bs,
                              pre=(y.reshape(1, ts * bs, D), wout0),
                              casts=((conv_w_in, 0), (conv_w_out, 0)))
    x, conv_p, xs, conv_s, up1, down1 = _conv(
        x, mod_p[1], xs, mod, 1, norm_g[1, 0], cwin, conv_w[0], cwout,
        jnp.transpose(state_conv[0], (1, 0, 2)), tm=512, t=ts,
        casts=((mlp_w_up, 1), (mlp_w_down, 1)))
    y_prompt, ys = _mlp(x, mod_p[1], xs, mod, 1, norm_g[1, 1], up1, down1, tm=512, seqs=bs,
                        final_g=final_norm_g)
    prompt_conv = conv_p[None]
    y_sample = jnp.transpose(ys.reshape(ts, bs, D), (1, 0, 2))
    sample_conv = jnp.transpose(conv_s, (1, 0, 2))[None]

    return (y_prompt, y_sample, prompt_c, prompt_n, prompt_m, prompt_conv,
            c1[None], sample_n, sample_m, sample_conv)
```

```python
import functools

import jax
import jax.numpy as jnp
from jax import lax
from jax.experimental import pallas as pl
from jax.experimental.pallas import tpu as pltpu

D = 1024
H = 8
DK = 64
DV = 128
HK = H * DK
NP = 2 * HK + 2 * D
FG = 128
LANES = 128
SUBLANES = 8
DFF = 4 * D
N_MOD = 6
CONV_W = 3
EPS = 1e-6
CHUNK = 128
FF_CHUNK = 1024
BF = jnp.bfloat16
F32 = jnp.float32
VMEM_LIMIT_BYTES = 56 * 1024 * 1024


def _dot(a, b):
    return jnp.dot(a, b, preferred_element_type=F32)


def _dot_nt(a, b):
    return lax.dot_general(a, b, (((1,), (1,)), ((), ())), preferred_element_type=F32)


def _dot_tn(a, b):
    return lax.dot_general(a, b, (((0,), (0,)), ((), ())), preferred_element_type=F32)


def _rms(x):
    return x * lax.rsqrt(jnp.mean(x * x, axis=-1, keepdims=True) + EPS)


def _modulate(x, g, shift, scale):
    return _rms(x) * g * (1.0 + scale) + shift


def _log_sigmoid(x):
    return jnp.minimum(x, 0.0) - jnp.log1p(jnp.exp(-jnp.abs(x)))


def _scan_lanes(x, op, ident, n):
    lane = lax.broadcasted_iota(jnp.int32, x.shape, 1) % n
    d = 1
    while d < n:
        x = op(x, jnp.where(lane >= d, pltpu.roll(x, d, 1), ident))
        d *= 2
    return x


def _by_group(fn, xs, mods):
    rows, r = xs[0].shape[0], mods[0].shape[0]
    if r == 1 or r == rows:
        return fn(*xs, *mods)
    return jnp.concatenate(
        [fn(*[x[i:i + r] for x in xs], *mods) for i in range(0, rows, r)], axis=0)


def _split_dot(x, m_ref):
    hi = x.astype(BF)
    lo = (x - hi.astype(F32)).astype(BF)
    m = m_ref[...]
    return _dot(hi, m) + _dot(lo, m)


def _cast_specs(jobs, steps, slab_of):
    ins, outs, shapes, args = [], [], [], []
    for w, layer in jobs:
        _, r, c = w.shape
        slab = r // steps
        assert slab * steps == r and slab % (2 * SUBLANES) == 0
        ins.append(pl.BlockSpec((1, slab, c), lambda *g, layer=layer: (layer, slab_of(*g), 0)))
        outs.append(pl.BlockSpec((slab, c), lambda *g: (slab_of(*g), 0)))
        shapes.append(jax.ShapeDtypeStruct((r, c), BF))
        args.append(w)
    return ins, outs, shapes, args


def _carry_casts(body, n_in, n_out, n_cast):
    def kernel(*refs):
        a, b = n_in, n_in + n_cast
        c, d = b + n_out, b + n_out + n_cast
        for src, dst in zip(refs[a:b], refs[c:d]):
            dst[...] = src[0].astype(BF)
        body(*refs[:a], *refs[b:c], *refs[d:])
    return kernel


def _const_spec(shape):
    nd = len(shape)
    return pl.BlockSpec(shape, lambda *_: (0,) * nd, pipeline_mode=pl.Buffered(1))


def _params(*sem):
    return pltpu.CompilerParams(dimension_semantics=sem, vmem_limit_bytes=VMEM_LIMIT_BYTES)


def _ada_kernel(c_ref, w_ref, b_ref, o_ref):
    c = c_ref[...]
    sc = (c * jax.nn.sigmoid(c)).astype(BF)
    o_ref[0, 0] = _dot(sc, w_ref[0].astype(BF)) + b_ref[0]


def _ada(c_all, ada_w, ada_b):
    depth, _, n = ada_w.shape
    nb = c_all.shape[0]
    return pl.pallas_call(
        _ada_kernel,
        grid=(depth, n // D),
        in_specs=[
            pl.BlockSpec((nb, D), lambda i, j: (0, 0)),
            pl.BlockSpec((1, D, D), lambda i, j: (i, 0, j)),
            pl.BlockSpec((1, 1, D), lambda i, j: (i, 0, j)),
        ],
        out_specs=pl.BlockSpec((1, 1, nb, D), lambda i, j: (i, j, 0, 0)),
        out_shape=jax.ShapeDtypeStruct((depth, n // D, nb, D), F32),
        compiler_params=_params("arbitrary", "arbitrary"),
        name="ada",
    )(c_all, ada_w, ada_b.reshape(depth, 1, n))


def _mlp_tile(x_ref, mod_ref, g_ref, wu_ref, wd_ref, fg_ref, o_ref, y_ref=None, wo_ref=None):
    def residual(xr, fr, gate):
        return xr + gate * fr

    def norm(xr, shift, scale):
        return _modulate(xr, g_ref[...], shift, scale)

    x = x_ref[0]
    if y_ref is not None:
        x = _by_group(residual, (x, _dot(y_ref[0].astype(BF), wo_ref[...])), (mod_ref[0, 2],))
    hb = _by_group(norm, (x,), (mod_ref[0, 3], mod_ref[0, 4])).astype(BF)
    acc = jnp.zeros(x.shape, F32)
    for c in range(DFF // FF_CHUNK):
        cs = slice(c * FF_CHUNK, (c + 1) * FF_CHUNK)
        hid = jnp.maximum(_dot(hb, wu_ref[:, cs]), 0.0)
        acc = acc + _dot((hid * hid).astype(BF), wd_ref[cs, :])
    x = _by_group(residual, (x, acc), (mod_ref[0, 5],))
    if fg_ref is not None:
        x = _rms(x) * fg_ref[...]
    o_ref[0] = x


def _mlp_kernel(*refs, n_prompt, pre, final):
    refs = list(refs)
    xp_ref, modp_ref, xs_ref = refs[:3]
    refs = refs[3:]
    ys_ref = wo_ref = fg_ref = None
    if pre:
        ys_ref, wo_ref = refs[:2]
        refs = refs[2:]
    mods_ref, g_ref, wu_ref, wd_ref = refs[:4]
    refs = refs[4:]
    if final:
        fg_ref = refs.pop(0)
    xpo_ref, xso_ref = refs
    step = pl.program_id(0)

    @pl.when(step < n_prompt)
    def _():
        _mlp_tile(xp_ref, modp_ref, g_ref, wu_ref, wd_ref, fg_ref, xpo_ref)

    @pl.when(step == n_prompt)
    def _():
        _mlp_tile(xs_ref, mods_ref, g_ref, wu_ref, wd_ref, fg_ref, xso_ref, ys_ref, wo_ref)


def _group_mod_spec(layer, r):
    return pl.BlockSpec((1, N_MOD, r, D), lambda *_: (layer, 0, 0, 0),
                        pipeline_mode=pl.Buffered(1))


def _prompt_tiles(nb, nt):
    def seq_tile(step):
        c = jnp.minimum(step, nb * nt - 1)
        return c // nt, c % nt

    def tok(tm):
        return pl.BlockSpec((1, tm, D), lambda s: (*seq_tile(s), 0))

    def per_seq(shape):
        return pl.BlockSpec((1,) + shape, lambda s: (seq_tile(s)[0],) + (0,) * len(shape))

    return tok, per_seq


def _whole_spec(shape):
    return pl.BlockSpec(shape, lambda *_: (0,) * len(shape))


def _mlp(xp, mod_p, xs, mod, layer, g, w_up, w_down, *, tm, seqs, pre=None, final_g=None,
         casts=()):
    nb, t, _ = xp.shape
    nt = t // tm
    n_prompt = nb * nt
    rows_s = xs.shape[1]
    tok, per_seq = _prompt_tiles(nb, nt)
    args = [xp, mod_p, xs]
    specs = [tok(tm), per_seq((N_MOD, 1, D)), _const_spec((1, rows_s, D))]
    if pre is not None:
        args += list(pre)
        specs += [_const_spec((1, rows_s, D)), _const_spec((D, D))]
    args += [mod, g.reshape(1, D), w_up, w_down]
    specs += [_group_mod_spec(layer, seqs), _const_spec((1, D)), _const_spec((D, DFF)),
              _const_spec((DFF, D))]
    if final_g is not None:
        args.append(final_g.reshape(1, D))
        specs.append(_const_spec((1, D)))
    c_in, c_out, c_shapes, c_args = _cast_specs(
        casts, n_prompt, lambda s: jnp.minimum(s, n_prompt - 1))
    body = functools.partial(_mlp_kernel, n_prompt=n_prompt, pre=pre is not None,
                             final=final_g is not None)
    return pl.pallas_call(
        _carry_casts(body, len(args), 2, len(casts)),
        grid=(n_prompt + 1,),
        in_specs=specs + c_in,
        out_specs=[tok(tm), _whole_spec((1, rows_s, D))] + c_out,
        out_shape=[jax.ShapeDtypeStruct(xp.shape, F32), jax.ShapeDtypeStruct(xs.shape, F32)]
        + c_shapes,
        compiler_params=_params("arbitrary"),
        name="mlp",
    )(*args, *c_args)


def _mlstm_prompt_kernel(x_ref, mod_ref, g_ref, wrow_ref, wcol_ref, bg_ref, gh_ref, wout_ref,
                         xo_ref, s_ref, m_ref, y_scr, *, tm):
    @pl.when(pl.program_id(1) == 0)
    def _():
        s_ref[...] = jnp.zeros(s_ref.shape, F32)
        m_ref[...] = jnp.zeros(m_ref.shape, F32)

    x = x_ref[0]
    hb = _modulate(x, g_ref[...], mod_ref[0, 0], mod_ref[0, 1]).astype(BF)
    n = CHUNK
    w_t = jnp.concatenate([wcol_ref[0:HK, :], wcol_ref[HK:HK + H, :],
                           wcol_ref[HK + LANES:HK + LANES + H, :]], axis=0)
    pcol = _dot_nt(w_t, hb)
    ig_all = pcol[HK:HK + H] + bg_ref[0:H]
    fg_all = pcol[HK + H:HK + 2 * H] + bg_ref[H:2 * H]
    b_all = _scan_lanes(_log_sigmoid(fg_all), jnp.add, 0.0, n)
    r_all = ig_all - b_all
    cm_all = _scan_lanes(r_all, jnp.maximum, -jnp.inf, n)
    prow = _dot(hb, wrow_ref[...])
    causal = (lax.broadcasted_iota(jnp.int32, (n, n), 0)
              >= lax.broadcasted_iota(jnp.int32, (n, n), 1))
    zeros_k = jnp.zeros((DK, n), BF)
    zeros_s = jnp.zeros((DK, 2 * DV), BF)
    zeros_v = jnp.zeros((n, DV), BF)
    ones_v = jnp.ones((n, DV), BF)
    pad = jnp.zeros((n - 2 * H, n), F32)

    for c in range(tm // n):
        ts = slice(c * n, (c + 1) * n)
        r = r_all[:, ts]
        m0 = m_ref[0]
        mx = jnp.maximum(m0, cm_all[:, ts])
        m = b_all[:, ts] + mx
        e_neg = jnp.exp(-m)
        mx_last = jnp.broadcast_to(mx[:, n - 1:n], (H, n))
        a = jnp.exp(r - mx_last)
        decay = jnp.exp(m0 - mx_last)
        m_ref[0] = jnp.broadcast_to(m[:, n - 1:n], (H, n))
        cols = jnp.concatenate([mx, e_neg, pad], axis=0).T

        for h in range(H):
            lo = h % 2 == 0
            mx_b = jnp.broadcast_to(cols[:, h:h + 1], (n, n))
            floor_b = jnp.broadcast_to(cols[:, H + h:H + h + 1], (n, DV))
            dec = jnp.exp(jnp.where(causal, r[h:h + 1, :] - mx_b, -jnp.inf))
            e_int = jnp.exp(m0[h:h + 1, :] - mx_b)
            qp = prow[ts, (h // 2) * 2 * DK:(h // 2 + 1) * 2 * DK]
            kt = pcol[h * DK:(h + 1) * DK, ts]
            ktb = kt.astype(BF)
            s = _dot(qp.astype(BF),
                     jnp.concatenate([ktb, zeros_k] if lo else [zeros_k, ktb], axis=0)) * dec
            st = s_ref[0, h]
            stb = st.astype(BF)
            v = prow[ts, HK + h * DV:HK + (h + 1) * DV].astype(BF)
            rhs = jnp.concatenate(
                [jnp.concatenate([v, zeros_v], axis=1)]
                + ([stb, zeros_s] if lo else [zeros_s, stb]), axis=0)
            lhs = jnp.concatenate([s.astype(BF), (qp * e_int).astype(BF)], axis=1)
            res = _dot(lhs, rhs)
            den = jnp.sum(s, axis=-1, keepdims=True) + res[:, DV:]
            hh = res[:, :DV] / jnp.maximum(jnp.abs(den), floor_b)
            hh = hh * lax.rsqrt(jnp.mean(hh * hh, axis=-1, keepdims=True) + EPS)
            y_scr[ts, h * DV:(h + 1) * DV] = hh
            ka = (kt * a[h:h + 1, :]).astype(BF)
            dc = decay[h:h + 1, :]
            s_ref[0, h] = (jnp.concatenate([dc, dc], axis=1) * st
                           + _dot(ka, jnp.concatenate([v, ones_v], axis=1)))

    o = prow[:, HK + D:]
    y = (y_scr[...] * gh_ref[...] * jax.nn.sigmoid(o)).astype(BF)
    xo_ref[0] = x + mod_ref[0, 2] * _dot(y, wout_ref[...])


def _mlstm_prompt(x, mod, g, w_row, w_col, b_g, g_head, w_out, *, tm, casts=()):
    nb, t, _ = x.shape
    nt = t // tm
    tok = pl.BlockSpec((1, tm, D), lambda b, i: (b, i, 0))
    args = (x, mod, g.reshape(1, D), w_row, w_col,
            jnp.broadcast_to(b_g.reshape(2 * H, 1), (2 * H, tm)), g_head.reshape(1, D), w_out)
    c_in, c_out, c_shapes, c_args = _cast_specs(casts, nb * nt, lambda b, i: b * nt + i)
    return pl.pallas_call(
        _carry_casts(functools.partial(_mlstm_prompt_kernel, tm=tm), len(args), 3, len(casts)),
        grid=(nb, nt),
        in_specs=[
            tok,
            pl.BlockSpec((1, N_MOD, 1, D), lambda b, i: (b, 0, 0, 0)),
            _const_spec((1, D)),
            _const_spec((D, HK + 2 * D)),
            _const_spec((HK + 2 * LANES, D)),
            _const_spec((2 * H, tm)),
            _const_spec((1, D)),
            _const_spec((D, D)),
        ] + c_in,
        out_specs=[
            tok,
            pl.BlockSpec((1, H, DK, 2 * DV), lambda b, i: (b, 0, 0, 0)),
            pl.BlockSpec((1, H, CHUNK), lambda b, i: (b, 0, 0)),
        ] + c_out,
        out_shape=[
            jax.ShapeDtypeStruct(x.shape, F32),
            jax.ShapeDtypeStruct((nb, H, DK, 2 * DV), F32),
            jax.ShapeDtypeStruct((nb, H, CHUNK), F32),
        ] + c_shapes,
        scratch_shapes=[pltpu.VMEM((tm, D), F32)],
        compiler_params=_params("arbitrary", "arbitrary"),
        name="mlstm_prompt",
    )(*args, *c_args)


def _mlstm_sample_pre_kernel(x_ref, mod_ref, g_ref, wrow_ref, wcol_ref, bg_ref, m0_ref, n0_ref,
                             gh_ref, seg_ref, e64_ref, e128_ref,
                             q_ref, ka_ref, v_ref, dec_ref, numi_ref, rd_ref, eint_ref, go_ref,
                             n1_ref, m1_ref, *, t, nb):
    def norm(xr, shift, scale):
        return _modulate(xr, g_ref[...], shift, scale)

    hb = _by_group(norm, (x_ref[0],), (mod_ref[0, 0], mod_ref[0, 1])).astype(BF)
    p = _dot(hb, wrow_ref[...])
    k_all = _dot_nt(hb, wcol_ref[0:HK, :])
    ig_all = _dot_nt(hb, wcol_ref[HK:HK + LANES, :]) + bg_ref[0:1]
    fg_all = _dot_nt(hb, wcol_ref[HK + LANES:HK + 2 * LANES, :]) + bg_ref[1:2]

    def rows(i):
        return slice(i * nb, (i + 1) * nb)

    q = [p[rows(i), 0:HK] for i in range(t)]
    k = [k_all[rows(i)] for i in range(t)]
    v = [p[rows(i), HK:HK + D] for i in range(t)]
    m0 = m0_ref[...]
    n0 = n0_ref[...]

    b, r, mx = [], [], []
    for i in range(t):
        lf = _log_sigmoid(fg_all[rows(i)])
        b.append(lf if i == 0 else b[-1] + lf)
        r.append(ig_all[rows(i)] - b[i])
        cm = r[i] if i == 0 else jnp.maximum(cm, r[i])
        mx.append(jnp.maximum(m0, cm))
    mx_last = mx[t - 1]
    decay = jnp.exp(m0 - mx_last)
    m1_ref[...] = b[t - 1] + mx_last
    dec_x = _split_dot(decay, e128_ref)
    n1 = _split_dot(decay, e64_ref) * n0

    for i in range(t):
        ka = k[i] * _split_dot(jnp.exp(r[i] - mx_last), e64_ref)
        n1 = n1 + ka
        seq_rows = pl.ds(i, nb, stride=t)
        for c in range(HK // LANES):
            cs = slice(c * LANES, (c + 1) * LANES)
            ka_ref[c, seq_rows, :] = ka[:, cs]
            q_ref[c, seq_rows, :] = q[i][:, cs]
        for c in range(D // LANES):
            cs = slice(c * LANES, (c + 1) * LANES)
            v_ref[c, seq_rows, :] = v[i][:, cs]
            dec_ref[c, seq_rows, :] = dec_x[:, cs]
        e_int = jnp.exp(m0 - mx[i])
        den = e_int * _split_dot(q[i] * n0, seg_ref)
        numi = jnp.zeros((nb, D), F32)
        for s in range(i + 1):
            w = _split_dot(q[i] * k[s], seg_ref) * jnp.exp(jnp.minimum(r[s] - mx[i], 0.0))
            den = den + w
            numi = numi + _split_dot(w, e128_ref) * v[s]
        numi_ref[rows(i), :] = numi
        rd_ref[rows(i), :] = 1.0 / jnp.maximum(jnp.abs(den), jnp.exp(-(b[i] + mx[i])))
        eint_ref[rows(i), :] = e_int
        go_ref[rows(i), :] = jax.nn.sigmoid(p[rows(i), HK + D:]) * gh_ref[...]
    n1_ref[...] = n1


def _mlstm_sample_pre(x, mod, layer, g, w_row, w_col, b_g, m0, n0, g_head, seg64, e64, e128, *, t):
    rows = x.shape[1]
    nb = rows // t

    def whole(shape):
        return pl.BlockSpec(shape, lambda i: (0,) * len(shape))

    outs = [(HK // LANES, rows, LANES), (HK // LANES, rows, LANES), (D // LANES, rows, LANES),
            (D // LANES, rows, LANES), (rows, D), (rows, FG), (rows, FG), (rows, D), (nb, HK),
            (nb, FG)]
    return pl.pallas_call(
        functools.partial(_mlstm_sample_pre_kernel, t=t, nb=nb),
        grid=(1,),
        in_specs=[
            whole((1, rows, D)),
            _group_mod_spec(layer, nb),
            _const_spec((1, D)),
            _const_spec((D, HK + 2 * D)),
            _const_spec((HK + 2 * LANES, D)),
            _const_spec((2, LANES)),
            whole((nb, FG)),
            whole((nb, HK)),
            _const_spec((1, D)),
            _const_spec((HK, FG)),
            _const_spec((FG, HK)),
            _const_spec((FG, D)),
        ],
        out_specs=[whole(s) for s in outs],
        out_shape=[jax.ShapeDtypeStruct(s, F32) for s in outs],
        compiler_params=_params("arbitrary"),
        name="mlstm_sample_pre",
    )(x, mod, g.reshape(1, D), w_row, w_col, b_g, m0, n0, g_head.reshape(1, D), seg64, e64, e128)


def _mlstm_sample_state_kernel(q_ref, ka_ref, v_ref, dec_ref, c0_ref, inter_ref, c1_ref, *, bb, t):
    per = SUBLANES // t
    seq_of_row = lax.broadcasted_iota(jnp.int32, (SUBLANES, 1), 0) // t
    zeros_c = jnp.zeros((DK, DV), BF)

    def tile(i, carry):
        rs = pl.ds(pl.multiple_of(i * SUBLANES, SUBLANES), SUBLANES)
        for h in range(H):
            lo = h % 2 == 0
            half = slice((h % 2) * DK, (h % 2 + 1) * DK)
            qp = q_ref[h // 2, rs, :].astype(BF)
            kap = ka_ref[h // 2, rs, :]
            vh = v_ref[h, rs, :].astype(BF)
            dech = dec_ref[h, rs, :]
            res = None
            for w in range(per):
                seq = i * per + w
                c0 = c0_ref[seq, h]
                c0b = c0.astype(BF)
                rw = _dot(qp, jnp.concatenate([c0b, zeros_c] if lo else [zeros_c, c0b], axis=0))
                res = rw if res is None else jnp.where(seq_of_row == w, rw, res)
                kaw = jnp.where(seq_of_row == w, kap, 0.0).astype(BF)
                c1_ref[seq, h] = dech[w * t:w * t + 1, :] * c0 + _dot_tn(kaw, vh)[half, :]
            inter_ref[h, rs, :] = res
        return carry

    lax.fori_loop(0, bb // per, tile, 0, unroll=4)


def _mlstm_sample_state(q, ka, v, dec, c0, *, t, bb):
    rows = q.shape[1]
    nb = rows // t
    assert SUBLANES % t == 0 and bb % (SUBLANES // t) == 0 and nb % bb == 0

    def tok(blocks):
        return pl.BlockSpec((blocks, bb * t, LANES), lambda i: (0, i, 0))

    state = pl.BlockSpec((bb, H, DK, DV), lambda i: (i, 0, 0, 0))
    return pl.pallas_call(
        functools.partial(_mlstm_sample_state_kernel, bb=bb, t=t),
        grid=(nb // bb,),
        in_specs=[tok(HK // LANES), tok(HK // LANES), tok(H), tok(H), state],
        out_specs=[tok(H), state],
        out_shape=[jax.ShapeDtypeStruct((H, rows, LANES), F32),
                   jax.ShapeDtypeStruct(c0.shape, F32)],
        compiler_params=_params("arbitrary"),
        name="mlstm_sample_state",
    )(q, ka, v, dec, c0)


def _mlstm_sample_post_kernel(inter_ref, numi_ref, rd_ref, eint_ref, go_ref, e128_ref, seg_ref,
                              y_ref, *, t, nb):
    for i in range(t):
        rs = slice(i * nb, (i + 1) * nb)
        inter = jnp.concatenate(
            [inter_ref[h, pl.ds(i, nb, stride=t), :] for h in range(H)], axis=1)
        num = numi_ref[rs, :] + _split_dot(eint_ref[rs, :], e128_ref) * inter
        hh = num * _split_dot(rd_ref[rs, :], e128_ref)
        ms = _split_dot(hh * hh, seg_ref) * (1.0 / DV)
        y_ref[rs, :] = hh * _split_dot(lax.rsqrt(ms + EPS), e128_ref) * go_ref[rs, :]


def _mlstm_sample_post(inter, numi, rd, eint, go, e128, seg128, *, t):
    rows = inter.shape[1]

    def whole(shape):
        return pl.BlockSpec(shape, lambda i: (0,) * len(shape))

    return pl.pallas_call(
        functools.partial(_mlstm_sample_post_kernel, t=t, nb=rows // t),
        grid=(1,),
        in_specs=[whole((H, rows, LANES)), whole((rows, D)), whole((rows, FG)), whole((rows, FG)),
                  whole((rows, D)), _const_spec((FG, D)), _const_spec((D, FG))],
        out_specs=whole((rows, D)),
        out_shape=jax.ShapeDtypeStruct((rows, D), F32),
        compiler_params=_params("arbitrary"),
        name="mlstm_sample_post",
    )(inter, numi, rd, eint, go, e128, seg128)


def _conv_taps(u, prev2, prev1, wc_ref):
    return wc_ref[0:1] * prev2 + wc_ref[1:2] * prev1 + wc_ref[2:3] * u


def _conv_prompt_tile(x_ref, mod_ref, g_ref, win_ref, wc_ref, wout_ref, xo_ref, st_ref, carry_scr,
                      first):
    @pl.when(first)
    def _():
        carry_scr[...] = jnp.zeros(carry_scr.shape, F32)

    x = x_ref[0]
    tm = x.shape[0]
    hb = _modulate(x, g_ref[...], mod_ref[0, 0], mod_ref[0, 1]).astype(BF)
    u = _dot(hb, win_ref[:, D:2 * D]) * _dot(hb, win_ref[:, 2 * D:])
    row = lax.broadcasted_iota(jnp.int32, (tm, D), 0)
    c0 = carry_scr[0:1]
    c1 = carry_scr[1:2]
    prev1 = jnp.where(row == 0, c1, pltpu.roll(u, 1, 0))
    prev2 = jnp.where(row == 0, c0, jnp.where(row == 1, c1, pltpu.roll(u, 2, 0)))
    y = _conv_taps(u, prev2, prev1, wc_ref)
    bg = _dot(hb, win_ref[:, :D])
    xo_ref[0] = x + mod_ref[0, 2] * _dot((bg * y).astype(BF), wout_ref[...])
    carry_scr[...] = u[tm - 2:tm]
    st_ref[0] = u[tm - 2:tm]


def _conv_sample_tile(x_ref, mod_ref, g_ref, win_ref, wc_ref, wout_ref, buf_ref, xo_ref, st_ref,
                      *, t, nb):
    def norm(xr, shift, scale):
        return _modulate(xr, g_ref[...], shift, scale)

    def residual(xr, fr, gate):
        return xr + gate * fr

    x = x_ref[0]
    hb = _by_group(norm, (x,), (mod_ref[0, 0], mod_ref[0, 1])).astype(BF)
    u = _dot(hb, win_ref[:, D:2 * D]) * _dot(hb, win_ref[:, 2 * D:])
    full = [buf_ref[j] for j in range(CONV_W - 1)] + [u[i * nb:(i + 1) * nb] for i in range(t)]
    y = jnp.concatenate([_conv_taps(full[i + 2], full[i], full[i + 1], wc_ref) for i in range(t)],
                        axis=0)
    bg = _dot(hb, win_ref[:, :D])
    out = _dot((bg * y).astype(BF), wout_ref[...])
    xo_ref[0] = _by_group(residual, (x, out), (mod_ref[0, 2],))
    for j in range(CONV_W - 1):
        st_ref[j] = full[t + j]


def _conv_kernel(xp_ref, modp_ref, xs_ref, mods_ref, g_ref, win_ref, wc_ref, wout_ref, buf_ref,
                 xpo_ref, stp_ref, xso_ref, sts_ref, carry_scr, *, n_prompt, nt, t, nb):
    step = pl.program_id(0)

    @pl.when(step < n_prompt)
    def _():
        _conv_prompt_tile(xp_ref, modp_ref, g_ref, win_ref, wc_ref, wout_ref, xpo_ref, stp_ref,
                          carry_scr, step % nt == 0)

    @pl.when(step == n_prompt)
    def _():
        _conv_sample_tile(xs_ref, mods_ref, g_ref, win_ref, wc_ref, wout_ref, buf_ref, xso_ref,
                          sts_ref, t=t, nb=nb)


def _conv(xp, mod_p, xs, mod, layer, g, w_in, w_conv, w_out, buf, *, tm, t, casts=()):
    nb, tp, _ = xp.shape
    nt = tp // tm
    n_prompt = nb * nt
    rows_s = xs.shape[1]
    seqs = rows_s // t
    tok, per_seq = _prompt_tiles(nb, nt)
    args = (xp, mod_p, xs, mod, g.reshape(1, D), w_in, w_conv, w_out, buf)
    c_in, c_out, c_shapes, c_args = _cast_specs(
        casts, n_prompt, lambda s: jnp.minimum(s, n_prompt - 1))
    body = functools.partial(_conv_kernel, n_prompt=n_prompt, nt=nt, t=t, nb=seqs)
    return pl.pallas_call(
        _carry_casts(body, len(args), 4, len(casts)),
        grid=(n_prompt + 1,),
        in_specs=[
            tok(tm),
            per_seq((N_MOD, 1, D)),
            _const_spec((1, rows_s, D)),
            _group_mod_spec(layer, seqs),
            _const_spec((1, D)),
            _const_spec((D, 3 * D)),
            _const_spec((CONV_W, D)),
            _const_spec((D, D)),
            _const_spec((CONV_W - 1, seqs, D)),
        ] + c_in,
        out_specs=[tok(tm), per_seq((CONV_W - 1, D)), _whole_spec((1, rows_s, D)),
                   _whole_spec((CONV_W - 1, seqs, D))] + c_out,
        out_shape=[jax.ShapeDtypeStruct(xp.shape, F32),
                   jax.ShapeDtypeStruct((nb, CONV_W - 1, D), F32),
                   jax.ShapeDtypeStruct(xs.shape, F32),
                   jax.ShapeDtypeStruct((CONV_W - 1, seqs, D), F32)] + c_shapes,
        scratch_shapes=[pltpu.VMEM((CONV_W - 1, D), F32)],
        compiler_params=_params("arbitrary"),
        name="conv",
    )(*args, *c_args)


def kernel(x_prompt, x_sample, c_prompt, c_sample, state_mlstm_C, state_mlstm_n, state_mlstm_m,
           state_conv, ada_w, ada_b, norm_g, final_norm_g, mlstm_w_in, mlstm_b_gates,
           mlstm_head_g, mlstm_w_out, conv_w_in, conv_w, conv_w_out, mlp_w_up, mlp_w_down):
    assert ada_w.shape[0] == 2 and mlstm_w_in.shape[0] == 1 and conv_w_in.shape[0] == 1
    bp, tp, _ = x_prompt.shape
    bs, ts, _ = x_sample.shape
    assert ts == 4 and tp % 512 == 0

    mod = _ada(jnp.concatenate([c_sample, c_prompt], axis=0), ada_w, ada_b)
    mod_p = jnp.transpose(mod[:, :, bs:], (0, 2, 1, 3))[:, :, :, None]

    w_in0 = mlstm_w_in[0]
    w_row = jnp.concatenate([w_in0[:, :HK], w_in0[:, 2 * HK:NP]], axis=1).astype(BF)
    gate_pad = jnp.zeros((D, LANES - H), F32)
    w_col = jnp.concatenate([w_in0[:, HK:2 * HK] * (DK ** -0.5), w_in0[:, NP:NP + H], gate_pad,
                             w_in0[:, NP + H:], gate_pad], axis=1).T.astype(BF)
    b_g = jnp.pad(mlstm_b_gates[0].reshape(2, H), ((0, 0), (0, LANES - H)))
    wout0 = mlstm_w_out[0].astype(BF)

    x, s_p, m_p, up0, down0 = _mlstm_prompt(
        x_prompt, mod_p[0], norm_g[0, 0], w_row, w_col, mlstm_b_gates[0], mlstm_head_g[0], wout0,
        tm=1024, casts=((mlp_w_up, 0), (mlp_w_down, 0)))
    prompt_c = s_p[None, :, :, :, :DV]
    prompt_n = s_p[None, :, :, :, DV]
    prompt_m = m_p[None, :, :, 0]

    xs = jnp.transpose(x_sample, (1, 0, 2)).reshape(1, ts * bs, D)
    head_of_v = jnp.arange(D, dtype=jnp.int32) // DV
    head_of_k = jnp.arange(HK, dtype=jnp.int32) // DK
    lanes = jnp.arange(FG, dtype=jnp.int32)
    e128 = (lanes[:, None] == head_of_v[None, :]).astype(BF)
    e64 = (lanes[:, None] == head_of_k[None, :]).astype(BF)
    m0 = jnp.pad(state_mlstm_m[0], ((0, 0), (0, FG - H)))
    n0 = state_mlstm_n[0].reshape(bs, HK)
    q, ka, v, dec, numi, rd, eint, go, n1, m1 = _mlstm_sample_pre(
        xs, mod, 0, norm_g[0, 0], w_row, w_col, b_g, m0, n0, mlstm_head_g[0], e64.T, e64, e128,
        t=ts)
    inter, c1 = _mlstm_sample_state(q, ka, v, dec, state_mlstm_C[0], t=ts, bb=32)
    y = _mlstm_sample_post(inter, numi, rd, eint, go, e128, e128.T, t=ts)
    sample_n = n1.reshape(bs, H, DK)[None]
    sample_m = m1[None, :, :H]

    x, xs, cwin, cwout = _mlp(x, mod_p[0], xs, mod, 0, norm_g[0, 1], up0, down0, tm=1024, seqs=bs,
                              pre=(y.reshape(1, ts * bs, D), wout0),
                              casts=((conv_w_in, 0), (conv_w_out, 0)))
    x, conv_p, xs, conv_s, up1, down1 = _conv(
        x, mod_p[1], xs, mod, 1, norm_g[1, 0], cwin, conv_w[0], cwout,
        jnp.transpose(state_conv[0], (1, 0, 2)), tm=1024, t=ts,
        casts=((mlp_w_up, 1), (mlp_w_down, 1)))
    y_prompt, ys = _mlp(x, mod_p[1], xs, mod, 1, norm_g[1, 1], up1, down1, tm=1024, seqs=bs,
                        final_g=final_norm_g)
    prompt_conv = conv_p[None]
    y_sample = jnp.transpose(ys.reshape(ts, bs, D), (1, 0, 2))
    sample_conv = jnp.transpose(conv_s, (1, 0, 2))[None]

    return (y_prompt, y_sample, prompt_c, prompt_n, prompt_m, prompt_conv,
            c1[None], sample_n, sample_m, sample_conv)
```

```python
import functools

import jax
import jax.numpy as jnp
from jax import lax
from jax.experimental import pallas as pl
from jax.experimental.pallas import tpu as pltpu

D = 1024
H = 8
DK = 64
DV = 128
HK = H * DK
NP = 2 * HK + 2 * D
FG = 128
LANES = 128
SUBLANES = 8
DFF = 4 * D
N_MOD = 6
CONV_W = 3
EPS = 1e-6
CHUNK = 128
FF_CHUNK = 1024
BF = jnp.bfloat16
F32 = jnp.float32
VMEM_LIMIT_BYTES = 56 * 1024 * 1024


def _dot(a, b):
    return jnp.dot(a, b, preferred_element_type=F32)


def _dot_nt(a, b):
    return lax.dot_general(a, b, (((1,), (1,)), ((), ())), preferred_element_type=F32)


def _dot_tn(a, b):
    return lax.dot_general(a, b, (((0,), (0,)), ((), ())), preferred_element_type=F32)


def _rms(x):
    return x * lax.rsqrt(jnp.mean(x * x, axis=-1, keepdims=True) + EPS)


def _modulate(x, g, shift, scale):
    return _rms(x) * g * (1.0 + scale) + shift


def _log_sigmoid(x):
    return jnp.minimum(x, 0.0) - jnp.log1p(jnp.exp(-jnp.abs(x)))


def _scan_lanes(x, op, ident, n):
    lane = lax.broadcasted_iota(jnp.int32, x.shape, 1) % n
    d = 1
    while d < n:
        x = op(x, jnp.where(lane >= d, pltpu.roll(x, d, 1), ident))
        d *= 2
    return x


def _by_group(fn, xs, mods):
    rows, r = xs[0].shape[0], mods[0].shape[0]
    if r == 1 or r == rows:
        return fn(*xs, *mods)
    return jnp.concatenate(
        [fn(*[x[i:i + r] for x in xs], *mods) for i in range(0, rows, r)], axis=0)


def _split_dot(x, m_ref):
    hi = x.astype(BF)
    lo = (x - hi.astype(F32)).astype(BF)
    m = m_ref[...]
    return _dot(hi, m) + _dot(lo, m)


def _cast_specs(jobs, steps, slab_of):
    ins, outs, shapes, args = [], [], [], []
    for w, layer in jobs:
        _, r, c = w.shape
        slab = r // steps
        assert slab * steps == r and slab % (2 * SUBLANES) == 0
        ins.append(pl.BlockSpec((1, slab, c), lambda *g, layer=layer: (layer, slab_of(*g), 0)))
        outs.append(pl.BlockSpec((slab, c), lambda *g: (slab_of(*g), 0)))
        shapes.append(jax.ShapeDtypeStruct((r, c), BF))
        args.append(w)
    return ins, outs, shapes, args


def _carry_casts(body, n_in, n_out, n_cast):
    def kernel(*refs):
        a, b = n_in, n_in + n_cast
        c, d = b + n_out, b + n_out + n_cast
        for src, dst in zip(refs[a:b], refs[c:d]):
            dst[...] = src[0].astype(BF)
        body(*refs[:a], *refs[b:c], *refs[d:])
    return kernel


def _const_spec(shape):
    nd = len(shape)
    return pl.BlockSpec(shape, lambda *_: (0,) * nd, pipeline_mode=pl.Buffered(1))


def _params(*sem):
    return pltpu.CompilerParams(dimension_semantics=sem, vmem_limit_bytes=VMEM_LIMIT_BYTES)


def _ada_kernel(c_ref, w_ref, b_ref, o_ref):
    c = c_ref[...]
    sc = (c * jax.nn.sigmoid(c)).astype(BF)
    o_ref[0, 0] = _dot(sc, w_ref[0].astype(BF)) + b_ref[0]


def _ada(c_all, ada_w, ada_b):
    depth, _, n = ada_w.shape
    nb = c_all.shape[0]
    return pl.pallas_call(
        _ada_kernel,
        grid=(depth, n // D),
        in_specs=[
            pl.BlockSpec((nb, D), lambda i, j: (0, 0)),
            pl.BlockSpec((1, D, D), lambda i, j: (i, 0, j)),
            pl.BlockSpec((1, 1, D), lambda i, j: (i, 0, j)),
        ],
        out_specs=pl.BlockSpec((1, 1, nb, D), lambda i, j: (i, j, 0, 0)),
        out_shape=jax.ShapeDtypeStruct((depth, n // D, nb, D), F32),
        compiler_params=_params("arbitrary", "arbitrary"),
        name="ada",
    )(c_all, ada_w, ada_b.reshape(depth, 1, n))


def _mlp_tile(x_ref, mod_ref, g_ref, wu_ref, wd_ref, fg_ref, o_ref, y_ref=None, wo_ref=None):
    def residual(xr, fr, gate):
        return xr + gate * fr

    def norm(xr, shift, scale):
        return _modulate(xr, g_ref[...], shift, scale)

    x = x_ref[0]
    if y_ref is not None:
        x = _by_group(residual, (x, _dot(y_ref[0].astype(BF), wo_ref[...])), (mod_ref[0, 2],))
    hb = _by_group(norm, (x,), (mod_ref[0, 3], mod_ref[0, 4])).astype(BF)
    acc = jnp.zeros(x.shape, F32)
    for c in range(DFF // FF_CHUNK):
        cs = slice(c * FF_CHUNK, (c + 1) * FF_CHUNK)
        hid = jnp.maximum(_dot(hb, wu_ref[:, cs]), 0.0)
        acc = acc + _dot((hid * hid).astype(BF), wd_ref[cs, :])
    x = _by_group(residual, (x, acc), (mod_ref[0, 5],))
    if fg_ref is not None:
        x = _rms(x) * fg_ref[...]
    o_ref[0] = x


def _mlp_kernel(*refs, n_prompt, pre, final):
    refs = list(refs)
    xp_ref, modp_ref, xs_ref = refs[:3]
    refs = refs[3:]
    ys_ref = wo_ref = fg_ref = None
    if pre:
        ys_ref, wo_ref = refs[:2]
        refs = refs[2:]
    mods_ref, g_ref, wu_ref, wd_ref = refs[:4]
    refs = refs[4:]
    if final:
        fg_ref = refs.pop(0)
    xpo_ref, xso_ref = refs
    step = pl.program_id(0)

    @pl.when(step < n_prompt)
    def _():
        _mlp_tile(xp_ref, modp_ref, g_ref, wu_ref, wd_ref, fg_ref, xpo_ref)

    @pl.when(step == n_prompt)
    def _():
        _mlp_tile(xs_ref, mods_ref, g_ref, wu_ref, wd_ref, fg_ref, xso_ref, ys_ref, wo_ref)


def _group_mod_spec(layer, r):
    return pl.BlockSpec((1, N_MOD, r, D), lambda *_: (layer, 0, 0, 0),
                        pipeline_mode=pl.Buffered(1))


def _prompt_tiles(nb, nt):
    def seq_tile(step):
        c = jnp.minimum(step, nb * nt - 1)
        return c // nt, c % nt

    def tok(tm):
        return pl.BlockSpec((1, tm, D), lambda s: (*seq_tile(s), 0))

    def per_seq(shape):
        return pl.BlockSpec((1,) + shape, lambda s: (seq_tile(s)[0],) + (0,) * len(shape))

    return tok, per_seq


def _whole_spec(shape):
    return pl.BlockSpec(shape, lambda *_: (0,) * len(shape))


def _mlp(xp, mod_p, xs, mod, layer, g, w_up, w_down, *, tm, seqs, pre=None, final_g=None,
         casts=()):
    nb, t, _ = xp.shape
    nt = t // tm
    n_prompt = nb * nt
    rows_s = xs.shape[1]
    tok, per_seq = _prompt_tiles(nb, nt)
    args = [xp, mod_p, xs]
    specs = [tok(tm), per_seq((N_MOD, 1, D)), _const_spec((1, rows_s, D))]
    if pre is not None:
        args += list(pre)
        specs += [_const_spec((1, rows_s, D)), _const_spec((D, D))]
    args += [mod, g.reshape(1, D), w_up, w_down]
    specs += [_group_mod_spec(layer, seqs), _const_spec((1, D)), _const_spec((D, DFF)),
              _const_spec((DFF, D))]
    if final_g is not None:
        args.append(final_g.reshape(1, D))
        specs.append(_const_spec((1, D)))
    c_in, c_out, c_shapes, c_args = _cast_specs(
        casts, n_prompt, lambda s: jnp.minimum(s, n_prompt - 1))
    body = functools.partial(_mlp_kernel, n_prompt=n_prompt, pre=pre is not None,
                             final=final_g is not None)
    return pl.pallas_call(
        _carry_casts(body, len(args), 2, len(casts)),
        grid=(n_prompt + 1,),
        in_specs=specs + c_in,
        out_specs=[tok(tm), _whole_spec((1, rows_s, D))] + c_out,
        out_shape=[jax.ShapeDtypeStruct(xp.shape, F32), jax.ShapeDtypeStruct(xs.shape, F32)]
        + c_shapes,
        compiler_params=_params("arbitrary"),
        name="mlp",
    )(*args, *c_args)


def _mlstm_prompt_kernel(x_ref, mod_ref, g_ref, wrow_ref, wcol_ref, bg_ref, gh_ref, wout_ref,
                         xo_ref, s_ref, m_ref, y_scr, *, tm):
    @pl.when(pl.program_id(1) == 0)
    def _():
        s_ref[...] = jnp.zeros(s_ref.shape, F32)
        m_ref[...] = jnp.zeros(m_ref.shape, F32)

    x = x_ref[0]
    hb = _modulate(x, g_ref[...], mod_ref[0, 0], mod_ref[0, 1]).astype(BF)
    n = CHUNK
    w_t = jnp.concatenate([wcol_ref[0:HK, :], wcol_ref[HK:HK + H, :],
                           wcol_ref[HK + LANES:HK + LANES + H, :]], axis=0)
    pcol = _dot_nt(w_t, hb)
    ig_all = pcol[HK:HK + H] + bg_ref[0:H]
    fg_all = pcol[HK + H:HK + 2 * H] + bg_ref[H:2 * H]
    b_all = _scan_lanes(_log_sigmoid(fg_all), jnp.add, 0.0, n)
    r_all = ig_all - b_all
    cm_all = _scan_lanes(r_all, jnp.maximum, -jnp.inf, n)
    prow = _dot(hb, wrow_ref[...])
    causal = (lax.broadcasted_iota(jnp.int32, (n, n), 0)
              >= lax.broadcasted_iota(jnp.int32, (n, n), 1))
    zeros_k = jnp.zeros((DK, n), BF)
    zeros_s = jnp.zeros((DK, 2 * DV), BF)
    zeros_v = jnp.zeros((n, DV), BF)
    ones_v = jnp.ones((n, DV), BF)
    pad = jnp.zeros((n - 2 * H, n), F32)

    for c in range(tm // n):
        ts = slice(c * n, (c + 1) * n)
        r = r_all[:, ts]
        m0 = m_ref[0]
        mx = jnp.maximum(m0, cm_all[:, ts])
        m = b_all[:, ts] + mx
        e_neg = jnp.exp(-m)
        mx_last = jnp.broadcast_to(mx[:, n - 1:n], (H, n))
        a = jnp.exp(r - mx_last)
        decay = jnp.exp(m0 - mx_last)
        m_ref[0] = jnp.broadcast_to(m[:, n - 1:n], (H, n))
        cols = jnp.concatenate([mx, e_neg, pad], axis=0).T

        for h in range(H):
            lo = h % 2 == 0
            mx_b = jnp.broadcast_to(cols[:, h:h + 1], (n, n))
            floor_b = jnp.broadcast_to(cols[:, H + h:H + h + 1], (n, DV))
            dec = jnp.exp(jnp.where(causal, r[h:h + 1, :] - mx_b, -jnp.inf))
            e_int = jnp.exp(m0[h:h + 1, :] - mx_b)
            qp = prow[ts, (h // 2) * 2 * DK:(h // 2 + 1) * 2 * DK]
            kt = pcol[h * DK:(h + 1) * DK, ts]
            ktb = kt.astype(BF)
            s = _dot(qp.astype(BF),
                     jnp.concatenate([ktb, zeros_k] if lo else [zeros_k, ktb], axis=0)) * dec
            st = s_ref[0, h]
            stb = st.astype(BF)
            v = prow[ts, HK + h * DV:HK + (h + 1) * DV].astype(BF)
            rhs = jnp.concatenate(
                [jnp.concatenate([v, zeros_v], axis=1)]
                + ([stb, zeros_s] if lo else [zeros_s, stb]), axis=0)
            lhs = jnp.concatenate([s.astype(BF), (qp * e_int).astype(BF)], axis=1)
            res = _dot(lhs, rhs)
            den = jnp.sum(s, axis=-1, keepdims=True) + res[:, DV:]
            hh = res[:, :DV] / jnp.maximum(jnp.abs(den), floor_b)
            hh = hh * lax.rsqrt(jnp.mean(hh * hh, axis=-1, keepdims=True) + EPS)
            y_scr[ts, h * DV:(h + 1) * DV] = hh
            ka = (kt * a[h:h + 1, :]).astype(BF)
            dc = decay[h:h + 1, :]
            s_ref[0, h] = (jnp.concatenate([dc, dc], axis=1) * st
                           + _dot(ka, jnp.concatenate([v, ones_v], axis=1)))

    o = prow[:, HK + D:]
    y = (y_scr[...] * gh_ref[...] * jax.nn.sigmoid(o)).astype(BF)
    xo_ref[0] = x + mod_ref[0, 2] * _dot(y, wout_ref[...])


def _mlstm_prompt(x, mod, g, w_row, w_col, b_g, g_head, w_out, *, tm, casts=()):
    nb, t, _ = x.shape
    nt = t // tm
    tok = pl.BlockSpec((1, tm, D), lambda b, i: (b, i, 0))
    args = (x, mod, g.reshape(1, D), w_row, w_col,
            jnp.broadcast_to(b_g.reshape(2 * H, 1), (2 * H, tm)), g_head.reshape(1, D), w_out)
    c_in, c_out, c_shapes, c_args = _cast_specs(casts, nb * nt, lambda b, i: b * nt + i)
    return pl.pallas_call(
        _carry_casts(functools.partial(_mlstm_prompt_kernel, tm=tm), len(args), 3, len(casts)),
        grid=(nb, nt),
        in_specs=[
            tok,
            pl.BlockSpec((1, N_MOD, 1, D), lambda b, i: (b, 0, 0, 0)),
            _const_spec((1, D)),
            _const_spec((D, HK + 2 * D)),
            _const_spec((HK + 2 * LANES, D)),
            _const_spec((2 * H, tm)),
            _const_spec((1, D)),
            _const_spec((D, D)),
        ] + c_in,
        out_specs=[
            tok,
            pl.BlockSpec((1, H, DK, 2 * DV), lambda b, i: (b, 0, 0, 0)),
            pl.BlockSpec((1, H, CHUNK), lambda b, i: (b, 0, 0)),
        ] + c_out,
        out_shape=[
            jax.ShapeDtypeStruct(x.shape, F32),
            jax.ShapeDtypeStruct((nb, H, DK, 2 * DV), F32),
            jax.ShapeDtypeStruct((nb, H, CHUNK), F32),
        ] + c_shapes,
        scratch_shapes=[pltpu.VMEM((tm, D), F32)],
        compiler_params=_params("arbitrary", "arbitrary"),
        name="mlstm_prompt",
    )(*args, *c_args)


def _mlstm_sample_pre_kernel(x_ref, mod_ref, g_ref, wrow_ref, wcol_ref, bg_ref, m0_ref, n0_ref,
                             gh_ref, seg_ref, e64_ref, e128_ref,
                             q_ref, ka_ref, v_ref, dec_ref, numi_ref, rd_ref, eint_ref, go_ref,
                             n1_ref, m1_ref, *, t, nb):
    def norm(xr, shift, scale):
        return _modulate(xr, g_ref[...], shift, scale)

    hb = _by_group(norm, (x_ref[0],), (mod_ref[0, 0], mod_ref[0, 1])).astype(BF)
    p = _dot(hb, wrow_ref[...])
    k_all = _dot_nt(hb, wcol_ref[0:HK, :])
    ig_all = _dot_nt(hb, wcol_ref[HK:HK + LANES, :]) + bg_ref[0:1]
    fg_all = _dot_nt(hb, wcol_ref[HK + LANES:HK + 2 * LANES, :]) + bg_ref[1:2]

    def rows(i):
        return slice(i * nb, (i + 1) * nb)

    q = [p[rows(i), 0:HK] for i in range(t)]
    k = [k_all[rows(i)] for i in range(t)]
    v = [p[rows(i), HK:HK + D] for i in range(t)]
    m0 = m0_ref[...]
    n0 = n0_ref[...]

    b, r, mx = [], [], []
    for i in range(t):
        lf = _log_sigmoid(fg_all[rows(i)])
        b.append(lf if i == 0 else b[-1] + lf)
        r.append(ig_all[rows(i)] - b[i])
        cm = r[i] if i == 0 else jnp.maximum(cm, r[i])
        mx.append(jnp.maximum(m0, cm))
    mx_last = mx[t - 1]
    decay = jnp.exp(m0 - mx_last)
    m1_ref[...] = b[t - 1] + mx_last
    dec_x = _split_dot(decay, e128_ref)
    n1 = _split_dot(decay, e64_ref) * n0

    for i in range(t):
        ka = k[i] * _split_dot(jnp.exp(r[i] - mx_last), e64_ref)
        n1 = n1 + ka
        seq_rows = pl.ds(i, nb, stride=t)
        for c in range(HK // LANES):
            cs = slice(c * LANES, (c + 1) * LANES)
            ka_ref[c, seq_rows, :] = ka[:, cs]
            q_ref[c, seq_rows, :] = q[i][:, cs]
        for c in range(D // LANES):
            cs = slice(c * LANES, (c + 1) * LANES)
            v_ref[c, seq_rows, :] = v[i][:, cs]
            dec_ref[c, seq_rows, :] = dec_x[:, cs]
        e_int = jnp.exp(m0 - mx[i])
        den = e_int * _split_dot(q[i] * n0, seg_ref)
        numi = jnp.zeros((nb, D), F32)
        for s in range(i + 1):
            w = _split_dot(q[i] * k[s], seg_ref) * jnp.exp(jnp.minimum(r[s] - mx[i], 0.0))
            den = den + w
            numi = numi + _split_dot(w, e128_ref) * v[s]
        numi_ref[rows(i), :] = numi
        rd_ref[rows(i), :] = 1.0 / jnp.maximum(jnp.abs(den), jnp.exp(-(b[i] + mx[i])))
        eint_ref[rows(i), :] = e_int
        go_ref[rows(i), :] = jax.nn.sigmoid(p[rows(i), HK + D:]) * gh_ref[...]
    n1_ref[...] = n1


def _mlstm_sample_pre(x, mod, layer, g, w_row, w_col, b_g, m0, n0, g_head, seg64, e64, e128, *, t):
    rows = x.shape[1]
    nb = rows // t

    def whole(shape):
        return pl.BlockSpec(shape, lambda i: (0,) * len(shape))

    outs = [(HK // LANES, rows, LANES), (HK // LANES, rows, LANES), (D // LANES, rows, LANES),
            (D // LANES, rows, LANES), (rows, D), (rows, FG), (rows, FG), (rows, D), (nb, HK),
            (nb, FG)]
    return pl.pallas_call(
        functools.partial(_mlstm_sample_pre_kernel, t=t, nb=nb),
        grid=(1,),
        in_specs=[
            whole((1, rows, D)),
            _group_mod_spec(layer, nb),
            _const_spec((1, D)),
            _const_spec((D, HK + 2 * D)),
            _const_spec((HK + 2 * LANES, D)),
            _const_spec((2, LANES)),
            whole((nb, FG)),
            whole((nb, HK)),
            _const_spec((1, D)),
            _const_spec((HK, FG)),
            _const_spec((FG, HK)),
            _const_spec((FG, D)),
        ],
        out_specs=[whole(s) for s in outs],
        out_shape=[jax.ShapeDtypeStruct(s, F32) for s in outs],
        compiler_params=_params("arbitrary"),
        name="mlstm_sample_pre",
    )(x, mod, g.reshape(1, D), w_row, w_col, b_g, m0, n0, g_head.reshape(1, D), seg64, e64, e128)


def _mlstm_sample_state_kernel(q_ref, ka_ref, v_ref, dec_ref, c0_ref, inter_ref, c1_ref, *, bb, t):
    per = SUBLANES // t
    seq_of_row = lax.broadcasted_iota(jnp.int32, (SUBLANES, 1), 0) // t
    zeros_c = jnp.zeros((DK, DV), BF)

    def tile(i, carry):
        rs = pl.ds(pl.multiple_of(i * SUBLANES, SUBLANES), SUBLANES)
        for h in range(H):
            lo = h % 2 == 0
            half = slice((h % 2) * DK, (h % 2 + 1) * DK)
            qp = q_ref[h // 2, rs, :].astype(BF)
            kap = ka_ref[h // 2, rs, :]
            vh = v_ref[h, rs, :].astype(BF)
            dech = dec_ref[h, rs, :]
            res = None
            for w in range(per):
                seq = i * per + w
                c0 = c0_ref[seq, h]
                c0b = c0.astype(BF)
                rw = _dot(qp, jnp.concatenate([c0b, zeros_c] if lo else [zeros_c, c0b], axis=0))
                res = rw if res is None else jnp.where(seq_of_row == w, rw, res)
                kaw = jnp.where(seq_of_row == w, kap, 0.0).astype(BF)
                c1_ref[seq, h] = dech[w * t:w * t + 1, :] * c0 + _dot_tn(kaw, vh)[half, :]
            inter_ref[h, rs, :] = res
        return carry

    lax.fori_loop(0, bb // per, tile, 0, unroll=4)


def _mlstm_sample_state(q, ka, v, dec, c0, *, t, bb):
    rows = q.shape[1]
    nb = rows // t
    assert SUBLANES % t == 0 and bb % (SUBLANES // t) == 0 and nb % bb == 0

    def tok(blocks):
        return pl.BlockSpec((blocks, bb * t, LANES), lambda i: (0, i, 0))

    state = pl.BlockSpec((bb, H, DK, DV), lambda i: (i, 0, 0, 0))
    return pl.pallas_call(
        functools.partial(_mlstm_sample_state_kernel, bb=bb, t=t),
        grid=(nb // bb,),
        in_specs=[tok(HK // LANES), tok(HK // LANES), tok(H), tok(H), state],
        out_specs=[tok(H), state],
        out_shape=[jax.ShapeDtypeStruct((H, rows, LANES), F32),
                   jax.ShapeDtypeStruct(c0.shape, F32)],
        compiler_params=_params("arbitrary"),
        name="mlstm_sample_state",
    )(q, ka, v, dec, c0)


def _mlstm_sample_post_kernel(inter_ref, numi_ref, rd_ref, eint_ref, go_ref, e128_ref, seg_ref,
                              y_ref, *, t, nb):
    for i in range(t):
        rs = slice(i * nb, (i + 1) * nb)
        inter = jnp.concatenate(
            [inter_ref[h, pl.ds(i, nb, stride=t), :] for h in range(H)], axis=1)
        num = numi_ref[rs, :] + _split_dot(eint_ref[rs, :], e128_ref) * inter
        hh = num * _split_dot(rd_ref[rs, :], e128_ref)
        ms = _split_dot(hh * hh, seg_ref) * (1.0 / DV)
        y_ref[rs, :] = hh * _split_dot(lax.rsqrt(ms + EPS), e128_ref) * go_ref[rs, :]


def _mlstm_sample_post(inter, numi, rd, eint, go, e128, seg128, *, t):
    rows = inter.shape[1]

    def whole(shape):
        return pl.BlockSpec(shape, lambda i: (0,) * len(shape))

    return pl.pallas_call(
        functools.partial(_mlstm_sample_post_kernel, t=t, nb=rows // t),
        grid=(1,),
        in_specs=[whole((H, rows, LANES)), whole((rows, D)), whole((rows, FG)), whole((rows, FG)),
                  whole((rows, D)), _const_spec((FG, D)), _const_spec((D, FG))],
        out_specs=whole((rows, D)),
        out_shape=jax.ShapeDtypeStruct((rows, D), F32),
        compiler_params=_params("arbitrary"),
        name="mlstm_sample_post",
    )(inter, numi, rd, eint, go, e128, seg128)


def _conv_taps(u, prev2, prev1, wc_ref):
    return wc_ref[0:1] * prev2 + wc_ref[1:2] * prev1 + wc_ref[2:3] * u


def _conv_prompt_tile(x_ref, mod_ref, g_ref, win_ref, wc_ref, wout_ref, xo_ref, st_ref, carry_scr,
                      first):
    @pl.when(first)
    def _():
        carry_scr[...] = jnp.zeros(carry_scr.shape, F32)

    x = x_ref[0]
    tm = x.shape[0]
    hb = _modulate(x, g_ref[...], mod_ref[0, 0], mod_ref[0, 1]).astype(BF)
    u = _dot(hb, win_ref[:, D:2 * D]) * _dot(hb, win_ref[:, 2 * D:])
    row = lax.broadcasted_iota(jnp.int32, (tm, D), 0)
    c0 = carry_scr[0:1]
    c1 = carry_scr[1:2]
    prev1 = jnp.where(row == 0, c1, pltpu.roll(u, 1, 0))
    prev2 = jnp.where(row == 0, c0, jnp.where(row == 1, c1, pltpu.roll(u, 2, 0)))
    y = _conv_taps(u, prev2, prev1, wc_ref)
    bg = _dot(hb, win_ref[:, :D])
    xo_ref[0] = x + mod_ref[0, 2] * _dot((bg * y).astype(BF), wout_ref[...])
    carry_scr[...] = u[tm - 2:tm]
    st_ref[0] = u[tm - 2:tm]


def _conv_sample_tile(x_ref, mod_ref, g_ref, win_ref, wc_ref, wout_ref, buf_ref, xo_ref, st_ref,
                      *, t, nb):
    def norm(xr, shift, scale):
        return _modulate(xr, g_ref[...], shift, scale)

    def residual(xr, fr, gate):
        return xr + gate * fr

    x = x_ref[0]
    hb = _by_group(norm, (x,), (mod_ref[0, 0], mod_ref[0, 1])).astype(BF)
    u = _dot(hb, win_ref[:, D:2 * D]) * _dot(hb, win_ref[:, 2 * D:])
    full = [buf_ref[j] for j in range(CONV_W - 1)] + [u[i * nb:(i + 1) * nb] for i in range(t)]
    y = jnp.concatenate([_conv_taps(full[i + 2], full[i], full[i + 1], wc_ref) for i in range(t)],
                        axis=0)
    bg = _dot(hb, win_ref[:, :D])
    out = _dot((bg * y).astype(BF), wout_ref[...])
    xo_ref[0] = _by_group(residual, (x, out), (mod_ref[0, 2],))
    for j in range(CONV_W - 1):
        st_ref[j] = full[t + j]


def _conv_kernel(xp_ref, modp_ref, xs_ref, mods_ref, g_ref, win_ref, wc_ref, wout_ref, buf_ref,
                 xpo_ref, stp_ref, xso_ref, sts_ref, carry_scr, *, n_prompt, nt, t, nb):
    step = pl.program_id(0)

    @pl.when(step < n_prompt)
    def _():
        _conv_prompt_tile(xp_ref, modp_ref, g_ref, win_ref, wc_ref, wout_ref, xpo_ref, stp_ref,
                          carry_scr, step % nt == 0)

    @pl.when(step == n_prompt)
    def _():
        _conv_sample_tile(xs_ref, mods_ref, g_ref, win_ref, wc_ref, wout_ref, buf_ref, xso_ref,
                          sts_ref, t=t, nb=nb)


def _conv(xp, mod_p, xs, mod, layer, g, w_in, w_conv, w_out, buf, *, tm, t, casts=()):
    nb, tp, _ = xp.shape
    nt = tp // tm
    n_prompt = nb * nt
    rows_s = xs.shape[1]
    seqs = rows_s // t
    tok, per_seq = _prompt_tiles(nb, nt)
    args = (xp, mod_p, xs, mod, g.reshape(1, D), w_in, w_conv, w_out, buf)
    c_in, c_out, c_shapes, c_args = _cast_specs(
        casts, n_prompt, lambda s: jnp.minimum(s, n_prompt - 1))
    body = functools.partial(_conv_kernel, n_prompt=n_prompt, nt=nt, t=t, nb=seqs)
    return pl.pallas_call(
        _carry_casts(body, len(args), 4, len(casts)),
        grid=(n_prompt + 1,),
        in_specs=[
            tok(tm),
            per_seq((N_MOD, 1, D)),
            _const_spec((1, rows_s, D)),
            _group_mod_spec(layer, seqs),
            _const_spec((1, D)),
            _const_spec((D, 3 * D)),
            _const_spec((CONV_W, D)),
            _const_spec((D, D)),
            _const_spec((CONV_W - 1, seqs, D)),
        ] + c_in,
        out_specs=[tok(tm), per_seq((CONV_W - 1, D)), _whole_spec((1, rows_s, D)),
                   _whole_spec((CONV_W - 1, seqs, D))] + c_out,
        out_shape=[jax.ShapeDtypeStruct(xp.shape, F32),
                   jax.ShapeDtypeStruct((nb, CONV_W - 1, D), F32),
                   jax.ShapeDtypeStruct(xs.shape, F32),
                   jax.ShapeDtypeStruct((CONV_W - 1, seqs, D), F32)] + c_shapes,
        scratch_shapes=[pltpu.VMEM((CONV_W - 1, D), F32)],
        compiler_params=_params("arbitrary"),
        name="conv",
    )(*args, *c_args)


def kernel(x_prompt, x_sample, c_prompt, c_sample, state_mlstm_C, state_mlstm_n, state_mlstm_m,
           state_conv, ada_w, ada_b, norm_g, final_norm_g, mlstm_w_in, mlstm_b_gates,
           mlstm_head_g, mlstm_w_out, conv_w_in, conv_w, conv_w_out, mlp_w_up, mlp_w_down):
    assert ada_w.shape[0] == 2 and mlstm_w_in.shape[0] == 1 and conv_w_in.shape[0] == 1
    bp, tp, _ = x_prompt.shape
    bs, ts, _ = x_sample.shape
    assert ts == 4 and tp % 512 == 0

    mod = _ada(jnp.concatenate([c_sample, c_prompt], axis=0), ada_w, ada_b)
    mod_p = jnp.transpose(mod[:, :, bs:], (0, 2, 1, 3))[:, :, :, None]

    w_in0 = mlstm_w_in[0]
    w_row = jnp.concatenate([w_in0[:, :HK], w_in0[:, 2 * HK:NP]], axis=1).astype(BF)
    gate_pad = jnp.zeros((D, LANES - H), F32)
    w_col = jnp.concatenate([w_in0[:, HK:2 * HK] * (DK ** -0.5), w_in0[:, NP:NP + H], gate_pad,
                             w_in0[:, NP + H:], gate_pad], axis=1).T.astype(BF)
    b_g = jnp.pad(mlstm_b_gates[0].reshape(2, H), ((0, 0), (0, LANES - H)))
    wout0 = mlstm_w_out[0].astype(BF)

    x, s_p, m_p, up0, down0 = _mlstm_prompt(
        x_prompt, mod_p[0], norm_g[0, 0], w_row, w_col, mlstm_b_gates[0], mlstm_head_g[0], wout0,
        tm=1024, casts=((mlp_w_up, 0), (mlp_w_down, 0)))
    prompt_c = s_p[None, :, :, :, :DV]
    prompt_n = s_p[None, :, :, :, DV]
    prompt_m = m_p[None, :, :, 0]

    xs = jnp.transpose(x_sample, (1, 0, 2)).reshape(1, ts * bs, D)
    head_of_v = jnp.arange(D, dtype=jnp.int32) // DV
    head_of_k = jnp.arange(HK, dtype=jnp.int32) // DK
    lanes = jnp.arange(FG, dtype=jnp.int32)
    e128 = (lanes[:, None] == head_of_v[None, :]).astype(BF)
    e64 = (lanes[:, None] == head_of_k[None, :]).astype(BF)
    m0 = jnp.pad(state_mlstm_m[0], ((0, 0), (0, FG - H)))
    n0 = state_mlstm_n[0].reshape(bs, HK)
    q, ka, v, dec, numi, rd, eint, go, n1, m1 = _mlstm_sample_pre(
        xs, mod, 0, norm_g[0, 0], w_row, w_col, b_g, m0, n0, mlstm_head_g[0], e64.T, e64, e128,
        t=ts)
    inter, c1 = _mlstm_sample_state(q, ka, v, dec, state_mlstm_C[0], t=ts, bb=32)
    y = _mlstm_sample_post(inter, numi, rd, eint, go, e128, e128.T, t=ts)
    sample_n = n1.reshape(bs, H, DK)[None]
    sample_m = m1[None, :, :H]

    x, xs, cwin, cwout = _mlp(x, mod_p[0], xs, mod, 0, norm_g[0, 1], up0, down0, tm=512, seqs=bs,
                              pre=(y.reshape(1, ts * bs, D), wout0),
                              casts=((conv_w_in, 0), (conv_w_out, 0)))
    x, conv_p, xs, conv_s, up1, down1 = _conv(
        x, mod_p[1], xs, mod, 1, norm_g[1, 0], cwin, conv_w[0], cwout,
        jnp.transpose(state_conv[0], (1, 0, 2)), tm=1024, t=ts,
        casts=((mlp_w_up, 1), (mlp_w_down, 1)))
    y_prompt, ys = _mlp(x, mod_p[1], xs, mod, 1, norm_g[1, 1], up1, down1, tm=1024, seqs=bs,
                        final_g=final_norm_g)
    prompt_conv = conv_p[None]
    y_sample = jnp.transpose(ys.reshape(ts, bs, D), (1, 0, 2))
    sample_conv = jnp.transpose(conv_s, (1, 0, 2))[None]

    return (y_prompt, y_sample, prompt_c, prompt_n, prompt_m, prompt_conv,
            c1[None], sample_n, sample_m, sample_conv)
```

```python
import functools

import jax
import jax.numpy as jnp
from jax import lax
from jax.experimental import pallas as pl
from jax.experimental.pallas import tpu as pltpu

D = 1024
H = 8
DK = 64
DV = 128
HK = H * DK
NP = 2 * HK + 2 * D
FG = 128
LANES = 128
SUBLANES = 8
DFF = 4 * D
N_MOD = 6
CONV_W = 3
EPS = 1e-6
CHUNK = 128
FF_CHUNK = 1024
BF = jnp.bfloat16
F32 = jnp.float32
VMEM_LIMIT_BYTES = 56 * 1024 * 1024


def _dot(a, b):
    return jnp.dot(a, b, preferred_element_type=F32)


def _dot_nt(a, b):
    return lax.dot_general(a, b, (((1,), (1,)), ((), ())), preferred_element_type=F32)


def _dot_tn(a, b):
    return lax.dot_general(a, b, (((0,), (0,)), ((), ())), preferred_element_type=F32)


def _rms(x):
    return x * lax.rsqrt(jnp.mean(x * x, axis=-1, keepdims=True) + EPS)


def _modulate(x, g, shift, scale):
    return _rms(x) * g * (1.0 + scale) + shift


def _log_sigmoid(x):
    return jnp.minimum(x, 0.0) - jnp.log1p(jnp.exp(-jnp.abs(x)))


def _scan_lanes(x, op, ident, n):
    lane = lax.broadcasted_iota(jnp.int32, x.shape, 1) % n
    d = 1
    while d < n:
        x = op(x, jnp.where(lane >= d, pltpu.roll(x, d, 1), ident))
        d *= 2
    return x


def _by_group(fn, xs, mods):
    rows, r = xs[0].shape[0], mods[0].shape[0]
    if r == 1 or r == rows:
        return fn(*xs, *mods)
    return jnp.concatenate(
        [fn(*[x[i:i + r] for x in xs], *mods) for i in range(0, rows, r)], axis=0)


def _split_dot(x, m_ref):
    hi = x.astype(BF)
    lo = (x - hi.astype(F32)).astype(BF)
    m = m_ref[...]
    return _dot(hi, m) + _dot(lo, m)


def _cast_specs(jobs, steps, slab_of):
    ins, outs, shapes, args = [], [], [], []
    for w, layer in jobs:
        _, r, c = w.shape
        slab = r // steps
        assert slab * steps == r and slab % (2 * SUBLANES) == 0
        ins.append(pl.BlockSpec((1, slab, c), lambda *g, layer=layer: (layer, slab_of(*g), 0)))
        outs.append(pl.BlockSpec((slab, c), lambda *g: (slab_of(*g), 0)))
        shapes.append(jax.ShapeDtypeStruct((r, c), BF))
        args.append(w)
    return ins, outs, shapes, args


def _carry_casts(body, n_in, n_out, n_cast):
    def kernel(*refs):
        a, b = n_in, n_in + n_cast
        c, d = b + n_out, b + n_out + n_cast
        for src, dst in zip(refs[a:b], refs[c:d]):
            dst[...] = src[0].astype(BF)
        body(*refs[:a], *refs[b:c], *refs[d:])
    return kernel


def _const_spec(shape):
    nd = len(shape)
    return pl.BlockSpec(shape, lambda *_: (0,) * nd, pipeline_mode=pl.Buffered(1))


def _params(*sem):
    return pltpu.CompilerParams(dimension_semantics=sem, vmem_limit_bytes=VMEM_LIMIT_BYTES)


def _ada_kernel(c_ref, w_ref, b_ref, o_ref):
    c = c_ref[...]
    sc = (c * jax.nn.sigmoid(c)).astype(BF)
    o_ref[0, 0] = _dot(sc, w_ref[0].astype(BF)) + b_ref[0]


def _ada(c_all, ada_w, ada_b):
    depth, _, n = ada_w.shape
    nb = c_all.shape[0]
    return pl.pallas_call(
        _ada_kernel,
        grid=(depth, n // D),
        in_specs=[
            pl.BlockSpec((nb, D), lambda i, j: (0, 0)),
            pl.BlockSpec((1, D, D), lambda i, j: (i, 0, j)),
            pl.BlockSpec((1, 1, D), lambda i, j: (i, 0, j)),
        ],
        out_specs=pl.BlockSpec((1, 1, nb, D), lambda i, j: (i, j, 0, 0)),
        out_shape=jax.ShapeDtypeStruct((depth, n // D, nb, D), F32),
        compiler_params=_params("arbitrary", "arbitrary"),
        name="ada",
    )(c_all, ada_w, ada_b.reshape(depth, 1, n))


def _mlp_tile(x_ref, mod_ref, g_ref, wu_ref, wd_ref, fg_ref, o_ref, y_ref=None, wo_ref=None):
    def residual(xr, fr, gate):
        return xr + gate * fr

    def norm(xr, shift, scale):
        return _modulate(xr, g_ref[...], shift, scale)

    x = x_ref[0]
    if y_ref is not None:
        x = _by_group(residual, (x, _dot(y_ref[0].astype(BF), wo_ref[...])), (mod_ref[0, 2],))
    hb = _by_group(norm, (x,), (mod_ref[0, 3], mod_ref[0, 4])).astype(BF)
    acc = jnp.zeros(x.shape, F32)
    for c in range(DFF // FF_CHUNK):
        cs = slice(c * FF_CHUNK, (c + 1) * FF_CHUNK)
        hid = jnp.maximum(_dot(hb, wu_ref[:, cs]), 0.0)
        acc = acc + _dot((hid * hid).astype(BF), wd_ref[cs, :])
    x = _by_group(residual, (x, acc), (mod_ref[0, 5],))
    if fg_ref is not None:
        x = _rms(x) * fg_ref[...]
    o_ref[0] = x


def _mlp_kernel(*refs, n_prompt, pre, final):
    refs = list(refs)
    xp_ref, modp_ref, xs_ref = refs[:3]
    refs = refs[3:]
    ys_ref = wo_ref = fg_ref = None
    if pre:
        ys_ref, wo_ref = refs[:2]
        refs = refs[2:]
    mods_ref, g_ref, wu_ref, wd_ref = refs[:4]
    refs = refs[4:]
    if final:
        fg_ref = refs.pop(0)
    xpo_ref, xso_ref = refs
    step = pl.program_id(0)

    @pl.when(step < n_prompt)
    def _():
        _mlp_tile(xp_ref, modp_ref, g_ref, wu_ref, wd_ref, fg_ref, xpo_ref)

    @pl.when(step == n_prompt)
    def _():
        _mlp_tile(xs_ref, mods_ref, g_ref, wu_ref, wd_ref, fg_ref, xso_ref, ys_ref, wo_ref)


def _group_mod_spec(layer, r):
    return pl.BlockSpec((1, N_MOD, r, D), lambda *_: (layer, 0, 0, 0),
                        pipeline_mode=pl.Buffered(1))


def _prompt_tiles(nb, nt):
    def seq_tile(step):
        c = jnp.minimum(step, nb * nt - 1)
        return c // nt, c % nt

    def tok(tm):
        return pl.BlockSpec((1, tm, D), lambda s: (*seq_tile(s), 0))

    def per_seq(shape):
        return pl.BlockSpec((1,) + shape, lambda s: (seq_tile(s)[0],) + (0,) * len(shape))

    return tok, per_seq


def _whole_spec(shape):
    return pl.BlockSpec(shape, lambda *_: (0,) * len(shape))


def _mlp(xp, mod_p, xs, mod, layer, g, w_up, w_down, *, tm, seqs, pre=None, final_g=None,
         casts=()):
    nb, t, _ = xp.shape
    nt = t // tm
    n_prompt = nb * nt
    rows_s = xs.shape[1]
    tok, per_seq = _prompt_tiles(nb, nt)
    args = [xp, mod_p, xs]
    specs = [tok(tm), per_seq((N_MOD, 1, D)), _const_spec((1, rows_s, D))]
    if pre is not None:
        args += list(pre)
        specs += [_const_spec((1, rows_s, D)), _const_spec((D, D))]
    args += [mod, g.reshape(1, D), w_up, w_down]
    specs += [_group_mod_spec(layer, seqs), _const_spec((1, D)), _const_spec((D, DFF)),
              _const_spec((DFF, D))]
    if final_g is not None:
        args.append(final_g.reshape(1, D))
        specs.append(_const_spec((1, D)))
    c_in, c_out, c_shapes, c_args = _cast_specs(
        casts, n_prompt, lambda s: jnp.minimum(s, n_prompt - 1))
    body = functools.partial(_mlp_kernel, n_prompt=n_prompt, pre=pre is not None,
                             final=final_g is not None)
    return pl.pallas_call(
        _carry_casts(body, len(args), 2, len(casts)),
        grid=(n_prompt + 1,),
        in_specs=specs + c_in,
        out_specs=[tok(tm), _whole_spec((1, rows_s, D))] + c_out,
        out_shape=[jax.ShapeDtypeStruct(xp.shape, F32), jax.ShapeDtypeStruct(xs.shape, F32)]
        + c_shapes,
        compiler_params=_params("arbitrary"),
        name="mlp",
    )(*args, *c_args)


def _mlstm_prompt_kernel(x_ref, mod_ref, g_ref, wrow_ref, wcol_ref, bg_ref, gh_ref, wout_ref,
                         xo_ref, s_ref, m_ref, y_scr, *, tm):
    @pl.when(pl.program_id(1) == 0)
    def _():
        s_ref[...] = jnp.zeros(s_ref.shape, F32)
        m_ref[...] = jnp.zeros(m_ref.shape, F32)

    x = x_ref[0]
    hb = _modulate(x, g_ref[...], mod_ref[0, 0], mod_ref[0, 1]).astype(BF)
    n = CHUNK
    w_t = jnp.concatenate([wcol_ref[0:HK, :], wcol_ref[HK:HK + H, :],
                           wcol_ref[HK + LANES:HK + LANES + H, :]], axis=0)
    pcol = _dot_nt(w_t, hb)
    ig_all = pcol[HK:HK + H] + bg_ref[0:H]
    fg_all = pcol[HK + H:HK + 2 * H] + bg_ref[H:2 * H]
    b_all = _scan_lanes(_log_sigmoid(fg_all), jnp.add, 0.0, n)
    r_all = ig_all - b_all
    cm_all = _scan_lanes(r_all, jnp.maximum, -jnp.inf, n)
    prow = _dot(hb, wrow_ref[...])
    causal = (lax.broadcasted_iota(jnp.int32, (n, n), 0)
              >= lax.broadcasted_iota(jnp.int32, (n, n), 1))
    zeros_k = jnp.zeros((DK, n), BF)
    zeros_s = jnp.zeros((DK, 2 * DV), BF)
    zeros_v = jnp.zeros((n, DV), BF)
    ones_v = jnp.ones((n, DV), BF)
    pad = jnp.zeros((n - 2 * H, n), F32)

    for c in range(tm // n):
        ts = slice(c * n, (c + 1) * n)
        r = r_all[:, ts]
        m0 = m_ref[0]
        mx = jnp.maximum(m0, cm_all[:, ts])
        m = b_all[:, ts] + mx
        e_neg = jnp.exp(-m)
        mx_last = jnp.broadcast_to(mx[:, n - 1:n], (H, n))
        a = jnp.exp(r - mx_last)
        decay = jnp.exp(m0 - mx_last)
        m_ref[0] = jnp.broadcast_to(m[:, n - 1:n], (H, n))
        cols = jnp.concatenate([mx, e_neg, pad], axis=0).T

        for h in range(H):
            lo = h % 2 == 0
            mx_b = jnp.broadcast_to(cols[:, h:h + 1], (n, n))
            floor_b = jnp.broadcast_to(cols[:, H + h:H + h + 1], (n, DV))
            dec = jnp.exp(jnp.where(causal, r[h:h + 1, :] - mx_b, -jnp.inf))
            e_int = jnp.exp(m0[h:h + 1, :] - mx_b)
            qp = prow[ts, (h // 2) * 2 * DK:(h // 2 + 1) * 2 * DK]
            kt = pcol[h * DK:(h + 1) * DK, ts]
            ktb = kt.astype(BF)
            s = _dot(qp.astype(BF),
                     jnp.concatenate([ktb, zeros_k] if lo else [zeros_k, ktb], axis=0)) * dec
            st = s_ref[0, h]
            stb = st.astype(BF)
            v = prow[ts, HK + h * DV:HK + (h + 1) * DV].astype(BF)
            rhs = jnp.concatenate(
                [jnp.concatenate([v, zeros_v], axis=1)]
                + ([stb, zeros_s] if lo else [zeros_s, stb]), axis=0)
            lhs = jnp.concatenate([s.astype(BF), (qp * e_int).astype(BF)], axis=1)
            res = _dot(lhs, rhs)
            den = jnp.sum(s, axis=-1, keepdims=True) + res[:, DV:]
            hh = res[:, :DV] / jnp.maximum(jnp.abs(den), floor_b)
            hh = hh * lax.rsqrt(jnp.mean(hh * hh, axis=-1, keepdims=True) + EPS)
            y_scr[ts, h * DV:(h + 1) * DV] = hh
            ka = (kt * a[h:h + 1, :]).astype(BF)
            dc = decay[h:h + 1, :]
            s_ref[0, h] = (jnp.concatenate([dc, dc], axis=1) * st
                           + _dot(ka, jnp.concatenate([v, ones_v], axis=1)))

    o = prow[:, HK + D:]
    y = (y_scr[...] * gh_ref[...] * jax.nn.sigmoid(o)).astype(BF)
    xo_ref[0] = x + mod_ref[0, 2] * _dot(y, wout_ref[...])


def _mlstm_prompt(x, mod, g, w_row, w_col, b_g, g_head, w_out, *, tm, casts=()):
    nb, t, _ = x.shape
    nt = t // tm
    tok = pl.BlockSpec((1, tm, D), lambda b, i: (b, i, 0))
    args = (x, mod, g.reshape(1, D), w_row, w_col,
            jnp.broadcast_to(b_g.reshape(2 * H, 1), (2 * H, tm)), g_head.reshape(1, D), w_out)
    c_in, c_out, c_shapes, c_args = _cast_specs(casts, nb * nt, lambda b, i: b * nt + i)
    return pl.pallas_call(
        _carry_casts(functools.partial(_mlstm_prompt_kernel, tm=tm), len(args), 3, len(casts)),
        grid=(nb, nt),
        in_specs=[
            tok,
            pl.BlockSpec((1, N_MOD, 1, D), lambda b, i: (b, 0, 0, 0)),
            _const_spec((1, D)),
            _const_spec((D, HK + 2 * D)),
            _const_spec((HK + 2 * LANES, D)),
            _const_spec((2 * H, tm)),
            _const_spec((1, D)),
            _const_spec((D, D)),
        ] + c_in,
        out_specs=[
            tok,
            pl.BlockSpec((1, H, DK, 2 * DV), lambda b, i: (b, 0, 0, 0)),
            pl.BlockSpec((1, H, CHUNK), lambda b, i: (b, 0, 0)),
        ] + c_out,
        out_shape=[
            jax.ShapeDtypeStruct(x.shape, F32),
            jax.ShapeDtypeStruct((nb, H, DK, 2 * DV), F32),
            jax.ShapeDtypeStruct((nb, H, CHUNK), F32),
        ] + c_shapes,
        scratch_shapes=[pltpu.VMEM((tm, D), F32)],
        compiler_params=_params("arbitrary", "arbitrary"),
        name="mlstm_prompt",
    )(*args, *c_args)


def _mlstm_sample_pre_kernel(x_ref, mod_ref, g_ref, wrow_ref, wcol_ref, bg_ref, m0_ref, n0_ref,
                             gh_ref, seg_ref, e64_ref, e128_ref,
                             q_ref, ka_ref, v_ref, dec_ref, numi_ref, rd_ref, eint_ref, go_ref,
                             n1_ref, m1_ref, *, t, nb):
    def norm(xr, shift, scale):
        return _modulate(xr, g_ref[...], shift, scale)

    hb = _by_group(norm, (x_ref[0],), (mod_ref[0, 0], mod_ref[0, 1])).astype(BF)
    p = _dot(hb, wrow_ref[...])
    k_all = _dot_nt(hb, wcol_ref[0:HK, :])
    ig_all = _dot_nt(hb, wcol_ref[HK:HK + LANES, :]) + bg_ref[0:1]
    fg_all = _dot_nt(hb, wcol_ref[HK + LANES:HK + 2 * LANES, :]) + bg_ref[1:2]

    def rows(i):
        return slice(i * nb, (i + 1) * nb)

    q = [p[rows(i), 0:HK] for i in range(t)]
    k = [k_all[rows(i)] for i in range(t)]
    v = [p[rows(i), HK:HK + D] for i in range(t)]
    m0 = m0_ref[...]
    n0 = n0_ref[...]

    b, r, mx = [], [], []
    for i in range(t):
        lf = _log_sigmoid(fg_all[rows(i)])
        b.append(lf if i == 0 else b[-1] + lf)
        r.append(ig_all[rows(i)] - b[i])
        cm = r[i] if i == 0 else jnp.maximum(cm, r[i])
        mx.append(jnp.maximum(m0, cm))
    mx_last = mx[t - 1]
    decay = jnp.exp(m0 - mx_last)
    m1_ref[...] = b[t - 1] + mx_last
    dec_x = _split_dot(decay, e128_ref)
    n1 = _split_dot(decay, e64_ref) * n0

    for i in range(t):
        ka = k[i] * _split_dot(jnp.exp(r[i] - mx_last), e64_ref)
        n1 = n1 + ka
        seq_rows = pl.ds(i, nb, stride=t)
        for c in range(HK // LANES):
            cs = slice(c * LANES, (c + 1) * LANES)
            ka_ref[c, seq_rows, :] = ka[:, cs]
            q_ref[c, seq_rows, :] = q[i][:, cs]
        for c in range(D // LANES):
            cs = slice(c * LANES, (c + 1) * LANES)
            v_ref[c, seq_rows, :] = v[i][:, cs]
            dec_ref[c, seq_rows, :] = dec_x[:, cs]
        e_int = jnp.exp(m0 - mx[i])
        den = e_int * _split_dot(q[i] * n0, seg_ref)
        numi = jnp.zeros((nb, D), F32)
        for s in range(i + 1):
            w = _split_dot(q[i] * k[s], seg_ref) * jnp.exp(jnp.minimum(r[s] - mx[i], 0.0))
            den = den + w
            numi = numi + _split_dot(w, e128_ref) * v[s]
        numi_ref[rows(i), :] = numi
        rd_ref[rows(i), :] = 1.0 / jnp.maximum(jnp.abs(den), jnp.exp(-(b[i] + mx[i])))
        eint_ref[rows(i), :] = e_int
        go_ref[rows(i), :] = jax.nn.sigmoid(p[rows(i), HK + D:]) * gh_ref[...]
    n1_ref[...] = n1


def _mlstm_sample_pre(x, mod, layer, g, w_row, w_col, b_g, m0, n0, g_head, seg64, e64, e128, *, t):
    rows = x.shape[1]
    nb = rows // t

    def whole(shape):
        return pl.BlockSpec(shape, lambda i: (0,) * len(shape))

    outs = [(HK // LANES, rows, LANES), (HK // LANES, rows, LANES), (D // LANES, rows, LANES),
            (D // LANES, rows, LANES), (rows, D), (rows, FG), (rows, FG), (rows, D), (nb, HK),
            (nb, FG)]
    return pl.pallas_call(
        functools.partial(_mlstm_sample_pre_kernel, t=t, nb=nb),
        grid=(1,),
        in_specs=[
            whole((1, rows, D)),
            _group_mod_spec(layer, nb),
            _const_spec((1, D)),
            _const_spec((D, HK + 2 * D)),
            _const_spec((HK + 2 * LANES, D)),
            _const_spec((2, LANES)),
            whole((nb, FG)),
            whole((nb, HK)),
            _const_spec((1, D)),
            _const_spec((HK, FG)),
            _const_spec((FG, HK)),
            _const_spec((FG, D)),
        ],
        out_specs=[whole(s) for s in outs],
        out_shape=[jax.ShapeDtypeStruct(s, F32) for s in outs],
        compiler_params=_params("arbitrary"),
        name="mlstm_sample_pre",
    )(x, mod, g.reshape(1, D), w_row, w_col, b_g, m0, n0, g_head.reshape(1, D), seg64, e64, e128)


def _mlstm_sample_state_kernel(q_ref, ka_ref, v_ref, dec_ref, c0_ref, inter_ref, c1_ref, *, bb, t):
    per = SUBLANES // t
    seq_of_row = lax.broadcasted_iota(jnp.int32, (SUBLANES, 1), 0) // t
    zeros_c = jnp.zeros((DK, DV), BF)

    def tile(i, carry):
        rs = pl.ds(pl.multiple_of(i * SUBLANES, SUBLANES), SUBLANES)
        for h in range(H):
            lo = h % 2 == 0
            half = slice((h % 2) * DK, (h % 2 + 1) * DK)
            qp = q_ref[h // 2, rs, :].astype(BF)
            kap = ka_ref[h // 2, rs, :]
            vh = v_ref[h, rs, :].astype(BF)
            dech = dec_ref[h, rs, :]
            res = None
            for w in range(per):
                seq = i * per + w
                c0 = c0_ref[seq, h]
                c0b = c0.astype(BF)
                rw = _dot(qp, jnp.concatenate([c0b, zeros_c] if lo else [zeros_c, c0b], axis=0))
                res = rw if res is None else jnp.where(seq_of_row == w, rw, res)
                kaw = jnp.where(seq_of_row == w, kap, 0.0).astype(BF)
                c1_ref[seq, h] = dech[w * t:w * t + 1, :] * c0 + _dot_tn(kaw, vh)[half, :]
            inter_ref[h, rs, :] = res
        return carry

    lax.fori_loop(0, bb // per, tile, 0, unroll=4)


def _mlstm_sample_state(q, ka, v, dec, c0, *, t, bb):
    rows = q.shape[1]
    nb = rows // t
    assert SUBLANES % t == 0 and bb % (SUBLANES // t) == 0 and nb % bb == 0

    def tok(blocks):
        return pl.BlockSpec((blocks, bb * t, LANES), lambda i: (0, i, 0))

    state = pl.BlockSpec((bb, H, DK, DV), lambda i: (i, 0, 0, 0))
    return pl.pallas_call(
        functools.partial(_mlstm_sample_state_kernel, bb=bb, t=t),
        grid=(nb // bb,),
        in_specs=[tok(HK // LANES), tok(HK // LANES), tok(H), tok(H), state],
        out_specs=[tok(H), state],
        out_shape=[jax.ShapeDtypeStruct((H, rows, LANES), F32),
                   jax.ShapeDtypeStruct(c0.shape, F32)],
        compiler_params=_params("arbitrary"),
        name="mlstm_sample_state",
    )(q, ka, v, dec, c0)


def _mlstm_sample_post_kernel(inter_ref, numi_ref, rd_ref, eint_ref, go_ref, e128_ref, seg_ref,
                              y_ref, *, t, nb):
    for i in range(t):
        rs = slice(i * nb, (i + 1) * nb)
        inter = jnp.concatenate(
            [inter_ref[h, pl.ds(i, nb, stride=t), :] for h in range(H)], axis=1)
        num = numi_ref[rs, :] + _split_dot(eint_ref[rs, :], e128_ref) * inter
        hh = num * _split_dot(rd_ref[rs, :], e128_ref)
        ms = _split_dot(hh * hh, seg_ref) * (1.0 / DV)
        y_ref[rs, :] = hh * _split_dot(lax.rsqrt(ms + EPS), e128_ref) * go_ref[rs, :]


def _mlstm_sample_post(inter, numi, rd, eint, go, e128, seg128, *, t):
    rows = inter.shape[1]

    def whole(shape):
        return pl.BlockSpec(shape, lambda i: (0,) * len(shape))

    return pl.pallas_call(
        functools.partial(_mlstm_sample_post_kernel, t=t, nb=rows // t),
        grid=(1,),
        in_specs=[whole((H, rows, LANES)), whole((rows, D)), whole((rows, FG)), whole((rows, FG)),
                  whole((rows, D)), _const_spec((FG, D)), _const_spec((D, FG))],
        out_specs=whole((rows, D)),
        out_shape=jax.ShapeDtypeStruct((rows, D), F32),
        compiler_params=_params("arbitrary"),
        name="mlstm_sample_post",
    )(inter, numi, rd, eint, go, e128, seg128)


def _conv_taps(u, prev2, prev1, wc_ref):
    return wc_ref[0:1] * prev2 + wc_ref[1:2] * prev1 + wc_ref[2:3] * u


def _conv_prompt_tile(x_ref, mod_ref, g_ref, win_ref, wc_ref, wout_ref, xo_ref, st_ref, carry_scr,
                      first):
    @pl.when(first)
    def _():
        carry_scr[...] = jnp.zeros(carry_scr.shape, F32)

    x = x_ref[0]
    tm = x.shape[0]
    hb = _modulate(x, g_ref[...], mod_ref[0, 0], mod_ref[0, 1]).astype(BF)
    u = _dot(hb, win_ref[:, D:2 * D]) * _dot(hb, win_ref[:, 2 * D:])
    row = lax.broadcasted_iota(jnp.int32, (tm, D), 0)
    c0 = carry_scr[0:1]
    c1 = carry_scr[1:2]
    prev1 = jnp.where(row == 0, c1, pltpu.roll(u, 1, 0))
    prev2 = jnp.where(row == 0, c0, jnp.where(row == 1, c1, pltpu.roll(u, 2, 0)))
    y = _conv_taps(u, prev2, prev1, wc_ref)
    bg = _dot(hb, win_ref[:, :D])
    xo_ref[0] = x + mod_ref[0, 2] * _dot((bg * y).astype(BF), wout_ref[...])
    carry_scr[...] = u[tm - 2:tm]
    st_ref[0] = u[tm - 2:tm]


def _conv_sample_tile(x_ref, mod_ref, g_ref, win_ref, wc_ref, wout_ref, buf_ref, xo_ref, st_ref,
                      *, t, nb):
    def norm(xr, shift, scale):
        return _modulate(xr, g_ref[...], shift, scale)

    def residual(xr, fr, gate):
        return xr + gate * fr

    x = x_ref[0]
    hb = _by_group(norm, (x,), (mod_ref[0, 0], mod_ref[0, 1])).astype(BF)
    u = _dot(hb, win_ref[:, D:2 * D]) * _dot(hb, win_ref[:, 2 * D:])
    full = [buf_ref[j] for j in range(CONV_W - 1)] + [u[i * nb:(i + 1) * nb] for i in range(t)]
    y = jnp.concatenate([_conv_taps(full[i + 2], full[i], full[i + 1], wc_ref) for i in range(t)],
                        axis=0)
    bg = _dot(hb, win_ref[:, :D])
    out = _dot((bg * y).astype(BF), wout_ref[...])
    xo_ref[0] = _by_group(residual, (x, out), (mod_ref[0, 2],))
    for j in range(CONV_W - 1):
        st_ref[j] = full[t + j]


def _conv_kernel(xp_ref, modp_ref, xs_ref, mods_ref, g_ref, win_ref, wc_ref, wout_ref, buf_ref,
                 xpo_ref, stp_ref, xso_ref, sts_ref, carry_scr, *, n_prompt, nt, t, nb):
    step = pl.program_id(0)

    @pl.when(step < n_prompt)
    def _():
        _conv_prompt_tile(xp_ref, modp_ref, g_ref, win_ref, wc_ref, wout_ref, xpo_ref, stp_ref,
                          carry_scr, step % nt == 0)

    @pl.when(step == n_prompt)
    def _():
        _conv_sample_tile(xs_ref, mods_ref, g_ref, win_ref, wc_ref, wout_ref, buf_ref, xso_ref,
                          sts_ref, t=t, nb=nb)


def _conv(xp, mod_p, xs, mod, layer, g, w_in, w_conv, w_out, buf, *, tm, t, casts=()):
    nb, tp, _ = xp.shape
    nt = tp // tm
    n_prompt = nb * nt
    rows_s = xs.shape[1]
    seqs = rows_s // t
    tok, per_seq = _prompt_tiles(nb, nt)
    args = (xp, mod_p, xs, mod, g.reshape(1, D), w_in, w_conv, w_out, buf)
    c_in, c_out, c_shapes, c_args = _cast_specs(
        casts, n_prompt, lambda s: jnp.minimum(s, n_prompt - 1))
    body = functools.partial(_conv_kernel, n_prompt=n_prompt, nt=nt, t=t, nb=seqs)
    return pl.pallas_call(
        _carry_casts(body, len(args), 4, len(casts)),
        grid=(n_prompt + 1,),
        in_specs=[
            tok(tm),
            per_seq((N_MOD, 1, D)),
            _const_spec((1, rows_s, D)),
            _group_mod_spec(layer, seqs),
            _const_spec((1, D)),
            _const_spec((D, 3 * D)),
            _const_spec((CONV_W, D)),
            _const_spec((D, D)),
            _const_spec((CONV_W - 1, seqs, D)),
        ] + c_in,
        out_specs=[tok(tm), per_seq((CONV_W - 1, D)), _whole_spec((1, rows_s, D)),
                   _whole_spec((CONV_W - 1, seqs, D))] + c_out,
        out_shape=[jax.ShapeDtypeStruct(xp.shape, F32),
                   jax.ShapeDtypeStruct((nb, CONV_W - 1, D), F32),
                   jax.ShapeDtypeStruct(xs.shape, F32),
                   jax.ShapeDtypeStruct((CONV_W - 1, seqs, D), F32)] + c_shapes,
        scratch_shapes=[pltpu.VMEM((CONV_W - 1, D), F32)],
        compiler_params=_params("arbitrary"),
        name="conv",
    )(*args, *c_args)


def kernel(x_prompt, x_sample, c_prompt, c_sample, state_mlstm_C, state_mlstm_n, state_mlstm_m,
           state_conv, ada_w, ada_b, norm_g, final_norm_g, mlstm_w_in, mlstm_b_gates,
           mlstm_head_g, mlstm_w_out, conv_w_in, conv_w, conv_w_out, mlp_w_up, mlp_w_down):
    assert ada_w.shape[0] == 2 and mlstm_w_in.shape[0] == 1 and conv_w_in.shape[0] == 1
    bp, tp, _ = x_prompt.shape
    bs, ts, _ = x_sample.shape
    assert ts == 4 and tp % 512 == 0

    mod = _ada(jnp.concatenate([c_sample, c_prompt], axis=0), ada_w, ada_b)
    mod_p = jnp.transpose(mod[:, :, bs:], (0, 2, 1, 3))[:, :, :, None]

    w_in0 = mlstm_w_in[0]
    w_row = jnp.concatenate([w_in0[:, :HK], w_in0[:, 2 * HK:NP]], axis=1).astype(BF)
    gate_pad = jnp.zeros((D, LANES - H), F32)
    w_col = jnp.concatenate([w_in0[:, HK:2 * HK] * (DK ** -0.5), w_in0[:, NP:NP + H], gate_pad,
                             w_in0[:, NP + H:], gate_pad], axis=1).T.astype(BF)
    b_g = jnp.pad(mlstm_b_gates[0].reshape(2, H), ((0, 0), (0, LANES - H)))
    wout0 = mlstm_w_out[0].astype(BF)

    x, s_p, m_p, up0, down0 = _mlstm_prompt(
        x_prompt, mod_p[0], norm_g[0, 0], w_row, w_col, mlstm_b_gates[0], mlstm_head_g[0], wout0,
        tm=1024, casts=((mlp_w_up, 0), (mlp_w_down, 0)))
    prompt_c = s_p[None, :, :, :, :DV]
    prompt_n = s_p[None, :, :, :, DV]
    prompt_m = m_p[None, :, :, 0]

    xs = jnp.transpose(x_sample, (1, 0, 2)).reshape(1, ts * bs, D)
    head_of_v = jnp.arange(D, dtype=jnp.int32) // DV
    head_of_k = jnp.arange(HK, dtype=jnp.int32) // DK
    lanes = jnp.arange(FG, dtype=jnp.int32)
    e128 = (lanes[:, None] == head_of_v[None, :]).astype(BF)
    e64 = (lanes[:, None] == head_of_k[None, :]).astype(BF)
    m0 = jnp.pad(state_mlstm_m[0], ((0, 0), (0, FG - H)))
    n0 = state_mlstm_n[0].reshape(bs, HK)
    q, ka, v, dec, numi, rd, eint, go, n1, m1 = _mlstm_sample_pre(
        xs, mod, 0, norm_g[0, 0], w_row, w_col, b_g, m0, n0, mlstm_head_g[0], e64.T, e64, e128,
        t=ts)
    inter, c1 = _mlstm_sample_state(q, ka, v, dec, state_mlstm_C[0], t=ts, bb=32)
    y = _mlstm_sample_post(inter, numi, rd, eint, go, e128, e128.T, t=ts)
    sample_n = n1.reshape(bs, H, DK)[None]
    sample_m = m1[None, :, :H]

    x, xs, cwin, cwout = _mlp(x, mod_p[0], xs, mod, 0, norm_g[0, 1], up0, down0, tm=512, seqs=bs,
                              pre=(y.reshape(1, ts * bs, D), wout0),
                              casts=((conv_w_in, 0), (conv_w_out, 0)))
    x, conv_p, xs, conv_s, up1, down1 = _conv(
        x, mod_p[1], xs, mod, 1, norm_g[1, 0], cwin, conv_w[0], cwout,
        jnp.transpose(state_conv[0], (1, 0, 2)), tm=1024, t=ts,
        casts=((mlp_w_up, 1), (mlp_w_down, 1)))
    y_prompt, ys = _mlp(x, mod_p[1], xs, mod, 1, norm_g[1, 1], up1, down1, tm=512, seqs=bs,
                        final_g=final_norm_g)
    prompt_conv = conv_p[None]
    y_sample = jnp.transpose(ys.reshape(ts, bs, D), (1, 0, 2))
    sample_conv = jnp.transpose(conv_s, (1, 0, 2))[None]

    return (y_prompt, y_sample, prompt_c, prompt_n, prompt_m, prompt_conv,
            c1[None], sample_n, sample_m, sample_conv)
```

```python
import functools

import jax
import numpy as np
import jax.numpy as jnp
from jax import lax
from jax.experimental import pallas as pl
from jax.experimental.pallas import tpu as pltpu

D = 1024
H = 8
DK = 64
DV = 128
HK = H * DK
NP = 2 * HK + 2 * D
FG = 128
LANES = 128
SUBLANES = 8
DFF = 4 * D
N_MOD = 6
CONV_W = 3
EPS = 1e-6
CHUNK = 128
FF_CHUNK = 1024
BF = jnp.bfloat16
F32 = jnp.float32
VMEM_LIMIT_BYTES = 56 * 1024 * 1024


def _dot(a, b):
    return jnp.dot(a, b, preferred_element_type=F32)


def _dot_nt(a, b):
    return lax.dot_general(a, b, (((1,), (1,)), ((), ())), preferred_element_type=F32)


def _dot_tn(a, b):
    return lax.dot_general(a, b, (((0,), (0,)), ((), ())), preferred_element_type=F32)


def _rms(x):
    return x * lax.rsqrt(jnp.mean(x * x, axis=-1, keepdims=True) + EPS)


def _modulate(x, g, shift, scale):
    return _rms(x) * g * (1.0 + scale) + shift


def _log_sigmoid(x):
    return jnp.minimum(x, 0.0) - jnp.log1p(jnp.exp(-jnp.abs(x)))


def _scan_lanes(x, op, ident, n):
    lane = lax.broadcasted_iota(jnp.int32, x.shape, 1) % n
    d = 1
    while d < n:
        x = op(x, jnp.where(lane >= d, pltpu.roll(x, d, 1), ident))
        d *= 2
    return x


def _by_group(fn, xs, mods):
    rows, r = xs[0].shape[0], mods[0].shape[0]
    if r == 1 or r == rows:
        return fn(*xs, *mods)
    return jnp.concatenate(
        [fn(*[x[i:i + r] for x in xs], *mods) for i in range(0, rows, r)], axis=0)


def _split_dot(x, m_ref):
    hi = x.astype(BF)
    lo = (x - hi.astype(F32)).astype(BF)
    m = m_ref[...]
    return _dot(hi, m) + _dot(lo, m)


def _cast_specs(jobs, steps, slab_of):
    ins, outs, shapes, args = [], [], [], []
    for w, layer in jobs:
        _, r, c = w.shape
        slab = r // steps
        assert slab * steps == r and slab % (2 * SUBLANES) == 0
        ins.append(pl.BlockSpec((1, slab, c), lambda *g, layer=layer: (layer, slab_of(*g), 0)))
        outs.append(pl.BlockSpec((slab, c), lambda *g: (slab_of(*g), 0)))
        shapes.append(jax.ShapeDtypeStruct((r, c), BF))
        args.append(w)
    return ins, outs, shapes, args


def _carry_casts(body, n_in, n_out, n_cast):
    def kernel(*refs):
        a, b = n_in, n_in + n_cast
        c, d = b + n_out, b + n_out + n_cast
        for src, dst in zip(refs[a:b], refs[c:d]):
            dst[...] = src[0].astype(BF)
        body(*refs[:a], *refs[b:c], *refs[d:])
    return kernel


def _head_expansion(width):
    return jnp.asarray(np.arange(FG)[:, None] == np.arange(H * width)[None, :] // width, BF)


def _const_spec(shape):
    nd = len(shape)
    return pl.BlockSpec(shape, lambda *_: (0,) * nd, pipeline_mode=pl.Buffered(1))


def _params(*sem):
    return pltpu.CompilerParams(dimension_semantics=sem, vmem_limit_bytes=VMEM_LIMIT_BYTES)


def _ada_kernel(c_ref, w_ref, b_ref, o_ref):
    c = c_ref[...]
    sc = (c * jax.nn.sigmoid(c)).astype(BF)
    o_ref[0, 0] = _dot(sc, w_ref[0].astype(BF)) + b_ref[0]


def _ada(c_all, ada_w, ada_b):
    depth, _, n = ada_w.shape
    nb = c_all.shape[0]
    return pl.pallas_call(
        _ada_kernel,
        grid=(depth, n // D),
        in_specs=[
            pl.BlockSpec((nb, D), lambda i, j: (0, 0)),
            pl.BlockSpec((1, D, D), lambda i, j: (i, 0, j)),
            pl.BlockSpec((1, 1, D), lambda i, j: (i, 0, j)),
        ],
        out_specs=pl.BlockSpec((1, 1, nb, D), lambda i, j: (i, j, 0, 0)),
        out_shape=jax.ShapeDtypeStruct((depth, n // D, nb, D), F32),
        compiler_params=_params("arbitrary", "arbitrary"),
        name="ada",
    )(c_all, ada_w, ada_b.reshape(depth, 1, n))


def _mlp_tile(x_ref, mod_ref, g_ref, wu_ref, wd_ref, fg_ref, o_ref, y_ref=None, wo_ref=None):
    def residual(xr, fr, gate):
        return xr + gate * fr

    def norm(xr, shift, scale):
        return _modulate(xr, g_ref[...], shift, scale)

    x = x_ref[0]
    if y_ref is not None:
        x = _by_group(residual, (x, _dot(y_ref[0].astype(BF), wo_ref[...])), (mod_ref[0, 2],))
    hb = _by_group(norm, (x,), (mod_ref[0, 3], mod_ref[0, 4])).astype(BF)
    acc = jnp.zeros(x.shape, F32)
    for c in range(DFF // FF_CHUNK):
        cs = slice(c * FF_CHUNK, (c + 1) * FF_CHUNK)
        hid = jnp.maximum(_dot(hb, wu_ref[:, cs]), 0.0)
        acc = acc + _dot((hid * hid).astype(BF), wd_ref[cs, :])
    x = _by_group(residual, (x, acc), (mod_ref[0, 5],))
    if fg_ref is not None:
        x = _rms(x) * fg_ref[...]
    o_ref[0] = x


def _mlp_kernel(*refs, n_prompt, pre, final):
    refs = list(refs)
    xp_ref, modp_ref, xs_ref = refs[:3]
    refs = refs[3:]
    ys_ref = wo_ref = fg_ref = None
    if pre:
        ys_ref, wo_ref = refs[:2]
        refs = refs[2:]
    mods_ref, g_ref, wu_ref, wd_ref = refs[:4]
    refs = refs[4:]
    if final:
        fg_ref = refs.pop(0)
    xpo_ref, xso_ref = refs
    step = pl.program_id(0)

    @pl.when(step < n_prompt)
    def _():
        _mlp_tile(xp_ref, modp_ref, g_ref, wu_ref, wd_ref, fg_ref, xpo_ref)

    @pl.when(step == n_prompt)
    def _():
        _mlp_tile(xs_ref, mods_ref, g_ref, wu_ref, wd_ref, fg_ref, xso_ref, ys_ref, wo_ref)


def _group_mod_spec(layer, r):
    return pl.BlockSpec((1, N_MOD, r, D), lambda *_: (layer, 0, 0, 0),
                        pipeline_mode=pl.Buffered(1))


def _prompt_tiles(nb, nt):
    def seq_tile(step):
        c = jnp.minimum(step, nb * nt - 1)
        return c // nt, c % nt

    def tok(tm):
        return pl.BlockSpec((1, tm, D), lambda s: (*seq_tile(s), 0))

    def per_seq(shape):
        return pl.BlockSpec((1,) + shape, lambda s: (seq_tile(s)[0],) + (0,) * len(shape))

    return tok, per_seq


def _whole_spec(shape):
    return pl.BlockSpec(shape, lambda *_: (0,) * len(shape))


def _mlp(xp, mod_p, xs, mod, layer, g, w_up, w_down, *, tm, seqs, pre=None, final_g=None,
         casts=()):
    nb, t, _ = xp.shape
    nt = t // tm
    n_prompt = nb * nt
    rows_s = xs.shape[1]
    tok, per_seq = _prompt_tiles(nb, nt)
    args = [xp, mod_p, xs]
    specs = [tok(tm), per_seq((N_MOD, 1, D)), _const_spec((1, rows_s, D))]
    if pre is not None:
        args += list(pre)
        specs += [_const_spec((1, rows_s, D)), _const_spec((D, D))]
    args += [mod, g.reshape(1, D), w_up, w_down]
    specs += [_group_mod_spec(layer, seqs), _const_spec((1, D)), _const_spec((D, DFF)),
              _const_spec((DFF, D))]
    if final_g is not None:
        args.append(final_g.reshape(1, D))
        specs.append(_const_spec((1, D)))
    c_in, c_out, c_shapes, c_args = _cast_specs(
        casts, n_prompt, lambda s: jnp.minimum(s, n_prompt - 1))
    body = functools.partial(_mlp_kernel, n_prompt=n_prompt, pre=pre is not None,
                             final=final_g is not None)
    return pl.pallas_call(
        _carry_casts(body, len(args), 2, len(casts)),
        grid=(n_prompt + 1,),
        in_specs=specs + c_in,
        out_specs=[tok(tm), _whole_spec((1, rows_s, D))] + c_out,
        out_shape=[jax.ShapeDtypeStruct(xp.shape, F32), jax.ShapeDtypeStruct(xs.shape, F32)]
        + c_shapes,
        compiler_params=_params("arbitrary"),
        name="mlp",
    )(*args, *c_args)


def _mlstm_prompt_kernel(x_ref, mod_ref, g_ref, wrow_ref, wcol_ref, bg_ref, gh_ref, wout_ref,
                         xo_ref, s_ref, m_ref, y_scr, *, tm):
    @pl.when(pl.program_id(1) == 0)
    def _():
        s_ref[...] = jnp.zeros(s_ref.shape, F32)
        m_ref[...] = jnp.zeros(m_ref.shape, F32)

    x = x_ref[0]
    hb = _modulate(x, g_ref[...], mod_ref[0, 0], mod_ref[0, 1]).astype(BF)
    n = CHUNK
    w_t = jnp.concatenate([wcol_ref[0:HK, :], wcol_ref[HK:HK + H, :],
                           wcol_ref[HK + LANES:HK + LANES + H, :]], axis=0)
    pcol = _dot_nt(w_t, hb)
    ig_all = pcol[HK:HK + H] + bg_ref[0:H]
    fg_all = pcol[HK + H:HK + 2 * H] + bg_ref[H:2 * H]
    b_all = _scan_lanes(_log_sigmoid(fg_all), jnp.add, 0.0, n)
    r_all = ig_all - b_all
    cm_all = _scan_lanes(r_all, jnp.maximum, -jnp.inf, n)
    prow = jnp.concatenate([_dot(hb, wrow_ref[:, 0:HK]), _dot(hb, wrow_ref[:, 2 * HK:NP])],
                           axis=1)
    causal = (lax.broadcasted_iota(jnp.int32, (n, n), 0)
              >= lax.broadcasted_iota(jnp.int32, (n, n), 1))
    zeros_k = jnp.zeros((DK, n), BF)
    zeros_s = jnp.zeros((DK, 2 * DV), BF)
    zeros_v = jnp.zeros((n, DV), BF)
    ones_v = jnp.ones((n, DV), BF)
    pad = jnp.zeros((n - 2 * H, n), F32)

    for c in range(tm // n):
        ts = slice(c * n, (c + 1) * n)
        r = r_all[:, ts]
        m0 = m_ref[0]
        mx = jnp.maximum(m0, cm_all[:, ts])
        m = b_all[:, ts] + mx
        e_neg = jnp.exp(-m)
        mx_last = jnp.broadcast_to(mx[:, n - 1:n], (H, n))
        a = jnp.exp(r - mx_last)
        decay = jnp.exp(m0 - mx_last)
        m_ref[0] = jnp.broadcast_to(m[:, n - 1:n], (H, n))
        cols = jnp.concatenate([mx, e_neg, pad], axis=0).T

        for h in range(H):
            lo = h % 2 == 0
            mx_b = jnp.broadcast_to(cols[:, h:h + 1], (n, n))
            floor_b = jnp.broadcast_to(cols[:, H + h:H + h + 1], (n, DV))
            dec = jnp.exp(jnp.where(causal, r[h:h + 1, :] - mx_b, -jnp.inf))
            e_int_b = jnp.exp(m0[h:h + 1, :] - mx_b)
            if lo:
                qp = prow[ts, (h // 2) * 2 * DK:(h // 2 + 1) * 2 * DK]
                qpb = qp.astype(BF)
            kt = pcol[h * DK:(h + 1) * DK, ts]
            ktb = kt.astype(BF)
            s = _dot(qpb, jnp.concatenate([ktb, zeros_k] if lo else [zeros_k, ktb], axis=0)) * dec
            st = s_ref[0, h]
            stb = st.astype(BF)
            v = prow[ts, HK + h * DV:HK + (h + 1) * DV].astype(BF)
            rhs = jnp.concatenate(
                [jnp.concatenate([v, zeros_v], axis=1)]
                + ([stb, zeros_s] if lo else [zeros_s, stb]), axis=0)
            lhs = jnp.concatenate([s.astype(BF), (qp * e_int_b).astype(BF)], axis=1)
            res = _dot(lhs, rhs)
            den = jnp.sum(s, axis=-1, keepdims=True) + res[:, DV:]
            hh = res[:, :DV] / jnp.maximum(jnp.abs(den), floor_b)
            hh = hh * lax.rsqrt(jnp.mean(hh * hh, axis=-1, keepdims=True) + EPS)
            y_scr[ts, h * DV:(h + 1) * DV] = hh
            ka = (kt * a[h:h + 1, :]).astype(BF)
            dc = decay[h:h + 1, :]
            s_ref[0, h] = (jnp.concatenate([dc, dc], axis=1) * st
                           + _dot(ka, jnp.concatenate([v, ones_v], axis=1)))

    o = prow[:, HK + D:]
    y = (y_scr[...] * gh_ref[...] * jax.nn.sigmoid(o)).astype(BF)
    xo_ref[0] = x + mod_ref[0, 2] * _dot(y, wout_ref[...])


def _mlstm_prompt(x, mod, g, w_row, w_col, b_g, g_head, w_out, *, tm, casts=()):
    nb, t, _ = x.shape
    nt = t // tm
    tok = pl.BlockSpec((1, tm, D), lambda b, i: (b, i, 0))
    args = (x, mod, g.reshape(1, D), w_row, w_col,
            jnp.broadcast_to(b_g.reshape(2 * H, 1), (2 * H, tm)), g_head.reshape(1, D), w_out)
    c_in, c_out, c_shapes, c_args = _cast_specs(casts, nb * nt, lambda b, i: b * nt + i)
    return pl.pallas_call(
        _carry_casts(functools.partial(_mlstm_prompt_kernel, tm=tm), len(args), 3, len(casts)),
        grid=(nb, nt),
        in_specs=[
            tok,
            pl.BlockSpec((1, N_MOD, 1, D), lambda b, i: (b, 0, 0, 0)),
            _const_spec((1, D)),
            _const_spec((D, NP)),
            _const_spec((HK + 2 * LANES, D)),
            _const_spec((2 * H, tm)),
            _const_spec((1, D)),
            _const_spec((D, D)),
        ] + c_in,
        out_specs=[
            tok,
            pl.BlockSpec((1, H, DK, 2 * DV), lambda b, i: (b, 0, 0, 0)),
            pl.BlockSpec((1, H, CHUNK), lambda b, i: (b, 0, 0)),
        ] + c_out,
        out_shape=[
            jax.ShapeDtypeStruct(x.shape, F32),
            jax.ShapeDtypeStruct((nb, H, DK, 2 * DV), F32),
            jax.ShapeDtypeStruct((nb, H, CHUNK), F32),
        ] + c_shapes,
        scratch_shapes=[pltpu.VMEM((tm, D), F32)],
        compiler_params=_params("arbitrary", "arbitrary"),
        name="mlstm_prompt",
    )(*args, *c_args)


def _mlstm_sample_pre_kernel(x_ref, mod_ref, g_ref, wrow_ref, wcol_ref, bg_ref, m0_ref, n0_ref,
                             gh_ref, seg_ref, e64_ref, e128_ref,
                             q_ref, ka_ref, v_ref, dec_ref, numi_ref, rd_ref, eint_ref, go_ref,
                             n1_ref, m1_ref, *, t, nb):
    def norm(xr, shift, scale):
        return _modulate(xr, g_ref[...], shift, scale)

    hb = _by_group(norm, (x_ref[0],), (mod_ref[0, 0], mod_ref[0, 1])).astype(BF)
    p = jnp.concatenate([_dot(hb, wrow_ref[:, 0:HK]), _dot(hb, wrow_ref[:, 2 * HK:NP])],
                        axis=1)
    k_all = _dot_nt(hb, wcol_ref[0:HK, :])
    ig_all = _dot_nt(hb, wcol_ref[HK:HK + LANES, :]) + bg_ref[0:1]
    fg_all = _dot_nt(hb, wcol_ref[HK + LANES:HK + 2 * LANES, :]) + bg_ref[1:2]

    def rows(i):
        return slice(i * nb, (i + 1) * nb)

    q = [p[rows(i), 0:HK] for i in range(t)]
    k = [k_all[rows(i)] for i in range(t)]
    v = [p[rows(i), HK:HK + D] for i in range(t)]
    m0 = m0_ref[...]
    n0 = n0_ref[...]

    b, r, mx = [], [], []
    for i in range(t):
        lf = _log_sigmoid(fg_all[rows(i)])
        b.append(lf if i == 0 else b[-1] + lf)
        r.append(ig_all[rows(i)] - b[i])
        cm = r[i] if i == 0 else jnp.maximum(cm, r[i])
        mx.append(jnp.maximum(m0, cm))
    mx_last = mx[t - 1]
    decay = jnp.exp(m0 - mx_last)
    m1_ref[...] = b[t - 1] + mx_last
    dec_x = _split_dot(decay, e128_ref)
    n1 = _split_dot(decay, e64_ref) * n0

    for i in range(t):
        ka = k[i] * _split_dot(jnp.exp(r[i] - mx_last), e64_ref)
        n1 = n1 + ka
        seq_rows = pl.ds(i, nb, stride=t)
        for c in range(HK // LANES):
            cs = slice(c * LANES, (c + 1) * LANES)
            ka_ref[c, seq_rows, :] = ka[:, cs]
            q_ref[c, seq_rows, :] = q[i][:, cs]
        for c in range(D // LANES):
            cs = slice(c * LANES, (c + 1) * LANES)
            v_ref[c, seq_rows, :] = v[i][:, cs]
            dec_ref[c, seq_rows, :] = dec_x[:, cs]
        e_int = jnp.exp(m0 - mx[i])
        den = e_int * _split_dot(q[i] * n0, seg_ref)
        numi = jnp.zeros((nb, D), F32)
        for s in range(i + 1):
            w = _split_dot(q[i] * k[s], seg_ref) * jnp.exp(jnp.minimum(r[s] - mx[i], 0.0))
            den = den + w
            numi = numi + _split_dot(w, e128_ref) * v[s]
        numi_ref[rows(i), :] = numi
        rd_ref[rows(i), :] = 1.0 / jnp.maximum(jnp.abs(den), jnp.exp(-(b[i] + mx[i])))
        eint_ref[rows(i), :] = e_int
        go_ref[rows(i), :] = jax.nn.sigmoid(p[rows(i), HK + D:]) * gh_ref[...]
    n1_ref[...] = n1


def _mlstm_sample_state_body(q_ref, ka_ref, v_ref, dec_ref, c0_ref, inter_ref, c1_ref, *, bb, t,
                             row0):
    per = SUBLANES // t
    seq_of_row = lax.broadcasted_iota(jnp.int32, (SUBLANES, 1), 0) // t
    zeros_c = jnp.zeros((DK, DV), BF)

    def tile(i, carry):
        rs = pl.ds(pl.multiple_of(row0 + i * SUBLANES, SUBLANES), SUBLANES)
        for h in range(H):
            lo = h % 2 == 0
            half = slice((h % 2) * DK, (h % 2 + 1) * DK)
            qp = q_ref[h // 2, rs, :].astype(BF)
            kap = ka_ref[h // 2, rs, :]
            vh = v_ref[h, rs, :].astype(BF)
            dech = dec_ref[h, rs, :]
            res = None
            for w in range(per):
                seq = i * per + w
                c0 = c0_ref[seq, h]
                c0b = c0.astype(BF)
                rw = _dot(qp, jnp.concatenate([c0b, zeros_c] if lo else [zeros_c, c0b], axis=0))
                res = rw if res is None else jnp.where(seq_of_row == w, rw, res)
                kaw = jnp.where(seq_of_row == w, kap, 0.0).astype(BF)
                c1_ref[seq, h] = dech[w * t:w * t + 1, :] * c0 + _dot_tn(kaw, vh)[half, :]
            inter_ref[h, rs, :] = res
        return carry

    lax.fori_loop(0, bb // per, tile, 0, unroll=4)


def _mlstm_sample_post_kernel(inter_ref, numi_ref, rd_ref, eint_ref, go_ref, e128_ref, seg_ref,
                              y_ref, *, t, nb):
    for i in range(t):
        rs = slice(i * nb, (i + 1) * nb)
        inter = jnp.concatenate(
            [inter_ref[h, pl.ds(i, nb, stride=t), :] for h in range(H)], axis=1)
        num = numi_ref[rs, :] + _split_dot(eint_ref[rs, :], e128_ref) * inter
        hh = num * _split_dot(rd_ref[rs, :], e128_ref)
        ms = _split_dot(hh * hh, seg_ref) * (1.0 / DV)
        y_ref[rs, :] = hh * _split_dot(lax.rsqrt(ms + EPS), e128_ref) * go_ref[rs, :]


def _mlstm_sample_kernel(x_ref, mod_ref, g_ref, wrow_ref, wcol_ref, bg_ref, m0_ref, n0_ref, gh_ref,
                         seg64_ref, e64_ref, e128_ref, seg128_ref, c0_ref,
                         y_ref, n1_ref, m1_ref, c1_ref,
                         q_s, ka_s, v_s, dec_s, inter_s, numi_s, rd_s, eint_s, go_s, *, t, nb, bb):
    step = pl.program_id(0)

    @pl.when(step == 0)
    def _():
        _mlstm_sample_pre_kernel(x_ref, mod_ref, g_ref, wrow_ref, wcol_ref, bg_ref, m0_ref, n0_ref,
                                 gh_ref, seg64_ref, e64_ref, e128_ref,
                                 q_s, ka_s, v_s, dec_s, numi_s, rd_s, eint_s, go_s, n1_ref, m1_ref,
                                 t=t, nb=nb)

    _mlstm_sample_state_body(q_s, ka_s, v_s, dec_s, c0_ref, inter_s, c1_ref, bb=bb, t=t,
                             row0=step * (bb * t))

    @pl.when(step == pl.num_programs(0) - 1)
    def _():
        _mlstm_sample_post_kernel(inter_s, numi_s, rd_s, eint_s, go_s, e128_ref, seg128_ref, y_ref,
                                  t=t, nb=nb)


def _mlstm_sample(x, mod, layer, g, w_row, w_col, b_g, m0, n0, g_head, c0, *, t, bb):
    rows = x.shape[1]
    nb = rows // t
    assert SUBLANES % t == 0 and bb % (SUBLANES // t) == 0 and nb % bb == 0
    e128 = _head_expansion(DV)
    e64 = _head_expansion(DK)
    state = pl.BlockSpec((bb, H, DK, DV), lambda i: (i, 0, 0, 0))
    return pl.pallas_call(
        functools.partial(_mlstm_sample_kernel, t=t, nb=nb, bb=bb),
        grid=(nb // bb,),
        in_specs=[
            _const_spec((1, rows, D)),
            _group_mod_spec(layer, nb),
            _const_spec((1, D)),
            _const_spec((D, NP)),
            _const_spec((HK + 2 * LANES, D)),
            _const_spec((2, LANES)),
            _const_spec((nb, FG)),
            _const_spec((nb, HK)),
            _const_spec((1, D)),
            _const_spec((HK, FG)),
            _const_spec((FG, HK)),
            _const_spec((FG, D)),
            _const_spec((D, FG)),
            state,
        ],
        out_specs=[_whole_spec((rows, D)), _whole_spec((nb, HK)), _whole_spec((nb, FG)), state],
        out_shape=[jax.ShapeDtypeStruct((rows, D), F32), jax.ShapeDtypeStruct((nb, HK), F32),
                   jax.ShapeDtypeStruct((nb, FG), F32), jax.ShapeDtypeStruct(c0.shape, F32)],
        scratch_shapes=[
            pltpu.VMEM((HK // LANES, rows, LANES), F32),
            pltpu.VMEM((HK // LANES, rows, LANES), F32),
            pltpu.VMEM((D // LANES, rows, LANES), F32),
            pltpu.VMEM((D // LANES, rows, LANES), F32),
            pltpu.VMEM((D // LANES, rows, LANES), F32),
            pltpu.VMEM((rows, D), F32),
            pltpu.VMEM((rows, FG), F32),
            pltpu.VMEM((rows, FG), F32),
            pltpu.VMEM((rows, D), F32),
        ],
        compiler_params=_params("arbitrary"),
        name="mlstm_sample",
    )(x, mod, g.reshape(1, D), w_row, w_col, b_g, m0, n0, g_head.reshape(1, D),
      e64.T, e64, e128, e128.T, c0)


def _conv_taps(u, prev2, prev1, wc_ref):
    return wc_ref[0:1] * prev2 + wc_ref[1:2] * prev1 + wc_ref[2:3] * u


def _conv_prompt_tile(x_ref, mod_ref, g_ref, win_ref, wc_ref, wout_ref, xo_ref, st_ref, carry_scr,
                      first):
    @pl.when(first)
    def _():
        carry_scr[...] = jnp.zeros(carry_scr.shape, F32)

    x = x_ref[0]
    tm = x.shape[0]
    hb = _modulate(x, g_ref[...], mod_ref[0, 0], mod_ref[0, 1]).astype(BF)
    u = _dot(hb, win_ref[:, D:2 * D]) * _dot(hb, win_ref[:, 2 * D:])
    row = lax.broadcasted_iota(jnp.int32, (tm, D), 0)
    c0 = carry_scr[0:1]
    c1 = carry_scr[1:2]
    prev1 = jnp.where(row == 0, c1, pltpu.roll(u, 1, 0))
    prev2 = jnp.where(row == 0, c0, jnp.where(row == 1, c1, pltpu.roll(u, 2, 0)))
    y = _conv_taps(u, prev2, prev1, wc_ref)
    bg = _dot(hb, win_ref[:, :D])
    xo_ref[0] = x + mod_ref[0, 2] * _dot((bg * y).astype(BF), wout_ref[...])
    carry_scr[...] = u[tm - 2:tm]
    st_ref[0] = u[tm - 2:tm]


def _conv_sample_tile(x_ref, mod_ref, g_ref, win_ref, wc_ref, wout_ref, buf_ref, xo_ref, st_ref,
                      *, t, nb):
    def norm(xr, shift, scale):
        return _modulate(xr, g_ref[...], shift, scale)

    def residual(xr, fr, gate):
        return xr + gate * fr

    x = x_ref[0]
    hb = _by_group(norm, (x,), (mod_ref[0, 0], mod_ref[0, 1])).astype(BF)
    u = _dot(hb, win_ref[:, D:2 * D]) * _dot(hb, win_ref[:, 2 * D:])
    full = [buf_ref[j] for j in range(CONV_W - 1)] + [u[i * nb:(i + 1) * nb] for i in range(t)]
    y = jnp.concatenate([_conv_taps(full[i + 2], full[i], full[i + 1], wc_ref) for i in range(t)],
                        axis=0)
    bg = _dot(hb, win_ref[:, :D])
    out = _dot((bg * y).astype(BF), wout_ref[...])
    xo_ref[0] = _by_group(residual, (x, out), (mod_ref[0, 2],))
    for j in range(CONV_W - 1):
        st_ref[j] = full[t + j]


def _conv_kernel(xp_ref, modp_ref, xs_ref, mods_ref, g_ref, win_ref, wc_ref, wout_ref, buf_ref,
                 xpo_ref, stp_ref, xso_ref, sts_ref, carry_scr, *, n_prompt, nt, t, nb):
    step = pl.program_id(0)

    @pl.when(step < n_prompt)
    def _():
        _conv_prompt_tile(xp_ref, modp_ref, g_ref, win_ref, wc_ref, wout_ref, xpo_ref, stp_ref,
                          carry_scr, step % nt == 0)

    @pl.when(step == n_prompt)
    def _():
        _conv_sample_tile(xs_ref, mods_ref, g_ref, win_ref, wc_ref, wout_ref, buf_ref, xso_ref,
                          sts_ref, t=t, nb=nb)


def _conv(xp, mod_p, xs, mod, layer, g, w_in, w_conv, w_out, buf, *, tm, t, casts=()):
    nb, tp, _ = xp.shape
    nt = tp // tm
    n_prompt = nb * nt
    rows_s = xs.shape[1]
    seqs = rows_s // t
    tok, per_seq = _prompt_tiles(nb, nt)
    args = (xp, mod_p, xs, mod, g.reshape(1, D), w_in, w_conv, w_out, buf)
    c_in, c_out, c_shapes, c_args = _cast_specs(
        casts, n_prompt, lambda s: jnp.minimum(s, n_prompt - 1))
    body = functools.partial(_conv_kernel, n_prompt=n_prompt, nt=nt, t=t, nb=seqs)
    return pl.pallas_call(
        _carry_casts(body, len(args), 4, len(casts)),
        grid=(n_prompt + 1,),
        in_specs=[
            tok(tm),
            per_seq((N_MOD, 1, D)),
            _const_spec((1, rows_s, D)),
            _group_mod_spec(layer, seqs),
            _const_spec((1, D)),
            _const_spec((D, 3 * D)),
            _const_spec((CONV_W, D)),
            _const_spec((D, D)),
            _const_spec((CONV_W - 1, seqs, D)),
        ] + c_in,
        out_specs=[tok(tm), per_seq((CONV_W - 1, D)), _whole_spec((1, rows_s, D)),
                   _whole_spec((CONV_W - 1, seqs, D))] + c_out,
        out_shape=[jax.ShapeDtypeStruct(xp.shape, F32),
                   jax.ShapeDtypeStruct((nb, CONV_W - 1, D), F32),
                   jax.ShapeDtypeStruct(xs.shape, F32),
                   jax.ShapeDtypeStruct((CONV_W - 1, seqs, D), F32)] + c_shapes,
        scratch_shapes=[pltpu.VMEM((CONV_W - 1, D), F32)],
        compiler_params=_params("arbitrary"),
        name="conv",
    )(*args, *c_args)


def kernel(x_prompt, x_sample, c_prompt, c_sample, state_mlstm_C, state_mlstm_n, state_mlstm_m,
           state_conv, ada_w, ada_b, norm_g, final_norm_g, mlstm_w_in, mlstm_b_gates,
           mlstm_head_g, mlstm_w_out, conv_w_in, conv_w, conv_w_out, mlp_w_up, mlp_w_down):
    assert ada_w.shape[0] == 2 and mlstm_w_in.shape[0] == 1 and conv_w_in.shape[0] == 1
    bp, tp, _ = x_prompt.shape
    bs, ts, _ = x_sample.shape
    assert ts == 4 and tp % 512 == 0

    mod = _ada(jnp.concatenate([c_sample, c_prompt], axis=0), ada_w, ada_b)
    mod_p = jnp.transpose(mod[:, :, bs:], (0, 2, 1, 3))[:, :, :, None]

    w_in0 = mlstm_w_in[0]
    w_row = w_in0[:, :NP].astype(BF)
    gate_pad = jnp.zeros((D, LANES - H), F32)
    w_col = jnp.concatenate([w_in0[:, HK:2 * HK] * (DK ** -0.5), w_in0[:, NP:NP + H], gate_pad,
                             w_in0[:, NP + H:], gate_pad], axis=1).T.astype(BF)
    b_g = jnp.pad(mlstm_b_gates[0].reshape(2, H), ((0, 0), (0, LANES - H)))
    wout0 = mlstm_w_out[0].astype(BF)

    x, s_p, m_p, up0, down0 = _mlstm_prompt(
        x_prompt, mod_p[0], norm_g[0, 0], w_row, w_col, mlstm_b_gates[0], mlstm_head_g[0], wout0,
        tm=1024, casts=((mlp_w_up, 0), (mlp_w_down, 0)))
    prompt_c = s_p[None, :, :, :, :DV]
    prompt_n = s_p[None, :, :, :, DV]
    prompt_m = m_p[None, :, :, 0]

    xs = jnp.transpose(x_sample, (1, 0, 2)).reshape(1, ts * bs, D)
    m0 = jnp.pad(state_mlstm_m[0], ((0, 0), (0, FG - H)))
    n0 = state_mlstm_n[0].reshape(bs, HK)
    y, n1, m1, c1 = _mlstm_sample(xs, mod, 0, norm_g[0, 0], w_row, w_col, b_g, m0, n0,
                                  mlstm_head_g[0], state_mlstm_C[0], t=ts, bb=16)
    sample_n = n1.reshape(bs, H, DK)[None]
    sample_m = m1[None, :, :H]

    x, xs, cwin, cwout = _mlp(x, mod_p[0], xs, mod, 0, norm_g[0, 1], up0, down0, tm=512, seqs=bs,
                              pre=(y.reshape(1, ts * bs, D), wout0),
                              casts=((conv_w_in, 0), (conv_w_out, 0)))
    x, conv_p, xs, conv_s, up1, down1 = _conv(
        x, mod_p[1], xs, mod, 1, norm_g[1, 0], cwin, conv_w[0], cwout,
        jnp.transpose(state_conv[0], (1, 0, 2)), tm=512, t=ts,
        casts=((mlp_w_up, 1), (mlp_w_down, 1)))
    y_prompt, ys = _mlp(x, mod_p[1], xs, mod, 1, norm_g[1, 1], up1, down1, tm=512, seqs=bs,
                        final_g=final_norm_g)
    prompt_conv = conv_p[None]
    y_sample = jnp.transpose(ys.reshape(ts, bs, D), (1, 0, 2))
    sample_conv = jnp.transpose(conv_s, (1, 0, 2))[None]

    return (y_prompt, y_sample, prompt_c, prompt_n, prompt_m, prompt_conv,
            c1[None], sample_n, sample_m, sample_conv)
```

```python
import functools

import jax
import numpy as np
import jax.numpy as jnp
from jax import lax
from jax.experimental import pallas as pl
from jax.experimental.pallas import tpu as pltpu

D = 1024
H = 8
DK = 64
DV = 128
HK = H * DK
NP = 2 * HK + 2 * D
FG = 128
LANES = 128
SUBLANES = 8
DFF = 4 * D
N_MOD = 6
CONV_W = 3
EPS = 1e-6
CHUNK = 128
FF_CHUNK = 1024
BF = jnp.bfloat16
F32 = jnp.float32
MIB = 1024 * 1024
VMEM_LIMIT_BYTES = 56 * MIB
VMEM_LIMIT_ADA = 16 * MIB
VMEM_LIMIT_MLP = 48 * MIB
VMEM_LIMIT_CONV = 40 * MIB


def _dot(a, b):
    return jnp.dot(a, b, preferred_element_type=F32)


def _dot_nt(a, b):
    return lax.dot_general(a, b, (((1,), (1,)), ((), ())), preferred_element_type=F32)


def _dot_tn(a, b):
    return lax.dot_general(a, b, (((0,), (0,)), ((), ())), preferred_element_type=F32)


def _rms(x):
    return x * lax.rsqrt(jnp.mean(x * x, axis=-1, keepdims=True) + EPS)


def _modulate(x, g, shift, scale):
    return _rms(x) * g * (1.0 + scale) + shift


def _log_sigmoid(x):
    return jnp.minimum(x, 0.0) - jnp.log1p(jnp.exp(-jnp.abs(x)))


def _scan_lanes(x, op, ident, n):
    lane = lax.broadcasted_iota(jnp.int32, x.shape, 1) % n
    d = 1
    while d < n:
        x = op(x, jnp.where(lane >= d, pltpu.roll(x, d, 1), ident))
        d *= 2
    return x


def _by_group(fn, xs, mods):
    rows, r = xs[0].shape[0], mods[0].shape[0]
    if r == 1 or r == rows:
        return fn(*xs, *mods)
    return jnp.concatenate(
        [fn(*[x[i:i + r] for x in xs], *mods) for i in range(0, rows, r)], axis=0)


def _split_dot(x, m_ref):
    hi = x.astype(BF)
    lo = (x - hi.astype(F32)).astype(BF)
    m = m_ref[...]
    return _dot(hi, m) + _dot(lo, m)


def _cast_specs(jobs, steps, slab_of):
    ins, outs, shapes, args = [], [], [], []
    for w, layer in jobs:
        _, r, c = w.shape
        slab = r // steps
        assert slab * steps == r and slab % (2 * SUBLANES) == 0
        ins.append(pl.BlockSpec((1, slab, c), lambda *g, layer=layer: (layer, slab_of(*g), 0)))
        outs.append(pl.BlockSpec((slab, c), lambda *g: (slab_of(*g), 0)))
        shapes.append(jax.ShapeDtypeStruct((r, c), BF))
        args.append(w)
    return ins, outs, shapes, args


def _carry_casts(body, n_in, n_out, n_cast):
    def kernel(*refs):
        a, b = n_in, n_in + n_cast
        c, d = b + n_out, b + n_out + n_cast
        for src, dst in zip(refs[a:b], refs[c:d]):
            dst[...] = src[0].astype(BF)
        body(*refs[:a], *refs[b:c], *refs[d:])
    return kernel


def _head_expansion(width):
    return jnp.asarray(np.arange(FG)[:, None] == np.arange(H * width)[None, :] // width, BF)


def _const_spec(shape):
    nd = len(shape)
    return pl.BlockSpec(shape, lambda *_: (0,) * nd, pipeline_mode=pl.Buffered(1))


def _params(*sem, vmem=VMEM_LIMIT_BYTES):
    return pltpu.CompilerParams(dimension_semantics=sem, vmem_limit_bytes=vmem)


def _ada_kernel(c_ref, w_ref, b_ref, o_ref):
    c = c_ref[...]
    sc = (c * jax.nn.sigmoid(c)).astype(BF)
    o_ref[0, 0] = _dot(sc, w_ref[0].astype(BF)) + b_ref[0]


def _ada(c_all, ada_w, ada_b):
    depth, _, n = ada_w.shape
    nb = c_all.shape[0]
    return pl.pallas_call(
        _ada_kernel,
        grid=(depth, n // D),
        in_specs=[
            pl.BlockSpec((nb, D), lambda i, j: (0, 0)),
            pl.BlockSpec((1, D, D), lambda i, j: (i, 0, j)),
            pl.BlockSpec((1, 1, D), lambda i, j: (i, 0, j)),
        ],
        out_specs=pl.BlockSpec((1, 1, nb, D), lambda i, j: (i, j, 0, 0)),
        out_shape=jax.ShapeDtypeStruct((depth, n // D, nb, D), F32),
        compiler_params=_params("arbitrary", "arbitrary", vmem=VMEM_LIMIT_ADA),
        name="ada",
    )(c_all, ada_w, ada_b.reshape(depth, 1, n))


def _mlp_tile(x_ref, mod_ref, g_ref, wu_ref, wd_ref, fg_ref, o_ref, y_ref=None, wo_ref=None):
    def residual(xr, fr, gate):
        return xr + gate * fr

    def norm(xr, shift, scale):
        return _modulate(xr, g_ref[...], shift, scale)

    x = x_ref[0]
    if y_ref is not None:
        x = _by_group(residual, (x, _dot(y_ref[0].astype(BF), wo_ref[...])), (mod_ref[0, 2],))
    hb = _by_group(norm, (x,), (mod_ref[0, 3], mod_ref[0, 4])).astype(BF)
    acc = jnp.zeros(x.shape, F32)
    for c in range(DFF // FF_CHUNK):
        cs = slice(c * FF_CHUNK, (c + 1) * FF_CHUNK)
        hid = jnp.maximum(_dot(hb, wu_ref[:, cs]), 0.0)
        acc = acc + _dot((hid * hid).astype(BF), wd_ref[cs, :])
    x = _by_group(residual, (x, acc), (mod_ref[0, 5],))
    if fg_ref is not None:
        x = _rms(x) * fg_ref[...]
    o_ref[0] = x


def _mlp_kernel(*refs, n_prompt, pre, final):
    refs = list(refs)
    xp_ref, modp_ref, xs_ref = refs[:3]
    refs = refs[3:]
    ys_ref = wo_ref = fg_ref = None
    if pre:
        ys_ref, wo_ref = refs[:2]
        refs = refs[2:]
    mods_ref, g_ref, wu_ref, wd_ref = refs[:4]
    refs = refs[4:]
    if final:
        fg_ref = refs.pop(0)
    xpo_ref, xso_ref = refs
    step = pl.program_id(0)

    @pl.when(step < n_prompt)
    def _():
        _mlp_tile(xp_ref, modp_ref, g_ref, wu_ref, wd_ref, fg_ref, xpo_ref)

    @pl.when(step == n_prompt)
    def _():
        _mlp_tile(xs_ref, mods_ref, g_ref, wu_ref, wd_ref, fg_ref, xso_ref, ys_ref, wo_ref)


def _group_mod_spec(layer, r):
    return pl.BlockSpec((1, N_MOD, r, D), lambda *_: (layer, 0, 0, 0),
                        pipeline_mode=pl.Buffered(1))


def _prompt_tiles(nb, nt):
    def seq_tile(step):
        c = jnp.minimum(step, nb * nt - 1)
        return c // nt, c % nt

    def tok(tm):
        return pl.BlockSpec((1, tm, D), lambda s: (*seq_tile(s), 0))

    def per_seq(shape):
        return pl.BlockSpec((1,) + shape, lambda s: (seq_tile(s)[0],) + (0,) * len(shape))

    return tok, per_seq


def _whole_spec(shape):
    return pl.BlockSpec(shape, lambda *_: (0,) * len(shape))


def _mlp(xp, mod_p, xs, mod, layer, g, w_up, w_down, *, tm, seqs, pre=None, final_g=None,
         casts=()):
    nb, t, _ = xp.shape
    nt = t // tm
    n_prompt = nb * nt
    rows_s = xs.shape[1]
    tok, per_seq = _prompt_tiles(nb, nt)
    args = [xp, mod_p, xs]
    specs = [tok(tm), per_seq((N_MOD, 1, D)), _const_spec((1, rows_s, D))]
    if pre is not None:
        args += list(pre)
        specs += [_const_spec((1, rows_s, D)), _const_spec((D, D))]
    args += [mod, g.reshape(1, D), w_up, w_down]
    specs += [_group_mod_spec(layer, seqs), _const_spec((1, D)), _const_spec((D, DFF)),
              _const_spec((DFF, D))]
    if final_g is not None:
        args.append(final_g.reshape(1, D))
        specs.append(_const_spec((1, D)))
    c_in, c_out, c_shapes, c_args = _cast_specs(
        casts, n_prompt, lambda s: jnp.minimum(s, n_prompt - 1))
    body = functools.partial(_mlp_kernel, n_prompt=n_prompt, pre=pre is not None,
                             final=final_g is not None)
    return pl.pallas_call(
        _carry_casts(body, len(args), 2, len(casts)),
        grid=(n_prompt + 1,),
        in_specs=specs + c_in,
        out_specs=[tok(tm), _whole_spec((1, rows_s, D))] + c_out,
        out_shape=[jax.ShapeDtypeStruct(xp.shape, F32), jax.ShapeDtypeStruct(xs.shape, F32)]
        + c_shapes,
        compiler_params=_params("arbitrary", vmem=VMEM_LIMIT_MLP),
        name="mlp",
    )(*args, *c_args)


def _mlstm_prompt_kernel(x_ref, mod_ref, g_ref, wrow_ref, wcol_ref, bg_ref, gh_ref, wout_ref,
                         xo_ref, s_ref, m_ref, y_scr, *, tm):
    @pl.when(pl.program_id(1) == 0)
    def _():
        s_ref[...] = jnp.zeros(s_ref.shape, F32)
        m_ref[...] = jnp.zeros(m_ref.shape, F32)

    x = x_ref[0]
    hb = _modulate(x, g_ref[...], mod_ref[0, 0], mod_ref[0, 1]).astype(BF)
    n = CHUNK
    w_t = jnp.concatenate([wcol_ref[0:HK, :], wcol_ref[HK:HK + H, :],
                           wcol_ref[HK + LANES:HK + LANES + H, :]], axis=0)
    pcol = _dot_nt(w_t, hb)
    ig_all = pcol[HK:HK + H] + bg_ref[0:H]
    fg_all = pcol[HK + H:HK + 2 * H] + bg_ref[H:2 * H]
    b_all = _scan_lanes(_log_sigmoid(fg_all), jnp.add, 0.0, n)
    r_all = ig_all - b_all
    cm_all = _scan_lanes(r_all, jnp.maximum, -jnp.inf, n)
    prow = _dot(hb, wrow_ref[...])
    causal = (lax.broadcasted_iota(jnp.int32, (n, n), 0)
              >= lax.broadcasted_iota(jnp.int32, (n, n), 1))
    zeros_k = jnp.zeros((DK, n), BF)
    zeros_s = jnp.zeros((DK, 2 * DV), BF)
    zeros_v = jnp.zeros((n, DV), BF)
    ones_v = jnp.ones((n, DV), BF)
    pad = jnp.zeros((n - 2 * H, n), F32)

    for c in range(tm // n):
        ts = slice(c * n, (c + 1) * n)
        r = r_all[:, ts]
        m0 = m_ref[0]
        mx = jnp.maximum(m0, cm_all[:, ts])
        m = b_all[:, ts] + mx
        e_neg = jnp.exp(-m)
        mx_last = jnp.broadcast_to(mx[:, n - 1:n], (H, n))
        a = jnp.exp(r - mx_last)
        decay = jnp.exp(m0 - mx_last)
        m_ref[0] = jnp.broadcast_to(m[:, n - 1:n], (H, n))
        cols = jnp.concatenate([mx, e_neg, pad], axis=0).T

        for h in range(H):
            lo = h % 2 == 0
            mx_b = jnp.broadcast_to(cols[:, h:h + 1], (n, n))
            floor_b = jnp.broadcast_to(cols[:, H + h:H + h + 1], (n, DV))
            dec = jnp.exp(jnp.where(causal, r[h:h + 1, :] - mx_b, -jnp.inf))
            e_int_b = jnp.exp(m0[h:h + 1, :] - mx_b)
            if lo:
                qp = prow[ts, (h // 2) * 2 * DK:(h // 2 + 1) * 2 * DK]
                qpb = qp.astype(BF)
            kt = pcol[h * DK:(h + 1) * DK, ts]
            ktb = kt.astype(BF)
            s = _dot(qpb, jnp.concatenate([ktb, zeros_k] if lo else [zeros_k, ktb], axis=0)) * dec
            st = s_ref[0, h]
            stb = st.astype(BF)
            v = prow[ts, HK + h * DV:HK + (h + 1) * DV].astype(BF)
            rhs = jnp.concatenate(
                [jnp.concatenate([v, zeros_v], axis=1)]
                + ([stb, zeros_s] if lo else [zeros_s, stb]), axis=0)
            lhs = jnp.concatenate([s.astype(BF), (qp * e_int_b).astype(BF)], axis=1)
            res = _dot(lhs, rhs)
            den = jnp.sum(s, axis=-1, keepdims=True) + res[:, DV:]
            hh = res[:, :DV] / jnp.maximum(jnp.abs(den), floor_b)
            hh = hh * lax.rsqrt(jnp.mean(hh * hh, axis=-1, keepdims=True) + EPS)
            y_scr[ts, h * DV:(h + 1) * DV] = hh
            ka = (kt * a[h:h + 1, :]).astype(BF)
            dc = decay[h:h + 1, :]
            s_ref[0, h] = (jnp.concatenate([dc, dc], axis=1) * st
                           + _dot(ka, jnp.concatenate([v, ones_v], axis=1)))

    o = prow[:, HK + D:]
    y = (y_scr[...] * gh_ref[...] * jax.nn.sigmoid(o)).astype(BF)
    xo_ref[0] = x + mod_ref[0, 2] * _dot(y, wout_ref[...])


def _mlstm_prompt(x, mod, g, w_row, w_col, b_g, g_head, w_out, *, tm, casts=()):
    nb, t, _ = x.shape
    nt = t // tm
    tok = pl.BlockSpec((1, tm, D), lambda b, i: (b, i, 0))
    args = (x, mod, g.reshape(1, D), w_row, w_col,
            jnp.broadcast_to(b_g.reshape(2 * H, 1), (2 * H, tm)), g_head.reshape(1, D), w_out)
    c_in, c_out, c_shapes, c_args = _cast_specs(casts, nb * nt, lambda b, i: b * nt + i)
    return pl.pallas_call(
        _carry_casts(functools.partial(_mlstm_prompt_kernel, tm=tm), len(args), 3, len(casts)),
        grid=(nb, nt),
        in_specs=[
            tok,
            pl.BlockSpec((1, N_MOD, 1, D), lambda b, i: (b, 0, 0, 0)),
            _const_spec((1, D)),
            _const_spec((D, HK + 2 * D)),
            _const_spec((HK + 2 * LANES, D)),
            _const_spec((2 * H, tm)),
            _const_spec((1, D)),
            _const_spec((D, D)),
        ] + c_in,
        out_specs=[
            tok,
            pl.BlockSpec((1, H, DK, 2 * DV), lambda b, i: (b, 0, 0, 0)),
            pl.BlockSpec((1, H, CHUNK), lambda b, i: (b, 0, 0)),
        ] + c_out,
        out_shape=[
            jax.ShapeDtypeStruct(x.shape, F32),
            jax.ShapeDtypeStruct((nb, H, DK, 2 * DV), F32),
            jax.ShapeDtypeStruct((nb, H, CHUNK), F32),
        ] + c_shapes,
        scratch_shapes=[pltpu.VMEM((tm, D), F32)],
        compiler_params=_params("arbitrary", "arbitrary"),
        name="mlstm_prompt",
    )(*args, *c_args)


def _mlstm_sample_pre_body(x_ref, mod_ref, g_ref, wrow_ref, wcol_ref, bg_ref, m0_ref, n0_ref,
                           gh_ref, seg_ref, e64_ref, e128_ref,
                           q_ref, ka_ref, v_ref, dec_ref, numi_ref, rd_ref, eint_ref, go_ref,
                           n1_ref, m1_ref, *, t, nb):
    def norm(xr, shift, scale):
        return _modulate(xr, g_ref[...], shift, scale)

    hb = _by_group(norm, (x_ref[0],), (mod_ref[0, 0], mod_ref[0, 1])).astype(BF)
    p = _dot(hb, wrow_ref[...])
    k_all = _dot_nt(hb, wcol_ref[0:HK, :])
    ig_all = _dot_nt(hb, wcol_ref[HK:HK + LANES, :]) + bg_ref[0:1]
    fg_all = _dot_nt(hb, wcol_ref[HK + LANES:HK + 2 * LANES, :]) + bg_ref[1:2]

    def rows(i):
        return slice(i * nb, (i + 1) * nb)

    q = [p[rows(i), 0:HK] for i in range(t)]
    k = [k_all[rows(i)] for i in range(t)]
    v = [p[rows(i), HK:HK + D] for i in range(t)]
    m0 = m0_ref[...]
    n0 = n0_ref[...]

    b, r, mx = [], [], []
    for i in range(t):
        lf = _log_sigmoid(fg_all[rows(i)])
        b.append(lf if i == 0 else b[-1] + lf)
        r.append(ig_all[rows(i)] - b[i])
        cm = r[i] if i == 0 else jnp.maximum(cm, r[i])
        mx.append(jnp.maximum(m0, cm))
    mx_last = mx[t - 1]
    decay = jnp.exp(m0 - mx_last)
    m1_ref[...] = b[t - 1] + mx_last
    dec_x = _split_dot(decay, e128_ref)
    n1 = _split_dot(decay, e64_ref) * n0

    for i in range(t):
        ka = k[i] * _split_dot(jnp.exp(r[i] - mx_last), e64_ref)
        n1 = n1 + ka
        seq_rows = pl.ds(i, nb, stride=t)
        for c in range(HK // LANES):
            cs = slice(c * LANES, (c + 1) * LANES)
            ka_ref[c, seq_rows, :] = ka[:, cs]
            q_ref[c, seq_rows, :] = q[i][:, cs]
        for c in range(D // LANES):
            cs = slice(c * LANES, (c + 1) * LANES)
            v_ref[c, seq_rows, :] = v[i][:, cs]
            dec_ref[c, seq_rows, :] = dec_x[:, cs]
        e_int = jnp.exp(m0 - mx[i])
        den = e_int * _split_dot(q[i] * n0, seg_ref)
        numi = jnp.zeros((nb, D), F32)
        for s in range(i + 1):
            w = _split_dot(q[i] * k[s], seg_ref) * jnp.exp(jnp.minimum(r[s] - mx[i], 0.0))
            den = den + w
            numi = numi + _split_dot(w, e128_ref) * v[s]
        numi_ref[rows(i), :] = numi
        rd_ref[rows(i), :] = 1.0 / jnp.maximum(jnp.abs(den), jnp.exp(-(b[i] + mx[i])))
        eint_ref[rows(i), :] = e_int
        go_ref[rows(i), :] = jax.nn.sigmoid(p[rows(i), HK + D:]) * gh_ref[...]
    n1_ref[...] = n1


def _mlstm_sample_state_body(q_ref, ka_ref, v_ref, dec_ref, c0_ref, inter_ref, c1_ref, *, bb, t,
                             row0):
    per = SUBLANES // t
    seq_of_row = lax.broadcasted_iota(jnp.int32, (SUBLANES, 1), 0) // t
    zeros_c = jnp.zeros((DK, DV), BF)

    def tile(i, carry):
        rs = pl.ds(pl.multiple_of(row0 + i * SUBLANES, SUBLANES), SUBLANES)
        for h in range(H):
            lo = h % 2 == 0
            half = slice((h % 2) * DK, (h % 2 + 1) * DK)
            qp = q_ref[h // 2, rs, :].astype(BF)
            kap = ka_ref[h // 2, rs, :]
            vh = v_ref[h, rs, :].astype(BF)
            dech = dec_ref[h, rs, :]
            res = None
            for w in range(per):
                seq = i * per + w
                c0 = c0_ref[seq, h]
                c0b = c0.astype(BF)
                rw = _dot(qp, jnp.concatenate([c0b, zeros_c] if lo else [zeros_c, c0b], axis=0))
                res = rw if res is None else jnp.where(seq_of_row == w, rw, res)
                kaw = jnp.where(seq_of_row == w, kap, 0.0).astype(BF)
                c1_ref[seq, h] = dech[w * t:w * t + 1, :] * c0 + _dot_tn(kaw, vh)[half, :]
            inter_ref[h, rs, :] = res
        return carry

    lax.fori_loop(0, bb // per, tile, 0, unroll=4)


def _mlstm_sample_post_body(inter_ref, numi_ref, rd_ref, eint_ref, go_ref, e128_ref, seg_ref,
                            y_ref, *, t, nb):
    for i in range(t):
        rs = slice(i * nb, (i + 1) * nb)
        inter = jnp.concatenate(
            [inter_ref[h, pl.ds(i, nb, stride=t), :] for h in range(H)], axis=1)
        num = numi_ref[rs, :] + _split_dot(eint_ref[rs, :], e128_ref) * inter
        hh = num * _split_dot(rd_ref[rs, :], e128_ref)
        ms = _split_dot(hh * hh, seg_ref) * (1.0 / DV)
        y_ref[rs, :] = hh * _split_dot(lax.rsqrt(ms + EPS), e128_ref) * go_ref[rs, :]


def _mlstm_sample_kernel(x_ref, mod_ref, g_ref, wrow_ref, wcol_ref, bg_ref, m0_ref, n0_ref, gh_ref,
                         seg64_ref, e64_ref, e128_ref, seg128_ref, c0_ref,
                         y_ref, n1_ref, m1_ref, c1_ref,
                         q_s, ka_s, v_s, dec_s, inter_s, numi_s, rd_s, eint_s, go_s, *, t, nb, bb):
    step = pl.program_id(0)

    @pl.when(step == 0)
    def _():
        _mlstm_sample_pre_body(x_ref, mod_ref, g_ref, wrow_ref, wcol_ref, bg_ref, m0_ref, n0_ref,
                               gh_ref, seg64_ref, e64_ref, e128_ref,
                               q_s, ka_s, v_s, dec_s, numi_s, rd_s, eint_s, go_s, n1_ref, m1_ref,
                               t=t, nb=nb)

    _mlstm_sample_state_body(q_s, ka_s, v_s, dec_s, c0_ref, inter_s, c1_ref, bb=bb, t=t,
                             row0=step * (bb * t))

    @pl.when(step == pl.num_programs(0) - 1)
    def _():
        _mlstm_sample_post_body(inter_s, numi_s, rd_s, eint_s, go_s, e128_ref, seg128_ref, y_ref,
                                t=t, nb=nb)


def _mlstm_sample(x, mod, layer, g, w_row, w_col, b_g, m0, n0, g_head, c0, *, t, bb):
    rows = x.shape[1]
    nb = rows // t
    assert SUBLANES % t == 0 and bb % (SUBLANES // t) == 0 and nb % bb == 0
    e128 = _head_expansion(DV)
    e64 = _head_expansion(DK)
    state = pl.BlockSpec((bb, H, DK, DV), lambda i: (i, 0, 0, 0))
    return pl.pallas_call(
        functools.partial(_mlstm_sample_kernel, t=t, nb=nb, bb=bb),
        grid=(nb // bb,),
        in_specs=[
            _const_spec((1, rows, D)),
            _group_mod_spec(layer, nb),
            _const_spec((1, D)),
            _const_spec((D, HK + 2 * D)),
            _const_spec((HK + 2 * LANES, D)),
            _const_spec((2, LANES)),
            _const_spec((nb, FG)),
            _const_spec((nb, HK)),
            _const_spec((1, D)),
            _const_spec((HK, FG)),
            _const_spec((FG, HK)),
            _const_spec((FG, D)),
            _const_spec((D, FG)),
            state,
        ],
        out_specs=[_whole_spec((rows, D)), _whole_spec((nb, HK)), _whole_spec((nb, FG)), state],
        out_shape=[jax.ShapeDtypeStruct((rows, D), F32), jax.ShapeDtypeStruct((nb, HK), F32),
                   jax.ShapeDtypeStruct((nb, FG), F32), jax.ShapeDtypeStruct(c0.shape, F32)],
        scratch_shapes=[
            pltpu.VMEM((HK // LANES, rows, LANES), F32),
            pltpu.VMEM((HK // LANES, rows, LANES), F32),
            pltpu.VMEM((D // LANES, rows, LANES), F32),
            pltpu.VMEM((D // LANES, rows, LANES), F32),
            pltpu.VMEM((D // LANES, rows, LANES), F32),
            pltpu.VMEM((rows, D), F32),
            pltpu.VMEM((rows, FG), F32),
            pltpu.VMEM((rows, FG), F32),
            pltpu.VMEM((rows, D), F32),
        ],
        compiler_params=_params("arbitrary"),
        name="mlstm_sample",
    )(x, mod, g.reshape(1, D), w_row, w_col, b_g, m0, n0, g_head.reshape(1, D),
      e64.T, e64, e128, e128.T, c0)


def _conv_taps(u, prev2, prev1, wc_ref):
    return wc_ref[0:1] * prev2 + wc_ref[1:2] * prev1 + wc_ref[2:3] * u


def _conv_prompt_tile(x_ref, mod_ref, g_ref, win_ref, wc_ref, wout_ref, xo_ref, st_ref, carry_scr,
                      first):
    @pl.when(first)
    def _():
        carry_scr[...] = jnp.zeros(carry_scr.shape, F32)

    x = x_ref[0]
    tm = x.shape[0]
    hb = _modulate(x, g_ref[...], mod_ref[0, 0], mod_ref[0, 1]).astype(BF)
    u = _dot(hb, win_ref[:, D:2 * D]) * _dot(hb, win_ref[:, 2 * D:])
    row = lax.broadcasted_iota(jnp.int32, (tm, D), 0)
    c0 = carry_scr[0:1]
    c1 = carry_scr[1:2]
    prev1 = jnp.where(row == 0, c1, pltpu.roll(u, 1, 0))
    prev2 = jnp.where(row == 0, c0, jnp.where(row == 1, c1, pltpu.roll(u, 2, 0)))
    y = _conv_taps(u, prev2, prev1, wc_ref)
    bg = _dot(hb, win_ref[:, :D])
    xo_ref[0] = x + mod_ref[0, 2] * _dot((bg * y).astype(BF), wout_ref[...])
    carry_scr[...] = u[tm - 2:tm]
    st_ref[0] = u[tm - 2:tm]


def _conv_sample_tile(x_ref, mod_ref, g_ref, win_ref, wc_ref, wout_ref, buf_ref, xo_ref, st_ref,
                      *, t, nb):
    def norm(xr, shift, scale):
        return _modulate(xr, g_ref[...], shift, scale)

    def residual(xr, fr, gate):
        return xr + gate * fr

    x = x_ref[0]
    hb = _by_group(norm, (x,), (mod_ref[0, 0], mod_ref[0, 1])).astype(BF)
    u = _dot(hb, win_ref[:, D:2 * D]) * _dot(hb, win_ref[:, 2 * D:])
    full = [buf_ref[j] for j in range(CONV_W - 1)] + [u[i * nb:(i + 1) * nb] for i in range(t)]
    y = jnp.concatenate([_conv_taps(full[i + 2], full[i], full[i + 1], wc_ref) for i in range(t)],
                        axis=0)
    bg = _dot(hb, win_ref[:, :D])
    out = _dot((bg * y).astype(BF), wout_ref[...])
    xo_ref[0] = _by_group(residual, (x, out), (mod_ref[0, 2],))
    for j in range(CONV_W - 1):
        st_ref[j] = full[t + j]


def _conv_kernel(xp_ref, modp_ref, xs_ref, mods_ref, g_ref, win_ref, wc_ref, wout_ref, buf_ref,
                 xpo_ref, stp_ref, xso_ref, sts_ref, carry_scr, *, n_prompt, nt, t, nb):
    step = pl.program_id(0)

    @pl.when(step < n_prompt)
    def _():
        _conv_prompt_tile(xp_ref, modp_ref, g_ref, win_ref, wc_ref, wout_ref, xpo_ref, stp_ref,
                          carry_scr, step % nt == 0)

    @pl.when(step == n_prompt)
    def _():
        _conv_sample_tile(xs_ref, mods_ref, g_ref, win_ref, wc_ref, wout_ref, buf_ref, xso_ref,
                          sts_ref, t=t, nb=nb)


def _conv(xp, mod_p, xs, mod, layer, g, w_in, w_conv, w_out, buf, *, tm, t, casts=()):
    nb, tp, _ = xp.shape
    nt = tp // tm
    n_prompt = nb * nt
    rows_s = xs.shape[1]
    seqs = rows_s // t
    tok, per_seq = _prompt_tiles(nb, nt)
    args = (xp, mod_p, xs, mod, g.reshape(1, D), w_in, w_conv, w_out, buf)
    c_in, c_out, c_shapes, c_args = _cast_specs(
        casts, n_prompt, lambda s: jnp.minimum(s, n_prompt - 1))
    body = functools.partial(_conv_kernel, n_prompt=n_prompt, nt=nt, t=t, nb=seqs)
    return pl.pallas_call(
        _carry_casts(body, len(args), 4, len(casts)),
        grid=(n_prompt + 1,),
        in_specs=[
            tok(tm),
            per_seq((N_MOD, 1, D)),
            _const_spec((1, rows_s, D)),
            _group_mod_spec(layer, seqs),
            _const_spec((1, D)),
            _const_spec((D, 3 * D)),
            _const_spec((CONV_W, D)),
            _const_spec((D, D)),
            _const_spec((CONV_W - 1, seqs, D)),
        ] + c_in,
        out_specs=[tok(tm), per_seq((CONV_W - 1, D)), _whole_spec((1, rows_s, D)),
                   _whole_spec((CONV_W - 1, seqs, D))] + c_out,
        out_shape=[jax.ShapeDtypeStruct(xp.shape, F32),
                   jax.ShapeDtypeStruct((nb, CONV_W - 1, D), F32),
                   jax.ShapeDtypeStruct(xs.shape, F32),
                   jax.ShapeDtypeStruct((CONV_W - 1, seqs, D), F32)] + c_shapes,
        scratch_shapes=[pltpu.VMEM((CONV_W - 1, D), F32)],
        compiler_params=_params("arbitrary", vmem=VMEM_LIMIT_CONV),
        name="conv",
    )(*args, *c_args)


def kernel(x_prompt, x_sample, c_prompt, c_sample, state_mlstm_C, state_mlstm_n, state_mlstm_m,
           state_conv, ada_w, ada_b, norm_g, final_norm_g, mlstm_w_in, mlstm_b_gates,
           mlstm_head_g, mlstm_w_out, conv_w_in, conv_w, conv_w_out, mlp_w_up, mlp_w_down):
    assert ada_w.shape[0] == 2 and mlstm_w_in.shape[0] == 1 and conv_w_in.shape[0] == 1
    bp, tp, _ = x_prompt.shape
    bs, ts, _ = x_sample.shape
    assert ts == 4 and tp % 512 == 0

    mod = _ada(jnp.concatenate([c_sample, c_prompt], axis=0), ada_w, ada_b)
    mod_p = jnp.transpose(mod[:, :, bs:], (0, 2, 1, 3))[:, :, :, None]

    w_in0 = mlstm_w_in[0]
    w_row = jnp.concatenate([w_in0[:, :HK], w_in0[:, 2 * HK:NP]], axis=1).astype(BF)
    gate_pad = jnp.zeros((D, LANES - H), F32)
    w_col = jnp.concatenate([w_in0[:, HK:2 * HK] * (DK ** -0.5), w_in0[:, NP:NP + H], gate_pad,
                             w_in0[:, NP + H:], gate_pad], axis=1).T.astype(BF)
    b_g = jnp.pad(mlstm_b_gates[0].reshape(2, H), ((0, 0), (0, LANES - H)))
    wout0 = mlstm_w_out[0].astype(BF)

    x, s_p, m_p, up0, down0 = _mlstm_prompt(
        x_prompt, mod_p[0], norm_g[0, 0], w_row, w_col, mlstm_b_gates[0], mlstm_head_g[0], wout0,
        tm=1024, casts=((mlp_w_up, 0), (mlp_w_down, 0)))
    prompt_c = s_p[None, :, :, :, :DV]
    prompt_n = s_p[None, :, :, :, DV]
    prompt_m = m_p[None, :, :, 0]

    xs = jnp.transpose(x_sample, (1, 0, 2)).reshape(1, ts * bs, D)
    m0 = jnp.pad(state_mlstm_m[0], ((0, 0), (0, FG - H)))
    n0 = state_mlstm_n[0].reshape(bs, HK)
    y, n1, m1, c1 = _mlstm_sample(xs, mod, 0, norm_g[0, 0], w_row, w_col, b_g, m0, n0,
                                  mlstm_head_g[0], state_mlstm_C[0], t=ts, bb=16)
    sample_n = n1.reshape(bs, H, DK)[None]
    sample_m = m1[None, :, :H]

    x, xs, cwin, cwout = _mlp(x, mod_p[0], xs, mod, 0, norm_g[0, 1], up0, down0, tm=512, seqs=bs,
                              pre=(y.reshape(1, ts * bs, D), wout0),
                              casts=((conv_w_in, 0), (conv_w_out, 0)))
    x, conv_p, xs, conv_s, up1, down1 = _conv(
        x, mod_p[1], xs, mod, 1, norm_g[1, 0], cwin, conv_w[0], cwout,
        jnp.transpose(state_conv[0], (1, 0, 2)), tm=512, t=ts,
        casts=((mlp_w_up, 1), (mlp_w_down, 1)))
    y_prompt, ys = _mlp(x, mod_p[1], xs, mod, 1, norm_g[1, 1], up1, down1, tm=512, seqs=bs,
                        final_g=final_norm_g)
    prompt_conv = conv_p[None]
    y_sample = jnp.transpose(ys.reshape(ts, bs, D), (1, 0, 2))
    sample_conv = jnp.transpose(conv_s, (1, 0, 2))[None]

    return (y_prompt, y_sample, prompt_c, prompt_n, prompt_m, prompt_conv,
            c1[None], sample_n, sample_m, sample_conv)
```

```python
import functools

import jax
import numpy as np
import jax.numpy as jnp
from jax import lax
from jax.experimental import pallas as pl
from jax.experimental.pallas import tpu as pltpu

D = 1024
H = 8
DK = 64
DV = 128
HK = H * DK
NP = 2 * HK + 2 * D
FG = 128
LANES = 128
SUBLANES = 8
DFF = 4 * D
N_MOD = 6
CONV_W = 3
EPS = 1e-6
CHUNK = 128
FF_CHUNK = 1024
BF = jnp.bfloat16
F32 = jnp.float32
MIB = 1024 * 1024
VMEM_LIMIT_BYTES = 54 * MIB
VMEM_LIMIT_ADA = 12 * MIB
VMEM_LIMIT_MLP = 44 * MIB
VMEM_LIMIT_CONV = 34 * MIB


def _dot(a, b):
    return jnp.dot(a, b, preferred_element_type=F32)


def _dot_nt(a, b):
    return lax.dot_general(a, b, (((1,), (1,)), ((), ())), preferred_element_type=F32)


def _dot_tn(a, b):
    return lax.dot_general(a, b, (((0,), (0,)), ((), ())), preferred_element_type=F32)


def _rms(x):
    return x * lax.rsqrt(jnp.mean(x * x, axis=-1, keepdims=True) + EPS)


def _modulate(x, g, shift, scale):
    return _rms(x) * g * (1.0 + scale) + shift


def _log_sigmoid(x):
    return jnp.minimum(x, 0.0) - jnp.log1p(jnp.exp(-jnp.abs(x)))


def _scan_lanes(x, op, ident, n):
    lane = lax.broadcasted_iota(jnp.int32, x.shape, 1) % n
    d = 1
    while d < n:
        x = op(x, jnp.where(lane >= d, pltpu.roll(x, d, 1), ident))
        d *= 2
    return x


def _by_group(fn, xs, mods):
    rows, r = xs[0].shape[0], mods[0].shape[0]
    if r == 1 or r == rows:
        return fn(*xs, *mods)
    return jnp.concatenate(
        [fn(*[x[i:i + r] for x in xs], *mods) for i in range(0, rows, r)], axis=0)


def _split_dot(x, m_ref):
    hi = x.astype(BF)
    lo = (x - hi.astype(F32)).astype(BF)
    m = m_ref[...]
    return _dot(hi, m) + _dot(lo, m)


def _cast_specs(jobs, steps, slab_of):
    ins, outs, shapes, args = [], [], [], []
    for w, layer in jobs:
        _, r, c = w.shape
        slab = r // steps
        assert slab * steps == r and slab % (2 * SUBLANES) == 0
        ins.append(pl.BlockSpec((1, slab, c), lambda *g, layer=layer: (layer, slab_of(*g), 0)))
        outs.append(pl.BlockSpec((slab, c), lambda *g: (slab_of(*g), 0)))
        shapes.append(jax.ShapeDtypeStruct((r, c), BF))
        args.append(w)
    return ins, outs, shapes, args


def _carry_casts(body, n_in, n_out, n_cast):
    def kernel(*refs):
        a, b = n_in, n_in + n_cast
        c, d = b + n_out, b + n_out + n_cast
        for src, dst in zip(refs[a:b], refs[c:d]):
            dst[...] = src[0].astype(BF)
        body(*refs[:a], *refs[b:c], *refs[d:])
    return kernel


def _head_expansion(width):
    return jnp.asarray(np.arange(FG)[:, None] == np.arange(H * width)[None, :] // width, BF)


def _const_spec(shape):
    nd = len(shape)
    return pl.BlockSpec(shape, lambda *_: (0,) * nd, pipeline_mode=pl.Buffered(1))


def _params(*sem, vmem=VMEM_LIMIT_BYTES):
    return pltpu.CompilerParams(dimension_semantics=sem, vmem_limit_bytes=vmem)


def _ada_kernel(c_ref, w_ref, b_ref, o_ref):
    c = c_ref[...]
    sc = (c * jax.nn.sigmoid(c)).astype(BF)
    o_ref[0, 0] = _dot(sc, w_ref[0].astype(BF)) + b_ref[0]


def _ada(c_all, ada_w, ada_b):
    depth, _, n = ada_w.shape
    nb = c_all.shape[0]
    return pl.pallas_call(
        _ada_kernel,
        grid=(depth, n // D),
        in_specs=[
            pl.BlockSpec((nb, D), lambda i, j: (0, 0)),
            pl.BlockSpec((1, D, D), lambda i, j: (i, 0, j)),
            pl.BlockSpec((1, 1, D), lambda i, j: (i, 0, j)),
        ],
        out_specs=pl.BlockSpec((1, 1, nb, D), lambda i, j: (i, j, 0, 0)),
        out_shape=jax.ShapeDtypeStruct((depth, n // D, nb, D), F32),
        compiler_params=_params("arbitrary", "arbitrary", vmem=VMEM_LIMIT_ADA),
        name="ada",
    )(c_all, ada_w, ada_b.reshape(depth, 1, n))


def _mlp_tile(x_ref, mod_ref, g_ref, wu_ref, wd_ref, fg_ref, o_ref, y_ref=None, wo_ref=None):
    def residual(xr, fr, gate):
        return xr + gate * fr

    def norm(xr, shift, scale):
        return _modulate(xr, g_ref[...], shift, scale)

    x = x_ref[0]
    if y_ref is not None:
        x = _by_group(residual, (x, _dot(y_ref[0].astype(BF), wo_ref[...])), (mod_ref[0, 2],))
    hb = _by_group(norm, (x,), (mod_ref[0, 3], mod_ref[0, 4])).astype(BF)
    acc = jnp.zeros(x.shape, F32)
    for c in range(DFF // FF_CHUNK):
        cs = slice(c * FF_CHUNK, (c + 1) * FF_CHUNK)
        hid = jnp.maximum(_dot(hb, wu_ref[:, cs]), 0.0)
        acc = acc + _dot((hid * hid).astype(BF), wd_ref[cs, :])
    x = _by_group(residual, (x, acc), (mod_ref[0, 5],))
    if fg_ref is not None:
        x = _rms(x) * fg_ref[...]
    o_ref[0] = x


def _mlp_kernel(*refs, n_prompt, pre, final):
    refs = list(refs)
    xp_ref, modp_ref, xs_ref = refs[:3]
    refs = refs[3:]
    ys_ref = wo_ref = fg_ref = None
    if pre:
        ys_ref, wo_ref = refs[:2]
        refs = refs[2:]
    mods_ref, g_ref, wu_ref, wd_ref = refs[:4]
    refs = refs[4:]
    if final:
        fg_ref = refs.pop(0)
    xpo_ref, xso_ref = refs
    step = pl.program_id(0)

    @pl.when(step < n_prompt)
    def _():
        _mlp_tile(xp_ref, modp_ref, g_ref, wu_ref, wd_ref, fg_ref, xpo_ref)

    @pl.when(step == n_prompt)
    def _():
        _mlp_tile(xs_ref, mods_ref, g_ref, wu_ref, wd_ref, fg_ref, xso_ref, ys_ref, wo_ref)


def _group_mod_spec(layer, r):
    return pl.BlockSpec((1, N_MOD, r, D), lambda *_: (layer, 0, 0, 0),
                        pipeline_mode=pl.Buffered(1))


def _prompt_tiles(nb, nt):
    def seq_tile(step):
        c = jnp.minimum(step, nb * nt - 1)
        return c // nt, c % nt

    def tok(tm):
        return pl.BlockSpec((1, tm, D), lambda s: (*seq_tile(s), 0))

    def per_seq(shape):
        return pl.BlockSpec((1,) + shape, lambda s: (seq_tile(s)[0],) + (0,) * len(shape))

    return tok, per_seq


def _whole_spec(shape):
    return pl.BlockSpec(shape, lambda *_: (0,) * len(shape))


def _mlp(xp, mod_p, xs, mod, layer, g, w_up, w_down, *, tm, seqs, pre=None, final_g=None,
         casts=()):
    nb, t, _ = xp.shape
    nt = t // tm
    n_prompt = nb * nt
    rows_s = xs.shape[1]
    tok, per_seq = _prompt_tiles(nb, nt)
    args = [xp, mod_p, xs]
    specs = [tok(tm), per_seq((N_MOD, 1, D)), _const_spec((1, rows_s, D))]
    if pre is not None:
        args += list(pre)
        specs += [_const_spec((1, rows_s, D)), _const_spec((D, D))]
    args += [mod, g.reshape(1, D), w_up, w_down]
    specs += [_group_mod_spec(layer, seqs), _const_spec((1, D)), _const_spec((D, DFF)),
              _const_spec((DFF, D))]
    if final_g is not None:
        args.append(final_g.reshape(1, D))
        specs.append(_const_spec((1, D)))
    c_in, c_out, c_shapes, c_args = _cast_specs(
        casts, n_prompt, lambda s: jnp.minimum(s, n_prompt - 1))
    body = functools.partial(_mlp_kernel, n_prompt=n_prompt, pre=pre is not None,
                             final=final_g is not None)
    return pl.pallas_call(
        _carry_casts(body, len(args), 2, len(casts)),
        grid=(n_prompt + 1,),
        in_specs=specs + c_in,
        out_specs=[tok(tm), _whole_spec((1, rows_s, D))] + c_out,
        out_shape=[jax.ShapeDtypeStruct(xp.shape, F32), jax.ShapeDtypeStruct(xs.shape, F32)]
        + c_shapes,
        compiler_params=_params("arbitrary", vmem=VMEM_LIMIT_MLP),
        name="mlp",
    )(*args, *c_args)


def _mlstm_prompt_kernel(x_ref, mod_ref, g_ref, wrow_ref, wcol_ref, bg_ref, gh_ref, wout_ref,
                         xo_ref, s_ref, m_ref, y_scr, *, tm):
    @pl.when(pl.program_id(1) == 0)
    def _():
        s_ref[...] = jnp.zeros(s_ref.shape, F32)
        m_ref[...] = jnp.zeros(m_ref.shape, F32)

    x = x_ref[0]
    hb = _modulate(x, g_ref[...], mod_ref[0, 0], mod_ref[0, 1]).astype(BF)
    n = CHUNK
    w_t = jnp.concatenate([wcol_ref[0:HK, :], wcol_ref[HK:HK + H, :],
                           wcol_ref[HK + LANES:HK + LANES + H, :]], axis=0)
    pcol = _dot_nt(w_t, hb)
    ig_all = pcol[HK:HK + H] + bg_ref[0:H]
    fg_all = pcol[HK + H:HK + 2 * H] + bg_ref[H:2 * H]
    b_all = _scan_lanes(_log_sigmoid(fg_all), jnp.add, 0.0, n)
    r_all = ig_all - b_all
    cm_all = _scan_lanes(r_all, jnp.maximum, -jnp.inf, n)
    prow = _dot(hb, wrow_ref[...])
    causal = (lax.broadcasted_iota(jnp.int32, (n, n), 0)
              >= lax.broadcasted_iota(jnp.int32, (n, n), 1))
    zeros_k = jnp.zeros((DK, n), BF)
    zeros_s = jnp.zeros((DK, 2 * DV), BF)
    zeros_v = jnp.zeros((n, DV), BF)
    ones_v = jnp.ones((n, DV), BF)
    pad = jnp.zeros((n - 2 * H, n), F32)

    for c in range(tm // n):
        ts = slice(c * n, (c + 1) * n)
        r = r_all[:, ts]
        m0 = m_ref[0]
        mx = jnp.maximum(m0, cm_all[:, ts])
        m = b_all[:, ts] + mx
        e_neg = jnp.exp(-m)
        mx_last = jnp.broadcast_to(mx[:, n - 1:n], (H, n))
        a = jnp.exp(r - mx_last)
        decay = jnp.exp(m0 - mx_last)
        m_ref[0] = jnp.broadcast_to(m[:, n - 1:n], (H, n))
        cols = jnp.concatenate([mx, e_neg, pad], axis=0).T

        for h in range(H):
            lo = h % 2 == 0
            mx_b = jnp.broadcast_to(cols[:, h:h + 1], (n, n))
            floor_b = jnp.broadcast_to(cols[:, H + h:H + h + 1], (n, DV))
            dec = jnp.exp(jnp.where(causal, r[h:h + 1, :] - mx_b, -jnp.inf))
            e_int_b = jnp.exp(m0[h:h + 1, :] - mx_b)
            if lo:
                qp = prow[ts, (h // 2) * 2 * DK:(h // 2 + 1) * 2 * DK]
                qpb = qp.astype(BF)
            kt = pcol[h * DK:(h + 1) * DK, ts]
            ktb = kt.astype(BF)
            s = _dot(qpb, jnp.concatenate([ktb, zeros_k] if lo else [zeros_k, ktb], axis=0)) * dec
            st = s_ref[0, h]
            stb = st.astype(BF)
            v = prow[ts, HK + h * DV:HK + (h + 1) * DV].astype(BF)
            rhs = jnp.concatenate(
                [jnp.concatenate([v, zeros_v], axis=1)]
                + ([stb, zeros_s] if lo else [zeros_s, stb]), axis=0)
            lhs = jnp.concatenate([s.astype(BF), (qp * e_int_b).astype(BF)], axis=1)
            res = _dot(lhs, rhs)
            den = jnp.sum(s, axis=-1, keepdims=True) + res[:, DV:]
            hh = res[:, :DV] / jnp.maximum(jnp.abs(den), floor_b)
            hh = hh * lax.rsqrt(jnp.mean(hh * hh, axis=-1, keepdims=True) + EPS)
            y_scr[ts, h * DV:(h + 1) * DV] = hh
            ka = (kt * a[h:h + 1, :]).astype(BF)
            dc = decay[h:h + 1, :]
            s_ref[0, h] = (jnp.concatenate([dc, dc], axis=1) * st
                           + _dot(ka, jnp.concatenate([v, ones_v], axis=1)))

    o = prow[:, HK + D:]
    y = (y_scr[...] * gh_ref[...] * jax.nn.sigmoid(o)).astype(BF)
    xo_ref[0] = x + mod_ref[0, 2] * _dot(y, wout_ref[...])


def _mlstm_prompt(x, mod, g, w_row, w_col, b_g, g_head, w_out, *, tm, casts=()):
    nb, t, _ = x.shape
    nt = t // tm
    tok = pl.BlockSpec((1, tm, D), lambda b, i: (b, i, 0))
    args = (x, mod, g.reshape(1, D), w_row, w_col,
            jnp.broadcast_to(b_g.reshape(2 * H, 1), (2 * H, tm)), g_head.reshape(1, D), w_out)
    c_in, c_out, c_shapes, c_args = _cast_specs(casts, nb * nt, lambda b, i: b * nt + i)
    return pl.pallas_call(
        _carry_casts(functools.partial(_mlstm_prompt_kernel, tm=tm), len(args), 3, len(casts)),
        grid=(nb, nt),
        in_specs=[
            tok,
            pl.BlockSpec((1, N_MOD, 1, D), lambda b, i: (b, 0, 0, 0)),
            _const_spec((1, D)),
            _const_spec((D, HK + 2 * D)),
            _const_spec((HK + 2 * LANES, D)),
            _const_spec((2 * H, tm)),
            _const_spec((1, D)),
            _const_spec((D, D)),
        ] + c_in,
        out_specs=[
            tok,
            pl.BlockSpec((1, H, DK, 2 * DV), lambda b, i: (b, 0, 0, 0)),
            pl.BlockSpec((1, H, CHUNK), lambda b, i: (b, 0, 0)),
        ] + c_out,
        out_shape=[
            jax.ShapeDtypeStruct(x.shape, F32),
            jax.ShapeDtypeStruct((nb, H, DK, 2 * DV), F32),
            jax.ShapeDtypeStruct((nb, H, CHUNK), F32),
        ] + c_shapes,
        scratch_shapes=[pltpu.VMEM((tm, D), F32)],
        compiler_params=_params("arbitrary", "arbitrary"),
        name="mlstm_prompt",
    )(*args, *c_args)


def _mlstm_sample_pre_body(x_ref, mod_ref, g_ref, wrow_ref, wcol_ref, bg_ref, m0_ref, n0_ref,
                           gh_ref, seg_ref, e64_ref, e128_ref,
                           q_ref, ka_ref, v_ref, dec_ref, numi_ref, rd_ref, eint_ref, go_ref,
                           n1_ref, m1_ref, *, t, nb):
    def norm(xr, shift, scale):
        return _modulate(xr, g_ref[...], shift, scale)

    hb = _by_group(norm, (x_ref[0],), (mod_ref[0, 0], mod_ref[0, 1])).astype(BF)
    p = _dot(hb, wrow_ref[...])
    k_all = _dot_nt(hb, wcol_ref[0:HK, :])
    ig_all = _dot_nt(hb, wcol_ref[HK:HK + LANES, :]) + bg_ref[0:1]
    fg_all = _dot_nt(hb, wcol_ref[HK + LANES:HK + 2 * LANES, :]) + bg_ref[1:2]

    def rows(i):
        return slice(i * nb, (i + 1) * nb)

    q = [p[rows(i), 0:HK] for i in range(t)]
    k = [k_all[rows(i)] for i in range(t)]
    v = [p[rows(i), HK:HK + D] for i in range(t)]
    m0 = m0_ref[...]
    n0 = n0_ref[...]

    b, r, mx = [], [], []
    for i in range(t):
        lf = _log_sigmoid(fg_all[rows(i)])
        b.append(lf if i == 0 else b[-1] + lf)
        r.append(ig_all[rows(i)] - b[i])
        cm = r[i] if i == 0 else jnp.maximum(cm, r[i])
        mx.append(jnp.maximum(m0, cm))
    mx_last = mx[t - 1]
    decay = jnp.exp(m0 - mx_last)
    m1_ref[...] = b[t - 1] + mx_last
    dec_x = _split_dot(decay, e128_ref)
    n1 = _split_dot(decay, e64_ref) * n0

    for i in range(t):
        ka = k[i] * _split_dot(jnp.exp(r[i] - mx_last), e64_ref)
        n1 = n1 + ka
        seq_rows = pl.ds(i, nb, stride=t)
        for c in range(HK // LANES):
            cs = slice(c * LANES, (c + 1) * LANES)
            ka_ref[c, seq_rows, :] = ka[:, cs]
            q_ref[c, seq_rows, :] = q[i][:, cs]
        for c in range(D // LANES):
            cs = slice(c * LANES, (c + 1) * LANES)
            v_ref[c, seq_rows, :] = v[i][:, cs]
            dec_ref[c, seq_rows, :] = dec_x[:, cs]
        e_int = jnp.exp(m0 - mx[i])
        den = e_int * _split_dot(q[i] * n0, seg_ref)
        numi = jnp.zeros((nb, D), F32)
        for s in range(i + 1):
            w = _split_dot(q[i] * k[s], seg_ref) * jnp.exp(jnp.minimum(r[s] - mx[i], 0.0))
            den = den + w
            numi = numi + _split_dot(w, e128_ref) * v[s]
        numi_ref[rows(i), :] = numi
        rd_ref[rows(i), :] = 1.0 / jnp.maximum(jnp.abs(den), jnp.exp(-(b[i] + mx[i])))
        eint_ref[rows(i), :] = e_int
        go_ref[rows(i), :] = jax.nn.sigmoid(p[rows(i), HK + D:]) * gh_ref[...]
    n1_ref[...] = n1


def _mlstm_sample_state_body(q_ref, ka_ref, v_ref, dec_ref, c0_ref, inter_ref, c1_ref, *, bb, t,
                             row0):
    per = SUBLANES // t
    seq_of_row = lax.broadcasted_iota(jnp.int32, (SUBLANES, 1), 0) // t
    zeros_c = jnp.zeros((DK, DV), BF)

    def tile(i, carry):
        rs = pl.ds(pl.multiple_of(row0 + i * SUBLANES, SUBLANES), SUBLANES)
        for h in range(H):
            lo = h % 2 == 0
            half = slice((h % 2) * DK, (h % 2 + 1) * DK)
            qp = q_ref[h // 2, rs, :].astype(BF)
            kap = ka_ref[h // 2, rs, :]
            vh = v_ref[h, rs, :].astype(BF)
            dech = dec_ref[h, rs, :]
            res = None
            for w in range(per):
                seq = i * per + w
                c0 = c0_ref[seq, h]
                c0b = c0.astype(BF)
                rw = _dot(qp, jnp.concatenate([c0b, zeros_c] if lo else [zeros_c, c0b], axis=0))
                res = rw if res is None else jnp.where(seq_of_row == w, rw, res)
                kaw = jnp.where(seq_of_row == w, kap, 0.0).astype(BF)
                c1_ref[seq, h] = dech[w * t:w * t + 1, :] * c0 + _dot_tn(kaw, vh)[half, :]
            inter_ref[h, rs, :] = res
        return carry

    lax.fori_loop(0, bb // per, tile, 0, unroll=4)


def _mlstm_sample_post_body(inter_ref, numi_ref, rd_ref, eint_ref, go_ref, e128_ref, seg_ref,
                            y_ref, *, t, nb):
    for i in range(t):
        rs = slice(i * nb, (i + 1) * nb)
        inter = jnp.concatenate(
            [inter_ref[h, pl.ds(i, nb, stride=t), :] for h in range(H)], axis=1)
        num = numi_ref[rs, :] + _split_dot(eint_ref[rs, :], e128_ref) * inter
        hh = num * _split_dot(rd_ref[rs, :], e128_ref)
        ms = _split_dot(hh * hh, seg_ref) * (1.0 / DV)
        y_ref[rs, :] = hh * _split_dot(lax.rsqrt(ms + EPS), e128_ref) * go_ref[rs, :]


def _mlstm_sample_kernel(x_ref, mod_ref, g_ref, wrow_ref, wcol_ref, bg_ref, m0_ref, n0_ref, gh_ref,
                         seg64_ref, e64_ref, e128_ref, seg128_ref, c0_ref,
                         y_ref, n1_ref, m1_ref, c1_ref,
                         q_s, ka_s, v_s, dec_s, inter_s, numi_s, rd_s, eint_s, go_s, *, t, nb, bb):
    step = pl.program_id(0)

    @pl.when(step == 0)
    def _():
        _mlstm_sample_pre_body(x_ref, mod_ref, g_ref, wrow_ref, wcol_ref, bg_ref, m0_ref, n0_ref,
                               gh_ref, seg64_ref, e64_ref, e128_ref,
                               q_s, ka_s, v_s, dec_s, numi_s, rd_s, eint_s, go_s, n1_ref, m1_ref,
                               t=t, nb=nb)

    _mlstm_sample_state_body(q_s, ka_s, v_s, dec_s, c0_ref, inter_s, c1_ref, bb=bb, t=t,
                             row0=step * (bb * t))

    @pl.when(step == pl.num_programs(0) - 1)
    def _():
        _mlstm_sample_post_body(inter_s, numi_s, rd_s, eint_s, go_s, e128_ref, seg128_ref, y_ref,
                                t=t, nb=nb)


def _mlstm_sample(x, mod, layer, g, w_row, w_col, b_g, m0, n0, g_head, c0, *, t, bb):
    rows = x.shape[1]
    nb = rows // t
    assert SUBLANES % t == 0 and bb % (SUBLANES // t) == 0 and nb % bb == 0
    e128 = _head_expansion(DV)
    e64 = _head_expansion(DK)
    state = pl.BlockSpec((bb, H, DK, DV), lambda i: (i, 0, 0, 0))
    return pl.pallas_call(
        functools.partial(_mlstm_sample_kernel, t=t, nb=nb, bb=bb),
        grid=(nb // bb,),
        in_specs=[
            _const_spec((1, rows, D)),
            _group_mod_spec(layer, nb),
            _const_spec((1, D)),
            _const_spec((D, HK + 2 * D)),
            _const_spec((HK + 2 * LANES, D)),
            _const_spec((2, LANES)),
            _const_spec((nb, FG)),
            _const_spec((nb, HK)),
            _const_spec((1, D)),
            _const_spec((HK, FG)),
            _const_spec((FG, HK)),
            _const_spec((FG, D)),
            _const_spec((D, FG)),
            state,
        ],
        out_specs=[_whole_spec((rows, D)), _whole_spec((nb, HK)), _whole_spec((nb, FG)), state],
        out_shape=[jax.ShapeDtypeStruct((rows, D), F32), jax.ShapeDtypeStruct((nb, HK), F32),
                   jax.ShapeDtypeStruct((nb, FG), F32), jax.ShapeDtypeStruct(c0.shape, F32)],
        scratch_shapes=[
            pltpu.VMEM((HK // LANES, rows, LANES), F32),
            pltpu.VMEM((HK // LANES, rows, LANES), F32),
            pltpu.VMEM((D // LANES, rows, LANES), F32),
            pltpu.VMEM((D // LANES, rows, LANES), F32),
            pltpu.VMEM((D // LANES, rows, LANES), F32),
            pltpu.VMEM((rows, D), F32),
            pltpu.VMEM((rows, FG), F32),
            pltpu.VMEM((rows, FG), F32),
            pltpu.VMEM((rows, D), F32),
        ],
        compiler_params=_params("arbitrary"),
        name="mlstm_sample",
    )(x, mod, g.reshape(1, D), w_row, w_col, b_g, m0, n0, g_head.reshape(1, D),
      e64.T, e64, e128, e128.T, c0)


def _conv_taps(u, prev2, prev1, wc_ref):
    return wc_ref[0:1] * prev2 + wc_ref[1:2] * prev1 + wc_ref[2:3] * u


def _conv_prompt_tile(x_ref, mod_ref, g_ref, win_ref, wc_ref, wout_ref, xo_ref, st_ref, carry_scr,
                      first):
    @pl.when(first)
    def _():
        carry_scr[...] = jnp.zeros(carry_scr.shape, F32)

    x = x_ref[0]
    tm = x.shape[0]
    hb = _modulate(x, g_ref[...], mod_ref[0, 0], mod_ref[0, 1]).astype(BF)
    u = _dot(hb, win_ref[:, D:2 * D]) * _dot(hb, win_ref[:, 2 * D:])
    row = lax.broadcasted_iota(jnp.int32, (tm, D), 0)
    c0 = carry_scr[0:1]
    c1 = carry_scr[1:2]
    prev1 = jnp.where(row == 0, c1, pltpu.roll(u, 1, 0))
    prev2 = jnp.where(row == 0, c0, jnp.where(row == 1, c1, pltpu.roll(u, 2, 0)))
    y = _conv_taps(u, prev2, prev1, wc_ref)
    bg = _dot(hb, win_ref[:, :D])
    xo_ref[0] = x + mod_ref[0, 2] * _dot((bg * y).astype(BF), wout_ref[...])
    carry_scr[...] = u[tm - 2:tm]
    st_ref[0] = u[tm - 2:tm]


def _conv_sample_tile(x_ref, mod_ref, g_ref, win_ref, wc_ref, wout_ref, buf_ref, xo_ref, st_ref,
                      *, t, nb):
    def norm(xr, shift, scale):
        return _modulate(xr, g_ref[...], shift, scale)

    def residual(xr, fr, gate):
        return xr + gate * fr

    x = x_ref[0]
    hb = _by_group(norm, (x,), (mod_ref[0, 0], mod_ref[0, 1])).astype(BF)
    u = _dot(hb, win_ref[:, D:2 * D]) * _dot(hb, win_ref[:, 2 * D:])
    full = [buf_ref[j] for j in range(CONV_W - 1)] + [u[i * nb:(i + 1) * nb] for i in range(t)]
    y = jnp.concatenate([_conv_taps(full[i + 2], full[i], full[i + 1], wc_ref) for i in range(t)],
                        axis=0)
    bg = _dot(hb, win_ref[:, :D])
    out = _dot((bg * y).astype(BF), wout_ref[...])
    xo_ref[0] = _by_group(residual, (x, out), (mod_ref[0, 2],))
    for j in range(CONV_W - 1):
        st_ref[j] = full[t + j]


def _conv_kernel(xp_ref, modp_ref, xs_ref, mods_ref, g_ref, win_ref, wc_ref, wout_ref, buf_ref,
                 xpo_ref, stp_ref, xso_ref, sts_ref, carry_scr, *, n_prompt, nt, t, nb):
    step = pl.program_id(0)

    @pl.when(step < n_prompt)
    def _():
        _conv_prompt_tile(xp_ref, modp_ref, g_ref, win_ref, wc_ref, wout_ref, xpo_ref, stp_ref,
                          carry_scr, step % nt == 0)

    @pl.when(step == n_prompt)
    def _():
        _conv_sample_tile(xs_ref, mods_ref, g_ref, win_ref, wc_ref, wout_ref, buf_ref, xso_ref,
                          sts_ref, t=t, nb=nb)


def _conv(xp, mod_p, xs, mod, layer, g, w_in, w_conv, w_out, buf, *, tm, t, casts=()):
    nb, tp, _ = xp.shape
    nt = tp // tm
    n_prompt = nb * nt
    rows_s = xs.shape[1]
    seqs = rows_s // t
    tok, per_seq = _prompt_tiles(nb, nt)
    args = (xp, mod_p, xs, mod, g.reshape(1, D), w_in, w_conv, w_out, buf)
    c_in, c_out, c_shapes, c_args = _cast_specs(
        casts, n_prompt, lambda s: jnp.minimum(s, n_prompt - 1))
    body = functools.partial(_conv_kernel, n_prompt=n_prompt, nt=nt, t=t, nb=seqs)
    return pl.pallas_call(
        _carry_casts(body, len(args), 4, len(casts)),
        grid=(n_prompt + 1,),
        in_specs=[
            tok(tm),
            per_seq((N_MOD, 1, D)),
            _const_spec((1, rows_s, D)),
            _group_mod_spec(layer, seqs),
            _const_spec((1, D)),
            _const_spec((D, 3 * D)),
            _const_spec((CONV_W, D)),
            _const_spec((D, D)),
            _const_spec((CONV_W - 1, seqs, D)),
        ] + c_in,
        out_specs=[tok(tm), per_seq((CONV_W - 1, D)), _whole_spec((1, rows_s, D)),
                   _whole_spec((CONV_W - 1, seqs, D))] + c_out,
        out_shape=[jax.ShapeDtypeStruct(xp.shape, F32),
                   jax.ShapeDtypeStruct((nb, CONV_W - 1, D), F32),
                   jax.ShapeDtypeStruct(xs.shape, F32),
                   jax.ShapeDtypeStruct((CONV_W - 1, seqs, D), F32)] + c_shapes,
        scratch_shapes=[pltpu.VMEM((CONV_W - 1, D), F32)],
        compiler_params=_params("arbitrary", vmem=VMEM_LIMIT_CONV),
        name="conv",
    )(*args, *c_args)


def kernel(x_prompt, x_sample, c_prompt, c_sample, state_mlstm_C, state_mlstm_n, state_mlstm_m,
           state_conv, ada_w, ada_b, norm_g, final_norm_g, mlstm_w_in, mlstm_b_gates,
           mlstm_head_g, mlstm_w_out, conv_w_in, conv_w, conv_w_out, mlp_w_up, mlp_w_down):
    assert ada_w.shape[0] == 2 and mlstm_w_in.shape[0] == 1 and conv_w_in.shape[0] == 1
    bp, tp, _ = x_prompt.shape
    bs, ts, _ = x_sample.shape
    assert ts == 4 and tp % 512 == 0

    mod = _ada(jnp.concatenate([c_sample, c_prompt], axis=0), ada_w, ada_b)
    mod_p = jnp.transpose(mod[:, :, bs:], (0, 2, 1, 3))[:, :, :, None]

    w_in0 = mlstm_w_in[0]
    w_row = jnp.concatenate([w_in0[:, :HK], w_in0[:, 2 * HK:NP]], axis=1).astype(BF)
    gate_pad = jnp.zeros((D, LANES - H), F32)
    w_col = jnp.concatenate([w_in0[:, HK:2 * HK] * (DK ** -0.5), w_in0[:, NP:NP + H], gate_pad,
                             w_in0[:, NP + H:], gate_pad], axis=1).T.astype(BF)
    b_g = jnp.pad(mlstm_b_gates[0].reshape(2, H), ((0, 0), (0, LANES - H)))
    wout0 = mlstm_w_out[0].astype(BF)

    x, s_p, m_p, up0, down0 = _mlstm_prompt(
        x_prompt, mod_p[0], norm_g[0, 0], w_row, w_col, mlstm_b_gates[0], mlstm_head_g[0], wout0,
        tm=1024, casts=((mlp_w_up, 0), (mlp_w_down, 0)))
    prompt_c = s_p[None, :, :, :, :DV]
    prompt_n = s_p[None, :, :, :, DV]
    prompt_m = m_p[None, :, :, 0]

    xs = jnp.transpose(x_sample, (1, 0, 2)).reshape(1, ts * bs, D)
    m0 = jnp.pad(state_mlstm_m[0], ((0, 0), (0, FG - H)))
    n0 = state_mlstm_n[0].reshape(bs, HK)
    y, n1, m1, c1 = _mlstm_sample(xs, mod, 0, norm_g[0, 0], w_row, w_col, b_g, m0, n0,
                                  mlstm_head_g[0], state_mlstm_C[0], t=ts, bb=16)
    sample_n = n1.reshape(bs, H, DK)[None]
    sample_m = m1[None, :, :H]

    x, xs, cwin, cwout = _mlp(x, mod_p[0], xs, mod, 0, norm_g[0, 1], up0, down0, tm=512, seqs=bs,
                              pre=(y.reshape(1, ts * bs, D), wout0),
                              casts=((conv_w_in, 0), (conv_w_out, 0)))
    x, conv_p, xs, conv_s, up1, down1 = _conv(
        x, mod_p[1], xs, mod, 1, norm_g[1, 0], cwin, conv_w[0], cwout,
        jnp.transpose(state_conv[0], (1, 0, 2)), tm=512, t=ts,
        casts=((mlp_w_up, 1), (mlp_w_down, 1)))
    y_prompt, ys = _mlp(x, mod_p[1], xs, mod, 1, norm_g[1, 1], up1, down1, tm=512, seqs=bs,
                        final_g=final_norm_g)
    prompt_conv = conv_p[None]
    y_sample = jnp.transpose(ys.reshape(ts, bs, D), (1, 0, 2))
    sample_conv = jnp.transpose(conv_s, (1, 0, 2))[None]

    return (y_prompt, y_sample, prompt_c, prompt_n, prompt_m, prompt_conv,
            c1[None], sample_n, sample_m, sample_conv)
```
